```python
import jax, jax.numpy as jnp
from jax import lax
import numpy as np

D_MODEL = 1024
BATCH = 8
SEQ = 2048
DEPTH = 4
DEC_BATCH = 128
DEC_SEQ = 4
PAST_LEN = 8192
PAGE_SIZE = 128

HEAD_DIM = 64
N_Q_HEADS = 12
N_KV_HEADS = 4
GQA_GROUP = N_Q_HEADS // N_KV_HEADS
WINDOW = 128
Q_BLOCK = WINDOW
ROPE_THETA = 10000.0
CONV_CH = N_Q_HEADS * HEAD_DIM
CONV_W = 3
N_MEM = 256
N_X_HEADS = 4
X_DIM = N_X_HEADS * HEAD_DIM
D_FF = 4 * D_MODEL
N_MIXERS = 2
N_ATT_LAYERS = (DEPTH + 1) // 2
N_CONV_LAYERS = DEPTH // 2
ATT_IN = N_Q_HEADS * HEAD_DIM + 2 * N_KV_HEADS * HEAD_DIM + X_DIM
CONV_IN = 3 * CONV_CH + X_DIM
MIX_OUT = CONV_CH + X_DIM
EPS = 1e-6
NEG = -1e30

kernel_name = "hybrid_swa_sink_shortconv_memxattn_step"


def _rms_norm(x, g):
    xf = x.astype(jnp.float32)
    y = xf * lax.rsqrt(jnp.mean(xf * xf, axis=-1, keepdims=True) + EPS) * g.astype(jnp.float32)
    return y.astype(x.dtype)


def _rope(x, pos):
    half = HEAD_DIM // 2
    inv = ROPE_THETA ** (-jnp.arange(half, dtype=jnp.float32) * 2.0 / HEAD_DIM)
    ang = pos[:, None] * inv[None, :]
    cos = jnp.cos(ang)[:, None, :]
    sin = jnp.sin(ang)[:, None, :]
    xf = x.astype(jnp.float32)
    x1, x2 = xf[..., :half], xf[..., half:]
    return jnp.concatenate([x1 * cos - x2 * sin, x2 * cos + x1 * sin], axis=-1).astype(x.dtype)


def _sink_softmax(s, mask, sink):
    s = jnp.where(mask, s, NEG)
    sink = sink.astype(jnp.float32)
    m = jnp.maximum(jnp.max(s, axis=-1, keepdims=True), sink)
    p = jnp.exp(s - m)
    return p / (jnp.sum(p, axis=-1, keepdims=True) + jnp.exp(sink - m))


def _swa_prompt(q, k, v, sink):
    n, t = q.shape[0], q.shape[1]
    nb = t // Q_BLOCK
    qb = q.reshape(n, nb, Q_BLOCK, N_KV_HEADS, GQA_GROUP, HEAD_DIM)
    pad = jnp.zeros((n, Q_BLOCK, N_KV_HEADS, HEAD_DIM), k.dtype)
    kb = jnp.concatenate([pad, k], axis=1).reshape(n, nb + 1, Q_BLOCK, N_KV_HEADS, HEAD_DIM)
    vb = jnp.concatenate([pad.astype(v.dtype), v], axis=1).reshape(n, nb + 1, Q_BLOCK, N_KV_HEADS, HEAD_DIM)
    kb = jnp.concatenate([kb[:, :-1], kb[:, 1:]], axis=2)
    vb = jnp.concatenate([vb[:, :-1], vb[:, 1:]], axis=2)
    s = jnp.einsum("bnqkgd,bnskd->bnkgqs", qb, kb, preferred_element_type=jnp.float32) * (HEAD_DIM ** -0.5)
    blk = jnp.arange(nb)[:, None, None] * Q_BLOCK
    qpos = blk + jnp.arange(Q_BLOCK)[None, :, None]
    kpos = blk - Q_BLOCK + jnp.arange(2 * Q_BLOCK)[None, None, :]
    d = qpos - kpos
    mask = (d >= 0) & (d < WINDOW) & (kpos >= 0)
    mask = mask[None, :, None, None]
    pr = _sink_softmax(s, mask, sink.reshape(N_KV_HEADS, GQA_GROUP, 1, 1))
    o = jnp.einsum("bnkgqs,bnskd->bnqkgd", pr.astype(vb.dtype), vb)
    return o.reshape(n, t, N_Q_HEADS, HEAD_DIM)


def _swa_sample(q, k, v, kbuf, vbuf, sink):
    n, t = q.shape[0], q.shape[1]
    nbuf = kbuf.shape[1]
    kk = jnp.concatenate([kbuf.astype(k.dtype), k], axis=1)
    vv = jnp.concatenate([vbuf.astype(v.dtype), v], axis=1)
    qg = q.reshape(n, t, N_KV_HEADS, GQA_GROUP, HEAD_DIM)
    s = jnp.einsum("bqkgd,bskd->bkgqs", qg, kk, preferred_element_type=jnp.float32) * (HEAD_DIM ** -0.5)
    qpos = PAST_LEN + jnp.arange(t)
    kpos = PAST_LEN - nbuf + jnp.arange(nbuf + t)
    d = qpos[:, None] - kpos[None, :]
    mask = (d >= 0) & (d < WINDOW)
    pr = _sink_softmax(s, mask, sink.reshape(N_KV_HEADS, GQA_GROUP, 1, 1))
    o = jnp.einsum("bkgqs,bskd->bqkgd", pr.astype(vv.dtype), vv).reshape(n, t, N_Q_HEADS, HEAD_DIM)
    return o, kk[:, -WINDOW:], vv[:, -WINDOW:]


def _causal_conv(u, prefix, w):
    t = u.shape[1]
    full = jnp.concatenate([prefix.astype(u.dtype), u], axis=1)
    y = full[:, 0:t] * w[0]
    for j in range(1, CONV_W):
        y = y + full[:, j:j + t] * w[j]
    return y, full[:, -(CONV_W - 1):]


def _mem_kv(mem, g, w_kv, k_g):
    n = mem.shape[0]
    kv = _rms_norm(mem, g) @ w_kv
    mk = kv[..., :X_DIM].reshape(n, N_MEM, N_X_HEADS, HEAD_DIM)
    mv = kv[..., X_DIM:].reshape(n, N_MEM, N_X_HEADS, HEAD_DIM)
    return _rms_norm(mk, k_g), mv


def _cross_attend(q, mk, mv):
    s = jnp.einsum("bthd,bmhd->bhtm", q, mk.astype(q.dtype), preferred_element_type=jnp.float32) * (HEAD_DIM ** -0.5)
    p = jax.nn.softmax(s, axis=-1)
    return jnp.einsum("bhtm,bmhd->bthd", p.astype(mv.dtype), mv)


def _trunk(x, pos0, mem_k, mem_v, swa_k_in, swa_v_in, conv_in, is_prompt, p):
    n, t = x.shape[0], x.shape[1]
    pos = pos0 + jnp.arange(t, dtype=jnp.float32)
    new_k, new_v, new_conv = [], [], []
    for i in range(DEPTH):
        j = i // N_MIXERS
        h = _rms_norm(x, p["norm_mix"][i])
        if i % N_MIXERS == 0:
            z = h @ p["w_in_att"][j]
            c0 = N_Q_HEADS * HEAD_DIM
            c1 = c0 + N_KV_HEADS * HEAD_DIM
            c2 = c1 + N_KV_HEADS * HEAD_DIM
            q = z[..., :c0].reshape(n, t, N_Q_HEADS, HEAD_DIM)
            k = z[..., c0:c1].reshape(n, t, N_KV_HEADS, HEAD_DIM)
            v = z[..., c1:c2].reshape(n, t, N_KV_HEADS, HEAD_DIM)
            qx = z[..., c2:]
            q = _rope(_rms_norm(q, p["q_norm_att"][j]), pos)
            k = _rope(_rms_norm(k, p["k_norm_att"][j]), pos)
            if is_prompt:
                o_self = _swa_prompt(q, k, v, p["sinks"][j])
                kb, vb = k[:, -WINDOW:], v[:, -WINDOW:]
            else:
                o_self, kb, vb = _swa_sample(q, k, v, swa_k_in[j], swa_v_in[j], p["sinks"][j])
            new_k.append(kb)
            new_v.append(vb)
            o_self = o_self.reshape(n, t, N_Q_HEADS * HEAD_DIM)
        else:
            z = h @ p["w_in_conv"][j]
            gb = z[..., :CONV_CH]
            gc = z[..., CONV_CH:2 * CONV_CH]
            u = z[..., 2 * CONV_CH:3 * CONV_CH]
            qx = z[..., 3 * CONV_CH:]
            if is_prompt:
                prefix = jnp.zeros((n, CONV_W - 1, CONV_CH), u.dtype)
            else:
                prefix = conv_in[j]
            y, st = _causal_conv(gc * u, prefix, p["conv_w"][j])
            new_conv.append(st)
            o_self = gb * y
        qx = _rms_norm(qx.reshape(n, t, N_X_HEADS, HEAD_DIM), p["q_norm_x"][i])
        o_x = _cross_attend(qx, mem_k[i], mem_v[i]).reshape(n, t, X_DIM)
        o = jnp.concatenate([o_self, o_x.astype(o_self.dtype)], axis=-1)
        x = x + o @ p["w_out"][i]
        hm = _rms_norm(x, p["norm_mlp"][i])
        x = x + jnp.square(jax.nn.relu(hm @ p["w_up"][i])) @ p["w_down"][i]
    return x, jnp.stack(new_k), jnp.stack(new_v), jnp.stack(new_conv)


def setup_inputs(seed: int = 0) -> dict:
    key = jax.random.key(seed)
    ks = jax.random.split(key, 24)
    f32 = jnp.float32

    def nrm(k, shape, scale=1.0):
        return jax.random.normal(k, shape, f32) * scale

    def gain(k, shape):
        return 1.0 + 0.05 * jax.random.normal(k, shape, f32)

    return {
        "x_prompt": nrm(ks[0], (BATCH, SEQ, D_MODEL)),
        "x_sample": nrm(ks[1], (DEC_BATCH, DEC_SEQ, D_MODEL)),
        "mem_prompt": nrm(ks[2], (BATCH, N_MEM, D_MODEL)),
        "cache_swa_k": nrm(ks[3], (N_ATT_LAYERS, DEC_BATCH, WINDOW, N_KV_HEADS, HEAD_DIM)),
        "cache_swa_v": nrm(ks[4], (N_ATT_LAYERS, DEC_BATCH, WINDOW, N_KV_HEADS, HEAD_DIM)),
        "state_conv": nrm(ks[5], (N_CONV_LAYERS, DEC_BATCH, CONV_W - 1, CONV_CH)),
        "cache_mem_k": nrm(ks[6], (DEPTH, DEC_BATCH, N_MEM, N_X_HEADS, HEAD_DIM)),
        "cache_mem_v": nrm(ks[7], (DEPTH, DEC_BATCH, N_MEM, N_X_HEADS, HEAD_DIM)),
        "norm_mix": gain(ks[8], (DEPTH, D_MODEL)),
        "w_in_att": nrm(ks[9], (N_ATT_LAYERS, D_MODEL, ATT_IN), D_MODEL ** -0.5),
        "q_norm_att": gain(ks[10], (N_ATT_LAYERS, HEAD_DIM)),
        "k_norm_att": gain(ks[11], (N_ATT_LAYERS, HEAD_DIM)),
        "sinks": nrm(ks[12], (N_ATT_LAYERS, N_Q_HEADS), 0.5),
        "w_in_conv": nrm(ks[13], (N_CONV_LAYERS, D_MODEL, CONV_IN), D_MODEL ** -0.5),
        "conv_w": nrm(ks[14], (N_CONV_LAYERS, CONV_W, CONV_CH), CONV_W ** -0.5),
        "norm_mem": gain(ks[15], (DEPTH, D_MODEL)),
        "w_mem_kv": nrm(ks[16], (DEPTH, D_MODEL, 2 * X_DIM), D_MODEL ** -0.5),
        "q_norm_x": gain(ks[17], (DEPTH, HEAD_DIM)),
        "k_norm_x": gain(ks[18], (DEPTH, HEAD_DIM)),
        "w_out": nrm(ks[19], (DEPTH, MIX_OUT, D_MODEL), MIX_OUT ** -0.5),
        "norm_mlp": gain(ks[20], (DEPTH, D_MODEL)),
        "w_up": nrm(ks[21], (DEPTH, D_MODEL, D_FF), D_MODEL ** -0.5),
        "w_down": nrm(ks[22], (DEPTH, D_FF, D_MODEL), D_FF ** -0.5),
    }


def reference(x_prompt, x_sample, mem_prompt, cache_swa_k, cache_swa_v, state_conv, cache_mem_k, cache_mem_v,
              norm_mix, w_in_att, q_norm_att, k_norm_att, sinks, w_in_conv, conv_w, norm_mem, w_mem_kv,
              q_norm_x, k_norm_x, w_out, norm_mlp, w_up, w_down):
    p = {
        "norm_mix": norm_mix, "w_in_att": w_in_att, "q_norm_att": q_norm_att, "k_norm_att": k_norm_att,
        "sinks": sinks, "w_in_conv": w_in_conv, "conv_w": conv_w, "q_norm_x": q_norm_x,
        "w_out": w_out, "norm_mlp": norm_mlp, "w_up": w_up, "w_down": w_down,
    }
    mks, mvs = [], []
    for i in range(DEPTH):
        mk, mv = _mem_kv(mem_prompt, norm_mem[i], w_mem_kv[i], k_norm_x[i])
        mks.append(mk)
        mvs.append(mv)
    new_mem_k_prompt = jnp.stack(mks)
    new_mem_v_prompt = jnp.stack(mvs)
    y_prompt, new_swa_k_prompt, new_swa_v_prompt, new_conv_prompt = _trunk(
        x_prompt, 0.0, new_mem_k_prompt, new_mem_v_prompt, None, None, None, True, p)
    y_sample, new_swa_k_sample, new_swa_v_sample, new_conv_sample = _trunk(
        x_sample, float(PAST_LEN), cache_mem_k, cache_mem_v, cache_swa_k, cache_swa_v, state_conv, False, p)
    return (y_prompt, y_sample, new_swa_k_prompt, new_swa_v_prompt, new_conv_prompt,
            new_mem_k_prompt, new_mem_v_prompt, new_swa_k_sample, new_swa_v_sample, new_conv_sample)
```

```python
import functools

import jax
import jax.numpy as jnp
import numpy as np
from jax import lax
from jax.experimental import pallas as pl
from jax.experimental.pallas import tpu as pltpu

F32 = jnp.float32
BF16 = jnp.bfloat16

D_MODEL = 1024
DEPTH = 4
HEAD_DIM = 64
N_Q_HEADS = 12
N_KV_HEADS = 4
GQA_GROUP = N_Q_HEADS // N_KV_HEADS
WINDOW = 128
PAST_LEN = 8192
ROPE_THETA = 10000.0
CONV_CH = N_Q_HEADS * HEAD_DIM
CONV_W = 3
N_X_HEADS = 4
X_DIM = N_X_HEADS * HEAD_DIM
KV_DIM = N_KV_HEADS * HEAD_DIM
D_FF = 4 * D_MODEL
ATT_IN = CONV_CH + 2 * KV_DIM + X_DIM
CONV_IN = 3 * CONV_CH + X_DIM
EPS = 1e-6
NEG = -1e30
SCALE = HEAD_DIM ** -0.5

LANES = 128
SUBLANES = 8
MXU_DIM = 256
TOKEN_TILE = 512
FF_CHUNK = 2048
SEQ_BLOCK = 8

_NT = (((1,), (1,)), ((), ()))


def _const_spec(shape):
    nd = len(shape)
    return pl.BlockSpec(shape, lambda *_: (0,) * nd, pipeline_mode=pl.Buffered(1))


def _rms_rows(x, g):
    return x * lax.rsqrt(jnp.mean(x * x, axis=-1, keepdims=True) + EPS) * g


def _head_blockdiag():
    r = lax.broadcasted_iota(jnp.int32, (MXU_DIM, MXU_DIM), 0) // HEAD_DIM
    c = lax.broadcasted_iota(jnp.int32, (MXU_DIM, MXU_DIM), 1) // HEAD_DIM
    return jnp.where(r == c, 1.0, 0.0).astype(BF16)


def _head_mean_sq(z, bd):
    sq = z * z
    hi = sq.astype(BF16)
    lo = (sq - hi.astype(F32)).astype(BF16)
    parts = []
    for c in range(z.shape[1] // MXU_DIM):
        sl = slice(c * MXU_DIM, (c + 1) * MXU_DIM)
        parts.append(jnp.dot(hi[:, sl], bd, preferred_element_type=F32)
                     + jnp.dot(lo[:, sl], bd, preferred_element_type=F32))
    ms = parts[0] if len(parts) == 1 else jnp.concatenate(parts, axis=1)
    return ms * (1.0 / HEAD_DIM)


def _rope(x, cos, sin_signed):
    t = x.shape[0]
    first_half = (lax.broadcasted_iota(jnp.int32, (t, LANES), 1) & (HEAD_DIM - 1)) < HEAD_DIM // 2
    parts = []
    for c in range(x.shape[1] // LANES):
        xc = x[:, c * LANES:(c + 1) * LANES]
        partner = jnp.where(first_half,
                            pltpu.roll(xc, LANES - HEAD_DIM // 2, axis=1),
                            pltpu.roll(xc, HEAD_DIM // 2, axis=1))
        parts.append(xc * cos + partner * sin_signed)
    return jnp.concatenate(parts, axis=1)


def _mem_kv_kernel(mem_ref, g_ref, w_ref, gk_ref, mk_ref, mv_ref):
    h = _rms_rows(mem_ref[...], g_ref[0]).astype(BF16)
    kv = jnp.dot(h, w_ref[0], preferred_element_type=F32)
    k = kv[:, :X_DIM]
    ms = _head_mean_sq(k, _head_blockdiag())
    mk_ref[0] = k * lax.rsqrt(ms + EPS) * gk_ref[0]
    mv_ref[0] = kv[:, X_DIM:]


def _mem_kv(mem2d, norm_mem, w_mem_kv, k_norm_x_tiled):
    n = mem2d.shape[0]
    tm = TOKEN_TILE
    return pl.pallas_call(
        _mem_kv_kernel,
        grid=(DEPTH, n // tm),
        in_specs=[
            pl.BlockSpec((tm, D_MODEL), lambda i, t: (t, 0)),
            pl.BlockSpec((1, 1, D_MODEL), lambda i, t: (i, 0, 0)),
            pl.BlockSpec((1, D_MODEL, 2 * X_DIM), lambda i, t: (i, 0, 0)),
            pl.BlockSpec((1, 1, X_DIM), lambda i, t: (i, 0, 0)),
        ],
        out_specs=[
            pl.BlockSpec((1, tm, X_DIM), lambda i, t: (i, t, 0)),
            pl.BlockSpec((1, tm, X_DIM), lambda i, t: (i, t, 0)),
        ],
        out_shape=[jax.ShapeDtypeStruct((DEPTH, n, X_DIM), F32)] * 2,
        name="mem_kv",
    )(mem2d, norm_mem, w_mem_kv, k_norm_x_tiled)


def _in_att_kernel(x_ref, g_ref, w_ref, gqk_ref, gx_ref, cos_ref, sin_ref,
                   q_ref, k_ref, v_ref, qx_ref, k32_ref, v32_ref):
    t = x_ref.shape[0]
    keep = k32_ref.shape[0]
    h = _rms_rows(x_ref[...], g_ref[...]).astype(BF16)
    z = jnp.dot(h, w_ref[...], preferred_element_type=F32)
    bd = _head_blockdiag()
    qk = z[:, :CONV_CH + KV_DIM]
    qk = qk * lax.rsqrt(_head_mean_sq(qk, bd) + EPS) * gqk_ref[...]
    qk = _rope(qk, cos_ref[...], sin_ref[...])
    k = qk[:, CONV_CH:]
    v = z[:, CONV_CH + KV_DIM:CONV_CH + 2 * KV_DIM]
    q_ref[...] = (qk[:, :CONV_CH] * SCALE).astype(q_ref.dtype)
    k_ref[...] = k.astype(k_ref.dtype)
    v_ref[...] = v.astype(v_ref.dtype)
    k32_ref[...] = k[t - keep:, :]
    v32_ref[...] = v[t - keep:, :]
    qx = z[:, CONV_CH + 2 * KV_DIM:]
    qx = qx * lax.rsqrt(_head_mean_sq(qx, bd) + EPS) * gx_ref[...]
    qx_ref[...] = (qx * SCALE).astype(qx_ref.dtype)


def _in_att(x2d, g, w, gqk, gx, cos, sin, *, tiles_per_seq, keep_rows, act_dtype):
    n = x2d.shape[0]
    tm = TOKEN_TILE
    nt = n // tm
    nseq = nt // tiles_per_seq
    row = lambda i: (i, 0)
    seq = lambda i: (i // tiles_per_seq, 0)
    return pl.pallas_call(
        _in_att_kernel,
        grid=(nt,),
        in_specs=[
            pl.BlockSpec((tm, D_MODEL), row),
            _const_spec((1, D_MODEL)),
            _const_spec((D_MODEL, ATT_IN)),
            _const_spec((1, CONV_CH + KV_DIM)),
            _const_spec((1, X_DIM)),
            pl.BlockSpec((tm, LANES), lambda i: (i % tiles_per_seq, 0)),
            pl.BlockSpec((tm, LANES), lambda i: (i % tiles_per_seq, 0)),
        ],
        out_specs=[
            pl.BlockSpec((tm, CONV_CH), row),
            pl.BlockSpec((tm, KV_DIM), row),
            pl.BlockSpec((tm, KV_DIM), row),
            pl.BlockSpec((tm, X_DIM), row),
            pl.BlockSpec((keep_rows, KV_DIM), seq),
            pl.BlockSpec((keep_rows, KV_DIM), seq),
        ],
        out_shape=[
            jax.ShapeDtypeStruct((n, CONV_CH), act_dtype),
            jax.ShapeDtypeStruct((n, KV_DIM), act_dtype),
            jax.ShapeDtypeStruct((n, KV_DIM), act_dtype),
            jax.ShapeDtypeStruct((n, X_DIM), act_dtype),
            jax.ShapeDtypeStruct((nseq * keep_rows, KV_DIM), F32),
            jax.ShapeDtypeStruct((nseq * keep_rows, KV_DIM), F32),
        ],
        name="in_att",
    )(x2d, g, w, gqk, gx, cos, sin)


def _in_conv_kernel(*refs, tiles_per_seq, period):
    if period:
        (x_ref, g_ref, w_ref, gx_ref, cw_ref, prefix_ref,
         o_ref, qx_ref, gt_ref, gbuf, pbuf) = refs
    else:
        (x_ref, g_ref, w_ref, gx_ref, cw_ref,
         o_ref, qx_ref, gt_ref, gbuf) = refs
    t = x_ref.shape[0]
    keep = gt_ref.shape[0]
    h = _rms_rows(x_ref[...], g_ref[...]).astype(BF16)
    z = jnp.dot(h, w_ref[...], preferred_element_type=F32)
    gate_b = z[:, :CONV_CH]
    u = z[:, CONV_CH:2 * CONV_CH] * z[:, 2 * CONV_CH:3 * CONV_CH]

    @pl.when(pl.program_id(0) % tiles_per_seq == 0)
    def _():
        gbuf[0:SUBLANES, :] = jnp.zeros((SUBLANES, CONV_CH), F32)

    gbuf[SUBLANES:SUBLANES + t, :] = u
    back1 = gbuf[SUBLANES - 1:SUBLANES - 1 + t, :]
    back2 = gbuf[SUBLANES - 2:SUBLANES - 2 + t, :]
    if period:
        pbuf[0:t, :] = prefix_ref[...]
        pbuf[t:t + SUBLANES, :] = jnp.zeros((SUBLANES, CONV_CH), F32)
        tok = lax.broadcasted_iota(jnp.int32, (t, CONV_CH), 0) % period
        back1 = jnp.where(tok >= 1, back1, pbuf[1:1 + t, :])
        back2 = jnp.where(tok >= 2, back2, pbuf[0:t, :])
    cw = cw_ref[...]
    y = back2 * cw[0:1, :] + back1 * cw[1:2, :] + u * cw[2:3, :]
    gbuf[0:SUBLANES, :] = gbuf[t:t + SUBLANES, :]
    o_ref[...] = (gate_b * y).astype(o_ref.dtype)
    gt_ref[...] = u[t - keep:, :]
    qx = z[:, 3 * CONV_CH:]
    qx = qx * lax.rsqrt(_head_mean_sq(qx, _head_blockdiag()) + EPS) * gx_ref[...]
    qx_ref[...] = (qx * SCALE).astype(qx_ref.dtype)


def _in_conv(x2d, g, w, gx, cw, prefix, *, tiles_per_seq, period, keep_rows, act_dtype):
    n = x2d.shape[0]
    tm = TOKEN_TILE
    nt = n // tm
    nseq = nt // tiles_per_seq
    row = lambda i: (i, 0)
    in_specs = [
        pl.BlockSpec((tm, D_MODEL), row),
        _const_spec((1, D_MODEL)),
        _const_spec((D_MODEL, CONV_IN)),
        _const_spec((1, X_DIM)),
        _const_spec((CONV_W, CONV_CH)),
    ]
    args = [x2d, g, w, gx, cw]
    scratch = [pltpu.VMEM((tm + 2 * SUBLANES, CONV_CH), F32)]
    if period:
        in_specs.append(pl.BlockSpec((tm, CONV_CH), row))
        args.append(prefix)
        scratch.append(pltpu.VMEM((tm + SUBLANES, CONV_CH), F32))
    return pl.pallas_call(
        functools.partial(_in_conv_kernel, tiles_per_seq=tiles_per_seq, period=period),
        grid=(nt,),
        in_specs=in_specs,
        out_specs=[
            pl.BlockSpec((tm, CONV_CH), row),
            pl.BlockSpec((tm, X_DIM), row),
            pl.BlockSpec((keep_rows, CONV_CH), lambda i: (i // tiles_per_seq, 0)),
        ],
        out_shape=[
            jax.ShapeDtypeStruct((n, CONV_CH), act_dtype),
            jax.ShapeDtypeStruct((n, X_DIM), act_dtype),
            jax.ShapeDtypeStruct((nseq * keep_rows, CONV_CH), F32),
        ],
        scratch_shapes=scratch,
        name="in_conv",
    )(*args)


def _swa_prompt_kernel(sinks_ref, q_ref, kc_ref, kp_ref, vc_ref, vp_ref, o_ref, kfull, vfull):
    tq = q_ref.shape[0]
    first_tile = pl.program_id(1) == 0
    kfull[0:WINDOW, :] = kp_ref[...]
    kfull[WINDOW:, :] = kc_ref[...]
    vfull[0:WINDOW, :] = vp_ref[...]
    vfull[WINDOW:, :] = vc_ref[...]
    rows = GQA_GROUP * WINDOW
    r = lax.broadcasted_iota(jnp.int32, (rows, 2 * WINDOW), 0) & (WINDOW - 1)
    c = lax.broadcasted_iota(jnp.int32, (rows, 2 * WINDOW), 1)
    band = (c > r) & (c - WINDOW <= r)

    def block(qb, carry):
        r0 = pl.multiple_of(qb * WINDOW, WINDOW)
        c_min = jnp.where(jnp.logical_and(first_tile, qb == 0), WINDOW, 0)
        mask = band & (c >= c_min)
        qblk = q_ref[pl.ds(r0, WINDOW), :]
        kcat = kfull[pl.ds(r0, 2 * WINDOW), :]
        vcat = vfull[pl.ds(r0, 2 * WINDOW), :]
        outs = [None] * N_Q_HEADS
        for kh in range(N_KV_HEADS):
            lo = kh * HEAD_DIM
            qs = jnp.concatenate(
                [qblk[:, g * KV_DIM + lo:g * KV_DIM + lo + HEAD_DIM] for g in range(GQA_GROUP)], axis=0)
            s = lax.dot_general(qs, kcat[:, lo:lo + HEAD_DIM], _NT, preferred_element_type=F32)
            s = jnp.where(mask, s, NEG)
            sink = jnp.concatenate(
                [jnp.full((WINDOW, 1), sinks_ref[kh * GQA_GROUP + g], F32) for g in range(GQA_GROUP)], axis=0)
            m = jnp.maximum(jnp.max(s, axis=-1, keepdims=True), sink)
            p = jnp.exp(s - m)
            den = jnp.sum(p, axis=-1, keepdims=True) + jnp.exp(sink - m)
            o = jnp.dot(p.astype(BF16), vcat[:, lo:lo + HEAD_DIM], preferred_element_type=F32) / den
            for g in range(GQA_GROUP):
                outs[g * N_KV_HEADS + kh] = o[g * WINDOW:(g + 1) * WINDOW, :]
        o_ref[pl.ds(r0, WINDOW), :] = jnp.concatenate(outs, axis=1).astype(o_ref.dtype)
        return carry

    lax.fori_loop(0, tq // WINDOW, block, 0)


def _swa_prompt(sinks, q, k, v, *, batch, seq):
    tq = TOKEN_TILE
    nt = seq // tq
    per = tq // WINDOW
    cur = lambda b, j: (b * nt + j, 0)
    prev = lambda b, j: (b * nt * per + jnp.maximum(j * per - 1, 0), 0)
    return pl.pallas_call(
        _swa_prompt_kernel,
        grid=(batch, nt),
        in_specs=[
            pl.BlockSpec(memory_space=pltpu.SMEM),
            pl.BlockSpec((tq, CONV_CH), cur),
            pl.BlockSpec((tq, KV_DIM), cur),
            pl.BlockSpec((WINDOW, KV_DIM), prev),
            pl.BlockSpec((tq, KV_DIM), cur),
            pl.BlockSpec((WINDOW, KV_DIM), prev),
        ],
        out_specs=pl.BlockSpec((tq, CONV_CH), cur),
        out_shape=jax.ShapeDtypeStruct(q.shape, BF16),
        scratch_shapes=[pltpu.VMEM((tq + WINDOW, KV_DIM), BF16)] * 2,
        name="swa_prompt",
    )(sinks, q, k, k, v, v)


def _swa_sample_kernel(sinks_ref, q_ref, k_ref, v_ref, kb_ref, vb_ref, o_ref, ko_ref, vo_ref, *, dec):
    nseq = kb_ref.shape[0]
    grp = N_KV_HEADS * dec
    rows = GQA_GROUP * grp
    ri = lax.broadcasted_iota(jnp.int32, (rows, KV_DIM), 0)
    li = lax.broadcasted_iota(jnp.int32, (rows, KV_DIM), 1)
    head_mask = ((ri % grp) // dec) == (li // HEAD_DIM)
    tok = lax.broadcasted_iota(jnp.int32, (rows, 1), 0) % dec
    cache_mask = lax.broadcasted_iota(jnp.int32, (rows, WINDOW), 1) > (
        lax.broadcasted_iota(jnp.int32, (rows, WINDOW), 0) % dec)
    sink = jnp.concatenate(
        [jnp.full((dec, 1), sinks_ref[kv * GQA_GROUP + g], F32)
         for g in range(GQA_GROUP) for kv in range(N_KV_HEADS)], axis=0)
    for s in range(nseq):
        qs = q_ref[pl.ds(s * dec, dec), :]
        kn = k_ref[pl.ds(s * dec, dec), :]
        vn = v_ref[pl.ds(s * dec, dec), :]
        qbd = jnp.concatenate(
            [jnp.concatenate([qs[:, g * KV_DIM:(g + 1) * KV_DIM]] * N_KV_HEADS, axis=0)
             for g in range(GQA_GROUP)], axis=0)
        qbd = jnp.where(head_mask, qbd, 0.0)
        sc = lax.dot_general(qbd.astype(BF16), kb_ref[s].astype(BF16), _NT, preferred_element_type=F32)
        sc = jnp.where(cache_mask, sc, NEG)
        m = jnp.maximum(jnp.max(sc, axis=-1, keepdims=True), sink)
        sn = []
        for j in range(dec):
            sj = jnp.sum(qbd * kn[j:j + 1, :], axis=-1, keepdims=True)
            sj = jnp.where(tok >= j, sj, NEG)
            sn.append(sj)
            m = jnp.maximum(m, sj)
        pc = jnp.exp(sc - m)
        den = jnp.sum(pc, axis=-1, keepdims=True) + jnp.exp(sink - m)
        o = jnp.dot(pc.astype(BF16), vb_ref[s].astype(BF16), preferred_element_type=F32)
        for j in range(dec):
            pj = jnp.exp(sn[j] - m)
            den = den + pj
            o = o + pj * vn[j:j + 1, :]
        o = jnp.where(head_mask, o / den, 0.0)
        folded = []
        for g in range(GQA_GROUP):
            og = o[g * grp:g * grp + dec, :]
            for kv in range(1, N_KV_HEADS):
                og = og + o[g * grp + kv * dec:g * grp + (kv + 1) * dec, :]
            folded.append(og)
        o_ref[pl.ds(s * dec, dec), :] = jnp.concatenate(folded, axis=1)
        ko_ref[s, 0:WINDOW - dec, :] = kb_ref[s, dec:WINDOW, :]
        ko_ref[s, WINDOW - dec:WINDOW, :] = kn
        vo_ref[s, 0:WINDOW - dec, :] = vb_ref[s, dec:WINDOW, :]
        vo_ref[s, WINDOW - dec:WINDOW, :] = vn


def _swa_sample(sinks, q, k, v, kbuf, vbuf, *, dec):
    nseq = kbuf.shape[0]
    sb = SEQ_BLOCK
    row = lambda i: (i, 0)
    cache = lambda i: (i, 0, 0)
    return pl.pallas_call(
        functools.partial(_swa_sample_kernel, dec=dec),
        grid=(nseq // sb,),
        in_specs=[
            pl.BlockSpec(memory_space=pltpu.SMEM),
            pl.BlockSpec((sb * dec, CONV_CH), row),
            pl.BlockSpec((sb * dec, KV_DIM), row),
            pl.BlockSpec((sb * dec, KV_DIM), row),
            pl.BlockSpec((sb, WINDOW, KV_DIM), cache),
            pl.BlockSpec((sb, WINDOW, KV_DIM), cache),
        ],
        out_specs=[
            pl.BlockSpec((sb * dec, CONV_CH), row),
            pl.BlockSpec((sb, WINDOW, KV_DIM), cache),
            pl.BlockSpec((sb, WINDOW, KV_DIM), cache),
        ],
        out_shape=[
            jax.ShapeDtypeStruct(q.shape, F32),
            jax.ShapeDtypeStruct(kbuf.shape, F32),
            jax.ShapeDtypeStruct(vbuf.shape, F32),
        ],
        name="swa_sample",
    )(sinks, q, k, v, kbuf, vbuf)


def _xattn_prompt_kernel(q_ref, mk_ref, mv_ref, o_ref):
    q = q_ref[...]
    mk = mk_ref[0].astype(BF16)
    mv = mv_ref[0].astype(BF16)
    outs = []
    for h in range(N_X_HEADS):
        sl = slice(h * HEAD_DIM, (h + 1) * HEAD_DIM)
        s = lax.dot_general(q[:, sl], mk[:, sl], _NT, preferred_element_type=F32)
        p = jnp.exp(s - jnp.max(s, axis=-1, keepdims=True))
        den = jnp.sum(p, axis=-1, keepdims=True)
        outs.append(jnp.dot(p.astype(BF16), mv[:, sl], preferred_element_type=F32) / den)
    o_ref[...] = jnp.concatenate(outs, axis=1).astype(o_ref.dtype)


def _xattn_prompt(qx, mk, mv, *, batch, seq):
    tq = TOKEN_TILE
    nt = seq // tq
    n_mem = mk.shape[1]
    return pl.pallas_call(
        _xattn_prompt_kernel,
        grid=(batch, nt),
        in_specs=[
            pl.BlockSpec((tq, X_DIM), lambda b, j: (b * nt + j, 0)),
            pl.BlockSpec((1, n_mem, X_DIM), lambda b, j: (b, 0, 0)),
            pl.BlockSpec((1, n_mem, X_DIM), lambda b, j: (b, 0, 0)),
        ],
        out_specs=pl.BlockSpec((tq, X_DIM), lambda b, j: (b * nt + j, 0)),
        out_shape=jax.ShapeDtypeStruct(qx.shape, BF16),
        name="xattn_prompt",
    )(qx, mk, mv)


def _xattn_sample_kernel(q_ref, mk_ref, mv_ref, o_ref, *, dec):
    nseq = mk_ref.shape[0]
    rows = N_X_HEADS * dec
    head_mask = (lax.broadcasted_iota(jnp.int32, (rows, X_DIM), 0) // dec) == (
        lax.broadcasted_iota(jnp.int32, (rows, X_DIM), 1) // HEAD_DIM)
    for s in range(nseq):
        qs = q_ref[pl.ds(s * dec, dec), :]
        qbd = jnp.where(head_mask, jnp.concatenate([qs] * N_X_HEADS, axis=0), 0.0).astype(BF16)
        sc = lax.dot_general(qbd, mk_ref[s].astype(BF16), _NT, preferred_element_type=F32)
        p = jnp.exp(sc - jnp.max(sc, axis=-1, keepdims=True))
        den = jnp.sum(p, axis=-1, keepdims=True)
        o = jnp.dot(p.astype(BF16), mv_ref[s].astype(BF16), preferred_element_type=F32) / den
        o = jnp.where(head_mask, o, 0.0)
        acc = o[0:dec, :]
        for h in range(1, N_X_HEADS):
            acc = acc + o[h * dec:(h + 1) * dec, :]
        o_ref[pl.ds(s * dec, dec), :] = acc


def _xattn_sample(qx, mk, mv, *, dec):
    nseq, n_mem = mk.shape[0], mk.shape[1]
    sb = SEQ_BLOCK
    return pl.pallas_call(
        functools.partial(_xattn_sample_kernel, dec=dec),
        grid=(nseq // sb,),
        in_specs=[
            pl.BlockSpec((sb * dec, X_DIM), lambda i: (i, 0)),
            pl.BlockSpec((sb, n_mem, X_DIM), lambda i: (i, 0, 0)),
            pl.BlockSpec((sb, n_mem, X_DIM), lambda i: (i, 0, 0)),
        ],
        out_specs=pl.BlockSpec((sb * dec, X_DIM), lambda i: (i, 0)),
        out_shape=jax.ShapeDtypeStruct(qx.shape, F32),
        name="xattn_sample",
    )(qx, mk, mv)


def _out_mlp_kernel(x_ref, os_ref, ox_ref, wo_ref, gm_ref, wu_ref, wd_ref, y_ref):
    o = jnp.concatenate([os_ref[...].astype(BF16), ox_ref[...].astype(BF16)], axis=1)
    x1 = x_ref[...] + jnp.dot(o, wo_ref[...], preferred_element_type=F32)
    hm = _rms_rows(x1, gm_ref[...]).astype(BF16)
    acc = x1
    for c in range(D_FF // FF_CHUNK):
        a = jnp.dot(hm, wu_ref[:, c * FF_CHUNK:(c + 1) * FF_CHUNK], preferred_element_type=F32)
        a = jnp.square(jnp.maximum(a, 0.0)).astype(BF16)
        acc = acc + jnp.dot(a, wd_ref[c * FF_CHUNK:(c + 1) * FF_CHUNK, :], preferred_element_type=F32)
    y_ref[...] = acc


def _out_mlp(x2d, o_self, o_x, wo, gm, wu, wd):
    n = x2d.shape[0]
    tm = TOKEN_TILE
    row = lambda i: (i, 0)
    return pl.pallas_call(
        _out_mlp_kernel,
        grid=(n // tm,),
        in_specs=[
            pl.BlockSpec((tm, D_MODEL), row),
            pl.BlockSpec((tm, CONV_CH), row),
            pl.BlockSpec((tm, X_DIM), row),
            _const_spec((CONV_CH + X_DIM, D_MODEL)),
            _const_spec((1, D_MODEL)),
            _const_spec((D_MODEL, D_FF)),
            _const_spec((D_FF, D_MODEL)),
        ],
        out_specs=pl.BlockSpec((tm, D_MODEL), row),
        out_shape=jax.ShapeDtypeStruct((n, D_MODEL), F32),
        name="out_mlp",
    )(x2d, o_self, o_x, wo, gm, wu, wd)


def _rope_tables(pos):
    half = HEAD_DIM // 2
    inv = ROPE_THETA ** (-jnp.arange(half, dtype=F32) * 2.0 / HEAD_DIM)
    ang = pos[:, None] * inv[None, :]
    cos, sin = jnp.cos(ang), jnp.sin(ang)
    reps = LANES // half
    return jnp.tile(cos, (1, reps)), jnp.concatenate([-sin, sin] * (reps // 2), axis=1)


def _group_major_cols(w):
    lead = w.shape[0]
    return (w.reshape(lead, N_KV_HEADS, GQA_GROUP, HEAD_DIM)
             .transpose(0, 2, 1, 3).reshape(lead, CONV_CH))


def _group_major_rows(w):
    cols = w.shape[1]
    return (w.reshape(N_KV_HEADS, GQA_GROUP, HEAD_DIM, cols)
             .transpose(1, 0, 2, 3).reshape(CONV_CH, cols))


def _tile_heads(g, n_heads):
    return jnp.tile(g, n_heads)[None, :]


def kernel(x_prompt, x_sample, mem_prompt, cache_swa_k, cache_swa_v, state_conv, cache_mem_k, cache_mem_v,
           norm_mix, w_in_att, q_norm_att, k_norm_att, sinks, w_in_conv, conv_w, norm_mem, w_mem_kv,
           q_norm_x, k_norm_x, w_out, norm_mlp, w_up, w_down):
    batch, seq, _ = x_prompt.shape
    nseq, dec, _ = x_sample.shape
    n_mem = mem_prompt.shape[1]
    past_len = PAST_LEN
    assert seq % TOKEN_TILE == 0 and (nseq * dec) == TOKEN_TILE and nseq % SEQ_BLOCK == 0
    n_att = w_in_att.shape[0]
    n_conv = w_in_conv.shape[0]

    w_att = [jnp.concatenate([_group_major_cols(w_in_att[j][:, :CONV_CH]), w_in_att[j][:, CONV_CH:]],
                             axis=1).astype(BF16) for j in range(n_att)]
    w_conv = [w_in_conv[j].astype(BF16) for j in range(n_conv)]
    w_o = []
    for i in range(DEPTH):
        if i % 2 == 0:
            w_o.append(jnp.concatenate([_group_major_rows(w_out[i][:CONV_CH]), w_out[i][CONV_CH:]],
                                       axis=0).astype(BF16))
        else:
            w_o.append(w_out[i].astype(BF16))
    w_u = [w_up[i].astype(BF16) for i in range(DEPTH)]
    w_d = [w_down[i].astype(BF16) for i in range(DEPTH)]
    w_mkv = w_mem_kv.astype(BF16)

    gqk = [jnp.concatenate([_tile_heads(q_norm_att[j], N_Q_HEADS), _tile_heads(k_norm_att[j], N_KV_HEADS)],
                           axis=1) for j in range(n_att)]
    gx = [_tile_heads(q_norm_x[i], N_X_HEADS) for i in range(DEPTH)]
    gkx = jnp.tile(k_norm_x, (1, N_X_HEADS))[:, None, :]

    cos_p, sin_p = _rope_tables(jnp.arange(seq, dtype=F32))
    cos_s, sin_s = _rope_tables(past_len + (jnp.arange(nseq * dec) % dec).astype(F32))

    mk_p, mv_p = _mem_kv(mem_prompt.reshape(batch * n_mem, D_MODEL), norm_mem[:, None, :], w_mkv, gkx)
    mk_p = mk_p.reshape(DEPTH, batch, n_mem, X_DIM)
    mv_p = mv_p.reshape(DEPTH, batch, n_mem, X_DIM)
    mk_s = cache_mem_k.reshape(DEPTH, nseq, n_mem, X_DIM)
    mv_s = cache_mem_v.reshape(DEPTH, nseq, n_mem, X_DIM)

    xp = x_prompt.reshape(batch * seq, D_MODEL)
    xs = x_sample.reshape(nseq * dec, D_MODEL)
    tiles_per_seq = seq // TOKEN_TILE
    k_p, v_p, c_p, k_s, v_s, c_s = [], [], [], [], [], []
    for i in range(DEPTH):
        j = i // 2
        g_mix = norm_mix[i][None, :]
        if i % 2 == 0:
            q, k, v, qx, k32, v32 = _in_att(xp, g_mix, w_att[j], gqk[j], gx[i], cos_p, sin_p,
                                             tiles_per_seq=tiles_per_seq, keep_rows=WINDOW, act_dtype=BF16)
            k_p.append(k32.reshape(batch, WINDOW, N_KV_HEADS, HEAD_DIM))
            v_p.append(v32.reshape(batch, WINDOW, N_KV_HEADS, HEAD_DIM))
            o_self = _swa_prompt(sinks[j], q, k, v, batch=batch, seq=seq)
        else:
            o_self, qx, tail = _in_conv(xp, g_mix, w_conv[j], gx[i], conv_w[j], None,
                                        tiles_per_seq=tiles_per_seq, period=0,
                                        keep_rows=SUBLANES, act_dtype=BF16)
            c_p.append(tail.reshape(batch, SUBLANES, CONV_CH)[:, SUBLANES - (CONV_W - 1):])
        o_x = _xattn_prompt(qx, mk_p[i], mv_p[i], batch=batch, seq=seq)
        xp = _out_mlp(xp, o_self, o_x, w_o[i], norm_mlp[i][None, :], w_u[i], w_d[i])

        if i % 2 == 0:
            q, k, v, qx, _, _ = _in_att(xs, g_mix, w_att[j], gqk[j], gx[i], cos_s, sin_s,
                                        tiles_per_seq=1, keep_rows=SUBLANES, act_dtype=F32)
            o_self, ko, vo = _swa_sample(sinks[j], q, k, v,
                                         cache_swa_k[j].reshape(nseq, WINDOW, KV_DIM),
                                         cache_swa_v[j].reshape(nseq, WINDOW, KV_DIM), dec=dec)
            k_s.append(ko.reshape(nseq, WINDOW, N_KV_HEADS, HEAD_DIM))
            v_s.append(vo.reshape(nseq, WINDOW, N_KV_HEADS, HEAD_DIM))
        else:
            prefix = jnp.pad(state_conv[j], ((0, 0), (0, dec - (CONV_W - 1)), (0, 0)))
            o_self, qx, tail = _in_conv(xs, g_mix, w_conv[j], gx[i], conv_w[j],
                                        prefix.reshape(nseq * dec, CONV_CH),
                                        tiles_per_seq=1, period=dec,
                                        keep_rows=nseq * dec, act_dtype=F32)
            c_s.append(tail.reshape(nseq, dec, CONV_CH)[:, dec - (CONV_W - 1):])
        o_x = _xattn_sample(qx, mk_s[i], mv_s[i], dec=dec)
        xs = _out_mlp(xs, o_self, o_x, w_o[i], norm_mlp[i][None, :], w_u[i], w_d[i])

    return (xp.reshape(batch, seq, D_MODEL),
            xs.reshape(nseq, dec, D_MODEL),
            jnp.stack(k_p), jnp.stack(v_p), jnp.stack(c_p),
            mk_p.reshape(DEPTH, batch, n_mem, N_X_HEADS, HEAD_DIM),
            mv_p.reshape(DEPTH, batch, n_mem, N_X_HEADS, HEAD_DIM),
            jnp.stack(k_s), jnp.stack(v_s), jnp.stack(c_s))
```

```python
import functools

import jax
import jax.numpy as jnp
from jax import lax
from jax.experimental import pallas as pl
from jax.experimental.pallas import tpu as pltpu

F32 = jnp.float32
BF16 = jnp.bfloat16

D_MODEL = 1024
DEPTH = 4
HEAD_DIM = 64
N_Q_HEADS = 12
N_KV_HEADS = 4
GQA_GROUP = N_Q_HEADS // N_KV_HEADS
WINDOW = 128
PAST_LEN = 8192
ROPE_THETA = 10000.0
CONV_CH = N_Q_HEADS * HEAD_DIM
CONV_W = 3
N_X_HEADS = 4
X_DIM = N_X_HEADS * HEAD_DIM
KV_DIM = N_KV_HEADS * HEAD_DIM
D_FF = 4 * D_MODEL
CONV_IN = 3 * CONV_CH + X_DIM
EPS = 1e-6
NEG = -1e30
SCALE = HEAD_DIM ** -0.5

LANES = 128
SUBLANES = 8
MXU_DIM = 256
TOKEN_TILE = 512
FF_CHUNK = 2048
SEQ_BLOCK = 8
PAIR = 2 * HEAD_DIM

_NT = (((1,), (1,)), ((), ()))


def _const_spec(shape):
    nd = len(shape)
    return pl.BlockSpec(shape, lambda *_: (0,) * nd, pipeline_mode=pl.Buffered(1))


def _layer_spec(shape, layer):
    nd = len(shape)
    return pl.BlockSpec((1,) + shape, lambda *_: (layer,) + (0,) * nd, pipeline_mode=pl.Buffered(1))


def _rms_rows(x, g):
    return x * lax.rsqrt(jnp.mean(x * x, axis=-1, keepdims=True) + EPS) * g


def _head_blockdiag():
    r = lax.broadcasted_iota(jnp.int32, (MXU_DIM, MXU_DIM), 0) // HEAD_DIM
    c = lax.broadcasted_iota(jnp.int32, (MXU_DIM, MXU_DIM), 1) // HEAD_DIM
    return jnp.where(r == c, 1.0, 0.0).astype(BF16)


def _head_mean_sq(z, bd):
    sq = z * z
    hi = sq.astype(BF16)
    lo = (sq - hi.astype(F32)).astype(BF16)
    parts = []
    for c in range(z.shape[1] // MXU_DIM):
        sl = slice(c * MXU_DIM, (c + 1) * MXU_DIM)
        parts.append(jnp.dot(hi[:, sl], bd, preferred_element_type=F32)
                     + jnp.dot(lo[:, sl], bd, preferred_element_type=F32))
    ms = parts[0] if len(parts) == 1 else jnp.concatenate(parts, axis=1)
    return ms * (1.0 / HEAD_DIM)


def _rope_rows(x, cos, sin_signed):
    t = x.shape[0]
    first_half = (lax.broadcasted_iota(jnp.int32, (t, LANES), 1) & (HEAD_DIM - 1)) < HEAD_DIM // 2
    parts = []
    for c in range(x.shape[1] // LANES):
        xc = x[:, c * LANES:(c + 1) * LANES]
        partner = jnp.where(first_half,
                            pltpu.roll(xc, LANES - HEAD_DIM // 2, axis=1),
                            pltpu.roll(xc, HEAD_DIM // 2, axis=1))
        parts.append(xc * cos + partner * sin_signed)
    return jnp.concatenate(parts, axis=1)


def _lane_tile(col, n):
    return col if n == LANES else jnp.concatenate([col] * (n // LANES), axis=1)


def _head_norm_cols(xt, g_col):
    t = xt.shape[1]
    g = _lane_tile(g_col, t)
    parts = []
    for h in range(xt.shape[0] // HEAD_DIM):
        blk = xt[h * HEAD_DIM:(h + 1) * HEAD_DIM, :]
        ms = jnp.mean(blk * blk, axis=0, keepdims=True)
        parts.append(blk * lax.rsqrt(ms + EPS) * g)
    return parts


def _softmax_rows(s, sink=None):
    m = jnp.max(s, axis=-1, keepdims=True)
    if sink is not None:
        m = jnp.maximum(m, sink)
    p = jnp.exp(s - m)
    den = jnp.sum(p, axis=-1, keepdims=True)
    if sink is not None:
        den = den + jnp.exp(sink - m)
    return p.astype(BF16), 1.0 / den


def _mem_kv_kernel(mem_ref, g_ref, wt_ref, gk_ref, mk_ref, mv_ref):
    h = _rms_rows(mem_ref[...], g_ref[0]).astype(BF16)
    kvt = lax.dot_general(wt_ref[0], h, _NT, preferred_element_type=F32)
    mk_ref[0, 0] = jnp.concatenate(_head_norm_cols(kvt[:X_DIM], gk_ref[0]), axis=0)
    mv_ref[0, 0] = kvt[X_DIM:]


def _mem_kv(mem2d, norm_mem, w_mem_kv_t, gk_cols, *, batch, n_mem):
    out = jax.ShapeDtypeStruct((DEPTH, batch, X_DIM, n_mem), F32)
    return pl.pallas_call(
        _mem_kv_kernel,
        grid=(DEPTH, batch),
        in_specs=[
            pl.BlockSpec((n_mem, D_MODEL), lambda i, b: (b, 0)),
            pl.BlockSpec((1, 1, D_MODEL), lambda i, b: (i, 0, 0)),
            pl.BlockSpec((1, 2 * X_DIM, D_MODEL), lambda i, b: (i, 0, 0)),
            pl.BlockSpec((1, HEAD_DIM, LANES), lambda i, b: (i, 0, 0)),
        ],
        out_specs=[pl.BlockSpec((1, 1, X_DIM, n_mem), lambda i, b: (i, b, 0, 0))] * 2,
        out_shape=[out, out],
        name="mem_kv",
    )(mem2d, norm_mem, w_mem_kv_t, gk_cols)


def _in_att_kernel(*refs, keep_cols):
    (x_ref, g_ref, wq_ref, wkv_ref, gq_ref, gx_ref, gk_ref, cos_ref, sin_ref, cost_ref, sint_ref,
     q_ref, qx_ref, kt_ref, vt_ref) = refs[:15]
    t = x_ref.shape[0]
    h = _rms_rows(x_ref[...], g_ref[0]).astype(BF16)
    bd = _head_blockdiag()
    z = jnp.dot(h, wq_ref[0], preferred_element_type=F32)
    q = z[:, :CONV_CH]
    q = q * lax.rsqrt(_head_mean_sq(q, bd) + EPS) * gq_ref[0]
    q_ref[...] = (_rope_rows(q, cos_ref[...], sin_ref[...]) * SCALE).astype(q_ref.dtype)
    qx = z[:, CONV_CH:]
    qx = qx * lax.rsqrt(_head_mean_sq(qx, bd) + EPS) * gx_ref[0]
    qx_ref[...] = (qx * SCALE).astype(qx_ref.dtype)

    kvt = lax.dot_general(wkv_ref[0], h, _NT, preferred_element_type=F32)
    cos_t, sin_t = cost_ref[...], sint_ref[...]
    half = HEAD_DIM // 2
    k_parts = []
    for blk in _head_norm_cols(kvt[:KV_DIM], gk_ref[0]):
        x1, x2 = blk[:half], blk[half:]
        k_parts += [x1 * cos_t - x2 * sin_t, x2 * cos_t + x1 * sin_t]
    kt = jnp.concatenate(k_parts, axis=0)
    vt = kvt[KV_DIM:]
    kt_ref[...] = kt.astype(kt_ref.dtype)
    vt_ref[...] = vt.astype(vt_ref.dtype)
    if keep_cols:
        kt32_ref, vt32_ref = refs[15:]
        kt32_ref[0] = kt[:, t - keep_cols:]
        vt32_ref[0] = vt[:, t - keep_cols:]


def _in_att(x2d, layer, j, g_mix, wq, wkv_t, gq, gx, gk_cols, cos, sin, cos_t, sin_t,
            *, tiles_per_seq, keep_cols, act_dtype):
    n = x2d.shape[0]
    tm = TOKEN_TILE
    nt = n // tm
    nseq = nt // tiles_per_seq
    row = lambda i: (i, 0)
    col = lambda i: (0, i)
    pos = lambda i: (i % tiles_per_seq, 0)
    pos_t = lambda i: (0, i % tiles_per_seq)
    out_specs = [
        pl.BlockSpec((tm, CONV_CH), row),
        pl.BlockSpec((tm, X_DIM), row),
        pl.BlockSpec((KV_DIM, tm), col),
        pl.BlockSpec((KV_DIM, tm), col),
    ]
    out_shape = [
        jax.ShapeDtypeStruct((n, CONV_CH), act_dtype),
        jax.ShapeDtypeStruct((n, X_DIM), act_dtype),
        jax.ShapeDtypeStruct((KV_DIM, n), act_dtype),
        jax.ShapeDtypeStruct((KV_DIM, n), act_dtype),
    ]
    if keep_cols:
        out_specs += [pl.BlockSpec((1, KV_DIM, keep_cols), lambda i: (i // tiles_per_seq, 0, 0))] * 2
        out_shape += [jax.ShapeDtypeStruct((nseq, KV_DIM, keep_cols), F32)] * 2
    return pl.pallas_call(
        functools.partial(_in_att_kernel, keep_cols=keep_cols),
        grid=(nt,),
        in_specs=[
            pl.BlockSpec((tm, D_MODEL), row),
            _layer_spec((1, D_MODEL), layer),
            _layer_spec((D_MODEL, CONV_CH + X_DIM), j),
            _layer_spec((2 * KV_DIM, D_MODEL), j),
            _layer_spec((1, CONV_CH), j),
            _layer_spec((1, X_DIM), layer),
            _layer_spec((HEAD_DIM, LANES), j),
            pl.BlockSpec((tm, LANES), pos),
            pl.BlockSpec((tm, LANES), pos),
            pl.BlockSpec((HEAD_DIM // 2, tm), pos_t),
            pl.BlockSpec((HEAD_DIM // 2, tm), pos_t),
        ],
        out_specs=out_specs,
        out_shape=out_shape,
        name="in_att",
    )(x2d, g_mix, wq, wkv_t, gq, gx, gk_cols, cos, sin, cos_t, sin_t)


def _in_conv_kernel(*refs, tiles_per_seq, period):
    if period:
        (x_ref, g_ref, w_ref, gx_ref, cw_ref, prefix_ref,
         o_ref, qx_ref, gt_ref, gbuf, pbuf) = refs
    else:
        (x_ref, g_ref, w_ref, gx_ref, cw_ref,
         o_ref, qx_ref, gt_ref, gbuf) = refs
    t = x_ref.shape[0]
    keep = gt_ref.shape[0]
    h = _rms_rows(x_ref[...], g_ref[0]).astype(BF16)
    z = jnp.dot(h, w_ref[0], preferred_element_type=F32)
    gate_b = z[:, :CONV_CH]
    u = z[:, CONV_CH:2 * CONV_CH] * z[:, 2 * CONV_CH:3 * CONV_CH]

    @pl.when(pl.program_id(0) % tiles_per_seq == 0)
    def _():
        gbuf[0:SUBLANES, :] = jnp.zeros((SUBLANES, CONV_CH), F32)

    gbuf[SUBLANES:SUBLANES + t, :] = u
    back1 = gbuf[SUBLANES - 1:SUBLANES - 1 + t, :]
    back2 = gbuf[SUBLANES - 2:SUBLANES - 2 + t, :]
    if period:
        pbuf[0:t, :] = prefix_ref[...]
        pbuf[t:t + SUBLANES, :] = jnp.zeros((SUBLANES, CONV_CH), F32)
        tok = lax.broadcasted_iota(jnp.int32, (t, CONV_CH), 0) % period
        back1 = jnp.where(tok >= 1, back1, pbuf[1:1 + t, :])
        back2 = jnp.where(tok >= 2, back2, pbuf[0:t, :])
    cw = cw_ref[0]
    y = back2 * cw[0:1, :] + back1 * cw[1:2, :] + u * cw[2:3, :]
    gbuf[0:SUBLANES, :] = gbuf[t:t + SUBLANES, :]
    o_ref[...] = (gate_b * y).astype(o_ref.dtype)
    gt_ref[...] = u[t - keep:, :]
    qx = z[:, 3 * CONV_CH:]
    qx = qx * lax.rsqrt(_head_mean_sq(qx, _head_blockdiag()) + EPS) * gx_ref[0]
    qx_ref[...] = (qx * SCALE).astype(qx_ref.dtype)


def _in_conv(x2d, layer, j, g_mix, w, gx, cw, prefix, *, tiles_per_seq, period, keep_rows, act_dtype):
    n = x2d.shape[0]
    tm = TOKEN_TILE
    nt = n // tm
    nseq = nt // tiles_per_seq
    row = lambda i: (i, 0)
    in_specs = [
        pl.BlockSpec((tm, D_MODEL), row),
        _layer_spec((1, D_MODEL), layer),
        _layer_spec((D_MODEL, CONV_IN), j),
        _layer_spec((1, X_DIM), layer),
        _layer_spec((CONV_W, CONV_CH), j),
    ]
    args = [x2d, g_mix, w, gx, cw]
    scratch = [pltpu.VMEM((tm + 2 * SUBLANES, CONV_CH), F32)]
    if period:
        in_specs.append(pl.BlockSpec((tm, CONV_CH), row))
        args.append(prefix)
        scratch.append(pltpu.VMEM((tm + SUBLANES, CONV_CH), F32))
    return pl.pallas_call(
        functools.partial(_in_conv_kernel, tiles_per_seq=tiles_per_seq, period=period),
        grid=(nt,),
        in_specs=in_specs,
        out_specs=[
            pl.BlockSpec((tm, CONV_CH), row),
            pl.BlockSpec((tm, X_DIM), row),
            pl.BlockSpec((keep_rows, CONV_CH), lambda i: (i // tiles_per_seq, 0)),
        ],
        out_shape=[
            jax.ShapeDtypeStruct((n, CONV_CH), act_dtype),
            jax.ShapeDtypeStruct((n, X_DIM), act_dtype),
            jax.ShapeDtypeStruct((nseq * keep_rows, CONV_CH), F32),
        ],
        scratch_shapes=scratch,
        name="in_conv",
    )(*args)


def _pair_lhs(slabs):
    low = lax.broadcasted_iota(jnp.int32, slabs[0].shape, 1) < HEAD_DIM
    zero = jnp.zeros_like(slabs[0])
    parts = []
    for s in slabs:
        parts += [jnp.where(low, s, zero), jnp.where(low, zero, s)]
    return jnp.concatenate(parts, axis=0)


def _swa_prompt_kernel(sinks_ref, q_ref, kc_ref, kp_ref, vc_ref, vp_ref, o_ref, kfull, vfull):
    tq = q_ref.shape[0]
    first_tile = pl.program_id(1) == 0
    kfull[:, 0:WINDOW] = kp_ref[...]
    kfull[:, WINDOW:] = kc_ref[...]
    vfull[:, 0:WINDOW] = vp_ref[...]
    vfull[:, WINDOW:] = vc_ref[...]
    r = lax.broadcasted_iota(jnp.int32, (WINDOW, 2 * WINDOW), 0)
    c = lax.broadcasted_iota(jnp.int32, (WINDOW, 2 * WINDOW), 1)
    band = (c > r) & (c - WINDOW <= r)
    band_first = band & (c >= jnp.where(first_tile, WINDOW, 0))
    low = lax.broadcasted_iota(jnp.int32, (WINDOW, PAIR), 1) < HEAD_DIM
    for qb in range(tq // WINDOW):
        r0 = qb * WINDOW
        mask = band_first if qb == 0 else band
        slabs = [None] * (GQA_GROUP * KV_DIM // PAIR)
        for pr in range(KV_DIM // PAIR):
            kslab = kfull[pr * PAIR:(pr + 1) * PAIR, r0:r0 + 2 * WINDOW]
            vslab = vfull[pr * PAIR:(pr + 1) * PAIR, r0:r0 + 2 * WINDOW]
            lhs = _pair_lhs([q_ref[r0:r0 + WINDOW, g * KV_DIM + pr * PAIR:g * KV_DIM + (pr + 1) * PAIR]
                             for g in range(GQA_GROUP)])
            s_all = jnp.dot(lhs, kslab, preferred_element_type=F32)
            probs, inv = [], []
            for g in range(GQA_GROUP):
                for e in range(2):
                    idx = g * 2 + e
                    s = jnp.where(mask, s_all[idx * WINDOW:(idx + 1) * WINDOW], NEG)
                    p, inv_den = _softmax_rows(s, sinks_ref[(2 * pr + e) * GQA_GROUP + g])
                    probs.append(p)
                    inv.append(inv_den)
            res = lax.dot_general(jnp.concatenate(probs, axis=0), vslab, _NT,
                                  preferred_element_type=F32)
            for g in range(GQA_GROUP):
                lo = res[(2 * g) * WINDOW:(2 * g + 1) * WINDOW] * inv[2 * g]
                hi = res[(2 * g + 1) * WINDOW:(2 * g + 2) * WINDOW] * inv[2 * g + 1]
                slabs[g * (KV_DIM // PAIR) + pr] = jnp.where(low, lo, hi)
        o_ref[r0:r0 + WINDOW, :] = jnp.concatenate(slabs, axis=1).astype(o_ref.dtype)


def _swa_prompt(sinks, q, kt, vt, *, batch, seq):
    tq = TOKEN_TILE
    nt = seq // tq
    per = tq // WINDOW
    rows = lambda b, j: (b * nt + j, 0)
    cur = lambda b, j: (0, b * nt + j)
    prev = lambda b, j: (0, b * nt * per + jnp.maximum(j * per - 1, 0))
    return pl.pallas_call(
        _swa_prompt_kernel,
        grid=(batch, nt),
        in_specs=[
            pl.BlockSpec(memory_space=pltpu.SMEM),
            pl.BlockSpec((tq, CONV_CH), rows),
            pl.BlockSpec((KV_DIM, tq), cur),
            pl.BlockSpec((KV_DIM, WINDOW), prev),
            pl.BlockSpec((KV_DIM, tq), cur),
            pl.BlockSpec((KV_DIM, WINDOW), prev),
        ],
        out_specs=pl.BlockSpec((tq, CONV_CH), rows),
        out_shape=jax.ShapeDtypeStruct(q.shape, BF16),
        scratch_shapes=[pltpu.VMEM((KV_DIM, tq + WINDOW), BF16)] * 2,
        name="swa_prompt",
    )(sinks, q, kt, kt, vt, vt)


def _swa_sample_kernel(sinks_ref, q_ref, kn_ref, vn_ref, kb_ref, vb_ref, o_ref, ko_ref, vo_ref, *, dec):
    nseq = kb_ref.shape[1]
    steps_per_tile = LANES // (nseq * dec)
    base = (pl.program_id(0) % steps_per_tile) * (nseq * dec)
    grp = N_KV_HEADS * dec
    rows = GQA_GROUP * grp
    ri = lax.broadcasted_iota(jnp.int32, (rows, KV_DIM), 0)
    li = lax.broadcasted_iota(jnp.int32, (rows, KV_DIM), 1)
    head_mask = ((ri % grp) // dec) == (li // HEAD_DIM)
    tok = lax.broadcasted_iota(jnp.int32, (rows, WINDOW + LANES), 0) % dec
    col = lax.broadcasted_iota(jnp.int32, (rows, WINDOW + LANES), 1)
    cache_mask = (col < WINDOW) & (col > tok)
    lane = lax.broadcasted_iota(jnp.int32, (KV_DIM, WINDOW), 1)
    sink = jnp.concatenate(
        [jnp.full((dec, 1), sinks_ref[kv * GQA_GROUP + g], F32)
         for g in range(GQA_GROUP) for kv in range(N_KV_HEADS)], axis=0)
    kn = kn_ref[...]
    vn = vn_ref[...]
    kn16 = kn.astype(BF16)
    vn16 = vn.astype(BF16)
    for s in range(nseq):
        off = base + s * dec
        new = col - (WINDOW + off)
        mask = cache_mask | ((new >= 0) & (new <= tok))
        qs = q_ref[pl.ds(s * dec, dec), :]
        qbd = jnp.concatenate(
            [jnp.concatenate([qs[:, g * KV_DIM:(g + 1) * KV_DIM]] * N_KV_HEADS, axis=0)
             for g in range(GQA_GROUP)], axis=0)
        qbd = jnp.where(head_mask, qbd, 0.0).astype(BF16)
        kb = kb_ref[0, s]
        vb = vb_ref[0, s]
        keys = jnp.concatenate([kb.astype(BF16), kn16], axis=1)
        vals = jnp.concatenate([vb.astype(BF16), vn16], axis=1)
        sc = jnp.where(mask, jnp.dot(qbd, keys, preferred_element_type=F32), NEG)
        p, inv_den = _softmax_rows(sc, sink)
        o = lax.dot_general(p, vals, _NT, preferred_element_type=F32) * inv_den
        o = jnp.where(head_mask, o, 0.0)
        folded = []
        for g in range(GQA_GROUP):
            og = o[g * grp:g * grp + dec, :]
            for kv in range(1, N_KV_HEADS):
                og = og + o[g * grp + kv * dec:g * grp + (kv + 1) * dec, :]
            folded.append(og)
        o_ref[pl.ds(s * dec, dec), :] = jnp.concatenate(folded, axis=1)
        shift_new = (WINDOW - dec + LANES - off) % LANES
        ko_ref[s] = jnp.where(lane < WINDOW - dec,
                              pltpu.roll(kb, WINDOW - dec, axis=1), pltpu.roll(kn, shift_new, axis=1))
        vo_ref[s] = jnp.where(lane < WINDOW - dec,
                              pltpu.roll(vb, WINDOW - dec, axis=1), pltpu.roll(vn, shift_new, axis=1))


def _swa_sample(sinks, q, kt_new, vt_new, kbuf, vbuf, j, *, dec):
    nseq = kbuf.shape[1]
    sb = SEQ_BLOCK
    steps_per_tile = LANES // (sb * dec)
    row = lambda i: (i, 0)
    tile = lambda i: (0, i // steps_per_tile)
    cache = lambda i: (i, 0, 0)
    cache_in = lambda i: (j, i, 0, 0)
    return pl.pallas_call(
        functools.partial(_swa_sample_kernel, dec=dec),
        grid=(nseq // sb,),
        in_specs=[
            pl.BlockSpec(memory_space=pltpu.SMEM),
            pl.BlockSpec((sb * dec, CONV_CH), row),
            pl.BlockSpec((KV_DIM, LANES), tile),
            pl.BlockSpec((KV_DIM, LANES), tile),
            pl.BlockSpec((1, sb, KV_DIM, WINDOW), cache_in),
            pl.BlockSpec((1, sb, KV_DIM, WINDOW), cache_in),
        ],
        out_specs=[
            pl.BlockSpec((sb * dec, CONV_CH), row),
            pl.BlockSpec((sb, KV_DIM, WINDOW), cache),
            pl.BlockSpec((sb, KV_DIM, WINDOW), cache),
        ],
        out_shape=[
            jax.ShapeDtypeStruct(q.shape, F32),
            jax.ShapeDtypeStruct(kbuf.shape[1:], F32),
            jax.ShapeDtypeStruct(vbuf.shape[1:], F32),
        ],
        name="swa_sample",
    )(sinks, q, kt_new, vt_new, kbuf, vbuf)


def _xattn_prompt_kernel(q_ref, mk_ref, mv_ref, o_ref):
    tq = q_ref.shape[0]
    mk = mk_ref[0, 0].astype(BF16)
    mv = mv_ref[0, 0].astype(BF16)
    low = lax.broadcasted_iota(jnp.int32, (WINDOW, PAIR), 1) < HEAD_DIM
    for rb in range(tq // WINDOW):
        r0 = rb * WINDOW
        slabs = []
        for pr in range(X_DIM // PAIR):
            lhs = _pair_lhs([q_ref[r0:r0 + WINDOW, pr * PAIR:(pr + 1) * PAIR]])
            s_all = jnp.dot(lhs, mk[pr * PAIR:(pr + 1) * PAIR], preferred_element_type=F32)
            p0, inv0 = _softmax_rows(s_all[:WINDOW])
            p1, inv1 = _softmax_rows(s_all[WINDOW:])
            res = lax.dot_general(jnp.concatenate([p0, p1], axis=0), mv[pr * PAIR:(pr + 1) * PAIR], _NT,
                                  preferred_element_type=F32)
            slabs.append(jnp.where(low, res[:WINDOW] * inv0, res[WINDOW:] * inv1))
        o_ref[r0:r0 + WINDOW, :] = jnp.concatenate(slabs, axis=1).astype(o_ref.dtype)


def _xattn_prompt(qx, mk, mv, layer, *, batch, seq):
    tq = TOKEN_TILE
    nt = seq // tq
    n_mem = mk.shape[3]
    mem = lambda b, j: (layer, b, 0, 0)
    return pl.pallas_call(
        _xattn_prompt_kernel,
        grid=(batch, nt),
        in_specs=[
            pl.BlockSpec((tq, X_DIM), lambda b, j: (b * nt + j, 0)),
            pl.BlockSpec((1, 1, X_DIM, n_mem), mem),
            pl.BlockSpec((1, 1, X_DIM, n_mem), mem),
        ],
        out_specs=pl.BlockSpec((tq, X_DIM), lambda b, j: (b * nt + j, 0)),
        out_shape=jax.ShapeDtypeStruct(qx.shape, BF16),
        name="xattn_prompt",
    )(qx, mk, mv)


def _xattn_sample_kernel(q_ref, mk_ref, mv_ref, o_ref, *, dec):
    nseq = mk_ref.shape[1]
    rows = N_X_HEADS * dec
    head_mask = (lax.broadcasted_iota(jnp.int32, (rows, X_DIM), 0) // dec) == (
        lax.broadcasted_iota(jnp.int32, (rows, X_DIM), 1) // HEAD_DIM)
    for s in range(nseq):
        qs = q_ref[pl.ds(s * dec, dec), :]
        qbd = jnp.where(head_mask, jnp.concatenate([qs] * N_X_HEADS, axis=0), 0.0).astype(BF16)
        sc = jnp.dot(qbd, mk_ref[0, s].astype(BF16), preferred_element_type=F32)
        p, inv_den = _softmax_rows(sc)
        o = lax.dot_general(p, mv_ref[0, s].astype(BF16), _NT, preferred_element_type=F32) * inv_den
        o = jnp.where(head_mask, o, 0.0)
        acc = o[0:dec, :]
        for h in range(1, N_X_HEADS):
            acc = acc + o[h * dec:(h + 1) * dec, :]
        o_ref[pl.ds(s * dec, dec), :] = acc


def _xattn_sample(qx, mk, mv, layer, *, dec):
    nseq, n_mem = mk.shape[1], mk.shape[3]
    sb = SEQ_BLOCK
    mem = lambda i: (layer, i, 0, 0)
    return pl.pallas_call(
        functools.partial(_xattn_sample_kernel, dec=dec),
        grid=(nseq // sb,),
        in_specs=[
            pl.BlockSpec((sb * dec, X_DIM), lambda i: (i, 0)),
            pl.BlockSpec((1, sb, X_DIM, n_mem), mem),
            pl.BlockSpec((1, sb, X_DIM, n_mem), mem),
        ],
        out_specs=pl.BlockSpec((sb * dec, X_DIM), lambda i: (i, 0)),
        out_shape=jax.ShapeDtypeStruct(qx.shape, F32),
        name="xattn_sample",
    )(qx, mk, mv)


def _out_mlp_kernel(x_ref, os_ref, ox_ref, wo_ref, gm_ref, wu_ref, wd_ref, y_ref):
    o = jnp.concatenate([os_ref[...].astype(BF16), ox_ref[...].astype(BF16)], axis=1)
    x1 = x_ref[...] + jnp.dot(o, wo_ref[0], preferred_element_type=F32)
    hm = _rms_rows(x1, gm_ref[0]).astype(BF16)
    acc = x1
    for c in range(D_FF // FF_CHUNK):
        a = jnp.dot(hm, wu_ref[0, :, c * FF_CHUNK:(c + 1) * FF_CHUNK], preferred_element_type=F32)
        a = jnp.square(jnp.maximum(a, 0.0)).astype(BF16)
        acc = acc + jnp.dot(a, wd_ref[0, c * FF_CHUNK:(c + 1) * FF_CHUNK, :], preferred_element_type=F32)
    y_ref[...] = acc


def _out_mlp(x2d, o_self, o_x, layer, wo, gm, wu, wd):
    n = x2d.shape[0]
    tm = TOKEN_TILE
    row = lambda i: (i, 0)
    return pl.pallas_call(
        _out_mlp_kernel,
        grid=(n // tm,),
        in_specs=[
            pl.BlockSpec((tm, D_MODEL), row),
            pl.BlockSpec((tm, CONV_CH), row),
            pl.BlockSpec((tm, X_DIM), row),
            _layer_spec((CONV_CH + X_DIM, D_MODEL), layer),
            _layer_spec((1, D_MODEL), layer),
            _layer_spec((D_MODEL, D_FF), layer),
            _layer_spec((D_FF, D_MODEL), layer),
        ],
        out_specs=pl.BlockSpec((tm, D_MODEL), row),
        out_shape=jax.ShapeDtypeStruct((n, D_MODEL), F32),
        name="out_mlp",
    )(x2d, o_self, o_x, wo, gm, wu, wd)


def _rope_tables(pos):
    half = HEAD_DIM // 2
    inv = ROPE_THETA ** (-jnp.arange(half, dtype=F32) * 2.0 / HEAD_DIM)
    ang = pos[:, None] * inv[None, :]
    return jnp.cos(ang), jnp.sin(ang)


def _rope_lane_tables(cos, sin):
    reps = LANES // (HEAD_DIM // 2)
    return jnp.tile(cos, (1, reps)), jnp.concatenate([-sin, sin] * (reps // 2), axis=1)


def _group_major(w, axis):
    shape = w.shape
    w = w.reshape(shape[:axis] + (N_KV_HEADS, GQA_GROUP, HEAD_DIM) + shape[axis + 1:])
    return jnp.swapaxes(w, axis, axis + 1).reshape(shape)


def _feature_major(cache):
    lead = cache.shape[:-3]
    pos, heads, hd = cache.shape[-3:]
    nd = len(lead)
    perm = tuple(range(nd)) + (nd + 1, nd + 2, nd)
    return jnp.transpose(cache, perm).reshape(lead + (heads * hd, pos))


def _position_major(cache_t, heads):
    lead = cache_t.shape[:-2]
    pos = cache_t.shape[-1]
    nd = len(lead)
    perm = tuple(range(nd)) + (nd + 2, nd, nd + 1)
    return jnp.transpose(cache_t.reshape(lead + (heads, HEAD_DIM, pos)), perm)


def _gain_cols(g):
    return jnp.broadcast_to(g[:, :, None], g.shape + (LANES,))


def kernel(x_prompt, x_sample, mem_prompt, cache_swa_k, cache_swa_v, state_conv, cache_mem_k, cache_mem_v,
           norm_mix, w_in_att, q_norm_att, k_norm_att, sinks, w_in_conv, conv_w, norm_mem, w_mem_kv,
           q_norm_x, k_norm_x, w_out, norm_mlp, w_up, w_down):
    batch, seq, _ = x_prompt.shape
    nseq, dec, _ = x_sample.shape
    n_mem = mem_prompt.shape[1]
    assert seq % TOKEN_TILE == 0 and (nseq * dec) == TOKEN_TILE and nseq % SEQ_BLOCK == 0
    assert LANES % (SEQ_BLOCK * dec) == 0

    kv0, kv1 = CONV_CH, CONV_CH + 2 * KV_DIM
    w_q = jnp.concatenate([_group_major(w_in_att[:, :, :kv0], 2), w_in_att[:, :, kv1:]], axis=2).astype(BF16)
    w_kv_t = jnp.swapaxes(w_in_att[:, :, kv0:kv1], 1, 2).astype(BF16)
    w_conv = w_in_conv.astype(BF16)
    att_layers = jnp.arange(DEPTH) % 2 == 0
    w_o = jnp.concatenate(
        [jnp.where(att_layers[:, None, None], _group_major(w_out[:, :CONV_CH], 1), w_out[:, :CONV_CH]),
         w_out[:, CONV_CH:]], axis=1).astype(BF16)
    w_u = w_up.astype(BF16)
    w_d = w_down.astype(BF16)
    w_mkv_t = jnp.swapaxes(w_mem_kv, 1, 2).astype(BF16)

    g_mix = norm_mix[:, None, :]
    g_mlp = norm_mlp[:, None, :]
    gq = jnp.tile(q_norm_att, (1, N_Q_HEADS))[:, None, :]
    gx = jnp.tile(q_norm_x, (1, N_X_HEADS))[:, None, :]
    gk_cols = _gain_cols(k_norm_att)
    gkx_cols = _gain_cols(k_norm_x)

    cos_p, sin_p = _rope_tables(jnp.arange(seq, dtype=F32))
    cos_s, sin_s = _rope_tables(PAST_LEN + (jnp.arange(nseq * dec) % dec).astype(F32))
    rope_p = _rope_lane_tables(cos_p, sin_p) + (cos_p.T, sin_p.T)
    rope_s = _rope_lane_tables(cos_s, sin_s) + (cos_s.T, sin_s.T)

    mk_p, mv_p = _mem_kv(mem_prompt.reshape(batch * n_mem, D_MODEL), norm_mem[:, None, :], w_mkv_t, gkx_cols,
                         batch=batch, n_mem=n_mem)
    mk_s = _feature_major(cache_mem_k)
    mv_s = _feature_major(cache_mem_v)
    kbuf = _feature_major(cache_swa_k)
    vbuf = _feature_major(cache_swa_v)

    xp = x_prompt.reshape(batch * seq, D_MODEL)
    xs = x_sample.reshape(nseq * dec, D_MODEL)
    tiles_per_seq = seq // TOKEN_TILE
    k_p, v_p, c_p, k_s, v_s, c_s = [], [], [], [], [], []
    for i in range(DEPTH):
        j = i // 2
        if i % 2 == 0:
            q, qx, kt, vt, kt32, vt32 = _in_att(xp, i, j, g_mix, w_q, w_kv_t, gq, gx, gk_cols, *rope_p,
                                                 tiles_per_seq=tiles_per_seq, keep_cols=WINDOW, act_dtype=BF16)
            k_p.append(kt32)
            v_p.append(vt32)
            o_self = _swa_prompt(sinks[j], q, kt, vt, batch=batch, seq=seq)
        else:
            o_self, qx, tail = _in_conv(xp, i, j, g_mix, w_conv, gx, conv_w, None,
                                        tiles_per_seq=tiles_per_seq, period=0,
                                        keep_rows=SUBLANES, act_dtype=BF16)
            c_p.append(tail.reshape(batch, SUBLANES, CONV_CH)[:, SUBLANES - (CONV_W - 1):])
        o_x = _xattn_prompt(qx, mk_p, mv_p, i, batch=batch, seq=seq)
        xp = _out_mlp(xp, o_self, o_x, i, w_o, g_mlp, w_u, w_d)

        if i % 2 == 0:
            q, qx, kt, vt = _in_att(xs, i, j, g_mix, w_q, w_kv_t, gq, gx, gk_cols, *rope_s,
                                    tiles_per_seq=1, keep_cols=0, act_dtype=F32)
            o_self, ko, vo = _swa_sample(sinks[j], q, kt, vt, kbuf, vbuf, j, dec=dec)
            k_s.append(ko)
            v_s.append(vo)
        else:
            prefix = jnp.pad(state_conv[j], ((0, 0), (0, dec - (CONV_W - 1)), (0, 0)))
            o_self, qx, tail = _in_conv(xs, i, j, g_mix, w_conv, gx, conv_w,
                                        prefix.reshape(nseq * dec, CONV_CH),
                                        tiles_per_seq=1, period=dec,
                                        keep_rows=nseq * dec, act_dtype=F32)
            c_s.append(tail.reshape(nseq, dec, CONV_CH)[:, dec - (CONV_W - 1):])
        o_x = _xattn_sample(qx, mk_s, mv_s, i, dec=dec)
        xs = _out_mlp(xs, o_self, o_x, i, w_o, g_mlp, w_u, w_d)

    return (xp.reshape(batch, seq, D_MODEL),
            xs.reshape(nseq, dec, D_MODEL),
            _position_major(jnp.stack(k_p), N_KV_HEADS),
            _position_major(jnp.stack(v_p), N_KV_HEADS),
            jnp.stack(c_p),
            _position_major(mk_p, N_X_HEADS),
            _position_major(mv_p, N_X_HEADS),
            _position_major(jnp.stack(k_s), N_KV_HEADS),
            _position_major(jnp.stack(v_s), N_KV_HEADS),
            jnp.stack(c_s))
```

```python
import functools

import jax
import jax.numpy as jnp
from jax import lax
from jax.experimental import pallas as pl
from jax.experimental.pallas import tpu as pltpu

F32 = jnp.float32
BF16 = jnp.bfloat16

D_MODEL = 1024
DEPTH = 4
HEAD_DIM = 64
N_Q_HEADS = 12
N_KV_HEADS = 4
GQA_GROUP = N_Q_HEADS // N_KV_HEADS
WINDOW = 128
PAST_LEN = 8192
ROPE_THETA = 10000.0
CONV_CH = N_Q_HEADS * HEAD_DIM
CONV_W = 3
N_X_HEADS = 4
X_DIM = N_X_HEADS * HEAD_DIM
KV_DIM = N_KV_HEADS * HEAD_DIM
D_FF = 4 * D_MODEL
CONV_IN = 3 * CONV_CH + X_DIM
EPS = 1e-6
NEG = -1e30
LOG2E = 1.4426950408889634
Q_SCALE = HEAD_DIM ** -0.5 * LOG2E

LANES = 128
SUBLANES = 8
MXU_DIM = 256
TOKEN_TILE = 512
FF_CHUNK = 2048
SEQ_BLOCK = 8
PAIR = 2 * HEAD_DIM

_NT = (((1,), (1,)), ((), ()))


def _const_spec(shape):
    nd = len(shape)
    return pl.BlockSpec(shape, lambda *_: (0,) * nd, pipeline_mode=pl.Buffered(1))


def _layer_spec(shape, layer):
    nd = len(shape)
    return pl.BlockSpec((1,) + shape, lambda *_: (layer,) + (0,) * nd, pipeline_mode=pl.Buffered(1))


def _rms_rows(x, g):
    return x * lax.rsqrt(jnp.mean(x * x, axis=-1, keepdims=True) + EPS) * g


def _head_blockdiag():
    r = lax.broadcasted_iota(jnp.int32, (MXU_DIM, MXU_DIM), 0) // HEAD_DIM
    c = lax.broadcasted_iota(jnp.int32, (MXU_DIM, MXU_DIM), 1) // HEAD_DIM
    return jnp.where(r == c, 1.0, 0.0).astype(BF16)


def _head_mean_sq(z, bd):
    sq = z * z
    hi = sq.astype(BF16)
    lo = (sq - hi.astype(F32)).astype(BF16)
    parts = []
    for c in range(z.shape[1] // MXU_DIM):
        sl = slice(c * MXU_DIM, (c + 1) * MXU_DIM)
        parts.append(jnp.dot(hi[:, sl], bd, preferred_element_type=F32)
                     + jnp.dot(lo[:, sl], bd, preferred_element_type=F32))
    ms = parts[0] if len(parts) == 1 else jnp.concatenate(parts, axis=1)
    return ms * (1.0 / HEAD_DIM)


def _rope_rows(x, cos, sin_signed):
    t = x.shape[0]
    first_half = (lax.broadcasted_iota(jnp.int32, (t, LANES), 1) & (HEAD_DIM - 1)) < HEAD_DIM // 2
    parts = []
    for c in range(x.shape[1] // LANES):
        xc = x[:, c * LANES:(c + 1) * LANES]
        partner = jnp.where(first_half,
                            pltpu.roll(xc, LANES - HEAD_DIM // 2, axis=1),
                            pltpu.roll(xc, HEAD_DIM // 2, axis=1))
        parts.append(xc * cos + partner * sin_signed)
    return jnp.concatenate(parts, axis=1)


def _lane_tile(col, n):
    return col if n == LANES else jnp.concatenate([col] * (n // LANES), axis=1)


def _head_norm_cols(xt, g_col):
    t = xt.shape[1]
    g = _lane_tile(g_col, t)
    parts = []
    for h in range(xt.shape[0] // HEAD_DIM):
        blk = xt[h * HEAD_DIM:(h + 1) * HEAD_DIM, :]
        ms = jnp.mean(blk * blk, axis=0, keepdims=True)
        parts.append(blk * lax.rsqrt(ms + EPS) * g)
    return parts


def _softmax_rows(s, sink=None):
    m = jnp.max(s, axis=-1, keepdims=True)
    if sink is not None:
        m = jnp.maximum(m, sink)
    p = jnp.exp2(s - m)
    den = jnp.sum(p, axis=-1, keepdims=True)
    if sink is not None:
        den = den + jnp.exp2(sink - m)
    return p.astype(BF16), 1.0 / den


def _mem_kv_kernel(mem_ref, g_ref, wt_ref, gk_ref, mk_ref, mv_ref):
    h = _rms_rows(mem_ref[...], g_ref[0]).astype(BF16)
    kvt = lax.dot_general(wt_ref[0], h, _NT, preferred_element_type=F32)
    mk_ref[0, 0] = jnp.concatenate(_head_norm_cols(kvt[:X_DIM], gk_ref[0]), axis=0)
    mv_ref[0, 0] = kvt[X_DIM:]


def _mem_kv(mem2d, norm_mem, w_mem_kv_t, gk_cols, *, batch, n_mem):
    out = jax.ShapeDtypeStruct((DEPTH, batch, X_DIM, n_mem), F32)
    return pl.pallas_call(
        _mem_kv_kernel,
        grid=(DEPTH, batch),
        in_specs=[
            pl.BlockSpec((n_mem, D_MODEL), lambda i, b: (b, 0)),
            pl.BlockSpec((1, 1, D_MODEL), lambda i, b: (i, 0, 0)),
            pl.BlockSpec((1, 2 * X_DIM, D_MODEL), lambda i, b: (i, 0, 0)),
            pl.BlockSpec((1, HEAD_DIM, LANES), lambda i, b: (i, 0, 0)),
        ],
        out_specs=[pl.BlockSpec((1, 1, X_DIM, n_mem), lambda i, b: (i, b, 0, 0))] * 2,
        out_shape=[out, out],
        name="mem_kv",
    )(mem2d, norm_mem, w_mem_kv_t, gk_cols)


def _in_att_kernel(*refs, keep_cols):
    (x_ref, g_ref, wq_ref, wkv_ref, gq_ref, gx_ref, gk_ref, cos_ref, sin_ref, cost_ref, sint_ref,
     q_ref, qx_ref, kt_ref, vt_ref) = refs[:15]
    t = x_ref.shape[0]
    h = _rms_rows(x_ref[...], g_ref[0]).astype(BF16)
    bd = _head_blockdiag()
    z = jnp.dot(h, wq_ref[0], preferred_element_type=F32)
    q = z[:, :CONV_CH]
    q = q * lax.rsqrt(_head_mean_sq(q, bd) + EPS) * gq_ref[0]
    q_ref[...] = (_rope_rows(q, cos_ref[...], sin_ref[...]) * Q_SCALE).astype(q_ref.dtype)
    qx = z[:, CONV_CH:]
    qx = qx * lax.rsqrt(_head_mean_sq(qx, bd) + EPS) * gx_ref[0]
    qx_ref[...] = (qx * Q_SCALE).astype(qx_ref.dtype)

    kvt = lax.dot_general(wkv_ref[0], h, _NT, preferred_element_type=F32)
    cos_t, sin_t = cost_ref[...], sint_ref[...]
    half = HEAD_DIM // 2
    k_parts = []
    for blk in _head_norm_cols(kvt[:KV_DIM], gk_ref[0]):
        x1, x2 = blk[:half], blk[half:]
        k_parts += [x1 * cos_t - x2 * sin_t, x2 * cos_t + x1 * sin_t]
    kt = jnp.concatenate(k_parts, axis=0)
    vt = kvt[KV_DIM:]
    kt_ref[...] = kt.astype(kt_ref.dtype)
    vt_ref[...] = vt.astype(vt_ref.dtype)
    if keep_cols:
        kt32_ref, vt32_ref = refs[15:]
        kt32_ref[0] = kt[:, t - keep_cols:]
        vt32_ref[0] = vt[:, t - keep_cols:]


def _in_att(x2d, layer, j, g_mix, wq, wkv_t, gq, gx, gk_cols, cos, sin, cos_t, sin_t,
            *, tiles_per_seq, keep_cols, act_dtype):
    n = x2d.shape[0]
    tm = TOKEN_TILE
    nt = n // tm
    nseq = nt // tiles_per_seq
    row = lambda i: (i, 0)
    col = lambda i: (0, i)
    pos = lambda i: (i % tiles_per_seq, 0)
    pos_t = lambda i: (0, i % tiles_per_seq)
    out_specs = [
        pl.BlockSpec((tm, CONV_CH), row),
        pl.BlockSpec((tm, X_DIM), row),
        pl.BlockSpec((KV_DIM, tm), col),
        pl.BlockSpec((KV_DIM, tm), col),
    ]
    out_shape = [
        jax.ShapeDtypeStruct((n, CONV_CH), act_dtype),
        jax.ShapeDtypeStruct((n, X_DIM), act_dtype),
        jax.ShapeDtypeStruct((KV_DIM, n), act_dtype),
        jax.ShapeDtypeStruct((KV_DIM, n), act_dtype),
    ]
    if keep_cols:
        out_specs += [pl.BlockSpec((1, KV_DIM, keep_cols), lambda i: (i // tiles_per_seq, 0, 0))] * 2
        out_shape += [jax.ShapeDtypeStruct((nseq, KV_DIM, keep_cols), F32)] * 2
    return pl.pallas_call(
        functools.partial(_in_att_kernel, keep_cols=keep_cols),
        grid=(nt,),
        in_specs=[
            pl.BlockSpec((tm, D_MODEL), row),
            _layer_spec((1, D_MODEL), layer),
            _layer_spec((D_MODEL, CONV_CH + X_DIM), j),
            _layer_spec((2 * KV_DIM, D_MODEL), j),
            _layer_spec((1, CONV_CH), j),
            _layer_spec((1, X_DIM), layer),
            _layer_spec((HEAD_DIM, LANES), j),
            pl.BlockSpec((tm, LANES), pos),
            pl.BlockSpec((tm, LANES), pos),
            pl.BlockSpec((HEAD_DIM // 2, tm), pos_t),
            pl.BlockSpec((HEAD_DIM // 2, tm), pos_t),
        ],
        out_specs=out_specs,
        out_shape=out_shape,
        name="in_att",
    )(x2d, g_mix, wq, wkv_t, gq, gx, gk_cols, cos, sin, cos_t, sin_t)


def _in_conv_kernel(*refs, tiles_per_seq, period):
    if period:
        (x_ref, g_ref, w_ref, gx_ref, cw_ref, prefix_ref,
         o_ref, qx_ref, gt_ref, gbuf, pbuf) = refs
    else:
        (x_ref, g_ref, w_ref, gx_ref, cw_ref,
         o_ref, qx_ref, gt_ref, gbuf) = refs
    t = x_ref.shape[0]
    keep = gt_ref.shape[0]
    h = _rms_rows(x_ref[...], g_ref[0]).astype(BF16)

    def proj(lo, width):
        return jnp.dot(h, w_ref[0, :, lo:lo + width], preferred_element_type=F32)

    @pl.when(pl.program_id(0) % tiles_per_seq == 0)
    def _():
        gbuf[0:SUBLANES, :] = jnp.zeros((SUBLANES, CONV_CH), F32)

    if period:
        pbuf[0:t, :] = prefix_ref[...]
        pbuf[t:t + SUBLANES, :] = jnp.zeros((SUBLANES, CONV_CH), F32)
        tok = lax.broadcasted_iota(jnp.int32, (t, MXU_DIM), 0) % period
    cw = cw_ref[0]
    for c in range(CONV_CH // MXU_DIM):
        sl = slice(c * MXU_DIM, (c + 1) * MXU_DIM)
        gate_b = proj(sl.start, MXU_DIM)
        u = proj(CONV_CH + sl.start, MXU_DIM) * proj(2 * CONV_CH + sl.start, MXU_DIM)
        gbuf[SUBLANES:SUBLANES + t, sl] = u
        back1 = gbuf[SUBLANES - 1:SUBLANES - 1 + t, sl]
        back2 = gbuf[SUBLANES - 2:SUBLANES - 2 + t, sl]
        if period:
            back1 = jnp.where(tok >= 1, back1, pbuf[1:1 + t, sl])
            back2 = jnp.where(tok >= 2, back2, pbuf[0:t, sl])
        y = back2 * cw[0:1, sl] + back1 * cw[1:2, sl] + u * cw[2:3, sl]
        o_ref[:, sl] = (gate_b * y).astype(o_ref.dtype)
        gt_ref[:, sl] = u[t - keep:, :]
    gbuf[0:SUBLANES, :] = gbuf[t:t + SUBLANES, :]
    qx = proj(3 * CONV_CH, X_DIM)
    qx = qx * lax.rsqrt(_head_mean_sq(qx, _head_blockdiag()) + EPS) * gx_ref[0]
    qx_ref[...] = (qx * Q_SCALE).astype(qx_ref.dtype)


def _in_conv(x2d, layer, j, g_mix, w, gx, cw, prefix, *, tiles_per_seq, period, keep_rows, act_dtype):
    n = x2d.shape[0]
    tm = TOKEN_TILE
    nt = n // tm
    nseq = nt // tiles_per_seq
    row = lambda i: (i, 0)
    in_specs = [
        pl.BlockSpec((tm, D_MODEL), row),
        _layer_spec((1, D_MODEL), layer),
        _layer_spec((D_MODEL, CONV_IN), j),
        _layer_spec((1, X_DIM), layer),
        _layer_spec((CONV_W, CONV_CH), j),
    ]
    args = [x2d, g_mix, w, gx, cw]
    scratch = [pltpu.VMEM((tm + 2 * SUBLANES, CONV_CH), F32)]
    if period:
        in_specs.append(pl.BlockSpec((tm, CONV_CH), row))
        args.append(prefix)
        scratch.append(pltpu.VMEM((tm + SUBLANES, CONV_CH), F32))
    return pl.pallas_call(
        functools.partial(_in_conv_kernel, tiles_per_seq=tiles_per_seq, period=period),
        grid=(nt,),
        in_specs=in_specs,
        out_specs=[
            pl.BlockSpec((tm, CONV_CH), row),
            pl.BlockSpec((tm, X_DIM), row),
            pl.BlockSpec((keep_rows, CONV_CH), lambda i: (i // tiles_per_seq, 0)),
        ],
        out_shape=[
            jax.ShapeDtypeStruct((n, CONV_CH), act_dtype),
            jax.ShapeDtypeStruct((n, X_DIM), act_dtype),
            jax.ShapeDtypeStruct((nseq * keep_rows, CONV_CH), F32),
        ],
        scratch_shapes=scratch,
        name="in_conv",
    )(*args)


def _pair_lhs(slabs):
    low = lax.broadcasted_iota(jnp.int32, slabs[0].shape, 1) < HEAD_DIM
    zero = jnp.zeros_like(slabs[0])
    parts = []
    for s in slabs:
        parts += [jnp.where(low, s, zero), jnp.where(low, zero, s)]
    return jnp.concatenate(parts, axis=0)


def _swa_prompt_kernel(sinks_ref, q_ref, kc_ref, kp_ref, vc_ref, vp_ref, o_ref, kfull, vfull, bias):
    tq = q_ref.shape[0]

    @pl.when(jnp.logical_and(pl.program_id(0) == 0, pl.program_id(1) == 0))
    def _():
        r = lax.broadcasted_iota(jnp.int32, (WINDOW, 2 * WINDOW), 0)
        c = lax.broadcasted_iota(jnp.int32, (WINDOW, 2 * WINDOW), 1)
        band = (c > r) & (c - WINDOW <= r)
        for kind, visible in enumerate((band, band & (c >= WINDOW))):
            base = jnp.where(visible, 0.0, NEG)
            for h in range(N_Q_HEADS):
                bias[kind, h] = jnp.where(c == 0, sinks_ref[h] * LOG2E, base)

    first_kind = jnp.where(pl.program_id(1) == 0, 1, 0)
    kfull[:, 0:WINDOW] = kp_ref[...]
    kfull[:, WINDOW:] = kc_ref[...]
    vfull[:, 0:WINDOW] = vp_ref[...]
    vfull[:, WINDOW:] = vc_ref[...]
    col0 = lax.broadcasted_iota(jnp.int32, (PAIR, 2 * WINDOW), 1) == 0
    zero_slab = jnp.zeros((PAIR, 2 * WINDOW), BF16)
    low = lax.broadcasted_iota(jnp.int32, (WINDOW, PAIR), 1) < HEAD_DIM
    for qb in range(tq // WINDOW):
        r0 = qb * WINDOW
        kind = first_kind if qb == 0 else 0
        slabs = [None] * (GQA_GROUP * KV_DIM // PAIR)
        for pr in range(KV_DIM // PAIR):
            kslab = jnp.where(col0, zero_slab, kfull[pr * PAIR:(pr + 1) * PAIR, r0:r0 + 2 * WINDOW])
            vslab = jnp.where(col0, zero_slab, vfull[pr * PAIR:(pr + 1) * PAIR, r0:r0 + 2 * WINDOW])
            lhs = _pair_lhs([q_ref[r0:r0 + WINDOW, g * KV_DIM + pr * PAIR:g * KV_DIM + (pr + 1) * PAIR]
                             for g in range(GQA_GROUP)])
            s_all = jnp.dot(lhs, kslab, preferred_element_type=F32)
            probs, inv = [], []
            for g in range(GQA_GROUP):
                for e in range(2):
                    idx = g * 2 + e
                    head = (2 * pr + e) * GQA_GROUP + g
                    p, inv_den = _softmax_rows(s_all[idx * WINDOW:(idx + 1) * WINDOW] + bias[kind, head])
                    probs.append(p)
                    inv.append(inv_den)
            res = lax.dot_general(jnp.concatenate(probs, axis=0), vslab, _NT,
                                  preferred_element_type=F32)
            for g in range(GQA_GROUP):
                lo = res[(2 * g) * WINDOW:(2 * g + 1) * WINDOW] * inv[2 * g]
                hi = res[(2 * g + 1) * WINDOW:(2 * g + 2) * WINDOW] * inv[2 * g + 1]
                slabs[g * (KV_DIM // PAIR) + pr] = jnp.where(low, lo, hi)
        o_ref[r0:r0 + WINDOW, :] = jnp.concatenate(slabs, axis=1).astype(o_ref.dtype)


def _swa_prompt(sinks, q, kt, vt, *, batch, seq):
    tq = TOKEN_TILE
    nt = seq // tq
    per = tq // WINDOW
    rows = lambda b, j: (b * nt + j, 0)
    cur = lambda b, j: (0, b * nt + j)
    prev = lambda b, j: (0, b * nt * per + jnp.maximum(j * per - 1, 0))
    return pl.pallas_call(
        _swa_prompt_kernel,
        grid=(batch, nt),
        in_specs=[
            pl.BlockSpec(memory_space=pltpu.SMEM),
            pl.BlockSpec((tq, CONV_CH), rows),
            pl.BlockSpec((KV_DIM, tq), cur),
            pl.BlockSpec((KV_DIM, WINDOW), prev),
            pl.BlockSpec((KV_DIM, tq), cur),
            pl.BlockSpec((KV_DIM, WINDOW), prev),
        ],
        out_specs=pl.BlockSpec((tq, CONV_CH), rows),
        out_shape=jax.ShapeDtypeStruct(q.shape, BF16),
        scratch_shapes=[pltpu.VMEM((KV_DIM, tq + WINDOW), BF16)] * 2
        + [pltpu.VMEM((2, N_Q_HEADS, WINDOW, 2 * WINDOW), F32)],
        name="swa_prompt",
    )(sinks, q, kt, kt, vt, vt)


def _swa_sample_kernel(sinks_ref, q_ref, kn_ref, vn_ref, kb_ref, vb_ref, o_ref, ko_ref, vo_ref, *, dec):
    nseq = kb_ref.shape[1]
    steps_per_tile = LANES // (nseq * dec)
    base = (pl.program_id(0) % steps_per_tile) * (nseq * dec)
    grp = N_KV_HEADS * dec
    rows = GQA_GROUP * grp
    ri = lax.broadcasted_iota(jnp.int32, (rows, KV_DIM), 0)
    li = lax.broadcasted_iota(jnp.int32, (rows, KV_DIM), 1)
    head_mask = ((ri % grp) // dec) == (li // HEAD_DIM)
    tok = lax.broadcasted_iota(jnp.int32, (rows, WINDOW + LANES), 0) % dec
    col = lax.broadcasted_iota(jnp.int32, (rows, WINDOW + LANES), 1)
    cache_mask = (col < WINDOW) & (col > tok)
    lane = lax.broadcasted_iota(jnp.int32, (KV_DIM, WINDOW), 1)
    sink = jnp.concatenate(
        [jnp.full((dec, 1), sinks_ref[kv * GQA_GROUP + g] * LOG2E, F32)
         for g in range(GQA_GROUP) for kv in range(N_KV_HEADS)], axis=0)
    kn = kn_ref[...]
    vn = vn_ref[...]
    kn16 = kn.astype(BF16)
    vn16 = vn.astype(BF16)
    for s in range(nseq):
        off = base + s * dec
        new = col - (WINDOW + off)
        mask = cache_mask | ((new >= 0) & (new <= tok))
        qs = q_ref[pl.ds(s * dec, dec), :]
        qbd = jnp.concatenate(
            [jnp.concatenate([qs[:, g * KV_DIM:(g + 1) * KV_DIM]] * N_KV_HEADS, axis=0)
             for g in range(GQA_GROUP)], axis=0)
        qbd = jnp.where(head_mask, qbd, 0.0).astype(BF16)
        kb = kb_ref[0, s]
        vb = vb_ref[0, s]
        keys = jnp.concatenate([kb.astype(BF16), kn16], axis=1)
        vals = jnp.concatenate([vb.astype(BF16), vn16], axis=1)
        sc = jnp.where(mask, jnp.dot(qbd, keys, preferred_element_type=F32), NEG)
        p, inv_den = _softmax_rows(sc, sink)
        o = lax.dot_general(p, vals, _NT, preferred_element_type=F32) * inv_den
        o = jnp.where(head_mask, o, 0.0)
        folded = []
        for g in range(GQA_GROUP):
            og = o[g * grp:g * grp + dec, :]
            for kv in range(1, N_KV_HEADS):
                og = og + o[g * grp + kv * dec:g * grp + (kv + 1) * dec, :]
            folded.append(og)
        o_ref[pl.ds(s * dec, dec), :] = jnp.concatenate(folded, axis=1)
        shift_new = (WINDOW - dec + LANES - off) % LANES
        ko_ref[s] = jnp.where(lane < WINDOW - dec,
                              pltpu.roll(kb, WINDOW - dec, axis=1), pltpu.roll(kn, shift_new, axis=1))
        vo_ref[s] = jnp.where(lane < WINDOW - dec,
                              pltpu.roll(vb, WINDOW - dec, axis=1), pltpu.roll(vn, shift_new, axis=1))


def _swa_sample(sinks, q, kt_new, vt_new, kbuf, vbuf, j, *, dec):
    nseq = kbuf.shape[1]
    sb = SEQ_BLOCK
    steps_per_tile = LANES // (sb * dec)
    row = lambda i: (i, 0)
    tile = lambda i: (0, i // steps_per_tile)
    cache = lambda i: (i, 0, 0)
    cache_in = lambda i: (j, i, 0, 0)
    return pl.pallas_call(
        functools.partial(_swa_sample_kernel, dec=dec),
        grid=(nseq // sb,),
        in_specs=[
            pl.BlockSpec(memory_space=pltpu.SMEM),
            pl.BlockSpec((sb * dec, CONV_CH), row),
            pl.BlockSpec((KV_DIM, LANES), tile),
            pl.BlockSpec((KV_DIM, LANES), tile),
            pl.BlockSpec((1, sb, KV_DIM, WINDOW), cache_in),
            pl.BlockSpec((1, sb, KV_DIM, WINDOW), cache_in),
        ],
        out_specs=[
            pl.BlockSpec((sb * dec, CONV_CH), row),
            pl.BlockSpec((sb, KV_DIM, WINDOW), cache),
            pl.BlockSpec((sb, KV_DIM, WINDOW), cache),
        ],
        out_shape=[
            jax.ShapeDtypeStruct(q.shape, F32),
            jax.ShapeDtypeStruct(kbuf.shape[1:], F32),
            jax.ShapeDtypeStruct(vbuf.shape[1:], F32),
        ],
        name="swa_sample",
    )(sinks, q, kt_new, vt_new, kbuf, vbuf)


def _xattn_prompt_kernel(q_ref, mk_ref, mv_ref, o_ref):
    tq = q_ref.shape[0]
    mk = mk_ref[0, 0].astype(BF16)
    mv = mv_ref[0, 0].astype(BF16)
    low = lax.broadcasted_iota(jnp.int32, (WINDOW, PAIR), 1) < HEAD_DIM
    for rb in range(tq // WINDOW):
        r0 = rb * WINDOW
        slabs = []
        for pr in range(X_DIM // PAIR):
            lhs = _pair_lhs([q_ref[r0:r0 + WINDOW, pr * PAIR:(pr + 1) * PAIR]])
            s_all = jnp.dot(lhs, mk[pr * PAIR:(pr + 1) * PAIR], preferred_element_type=F32)
            p0, inv0 = _softmax_rows(s_all[:WINDOW])
            p1, inv1 = _softmax_rows(s_all[WINDOW:])
            res = lax.dot_general(jnp.concatenate([p0, p1], axis=0), mv[pr * PAIR:(pr + 1) * PAIR], _NT,
                                  preferred_element_type=F32)
            slabs.append(jnp.where(low, res[:WINDOW] * inv0, res[WINDOW:] * inv1))
        o_ref[r0:r0 + WINDOW, :] = jnp.concatenate(slabs, axis=1).astype(o_ref.dtype)


def _xattn_prompt(qx, mk, mv, layer, *, batch, seq):
    tq = TOKEN_TILE
    nt = seq // tq
    n_mem = mk.shape[3]
    mem = lambda b, j: (layer, b, 0, 0)
    return pl.pallas_call(
        _xattn_prompt_kernel,
        grid=(batch, nt),
        in_specs=[
            pl.BlockSpec((tq, X_DIM), lambda b, j: (b * nt + j, 0)),
            pl.BlockSpec((1, 1, X_DIM, n_mem), mem),
            pl.BlockSpec((1, 1, X_DIM, n_mem), mem),
        ],
        out_specs=pl.BlockSpec((tq, X_DIM), lambda b, j: (b * nt + j, 0)),
        out_shape=jax.ShapeDtypeStruct(qx.shape, BF16),
        name="xattn_prompt",
    )(qx, mk, mv)


def _xattn_sample_kernel(q_ref, mk_ref, mv_ref, o_ref, *, dec):
    nseq = mk_ref.shape[1]
    rows = N_X_HEADS * dec
    head_mask = (lax.broadcasted_iota(jnp.int32, (rows, X_DIM), 0) // dec) == (
        lax.broadcasted_iota(jnp.int32, (rows, X_DIM), 1) // HEAD_DIM)
    for s in range(nseq):
        qs = q_ref[pl.ds(s * dec, dec), :]
        qbd = jnp.where(head_mask, jnp.concatenate([qs] * N_X_HEADS, axis=0), 0.0).astype(BF16)
        sc = jnp.dot(qbd, mk_ref[0, s].astype(BF16), preferred_element_type=F32)
        p, inv_den = _softmax_rows(sc)
        o = lax.dot_general(p, mv_ref[0, s].astype(BF16), _NT, preferred_element_type=F32) * inv_den
        o = jnp.where(head_mask, o, 0.0)
        acc = o[0:dec, :]
        for h in range(1, N_X_HEADS):
            acc = acc + o[h * dec:(h + 1) * dec, :]
        o_ref[pl.ds(s * dec, dec), :] = acc


def _xattn_sample(qx, mk, mv, layer, *, dec):
    nseq, n_mem = mk.shape[1], mk.shape[3]
    sb = SEQ_BLOCK
    mem = lambda i: (layer, i, 0, 0)
    return pl.pallas_call(
        functools.partial(_xattn_sample_kernel, dec=dec),
        grid=(nseq // sb,),
        in_specs=[
            pl.BlockSpec((sb * dec, X_DIM), lambda i: (i, 0)),
            pl.BlockSpec((1, sb, X_DIM, n_mem), mem),
            pl.BlockSpec((1, sb, X_DIM, n_mem), mem),
        ],
        out_specs=pl.BlockSpec((sb * dec, X_DIM), lambda i: (i, 0)),
        out_shape=jax.ShapeDtypeStruct(qx.shape, F32),
        name="xattn_sample",
    )(qx, mk, mv)


def _out_mlp_kernel(x_ref, os_ref, ox_ref, wo_ref, gm_ref, wu_ref, wd_ref, y_ref):
    o = jnp.concatenate([os_ref[...].astype(BF16), ox_ref[...].astype(BF16)], axis=1)
    x1 = x_ref[...] + jnp.dot(o, wo_ref[0], preferred_element_type=F32)
    hm = _rms_rows(x1, gm_ref[0]).astype(BF16)
    acc = x1
    for c in range(D_FF // FF_CHUNK):
        a = jnp.dot(hm, wu_ref[0, :, c * FF_CHUNK:(c + 1) * FF_CHUNK], preferred_element_type=F32)
        a = jnp.square(jnp.maximum(a, 0.0)).astype(BF16)
        acc = acc + jnp.dot(a, wd_ref[0, c * FF_CHUNK:(c + 1) * FF_CHUNK, :], preferred_element_type=F32)
    y_ref[...] = acc


def _out_mlp(x2d, o_self, o_x, layer, wo, gm, wu, wd):
    n = x2d.shape[0]
    tm = TOKEN_TILE
    row = lambda i: (i, 0)
    return pl.pallas_call(
        _out_mlp_kernel,
        grid=(n // tm,),
        in_specs=[
            pl.BlockSpec((tm, D_MODEL), row),
            pl.BlockSpec((tm, CONV_CH), row),
            pl.BlockSpec((tm, X_DIM), row),
            _layer_spec((CONV_CH + X_DIM, D_MODEL), layer),
            _layer_spec((1, D_MODEL), layer),
            _layer_spec((D_MODEL, D_FF), layer),
            _layer_spec((D_FF, D_MODEL), layer),
        ],
        out_specs=pl.BlockSpec((tm, D_MODEL), row),
        out_shape=jax.ShapeDtypeStruct((n, D_MODEL), F32),
        name="out_mlp",
    )(x2d, o_self, o_x, wo, gm, wu, wd)


def _rope_tables(pos):
    half = HEAD_DIM // 2
    inv = ROPE_THETA ** (-jnp.arange(half, dtype=F32) * 2.0 / HEAD_DIM)
    ang = pos[:, None] * inv[None, :]
    return jnp.cos(ang), jnp.sin(ang)


def _rope_lane_tables(cos, sin):
    reps = LANES // (HEAD_DIM // 2)
    return jnp.tile(cos, (1, reps)), jnp.concatenate([-sin, sin] * (reps // 2), axis=1)


def _group_major(w, axis):
    shape = w.shape
    w = w.reshape(shape[:axis] + (N_KV_HEADS, GQA_GROUP, HEAD_DIM) + shape[axis + 1:])
    return jnp.swapaxes(w, axis, axis + 1).reshape(shape)


def _feature_major(cache):
    lead = cache.shape[:-3]
    pos, heads, hd = cache.shape[-3:]
    nd = len(lead)
    perm = tuple(range(nd)) + (nd + 1, nd + 2, nd)
    return jnp.transpose(cache, perm).reshape(lead + (heads * hd, pos))


def _position_major(cache_t, heads):
    lead = cache_t.shape[:-2]
    pos = cache_t.shape[-1]
    nd = len(lead)
    perm = tuple(range(nd)) + (nd + 2, nd, nd + 1)
    return jnp.transpose(cache_t.reshape(lead + (heads, HEAD_DIM, pos)), perm)


def _gain_cols(g):
    return jnp.broadcast_to(g[:, :, None], g.shape + (LANES,))


def kernel(x_prompt, x_sample, mem_prompt, cache_swa_k, cache_swa_v, state_conv, cache_mem_k, cache_mem_v,
           norm_mix, w_in_att, q_norm_att, k_norm_att, sinks, w_in_conv, conv_w, norm_mem, w_mem_kv,
           q_norm_x, k_norm_x, w_out, norm_mlp, w_up, w_down):
    batch, seq, _ = x_prompt.shape
    nseq, dec, _ = x_sample.shape
    n_mem = mem_prompt.shape[1]
    assert seq % TOKEN_TILE == 0 and (nseq * dec) == TOKEN_TILE and nseq % SEQ_BLOCK == 0
    assert LANES % (SEQ_BLOCK * dec) == 0

    kv0, kv1 = CONV_CH, CONV_CH + 2 * KV_DIM
    w_q = jnp.concatenate([_group_major(w_in_att[:, :, :kv0], 2), w_in_att[:, :, kv1:]], axis=2).astype(BF16)
    w_kv_t = jnp.swapaxes(w_in_att[:, :, kv0:kv1], 1, 2).astype(BF16)
    w_conv = w_in_conv.astype(BF16)
    att_layers = jnp.arange(DEPTH) % 2 == 0
    w_o = jnp.concatenate(
        [jnp.where(att_layers[:, None, None], _group_major(w_out[:, :CONV_CH], 1), w_out[:, :CONV_CH]),
         w_out[:, CONV_CH:]], axis=1).astype(BF16)
    w_u = w_up.astype(BF16)
    w_d = w_down.astype(BF16)
    w_mkv_t = jnp.swapaxes(w_mem_kv, 1, 2).astype(BF16)

    g_mix = norm_mix[:, None, :]
    g_mlp = norm_mlp[:, None, :]
    gq = jnp.tile(q_norm_att, (1, N_Q_HEADS))[:, None, :]
    gx = jnp.tile(q_norm_x, (1, N_X_HEADS))[:, None, :]
    gk_cols = _gain_cols(k_norm_att)
    gkx_cols = _gain_cols(k_norm_x)

    cos_p, sin_p = _rope_tables(jnp.arange(seq, dtype=F32))
    cos_s, sin_s = _rope_tables(PAST_LEN + (jnp.arange(nseq * dec) % dec).astype(F32))
    rope_p = _rope_lane_tables(cos_p, sin_p) + (cos_p.T, sin_p.T)
    rope_s = _rope_lane_tables(cos_s, sin_s) + (cos_s.T, sin_s.T)

    mk_p, mv_p = _mem_kv(mem_prompt.reshape(batch * n_mem, D_MODEL), norm_mem[:, None, :], w_mkv_t, gkx_cols,
                         batch=batch, n_mem=n_mem)
    mk_s = _feature_major(cache_mem_k)
    mv_s = _feature_major(cache_mem_v)
    kbuf = _feature_major(cache_swa_k)
    vbuf = _feature_major(cache_swa_v)

    xp = x_prompt.reshape(batch * seq, D_MODEL)
    xs = x_sample.reshape(nseq * dec, D_MODEL)
    tiles_per_seq = seq // TOKEN_TILE
    k_p, v_p, c_p, k_s, v_s, c_s = [], [], [], [], [], []
    for i in range(DEPTH):
        j = i // 2
        if i % 2 == 0:
            q, qx, kt, vt, kt32, vt32 = _in_att(xp, i, j, g_mix, w_q, w_kv_t, gq, gx, gk_cols, *rope_p,
                                                 tiles_per_seq=tiles_per_seq, keep_cols=WINDOW, act_dtype=BF16)
            k_p.append(kt32)
            v_p.append(vt32)
            o_self = _swa_prompt(sinks[j], q, kt, vt, batch=batch, seq=seq)
        else:
            o_self, qx, tail = _in_conv(xp, i, j, g_mix, w_conv, gx, conv_w, None,
                                        tiles_per_seq=tiles_per_seq, period=0,
                                        keep_rows=SUBLANES, act_dtype=BF16)
            c_p.append(tail.reshape(batch, SUBLANES, CONV_CH)[:, SUBLANES - (CONV_W - 1):])
        o_x = _xattn_prompt(qx, mk_p, mv_p, i, batch=batch, seq=seq)
        xp = _out_mlp(xp, o_self, o_x, i, w_o, g_mlp, w_u, w_d)

        if i % 2 == 0:
            q, qx, kt, vt = _in_att(xs, i, j, g_mix, w_q, w_kv_t, gq, gx, gk_cols, *rope_s,
                                    tiles_per_seq=1, keep_cols=0, act_dtype=F32)
            o_self, ko, vo = _swa_sample(sinks[j], q, kt, vt, kbuf, vbuf, j, dec=dec)
            k_s.append(ko)
            v_s.append(vo)
        else:
            prefix = jnp.pad(state_conv[j], ((0, 0), (0, dec - (CONV_W - 1)), (0, 0)))
            o_self, qx, tail = _in_conv(xs, i, j, g_mix, w_conv, gx, conv_w,
                                        prefix.reshape(nseq * dec, CONV_CH),
                                        tiles_per_seq=1, period=dec,
                                        keep_rows=nseq * dec, act_dtype=F32)
            c_s.append(tail.reshape(nseq, dec, CONV_CH)[:, dec - (CONV_W - 1):])
        o_x = _xattn_sample(qx, mk_s, mv_s, i, dec=dec)
        xs = _out_mlp(xs, o_self, o_x, i, w_o, g_mlp, w_u, w_d)

    return (xp.reshape(batch, seq, D_MODEL),
            xs.reshape(nseq, dec, D_MODEL),
            _position_major(jnp.stack(k_p), N_KV_HEADS),
            _position_major(jnp.stack(v_p), N_KV_HEADS),
            jnp.stack(c_p),
            _position_major(mk_p, N_X_HEADS),
            _position_major(mv_p, N_X_HEADS),
            _position_major(jnp.stack(k_s), N_KV_HEADS),
            _position_major(jnp.stack(v_s), N_KV_HEADS),
            jnp.stack(c_s))
```

```python
import functools

import jax
import jax.numpy as jnp
from jax import lax
from jax.experimental import pallas as pl
from jax.experimental.pallas import tpu as pltpu

F32 = jnp.float32
BF16 = jnp.bfloat16

D_MODEL = 1024
DEPTH = 4
HEAD_DIM = 64
N_Q_HEADS = 12
N_KV_HEADS = 4
GQA_GROUP = N_Q_HEADS // N_KV_HEADS
WINDOW = 128
PAST_LEN = 8192
ROPE_THETA = 10000.0
CONV_CH = N_Q_HEADS * HEAD_DIM
CONV_W = 3
N_X_HEADS = 4
X_DIM = N_X_HEADS * HEAD_DIM
KV_DIM = N_KV_HEADS * HEAD_DIM
D_FF = 4 * D_MODEL
CONV_IN = 3 * CONV_CH + X_DIM
EPS = 1e-6
NEG = -1e30
LOG2E = 1.4426950408889634
Q_SCALE = HEAD_DIM ** -0.5 * LOG2E

LANES = 128
SUBLANES = 8
MXU_DIM = 256
TOKEN_TILE = 512
FF_CHUNK = 2048
SEQ_BLOCK = 8
PAIR = 2 * HEAD_DIM

_NT = (((1,), (1,)), ((), ()))


def _const_spec(shape):
    nd = len(shape)
    return pl.BlockSpec(shape, lambda *_: (0,) * nd, pipeline_mode=pl.Buffered(1))


def _layer_spec(shape, layer):
    nd = len(shape)
    return pl.BlockSpec((1,) + shape, lambda *_: (layer,) + (0,) * nd, pipeline_mode=pl.Buffered(1))


def _rms_rows(x, g):
    return x * lax.rsqrt(jnp.mean(x * x, axis=-1, keepdims=True) + EPS) * g


def _head_blockdiag():
    r = lax.broadcasted_iota(jnp.int32, (MXU_DIM, MXU_DIM), 0) // HEAD_DIM
    c = lax.broadcasted_iota(jnp.int32, (MXU_DIM, MXU_DIM), 1) // HEAD_DIM
    return jnp.where(r == c, 1.0, 0.0).astype(BF16)


def _head_mean_sq(z, bd):
    sq = z * z
    hi = sq.astype(BF16)
    lo = (sq - hi.astype(F32)).astype(BF16)
    parts = []
    for c in range(z.shape[1] // MXU_DIM):
        sl = slice(c * MXU_DIM, (c + 1) * MXU_DIM)
        parts.append(jnp.dot(hi[:, sl], bd, preferred_element_type=F32)
                     + jnp.dot(lo[:, sl], bd, preferred_element_type=F32))
    ms = parts[0] if len(parts) == 1 else jnp.concatenate(parts, axis=1)
    return ms * (1.0 / HEAD_DIM)


def _rope_rows(x, cos, sin_signed):
    t = x.shape[0]
    first_half = (lax.broadcasted_iota(jnp.int32, (t, LANES), 1) & (HEAD_DIM - 1)) < HEAD_DIM // 2
    parts = []
    for c in range(x.shape[1] // LANES):
        xc = x[:, c * LANES:(c + 1) * LANES]
        partner = jnp.where(first_half,
                            pltpu.roll(xc, LANES - HEAD_DIM // 2, axis=1),
                            pltpu.roll(xc, HEAD_DIM // 2, axis=1))
        parts.append(xc * cos + partner * sin_signed)
    return jnp.concatenate(parts, axis=1)


def _lane_tile(col, n):
    return col if n == LANES else jnp.concatenate([col] * (n // LANES), axis=1)


def _head_norm_cols(xt, g_col):
    t = xt.shape[1]
    g = _lane_tile(g_col, t)
    parts = []
    for h in range(xt.shape[0] // HEAD_DIM):
        blk = xt[h * HEAD_DIM:(h + 1) * HEAD_DIM, :]
        ms = jnp.mean(blk * blk, axis=0, keepdims=True)
        parts.append(blk * lax.rsqrt(ms + EPS) * g)
    return parts


def _softmax_rows(s, sink=None):
    m = jnp.max(s, axis=-1, keepdims=True)
    if sink is not None:
        m = jnp.maximum(m, sink)
    p = jnp.exp2(s - m)
    den = jnp.sum(p, axis=-1, keepdims=True)
    if sink is not None:
        den = den + jnp.exp2(sink - m)
    return p.astype(BF16), 1.0 / den


def _mem_kv_kernel(mem_ref, g_ref, wt_ref, gk_ref, mk_ref, mv_ref):
    x = mem_ref[...]
    xn = x * lax.rsqrt(jnp.mean(x * x, axis=-1, keepdims=True) + EPS)
    for i in range(DEPTH):
        h = (xn * g_ref[i]).astype(BF16)
        kvt = lax.dot_general(wt_ref[i], h, _NT, preferred_element_type=F32)
        mk_ref[i, 0] = jnp.concatenate(_head_norm_cols(kvt[:X_DIM], gk_ref[i]), axis=0)
        mv_ref[i, 0] = kvt[X_DIM:]


def _mem_kv(mem2d, norm_mem, w_mem_kv_t, gk_cols, *, batch, n_mem):
    out = jax.ShapeDtypeStruct((DEPTH, batch, X_DIM, n_mem), F32)
    return pl.pallas_call(
        _mem_kv_kernel,
        grid=(batch,),
        in_specs=[
            pl.BlockSpec((n_mem, D_MODEL), lambda b: (b, 0)),
            _const_spec((DEPTH, 1, D_MODEL)),
            _const_spec((DEPTH, 2 * X_DIM, D_MODEL)),
            _const_spec((DEPTH, HEAD_DIM, LANES)),
        ],
        out_specs=[pl.BlockSpec((DEPTH, 1, X_DIM, n_mem), lambda b: (0, b, 0, 0))] * 2,
        out_shape=[out, out],
        name="mem_kv",
    )(mem2d, norm_mem, w_mem_kv_t, gk_cols)


def _in_att_kernel(*refs, keep_cols):
    (x_ref, g_ref, wq_ref, wkv_ref, gq_ref, gx_ref, gk_ref, cos_ref, sin_ref, cost_ref, sint_ref,
     q_ref, qx_ref, kt_ref, vt_ref) = refs[:15]
    t = x_ref.shape[0]
    h = _rms_rows(x_ref[...], g_ref[0]).astype(BF16)
    bd = _head_blockdiag()
    z = jnp.dot(h, wq_ref[0], preferred_element_type=F32)
    q = z[:, :CONV_CH]
    q = q * lax.rsqrt(_head_mean_sq(q, bd) + EPS) * gq_ref[0]
    q_ref[...] = (_rope_rows(q, cos_ref[...], sin_ref[...]) * Q_SCALE).astype(q_ref.dtype)
    qx = z[:, CONV_CH:]
    qx = qx * lax.rsqrt(_head_mean_sq(qx, bd) + EPS) * gx_ref[0]
    qx_ref[...] = (qx * Q_SCALE).astype(qx_ref.dtype)

    kvt = lax.dot_general(wkv_ref[0], h, _NT, preferred_element_type=F32)
    cos_t, sin_t = cost_ref[...], sint_ref[...]
    half = HEAD_DIM // 2
    k_parts = []
    for blk in _head_norm_cols(kvt[:KV_DIM], gk_ref[0]):
        x1, x2 = blk[:half], blk[half:]
        k_parts += [x1 * cos_t - x2 * sin_t, x2 * cos_t + x1 * sin_t]
    kt = jnp.concatenate(k_parts, axis=0)
    vt = kvt[KV_DIM:]
    kt_ref[...] = kt.astype(kt_ref.dtype)
    vt_ref[...] = vt.astype(vt_ref.dtype)
    if keep_cols:
        kt32_ref, vt32_ref = refs[15:]
        kt32_ref[0] = kt[:, t - keep_cols:]
        vt32_ref[0] = vt[:, t - keep_cols:]


def _in_att(x2d, layer, j, g_mix, wq, wkv_t, gq, gx, gk_cols, cos, sin, cos_t, sin_t,
            *, tiles_per_seq, keep_cols, act_dtype):
    n = x2d.shape[0]
    tm = TOKEN_TILE
    nt = n // tm
    nseq = nt // tiles_per_seq
    row = lambda i: (i, 0)
    col = lambda i: (0, i)
    pos = lambda i: (i % tiles_per_seq, 0)
    pos_t = lambda i: (0, i % tiles_per_seq)
    out_specs = [
        pl.BlockSpec((tm, CONV_CH), row),
        pl.BlockSpec((tm, X_DIM), row),
        pl.BlockSpec((KV_DIM, tm), col),
        pl.BlockSpec((KV_DIM, tm), col),
    ]
    out_shape = [
        jax.ShapeDtypeStruct((n, CONV_CH), act_dtype),
        jax.ShapeDtypeStruct((n, X_DIM), act_dtype),
        jax.ShapeDtypeStruct((KV_DIM, n), act_dtype),
        jax.ShapeDtypeStruct((KV_DIM, n), act_dtype),
    ]
    if keep_cols:
        out_specs += [pl.BlockSpec((1, KV_DIM, keep_cols), lambda i: (i // tiles_per_seq, 0, 0))] * 2
        out_shape += [jax.ShapeDtypeStruct((nseq, KV_DIM, keep_cols), F32)] * 2
    return pl.pallas_call(
        functools.partial(_in_att_kernel, keep_cols=keep_cols),
        grid=(nt,),
        in_specs=[
            pl.BlockSpec((tm, D_MODEL), row),
            _layer_spec((1, D_MODEL), layer),
            _layer_spec((D_MODEL, CONV_CH + X_DIM), j),
            _layer_spec((2 * KV_DIM, D_MODEL), j),
            _layer_spec((1, CONV_CH), j),
            _layer_spec((1, X_DIM), layer),
            _layer_spec((HEAD_DIM, LANES), j),
            pl.BlockSpec((tm, LANES), pos),
            pl.BlockSpec((tm, LANES), pos),
            pl.BlockSpec((HEAD_DIM // 2, tm), pos_t),
            pl.BlockSpec((HEAD_DIM // 2, tm), pos_t),
        ],
        out_specs=out_specs,
        out_shape=out_shape,
        name="in_att",
    )(x2d, g_mix, wq, wkv_t, gq, gx, gk_cols, cos, sin, cos_t, sin_t)


def _in_conv_kernel(*refs, tiles_per_seq, period):
    if period:
        (x_ref, g_ref, w_ref, gx_ref, cw_ref, prefix_ref,
         o_ref, qx_ref, gt_ref, gbuf, pbuf) = refs
    else:
        (x_ref, g_ref, w_ref, gx_ref, cw_ref,
         o_ref, qx_ref, gt_ref, gbuf) = refs
    t = x_ref.shape[0]
    keep = gt_ref.shape[0]
    h = _rms_rows(x_ref[...], g_ref[0]).astype(BF16)

    def proj(lo, width):
        return jnp.dot(h, w_ref[0, :, lo:lo + width], preferred_element_type=F32)

    @pl.when(pl.program_id(0) % tiles_per_seq == 0)
    def _():
        gbuf[0:SUBLANES, :] = jnp.zeros((SUBLANES, CONV_CH), F32)

    if period:
        pbuf[0:t, :] = prefix_ref[...]
        pbuf[t:t + SUBLANES, :] = jnp.zeros((SUBLANES, CONV_CH), F32)
        tok = lax.broadcasted_iota(jnp.int32, (t, MXU_DIM), 0) % period
    cw = cw_ref[0]
    for c in range(CONV_CH // MXU_DIM):
        sl = slice(c * MXU_DIM, (c + 1) * MXU_DIM)
        gate_b = proj(sl.start, MXU_DIM)
        u = proj(CONV_CH + sl.start, MXU_DIM) * proj(2 * CONV_CH + sl.start, MXU_DIM)
        gbuf[SUBLANES:SUBLANES + t, sl] = u
        back1 = gbuf[SUBLANES - 1:SUBLANES - 1 + t, sl]
        back2 = gbuf[SUBLANES - 2:SUBLANES - 2 + t, sl]
        if period:
            back1 = jnp.where(tok >= 1, back1, pbuf[1:1 + t, sl])
            back2 = jnp.where(tok >= 2, back2, pbuf[0:t, sl])
        y = back2 * cw[0:1, sl] + back1 * cw[1:2, sl] + u * cw[2:3, sl]
        o_ref[:, sl] = (gate_b * y).astype(o_ref.dtype)
        gt_ref[:, sl] = u[t - keep:, :]
    gbuf[0:SUBLANES, :] = gbuf[t:t + SUBLANES, :]
    qx = proj(3 * CONV_CH, X_DIM)
    qx = qx * lax.rsqrt(_head_mean_sq(qx, _head_blockdiag()) + EPS) * gx_ref[0]
    qx_ref[...] = (qx * Q_SCALE).astype(qx_ref.dtype)


def _in_conv(x2d, layer, j, g_mix, w, gx, cw, prefix, *, tiles_per_seq, period, keep_rows, act_dtype):
    n = x2d.shape[0]
    tm = TOKEN_TILE
    nt = n // tm
    nseq = nt // tiles_per_seq
    row = lambda i: (i, 0)
    in_specs = [
        pl.BlockSpec((tm, D_MODEL), row),
        _layer_spec((1, D_MODEL), layer),
        _layer_spec((D_MODEL, CONV_IN), j),
        _layer_spec((1, X_DIM), layer),
        _layer_spec((CONV_W, CONV_CH), j),
    ]
    args = [x2d, g_mix, w, gx, cw]
    scratch = [pltpu.VMEM((tm + 2 * SUBLANES, CONV_CH), F32)]
    if period:
        in_specs.append(pl.BlockSpec((tm, CONV_CH), row))
        args.append(prefix)
        scratch.append(pltpu.VMEM((tm + SUBLANES, CONV_CH), F32))
    return pl.pallas_call(
        functools.partial(_in_conv_kernel, tiles_per_seq=tiles_per_seq, period=period),
        grid=(nt,),
        in_specs=in_specs,
        out_specs=[
            pl.BlockSpec((tm, CONV_CH), row),
            pl.BlockSpec((tm, X_DIM), row),
            pl.BlockSpec((keep_rows, CONV_CH), lambda i: (i // tiles_per_seq, 0)),
        ],
        out_shape=[
            jax.ShapeDtypeStruct((n, CONV_CH), act_dtype),
            jax.ShapeDtypeStruct((n, X_DIM), act_dtype),
            jax.ShapeDtypeStruct((nseq * keep_rows, CONV_CH), F32),
        ],
        scratch_shapes=scratch,
        name="in_conv",
    )(*args)


def _pair_lhs(slabs):
    low = lax.broadcasted_iota(jnp.int32, slabs[0].shape, 1) < HEAD_DIM
    zero = jnp.zeros_like(slabs[0])
    parts = []
    for s in slabs:
        parts += [jnp.where(low, s, zero), jnp.where(low, zero, s)]
    return jnp.concatenate(parts, axis=0)


def _swa_prompt_kernel(sinks_ref, q_ref, kc_ref, kp_ref, vc_ref, vp_ref, o_ref, kfull, vfull, bias):
    tq = q_ref.shape[0]

    @pl.when(jnp.logical_and(pl.program_id(0) == 0, pl.program_id(1) == 0))
    def _():
        r = lax.broadcasted_iota(jnp.int32, (WINDOW, 2 * WINDOW), 0)
        c = lax.broadcasted_iota(jnp.int32, (WINDOW, 2 * WINDOW), 1)
        band = (c > r) & (c - WINDOW <= r)
        for kind, visible in enumerate((band, band & (c >= WINDOW))):
            base = jnp.where(visible, 0.0, NEG)
            for h in range(N_Q_HEADS):
                bias[kind, h] = jnp.where(c == 0, sinks_ref[h] * LOG2E, base)

    first_kind = jnp.where(pl.program_id(1) == 0, 1, 0)
    kfull[:, 0:WINDOW] = kp_ref[...]
    kfull[:, WINDOW:] = kc_ref[...]
    vfull[:, 0:WINDOW] = vp_ref[...]
    vfull[:, WINDOW:] = vc_ref[...]
    col0 = lax.broadcasted_iota(jnp.int32, (PAIR, 2 * WINDOW), 1) == 0
    zero_slab = jnp.zeros((PAIR, 2 * WINDOW), BF16)
    low = lax.broadcasted_iota(jnp.int32, (WINDOW, PAIR), 1) < HEAD_DIM
    for qb in range(tq // WINDOW):
        r0 = qb * WINDOW
        kind = first_kind if qb == 0 else 0
        slabs = [None] * (GQA_GROUP * KV_DIM // PAIR)
        for pr in range(KV_DIM // PAIR):
            kslab = jnp.where(col0, zero_slab, kfull[pr * PAIR:(pr + 1) * PAIR, r0:r0 + 2 * WINDOW])
            vslab = jnp.where(col0, zero_slab, vfull[pr * PAIR:(pr + 1) * PAIR, r0:r0 + 2 * WINDOW])
            lhs = _pair_lhs([q_ref[r0:r0 + WINDOW, g * KV_DIM + pr * PAIR:g * KV_DIM + (pr + 1) * PAIR]
                             for g in range(GQA_GROUP)])
            s_all = jnp.dot(lhs, kslab, preferred_element_type=F32)
            probs, inv = [], []
            for g in range(GQA_GROUP):
                for e in range(2):
                    idx = g * 2 + e
                    head = (2 * pr + e) * GQA_GROUP + g
                    p, inv_den = _softmax_rows(s_all[idx * WINDOW:(idx + 1) * WINDOW] + bias[kind, head])
                    probs.append(p)
                    inv.append(inv_den)
            res = lax.dot_general(jnp.concatenate(probs, axis=0), vslab, _NT,
                                  preferred_element_type=F32)
            for g in range(GQA_GROUP):
                lo = res[(2 * g) * WINDOW:(2 * g + 1) * WINDOW] * inv[2 * g]
                hi = res[(2 * g + 1) * WINDOW:(2 * g + 2) * WINDOW] * inv[2 * g + 1]
                slabs[g * (KV_DIM // PAIR) + pr] = jnp.where(low, lo, hi)
        o_ref[r0:r0 + WINDOW, :] = jnp.concatenate(slabs, axis=1).astype(o_ref.dtype)


def _swa_prompt(sinks, q, kt, vt, *, batch, seq):
    tq = TOKEN_TILE
    nt = seq // tq
    per = tq // WINDOW
    rows = lambda b, j: (b * nt + j, 0)
    cur = lambda b, j: (0, b * nt + j)
    prev = lambda b, j: (0, b * nt * per + jnp.maximum(j * per - 1, 0))
    return pl.pallas_call(
        _swa_prompt_kernel,
        grid=(batch, nt),
        in_specs=[
            pl.BlockSpec(memory_space=pltpu.SMEM),
            pl.BlockSpec((tq, CONV_CH), rows),
            pl.BlockSpec((KV_DIM, tq), cur),
            pl.BlockSpec((KV_DIM, WINDOW), prev),
            pl.BlockSpec((KV_DIM, tq), cur),
            pl.BlockSpec((KV_DIM, WINDOW), prev),
        ],
        out_specs=pl.BlockSpec((tq, CONV_CH), rows),
        out_shape=jax.ShapeDtypeStruct(q.shape, BF16),
        scratch_shapes=[pltpu.VMEM((KV_DIM, tq + WINDOW), BF16)] * 2
        + [pltpu.VMEM((2, N_Q_HEADS, WINDOW, 2 * WINDOW), F32)],
        name="swa_prompt",
    )(sinks, q, kt, kt, vt, vt)


def _swa_sample_kernel(*refs, dec, carried):
    if carried:
        sinks_ref, q_ref, kn_ref, vn_ref, kb_ref, vb_ref, _, _, o_ref, ko_ref, vo_ref = refs
    else:
        sinks_ref, q_ref, kn_ref, vn_ref, kb_ref, vb_ref, o_ref, ko_ref, vo_ref = refs
    nseq = kb_ref.shape[1]
    steps_per_tile = LANES // (nseq * dec)
    base = (pl.program_id(0) % steps_per_tile) * (nseq * dec)
    grp = N_KV_HEADS * dec
    rows = GQA_GROUP * grp
    ri = lax.broadcasted_iota(jnp.int32, (rows, KV_DIM), 0)
    li = lax.broadcasted_iota(jnp.int32, (rows, KV_DIM), 1)
    head_mask = ((ri % grp) // dec) == (li // HEAD_DIM)
    tok = lax.broadcasted_iota(jnp.int32, (rows, WINDOW + LANES), 0) % dec
    col = lax.broadcasted_iota(jnp.int32, (rows, WINDOW + LANES), 1)
    cache_mask = (col < WINDOW) & (col > tok)
    lane = lax.broadcasted_iota(jnp.int32, (KV_DIM, WINDOW), 1)
    sink = jnp.concatenate(
        [jnp.full((dec, 1), sinks_ref[kv * GQA_GROUP + g] * LOG2E, F32)
         for g in range(GQA_GROUP) for kv in range(N_KV_HEADS)], axis=0)
    kn = kn_ref[...]
    vn = vn_ref[...]
    kn16 = kn.astype(BF16)
    vn16 = vn.astype(BF16)
    for s in range(nseq):
        off = base + s * dec
        new = col - (WINDOW + off)
        mask = cache_mask | ((new >= 0) & (new <= tok))
        qs = q_ref[pl.ds(s * dec, dec), :]
        qbd = jnp.concatenate(
            [jnp.concatenate([qs[:, g * KV_DIM:(g + 1) * KV_DIM]] * N_KV_HEADS, axis=0)
             for g in range(GQA_GROUP)], axis=0)
        qbd = jnp.where(head_mask, qbd, 0.0).astype(BF16)
        kb = kb_ref[0, s]
        vb = vb_ref[0, s]
        keys = jnp.concatenate([kb.astype(BF16), kn16], axis=1)
        vals = jnp.concatenate([vb.astype(BF16), vn16], axis=1)
        sc = jnp.where(mask, jnp.dot(qbd, keys, preferred_element_type=F32), NEG)
        p, inv_den = _softmax_rows(sc, sink)
        o = lax.dot_general(p, vals, _NT, preferred_element_type=F32) * inv_den
        o = jnp.where(head_mask, o, 0.0)
        folded = []
        for g in range(GQA_GROUP):
            og = o[g * grp:g * grp + dec, :]
            for kv in range(1, N_KV_HEADS):
                og = og + o[g * grp + kv * dec:g * grp + (kv + 1) * dec, :]
            folded.append(og)
        o_ref[pl.ds(s * dec, dec), :] = jnp.concatenate(folded, axis=1)
        shift_new = (WINDOW - dec + LANES - off) % LANES
        ko_ref[0, s] = jnp.where(lane < WINDOW - dec,
                                 pltpu.roll(kb, WINDOW - dec, axis=1), pltpu.roll(kn, shift_new, axis=1))
        vo_ref[0, s] = jnp.where(lane < WINDOW - dec,
                                 pltpu.roll(vb, WINDOW - dec, axis=1), pltpu.roll(vn, shift_new, axis=1))


def _swa_sample(sinks, q, kt_new, vt_new, kbuf, vbuf, j, carry, *, dec):
    nseq = kbuf.shape[1]
    sb = SEQ_BLOCK
    steps_per_tile = LANES // (sb * dec)
    row = lambda i: (i, 0)
    tile = lambda i: (0, i // steps_per_tile)
    cache = lambda i: (j, i, 0, 0)
    in_specs = [
        pl.BlockSpec(memory_space=pltpu.SMEM),
        pl.BlockSpec((sb * dec, CONV_CH), row),
        pl.BlockSpec((KV_DIM, LANES), tile),
        pl.BlockSpec((KV_DIM, LANES), tile),
        pl.BlockSpec((1, sb, KV_DIM, WINDOW), cache),
        pl.BlockSpec((1, sb, KV_DIM, WINDOW), cache),
    ]
    args = [sinks, q, kt_new, vt_new, kbuf, vbuf]
    aliases = {}
    if carry is not None:
        aliases = {len(args): 1, len(args) + 1: 2}
        in_specs += [pl.BlockSpec(memory_space=pl.ANY)] * 2
        args += list(carry)
    return pl.pallas_call(
        functools.partial(_swa_sample_kernel, dec=dec, carried=carry is not None),
        grid=(nseq // sb,),
        in_specs=in_specs,
        out_specs=[
            pl.BlockSpec((sb * dec, CONV_CH), row),
            pl.BlockSpec((1, sb, KV_DIM, WINDOW), cache),
            pl.BlockSpec((1, sb, KV_DIM, WINDOW), cache),
        ],
        out_shape=[
            jax.ShapeDtypeStruct(q.shape, F32),
            jax.ShapeDtypeStruct(kbuf.shape, F32),
            jax.ShapeDtypeStruct(vbuf.shape, F32),
        ],
        input_output_aliases=aliases,
        name="swa_sample",
    )(*args)


def _xattn_prompt_kernel(q_ref, mk_ref, mv_ref, o_ref):
    tq = q_ref.shape[0]
    mk = mk_ref[0, 0].astype(BF16)
    mv = mv_ref[0, 0].astype(BF16)
    low = lax.broadcasted_iota(jnp.int32, (WINDOW, PAIR), 1) < HEAD_DIM
    for rb in range(tq // WINDOW):
        r0 = rb * WINDOW
        slabs = []
        for pr in range(X_DIM // PAIR):
            lhs = _pair_lhs([q_ref[r0:r0 + WINDOW, pr * PAIR:(pr + 1) * PAIR]])
            s_all = jnp.dot(lhs, mk[pr * PAIR:(pr + 1) * PAIR], preferred_element_type=F32)
            p0, inv0 = _softmax_rows(s_all[:WINDOW])
            p1, inv1 = _softmax_rows(s_all[WINDOW:])
            res = lax.dot_general(jnp.concatenate([p0, p1], axis=0), mv[pr * PAIR:(pr + 1) * PAIR], _NT,
                                  preferred_element_type=F32)
            slabs.append(jnp.where(low, res[:WINDOW] * inv0, res[WINDOW:] * inv1))
        o_ref[r0:r0 + WINDOW, :] = jnp.concatenate(slabs, axis=1).astype(o_ref.dtype)


def _xattn_prompt(qx, mk, mv, layer, *, batch, seq):
    tq = TOKEN_TILE
    nt = seq // tq
    n_mem = mk.shape[3]
    mem = lambda b, j: (layer, b, 0, 0)
    return pl.pallas_call(
        _xattn_prompt_kernel,
        grid=(batch, nt),
        in_specs=[
            pl.BlockSpec((tq, X_DIM), lambda b, j: (b * nt + j, 0)),
            pl.BlockSpec((1, 1, X_DIM, n_mem), mem),
            pl.BlockSpec((1, 1, X_DIM, n_mem), mem),
        ],
        out_specs=pl.BlockSpec((tq, X_DIM), lambda b, j: (b * nt + j, 0)),
        out_shape=jax.ShapeDtypeStruct(qx.shape, BF16),
        name="xattn_prompt",
    )(qx, mk, mv)


def _xattn_sample_kernel(q_ref, mk_ref, mv_ref, o_ref, *, dec):
    nseq = mk_ref.shape[1]
    rows = N_X_HEADS * dec
    head_mask = (lax.broadcasted_iota(jnp.int32, (rows, X_DIM), 0) // dec) == (
        lax.broadcasted_iota(jnp.int32, (rows, X_DIM), 1) // HEAD_DIM)
    for s in range(nseq):
        qs = q_ref[pl.ds(s * dec, dec), :]
        qbd = jnp.where(head_mask, jnp.concatenate([qs] * N_X_HEADS, axis=0), 0.0).astype(BF16)
        sc = jnp.dot(qbd, mk_ref[0, s].astype(BF16), preferred_element_type=F32)
        p, inv_den = _softmax_rows(sc)
        o = lax.dot_general(p, mv_ref[0, s].astype(BF16), _NT, preferred_element_type=F32) * inv_den
        o = jnp.where(head_mask, o, 0.0)
        acc = o[0:dec, :]
        for h in range(1, N_X_HEADS):
            acc = acc + o[h * dec:(h + 1) * dec, :]
        o_ref[pl.ds(s * dec, dec), :] = acc


def _xattn_sample(qx, mk, mv, layer, *, dec):
    nseq, n_mem = mk.shape[1], mk.shape[3]
    sb = 2 * SEQ_BLOCK
    mem = lambda i: (layer, i, 0, 0)
    return pl.pallas_call(
        functools.partial(_xattn_sample_kernel, dec=dec),
        grid=(nseq // sb,),
        in_specs=[
            pl.BlockSpec((sb * dec, X_DIM), lambda i: (i, 0)),
            pl.BlockSpec((1, sb, X_DIM, n_mem), mem),
            pl.BlockSpec((1, sb, X_DIM, n_mem), mem),
        ],
        out_specs=pl.BlockSpec((sb * dec, X_DIM), lambda i: (i, 0)),
        out_shape=jax.ShapeDtypeStruct(qx.shape, F32),
        name="xattn_sample",
    )(qx, mk, mv)


def _out_mlp_kernel(x_ref, os_ref, ox_ref, wo_ref, gm_ref, wu_ref, wd_ref, y_ref):
    o = jnp.concatenate([os_ref[...].astype(BF16), ox_ref[...].astype(BF16)], axis=1)
    x1 = x_ref[...] + jnp.dot(o, wo_ref[0], preferred_element_type=F32)
    hm = _rms_rows(x1, gm_ref[0]).astype(BF16)
    acc = x1
    for c in range(D_FF // FF_CHUNK):
        a = jnp.dot(hm, wu_ref[0, :, c * FF_CHUNK:(c + 1) * FF_CHUNK], preferred_element_type=F32)
        a = jnp.square(jnp.maximum(a, 0.0)).astype(BF16)
        acc = acc + jnp.dot(a, wd_ref[0, c * FF_CHUNK:(c + 1) * FF_CHUNK, :], preferred_element_type=F32)
    y_ref[...] = acc


def _out_mlp(x2d, o_self, o_x, layer, wo, gm, wu, wd):
    n = x2d.shape[0]
    tm = TOKEN_TILE
    row = lambda i: (i, 0)
    return pl.pallas_call(
        _out_mlp_kernel,
        grid=(n // tm,),
        in_specs=[
            pl.BlockSpec((tm, D_MODEL), row),
            pl.BlockSpec((tm, CONV_CH), row),
            pl.BlockSpec((tm, X_DIM), row),
            _layer_spec((CONV_CH + X_DIM, D_MODEL), layer),
            _layer_spec((1, D_MODEL), layer),
            _layer_spec((D_MODEL, D_FF), layer),
            _layer_spec((D_FF, D_MODEL), layer),
        ],
        out_specs=pl.BlockSpec((tm, D_MODEL), row),
        out_shape=jax.ShapeDtypeStruct((n, D_MODEL), F32),
        name="out_mlp",
    )(x2d, o_self, o_x, wo, gm, wu, wd)


def _rope_tables(pos):
    half = HEAD_DIM // 2
    inv = ROPE_THETA ** (-jnp.arange(half, dtype=F32) * 2.0 / HEAD_DIM)
    ang = pos[:, None] * inv[None, :]
    return jnp.cos(ang), jnp.sin(ang)


def _rope_lane_tables(cos, sin):
    reps = LANES // (HEAD_DIM // 2)
    return jnp.tile(cos, (1, reps)), jnp.concatenate([-sin, sin] * (reps // 2), axis=1)


def _group_major(w, axis):
    shape = w.shape
    w = w.reshape(shape[:axis] + (N_KV_HEADS, GQA_GROUP, HEAD_DIM) + shape[axis + 1:])
    return jnp.swapaxes(w, axis, axis + 1).reshape(shape)


def _feature_major(cache):
    lead = cache.shape[:-3]
    pos, heads, hd = cache.shape[-3:]
    nd = len(lead)
    perm = tuple(range(nd)) + (nd + 1, nd + 2, nd)
    return jnp.transpose(cache, perm).reshape(lead + (heads * hd, pos))


def _position_major(cache_t, heads):
    lead = cache_t.shape[:-2]
    pos = cache_t.shape[-1]
    nd = len(lead)
    perm = tuple(range(nd)) + (nd + 2, nd, nd + 1)
    return jnp.transpose(cache_t.reshape(lead + (heads, HEAD_DIM, pos)), perm)


def _gain_cols(g):
    return jnp.broadcast_to(g[:, :, None], g.shape + (LANES,))


def kernel(x_prompt, x_sample, mem_prompt, cache_swa_k, cache_swa_v, state_conv, cache_mem_k, cache_mem_v,
           norm_mix, w_in_att, q_norm_att, k_norm_att, sinks, w_in_conv, conv_w, norm_mem, w_mem_kv,
           q_norm_x, k_norm_x, w_out, norm_mlp, w_up, w_down):
    batch, seq, _ = x_prompt.shape
    nseq, dec, _ = x_sample.shape
    n_mem = mem_prompt.shape[1]
    assert seq % TOKEN_TILE == 0 and (nseq * dec) == TOKEN_TILE and nseq % SEQ_BLOCK == 0
    assert LANES % (SEQ_BLOCK * dec) == 0

    kv0, kv1 = CONV_CH, CONV_CH + 2 * KV_DIM
    w_q = jnp.concatenate([_group_major(w_in_att[:, :, :kv0], 2), w_in_att[:, :, kv1:]], axis=2).astype(BF16)
    w_kv_t = jnp.swapaxes(w_in_att[:, :, kv0:kv1], 1, 2).astype(BF16)
    w_conv = w_in_conv.astype(BF16)
    att_layers = jnp.arange(DEPTH) % 2 == 0
    w_o = jnp.concatenate(
        [jnp.where(att_layers[:, None, None], _group_major(w_out[:, :CONV_CH], 1), w_out[:, :CONV_CH]),
         w_out[:, CONV_CH:]], axis=1).astype(BF16)
    w_u = w_up.astype(BF16)
    w_d = w_down.astype(BF16)
    w_mkv_t = jnp.swapaxes(w_mem_kv, 1, 2).astype(BF16)

    g_mix = norm_mix[:, None, :]
    g_mlp = norm_mlp[:, None, :]
    gq = jnp.tile(q_norm_att, (1, N_Q_HEADS))[:, None, :]
    gx = jnp.tile(q_norm_x, (1, N_X_HEADS))[:, None, :]
    gk_cols = _gain_cols(k_norm_att)
    gkx_cols = _gain_cols(k_norm_x)

    cos_p, sin_p = _rope_tables(jnp.arange(seq, dtype=F32))
    cos_s, sin_s = _rope_tables(PAST_LEN + (jnp.arange(nseq * dec) % dec).astype(F32))
    rope_p = _rope_lane_tables(cos_p, sin_p) + (cos_p.T, sin_p.T)
    rope_s = _rope_lane_tables(cos_s, sin_s) + (cos_s.T, sin_s.T)

    mk_p, mv_p = _mem_kv(mem_prompt.reshape(batch * n_mem, D_MODEL), norm_mem[:, None, :], w_mkv_t, gkx_cols,
                         batch=batch, n_mem=n_mem)
    mk_s = _feature_major(cache_mem_k)
    mv_s = _feature_major(cache_mem_v)
    kbuf = _feature_major(cache_swa_k)
    vbuf = _feature_major(cache_swa_v)

    xp = x_prompt.reshape(batch * seq, D_MODEL)
    xs = x_sample.reshape(nseq * dec, D_MODEL)
    tiles_per_seq = seq // TOKEN_TILE
    k_p, v_p, c_p, c_s = [], [], [], []
    cache_s = None
    for i in range(DEPTH):
        j = i // 2
        if i % 2 == 0:
            q, qx, kt, vt, kt32, vt32 = _in_att(xp, i, j, g_mix, w_q, w_kv_t, gq, gx, gk_cols, *rope_p,
                                                 tiles_per_seq=tiles_per_seq, keep_cols=WINDOW, act_dtype=BF16)
            k_p.append(kt32)
            v_p.append(vt32)
            o_self = _swa_prompt(sinks[j], q, kt, vt, batch=batch, seq=seq)
        else:
            o_self, qx, tail = _in_conv(xp, i, j, g_mix, w_conv, gx, conv_w, None,
                                        tiles_per_seq=tiles_per_seq, period=0,
                                        keep_rows=SUBLANES, act_dtype=BF16)
            c_p.append(tail.reshape(batch, SUBLANES, CONV_CH)[:, SUBLANES - (CONV_W - 1):])
        o_x = _xattn_prompt(qx, mk_p, mv_p, i, batch=batch, seq=seq)
        xp = _out_mlp(xp, o_self, o_x, i, w_o, g_mlp, w_u, w_d)

        if i % 2 == 0:
            q, qx, kt, vt = _in_att(xs, i, j, g_mix, w_q, w_kv_t, gq, gx, gk_cols, *rope_s,
                                    tiles_per_seq=1, keep_cols=0, act_dtype=F32)
            o_self, *cache_s = _swa_sample(sinks[j], q, kt, vt, kbuf, vbuf, j, cache_s, dec=dec)
        else:
            prefix = jnp.pad(state_conv[j], ((0, 0), (0, dec - (CONV_W - 1)), (0, 0)))
            o_self, qx, tail = _in_conv(xs, i, j, g_mix, w_conv, gx, conv_w,
                                        prefix.reshape(nseq * dec, CONV_CH),
                                        tiles_per_seq=1, period=dec,
                                        keep_rows=nseq * dec, act_dtype=F32)
            c_s.append(tail.reshape(nseq, dec, CONV_CH)[:, dec - (CONV_W - 1):])
        o_x = _xattn_sample(qx, mk_s, mv_s, i, dec=dec)
        xs = _out_mlp(xs, o_self, o_x, i, w_o, g_mlp, w_u, w_d)

    return (xp.reshape(batch, seq, D_MODEL),
            xs.reshape(nseq, dec, D_MODEL),
            _position_major(jnp.stack(k_p), N_KV_HEADS),
            _position_major(jnp.stack(v_p), N_KV_HEADS),
            jnp.stack(c_p),
            _position_major(mk_p, N_X_HEADS),
            _position_major(mv_p, N_X_HEADS),
            _position_major(cache_s[0], N_KV_HEADS),
            _position_major(cache_s[1], N_KV_HEADS),
            jnp.stack(c_s))
```

```python
import functools

import jax
import jax.numpy as jnp
from jax import lax
from jax.experimental import pallas as pl
from jax.experimental.pallas import tpu as pltpu

F32 = jnp.float32
BF16 = jnp.bfloat16

D_MODEL = 1024
DEPTH = 4
HEAD_DIM = 64
N_Q_HEADS = 12
N_KV_HEADS = 4
GQA_GROUP = N_Q_HEADS // N_KV_HEADS
WINDOW = 128
PAST_LEN = 8192
ROPE_THETA = 10000.0
CONV_CH = N_Q_HEADS * HEAD_DIM
CONV_W = 3
N_X_HEADS = 4
X_DIM = N_X_HEADS * HEAD_DIM
KV_DIM = N_KV_HEADS * HEAD_DIM
D_FF = 4 * D_MODEL
CONV_IN = 3 * CONV_CH + X_DIM
EPS = 1e-6
NEG = -1e30
LOG2E = 1.4426950408889634
Q_SCALE = HEAD_DIM ** -0.5 * LOG2E

LANES = 128
SUBLANES = 8
MXU_DIM = 256
TOKEN_TILE = 512
FF_CHUNK = 2048
SEQ_BLOCK = 16
SEQ_GROUP = 16
PAIR = 2 * HEAD_DIM

_NT = (((1,), (1,)), ((), ()))


def _const_spec(shape):
    nd = len(shape)
    return pl.BlockSpec(shape, lambda *_: (0,) * nd, pipeline_mode=pl.Buffered(1))


def _layer_spec(shape, layer):
    nd = len(shape)
    return pl.BlockSpec((1,) + shape, lambda *_: (layer,) + (0,) * nd, pipeline_mode=pl.Buffered(1))


def _rms_rows(x, g):
    return x * lax.rsqrt(jnp.mean(x * x, axis=-1, keepdims=True) + EPS) * g


def _head_blockdiag():
    r = lax.broadcasted_iota(jnp.int32, (MXU_DIM, MXU_DIM), 0) // HEAD_DIM
    c = lax.broadcasted_iota(jnp.int32, (MXU_DIM, MXU_DIM), 1) // HEAD_DIM
    return jnp.where(r == c, 1.0, 0.0).astype(BF16)


def _head_mean_sq(z, bd):
    sq = z * z
    hi = sq.astype(BF16)
    lo = (sq - hi.astype(F32)).astype(BF16)
    parts = []
    for c in range(z.shape[1] // MXU_DIM):
        sl = slice(c * MXU_DIM, (c + 1) * MXU_DIM)
        parts.append(jnp.dot(hi[:, sl], bd, preferred_element_type=F32)
                     + jnp.dot(lo[:, sl], bd, preferred_element_type=F32))
    ms = parts[0] if len(parts) == 1 else jnp.concatenate(parts, axis=1)
    return ms * (1.0 / HEAD_DIM)


def _rope_rows(x, cos, sin_signed):
    t = x.shape[0]
    first_half = (lax.broadcasted_iota(jnp.int32, (t, LANES), 1) & (HEAD_DIM - 1)) < HEAD_DIM // 2
    parts = []
    for c in range(x.shape[1] // LANES):
        xc = x[:, c * LANES:(c + 1) * LANES]
        partner = jnp.where(first_half,
                            pltpu.roll(xc, LANES - HEAD_DIM // 2, axis=1),
                            pltpu.roll(xc, HEAD_DIM // 2, axis=1))
        parts.append(xc * cos + partner * sin_signed)
    return jnp.concatenate(parts, axis=1)


def _lane_tile(col, n):
    return col if n == LANES else jnp.concatenate([col] * (n // LANES), axis=1)


def _head_norm_cols(xt, g_col):
    t = xt.shape[1]
    g = _lane_tile(g_col, t)
    parts = []
    for h in range(xt.shape[0] // HEAD_DIM):
        blk = xt[h * HEAD_DIM:(h + 1) * HEAD_DIM, :]
        ms = jnp.mean(blk * blk, axis=0, keepdims=True)
        parts.append(blk * lax.rsqrt(ms + EPS) * g)
    return parts


def _softmax_rows(s, sink=None):
    m = jnp.max(s, axis=-1, keepdims=True)
    if sink is not None:
        m = jnp.maximum(m, sink)
    p = jnp.exp2(s - m)
    den = jnp.sum(p, axis=-1, keepdims=True)
    if sink is not None:
        den = den + jnp.exp2(sink - m)
    return p.astype(BF16), 1.0 / den


def _mem_kv_kernel(mem_ref, g_ref, wt_ref, gk_ref, mk_ref, mv_ref):
    x = mem_ref[...]
    xn = x * lax.rsqrt(jnp.mean(x * x, axis=-1, keepdims=True) + EPS)
    for i in range(DEPTH):
        h = (xn * g_ref[i]).astype(BF16)
        kvt = lax.dot_general(wt_ref[i], h, _NT, preferred_element_type=F32)
        mk_ref[i, 0] = jnp.concatenate(_head_norm_cols(kvt[:X_DIM], gk_ref[i]), axis=0)
        mv_ref[i, 0] = kvt[X_DIM:]


def _mem_kv(mem2d, norm_mem, w_mem_kv_t, gk_cols, *, batch, n_mem):
    out = jax.ShapeDtypeStruct((DEPTH, batch, X_DIM, n_mem), F32)
    return pl.pallas_call(
        _mem_kv_kernel,
        grid=(batch,),
        in_specs=[
            pl.BlockSpec((n_mem, D_MODEL), lambda b: (b, 0)),
            _const_spec((DEPTH, 1, D_MODEL)),
            _const_spec((DEPTH, 2 * X_DIM, D_MODEL)),
            _const_spec((DEPTH, HEAD_DIM, LANES)),
        ],
        out_specs=[pl.BlockSpec((DEPTH, 1, X_DIM, n_mem), lambda b: (0, b, 0, 0))] * 2,
        out_shape=[out, out],
        name="mem_kv",
    )(mem2d, norm_mem, w_mem_kv_t, gk_cols)


def _in_att_kernel(*refs, keep_cols):
    (x_ref, g_ref, wq_ref, wkv_ref, gq_ref, gx_ref, gk_ref, cos_ref, sin_ref, cost_ref, sint_ref,
     q_ref, qx_ref, kt_ref, vt_ref) = refs[:15]
    t = x_ref.shape[0]
    h = _rms_rows(x_ref[...], g_ref[0]).astype(BF16)
    bd = _head_blockdiag()
    z = jnp.dot(h, wq_ref[0], preferred_element_type=F32)
    q = z[:, :CONV_CH]
    q = q * lax.rsqrt(_head_mean_sq(q, bd) + EPS) * gq_ref[0]
    q_ref[...] = (_rope_rows(q, cos_ref[...], sin_ref[...]) * Q_SCALE).astype(q_ref.dtype)
    qx = z[:, CONV_CH:]
    qx = qx * lax.rsqrt(_head_mean_sq(qx, bd) + EPS) * gx_ref[0]
    qx_ref[...] = (qx * Q_SCALE).astype(qx_ref.dtype)

    kvt = lax.dot_general(wkv_ref[0], h, _NT, preferred_element_type=F32)
    cos_t, sin_t = cost_ref[...], sint_ref[...]
    half = HEAD_DIM // 2
    k_parts = []
    for blk in _head_norm_cols(kvt[:KV_DIM], gk_ref[0]):
        x1, x2 = blk[:half], blk[half:]
        k_parts += [x1 * cos_t - x2 * sin_t, x2 * cos_t + x1 * sin_t]
    kt = jnp.concatenate(k_parts, axis=0)
    vt = kvt[KV_DIM:]
    kt_ref[...] = kt.astype(kt_ref.dtype)
    vt_ref[...] = vt.astype(vt_ref.dtype)
    if keep_cols:
        kt32_ref, vt32_ref = refs[15:]
        kt32_ref[0] = kt[:, t - keep_cols:]
        vt32_ref[0] = vt[:, t - keep_cols:]


def _in_att(x2d, layer, j, g_mix, wq, wkv_t, gq, gx, gk_cols, cos, sin, cos_t, sin_t,
            *, tiles_per_seq, keep_cols, act_dtype):
    n = x2d.shape[0]
    tm = TOKEN_TILE
    nt = n // tm
    nseq = nt // tiles_per_seq
    row = lambda i: (i, 0)
    col = lambda i: (0, i)
    pos = lambda i: (i % tiles_per_seq, 0)
    pos_t = lambda i: (0, i % tiles_per_seq)
    out_specs = [
        pl.BlockSpec((tm, CONV_CH), row),
        pl.BlockSpec((tm, X_DIM), row),
        pl.BlockSpec((KV_DIM, tm), col),
        pl.BlockSpec((KV_DIM, tm), col),
    ]
    out_shape = [
        jax.ShapeDtypeStruct((n, CONV_CH), act_dtype),
        jax.ShapeDtypeStruct((n, X_DIM), act_dtype),
        jax.ShapeDtypeStruct((KV_DIM, n), act_dtype),
        jax.ShapeDtypeStruct((KV_DIM, n), act_dtype),
    ]
    if keep_cols:
        out_specs += [pl.BlockSpec((1, KV_DIM, keep_cols), lambda i: (i // tiles_per_seq, 0, 0))] * 2
        out_shape += [jax.ShapeDtypeStruct((nseq, KV_DIM, keep_cols), F32)] * 2
    return pl.pallas_call(
        functools.partial(_in_att_kernel, keep_cols=keep_cols),
        grid=(nt,),
        in_specs=[
            pl.BlockSpec((tm, D_MODEL), row),
            _layer_spec((1, D_MODEL), layer),
            _layer_spec((D_MODEL, CONV_CH + X_DIM), j),
            _layer_spec((2 * KV_DIM, D_MODEL), j),
            _layer_spec((1, CONV_CH), j),
            _layer_spec((1, X_DIM), layer),
            _layer_spec((HEAD_DIM, LANES), j),
            pl.BlockSpec((tm, LANES), pos),
            pl.BlockSpec((tm, LANES), pos),
            pl.BlockSpec((HEAD_DIM // 2, tm), pos_t),
            pl.BlockSpec((HEAD_DIM // 2, tm), pos_t),
        ],
        out_specs=out_specs,
        out_shape=out_shape,
        name="in_att",
    )(x2d, g_mix, wq, wkv_t, gq, gx, gk_cols, cos, sin, cos_t, sin_t)


def _in_conv_kernel(*refs, tiles_per_seq, period):
    if period:
        (x_ref, g_ref, w_ref, gx_ref, cw_ref, prefix_ref,
         o_ref, qx_ref, gt_ref, gbuf, pbuf) = refs
    else:
        (x_ref, g_ref, w_ref, gx_ref, cw_ref,
         o_ref, qx_ref, gt_ref, gbuf) = refs
    t = x_ref.shape[0]
    keep = gt_ref.shape[0]
    h = _rms_rows(x_ref[...], g_ref[0]).astype(BF16)

    def proj(lo, width):
        return jnp.dot(h, w_ref[0, :, lo:lo + width], preferred_element_type=F32)

    @pl.when(pl.program_id(0) % tiles_per_seq == 0)
    def _():
        gbuf[0:SUBLANES, :] = jnp.zeros((SUBLANES, CONV_CH), F32)

    if period:
        pbuf[0:t, :] = prefix_ref[...]
        pbuf[t:t + SUBLANES, :] = jnp.zeros((SUBLANES, CONV_CH), F32)
        tok = lax.broadcasted_iota(jnp.int32, (t, MXU_DIM), 0) % period
    cw = cw_ref[0]
    for c in range(CONV_CH // MXU_DIM):
        sl = slice(c * MXU_DIM, (c + 1) * MXU_DIM)
        gate_b = proj(sl.start, MXU_DIM)
        u = proj(CONV_CH + sl.start, MXU_DIM) * proj(2 * CONV_CH + sl.start, MXU_DIM)
        gbuf[SUBLANES:SUBLANES + t, sl] = u
        back1 = gbuf[SUBLANES - 1:SUBLANES - 1 + t, sl]
        back2 = gbuf[SUBLANES - 2:SUBLANES - 2 + t, sl]
        if period:
            back1 = jnp.where(tok >= 1, back1, pbuf[1:1 + t, sl])
            back2 = jnp.where(tok >= 2, back2, pbuf[0:t, sl])
        y = back2 * cw[0:1, sl] + back1 * cw[1:2, sl] + u * cw[2:3, sl]
        o_ref[:, sl] = (gate_b * y).astype(o_ref.dtype)
        gt_ref[:, sl] = u[t - keep:, :]
    gbuf[0:SUBLANES, :] = gbuf[t:t + SUBLANES, :]
    qx = proj(3 * CONV_CH, X_DIM)
    qx = qx * lax.rsqrt(_head_mean_sq(qx, _head_blockdiag()) + EPS) * gx_ref[0]
    qx_ref[...] = (qx * Q_SCALE).astype(qx_ref.dtype)


def _in_conv(x2d, layer, j, g_mix, w, gx, cw, prefix, *, tiles_per_seq, period, keep_rows, act_dtype):
    n = x2d.shape[0]
    tm = TOKEN_TILE
    nt = n // tm
    nseq = nt // tiles_per_seq
    row = lambda i: (i, 0)
    in_specs = [
        pl.BlockSpec((tm, D_MODEL), row),
        _layer_spec((1, D_MODEL), layer),
        _layer_spec((D_MODEL, CONV_IN), j),
        _layer_spec((1, X_DIM), layer),
        _layer_spec((CONV_W, CONV_CH), j),
    ]
    args = [x2d, g_mix, w, gx, cw]
    scratch = [pltpu.VMEM((tm + 2 * SUBLANES, CONV_CH), F32)]
    if period:
        in_specs.append(pl.BlockSpec((tm, CONV_CH), row))
        args.append(prefix)
        scratch.append(pltpu.VMEM((tm + SUBLANES, CONV_CH), F32))
    return pl.pallas_call(
        functools.partial(_in_conv_kernel, tiles_per_seq=tiles_per_seq, period=period),
        grid=(nt,),
        in_specs=in_specs,
        out_specs=[
            pl.BlockSpec((tm, CONV_CH), row),
            pl.BlockSpec((tm, X_DIM), row),
            pl.BlockSpec((keep_rows, CONV_CH), lambda i: (i // tiles_per_seq, 0)),
        ],
        out_shape=[
            jax.ShapeDtypeStruct((n, CONV_CH), act_dtype),
            jax.ShapeDtypeStruct((n, X_DIM), act_dtype),
            jax.ShapeDtypeStruct((nseq * keep_rows, CONV_CH), F32),
        ],
        scratch_shapes=scratch,
        name="in_conv",
    )(*args)


def _pair_lhs(slabs):
    low = lax.broadcasted_iota(jnp.int32, slabs[0].shape, 1) < HEAD_DIM
    zero = jnp.zeros_like(slabs[0])
    parts = []
    for s in slabs:
        parts += [jnp.where(low, s, zero), jnp.where(low, zero, s)]
    return jnp.concatenate(parts, axis=0)


def _swa_prompt_kernel(sinks_ref, q_ref, kc_ref, kp_ref, vc_ref, vp_ref, o_ref, kfull, vfull, bias):
    tq = q_ref.shape[0]

    @pl.when(jnp.logical_and(pl.program_id(0) == 0, pl.program_id(1) == 0))
    def _():
        r = lax.broadcasted_iota(jnp.int32, (WINDOW, 2 * WINDOW), 0)
        c = lax.broadcasted_iota(jnp.int32, (WINDOW, 2 * WINDOW), 1)
        band = (c > r) & (c - WINDOW <= r)
        for kind, visible in enumerate((band, band & (c >= WINDOW))):
            base = jnp.where(visible, 0.0, NEG)
            for h in range(N_Q_HEADS):
                bias[kind, h] = jnp.where(c == 0, sinks_ref[h] * LOG2E, base)

    first_kind = jnp.where(pl.program_id(1) == 0, 1, 0)
    kfull[:, 0:WINDOW] = kp_ref[...]
    kfull[:, WINDOW:] = kc_ref[...]
    vfull[:, 0:WINDOW] = vp_ref[...]
    vfull[:, WINDOW:] = vc_ref[...]
    col0 = lax.broadcasted_iota(jnp.int32, (PAIR, 2 * WINDOW), 1) == 0
    zero_slab = jnp.zeros((PAIR, 2 * WINDOW), BF16)
    low = lax.broadcasted_iota(jnp.int32, (WINDOW, PAIR), 1) < HEAD_DIM
    for qb in range(tq // WINDOW):
        r0 = qb * WINDOW
        kind = first_kind if qb == 0 else 0
        slabs = [None] * (GQA_GROUP * KV_DIM // PAIR)
        for pr in range(KV_DIM // PAIR):
            kslab = jnp.where(col0, zero_slab, kfull[pr * PAIR:(pr + 1) * PAIR, r0:r0 + 2 * WINDOW])
            vslab = jnp.where(col0, zero_slab, vfull[pr * PAIR:(pr + 1) * PAIR, r0:r0 + 2 * WINDOW])
            lhs = _pair_lhs([q_ref[r0:r0 + WINDOW, g * KV_DIM + pr * PAIR:g * KV_DIM + (pr + 1) * PAIR]
                             for g in range(GQA_GROUP)])
            s_all = jnp.dot(lhs, kslab, preferred_element_type=F32)
            probs, inv = [], []
            for g in range(GQA_GROUP):
                for e in range(2):
                    idx = g * 2 + e
                    head = (2 * pr + e) * GQA_GROUP + g
                    p, inv_den = _softmax_rows(s_all[idx * WINDOW:(idx + 1) * WINDOW] + bias[kind, head])
                    probs.append(p)
                    inv.append(inv_den)
            res = lax.dot_general(jnp.concatenate(probs, axis=0), vslab, _NT,
                                  preferred_element_type=F32)
            for g in range(GQA_GROUP):
                lo = res[(2 * g) * WINDOW:(2 * g + 1) * WINDOW] * inv[2 * g]
                hi = res[(2 * g + 1) * WINDOW:(2 * g + 2) * WINDOW] * inv[2 * g + 1]
                slabs[g * (KV_DIM // PAIR) + pr] = jnp.where(low, lo, hi)
        o_ref[r0:r0 + WINDOW, :] = jnp.concatenate(slabs, axis=1).astype(o_ref.dtype)


def _swa_prompt(sinks, q, kt, vt, *, batch, seq):
    tq = TOKEN_TILE
    nt = seq // tq
    per = tq // WINDOW
    rows = lambda b, j: (b * nt + j, 0)
    cur = lambda b, j: (0, b * nt + j)
    prev = lambda b, j: (0, b * nt * per + jnp.maximum(j * per - 1, 0))
    return pl.pallas_call(
        _swa_prompt_kernel,
        grid=(batch, nt),
        in_specs=[
            pl.BlockSpec(memory_space=pltpu.SMEM),
            pl.BlockSpec((tq, CONV_CH), rows),
            pl.BlockSpec((KV_DIM, tq), cur),
            pl.BlockSpec((KV_DIM, WINDOW), prev),
            pl.BlockSpec((KV_DIM, tq), cur),
            pl.BlockSpec((KV_DIM, WINDOW), prev),
        ],
        out_specs=pl.BlockSpec((tq, CONV_CH), rows),
        out_shape=jax.ShapeDtypeStruct(q.shape, BF16),
        scratch_shapes=[pltpu.VMEM((KV_DIM, tq + WINDOW), BF16)] * 2
        + [pltpu.VMEM((2, N_Q_HEADS, WINDOW, 2 * WINDOW), F32)],
        name="swa_prompt",
    )(sinks, q, kt, kt, vt, vt)


def _swa_sample_kernel(*refs, dec, carried, slot):
    if carried:
        sinks_ref, q_ref, kn_ref, vn_ref, kb_ref, vb_ref, _, _, o_ref, ko_ref, vo_ref = refs
    else:
        sinks_ref, q_ref, kn_ref, vn_ref, kb_ref, vb_ref, o_ref, ko_ref, vo_ref = refs
    nseq = kb_ref.shape[1]
    for other in range(ko_ref.shape[0]):
        if other != slot:
            ko_ref[other] = jnp.zeros(ko_ref.shape[1:], F32)
            vo_ref[other] = jnp.zeros(vo_ref.shape[1:], F32)
    steps_per_tile = LANES // (nseq * dec)
    base = (pl.program_id(0) % steps_per_tile) * (nseq * dec)
    grp = N_KV_HEADS * dec
    rows = GQA_GROUP * grp
    ri = lax.broadcasted_iota(jnp.int32, (rows, KV_DIM), 0)
    li = lax.broadcasted_iota(jnp.int32, (rows, KV_DIM), 1)
    head_mask = ((ri % grp) // dec) == (li // HEAD_DIM)
    tok = lax.broadcasted_iota(jnp.int32, (rows, WINDOW + LANES), 0) % dec
    col = lax.broadcasted_iota(jnp.int32, (rows, WINDOW + LANES), 1)
    cache_mask = (col < WINDOW) & (col > tok)
    lane = lax.broadcasted_iota(jnp.int32, (KV_DIM, WINDOW), 1)
    sink = jnp.concatenate(
        [jnp.full((dec, 1), sinks_ref[kv * GQA_GROUP + g] * LOG2E, F32)
         for g in range(GQA_GROUP) for kv in range(N_KV_HEADS)], axis=0)
    kn = kn_ref[...]
    vn = vn_ref[...]
    kn16 = kn.astype(BF16)
    vn16 = vn.astype(BF16)
    for s0 in range(0, nseq, SEQ_GROUP):
        seqs = range(s0, s0 + SEQ_GROUP)
        scores, values = [], []
        for s in seqs:
            off = base + s * dec
            new = col - (WINDOW + off)
            mask = cache_mask | ((new >= 0) & (new <= tok))
            qs = q_ref[pl.ds(s * dec, dec), :]
            qbd = jnp.concatenate(
                [jnp.concatenate([qs[:, g * KV_DIM:(g + 1) * KV_DIM]] * N_KV_HEADS, axis=0)
                 for g in range(GQA_GROUP)], axis=0)
            qbd = jnp.where(head_mask, qbd, 0.0).astype(BF16)
            keys = jnp.concatenate([kb_ref[0, s].astype(BF16), kn16], axis=1)
            values.append(jnp.concatenate([vb_ref[0, s].astype(BF16), vn16], axis=1))
            scores.append(jnp.where(mask, jnp.dot(qbd, keys, preferred_element_type=F32), NEG))
        probs = [_softmax_rows(sc, sink) for sc in scores]
        for s, (p, inv_den), vals in zip(seqs, probs, values):
            o = lax.dot_general(p, vals, _NT, preferred_element_type=F32) * inv_den
            o = jnp.where(head_mask, o, 0.0)
            folded = []
            for g in range(GQA_GROUP):
                og = o[g * grp:g * grp + dec, :]
                for kv in range(1, N_KV_HEADS):
                    og = og + o[g * grp + kv * dec:g * grp + (kv + 1) * dec, :]
                folded.append(og)
            o_ref[pl.ds(s * dec, dec), :] = jnp.concatenate(folded, axis=1)
        for s in seqs:
            shift_new = (WINDOW - dec + LANES - (base + s * dec)) % LANES
            ko_ref[slot, s] = jnp.where(lane < WINDOW - dec, pltpu.roll(kb_ref[0, s], WINDOW - dec, axis=1),
                                     pltpu.roll(kn, shift_new, axis=1))
            vo_ref[slot, s] = jnp.where(lane < WINDOW - dec, pltpu.roll(vb_ref[0, s], WINDOW - dec, axis=1),
                                     pltpu.roll(vn, shift_new, axis=1))


def _swa_sample(sinks, q, kt_new, vt_new, kbuf, vbuf, j, carry, *, dec):
    nseq = kbuf.shape[1]
    sb = SEQ_BLOCK
    steps_per_tile = LANES // (sb * dec)
    row = lambda i: (i, 0)
    tile = lambda i: (0, i // steps_per_tile)
    cache = lambda i: (j, i, 0, 0)
    if carry is None:
        new_cache = pl.BlockSpec((kbuf.shape[0], sb, KV_DIM, WINDOW), lambda i: (0, i, 0, 0))
    else:
        new_cache = pl.BlockSpec((1, sb, KV_DIM, WINDOW), cache)
    in_specs = [
        pl.BlockSpec(memory_space=pltpu.SMEM),
        pl.BlockSpec((sb * dec, CONV_CH), row),
        pl.BlockSpec((KV_DIM, LANES), tile),
        pl.BlockSpec((KV_DIM, LANES), tile),
        pl.BlockSpec((1, sb, KV_DIM, WINDOW), cache),
        pl.BlockSpec((1, sb, KV_DIM, WINDOW), cache),
    ]
    args = [sinks, q, kt_new, vt_new, kbuf, vbuf]
    aliases = {}
    if carry is not None:
        aliases = {len(args): 1, len(args) + 1: 2}
        in_specs += [pl.BlockSpec(memory_space=pl.ANY)] * 2
        args += list(carry)
    return pl.pallas_call(
        functools.partial(_swa_sample_kernel, dec=dec, carried=carry is not None,
                          slot=j if carry is None else 0),
        grid=(nseq // sb,),
        in_specs=in_specs,
        out_specs=[
            pl.BlockSpec((sb * dec, CONV_CH), row),
            new_cache,
            new_cache,
        ],
        out_shape=[
            jax.ShapeDtypeStruct(q.shape, F32),
            jax.ShapeDtypeStruct(kbuf.shape, F32),
            jax.ShapeDtypeStruct(vbuf.shape, F32),
        ],
        input_output_aliases=aliases,
        name="swa_sample",
    )(*args)


def _xattn_prompt_kernel(q_ref, mk_ref, mv_ref, o_ref):
    tq = q_ref.shape[0]
    mk = mk_ref[0, 0].astype(BF16)
    mv = mv_ref[0, 0].astype(BF16)
    low = lax.broadcasted_iota(jnp.int32, (WINDOW, PAIR), 1) < HEAD_DIM
    for rb in range(tq // WINDOW):
        r0 = rb * WINDOW
        slabs = []
        for pr in range(X_DIM // PAIR):
            lhs = _pair_lhs([q_ref[r0:r0 + WINDOW, pr * PAIR:(pr + 1) * PAIR]])
            s_all = jnp.dot(lhs, mk[pr * PAIR:(pr + 1) * PAIR], preferred_element_type=F32)
            p0, inv0 = _softmax_rows(s_all[:WINDOW])
            p1, inv1 = _softmax_rows(s_all[WINDOW:])
            res = lax.dot_general(jnp.concatenate([p0, p1], axis=0), mv[pr * PAIR:(pr + 1) * PAIR], _NT,
                                  preferred_element_type=F32)
            slabs.append(jnp.where(low, res[:WINDOW] * inv0, res[WINDOW:] * inv1))
        o_ref[r0:r0 + WINDOW, :] = jnp.concatenate(slabs, axis=1).astype(o_ref.dtype)


def _xattn_prompt(qx, mk, mv, layer, *, batch, seq):
    tq = TOKEN_TILE
    nt = seq // tq
    n_mem = mk.shape[3]
    mem = lambda b, j: (layer, b, 0, 0)
    return pl.pallas_call(
        _xattn_prompt_kernel,
        grid=(batch, nt),
        in_specs=[
            pl.BlockSpec((tq, X_DIM), lambda b, j: (b * nt + j, 0)),
            pl.BlockSpec((1, 1, X_DIM, n_mem), mem),
            pl.BlockSpec((1, 1, X_DIM, n_mem), mem),
        ],
        out_specs=pl.BlockSpec((tq, X_DIM), lambda b, j: (b * nt + j, 0)),
        out_shape=jax.ShapeDtypeStruct(qx.shape, BF16),
        name="xattn_prompt",
    )(qx, mk, mv)


def _xattn_sample_kernel(q_ref, mk_ref, mv_ref, o_ref, *, dec):
    nseq = mk_ref.shape[1]
    rows = N_X_HEADS * dec
    head_mask = (lax.broadcasted_iota(jnp.int32, (rows, X_DIM), 0) // dec) == (
        lax.broadcasted_iota(jnp.int32, (rows, X_DIM), 1) // HEAD_DIM)
    for s0 in range(0, nseq, SEQ_GROUP):
        seqs = range(s0, s0 + SEQ_GROUP)
        scores = []
        for s in seqs:
            qs = q_ref[pl.ds(s * dec, dec), :]
            qbd = jnp.where(head_mask, jnp.concatenate([qs] * N_X_HEADS, axis=0), 0.0).astype(BF16)
            scores.append(jnp.dot(qbd, mk_ref[0, s].astype(BF16), preferred_element_type=F32))
        probs = [_softmax_rows(sc) for sc in scores]
        for s, (p, inv_den) in zip(seqs, probs):
            o = lax.dot_general(p, mv_ref[0, s].astype(BF16), _NT, preferred_element_type=F32) * inv_den
            o = jnp.where(head_mask, o, 0.0)
            acc = o[0:dec, :]
            for h in range(1, N_X_HEADS):
                acc = acc + o[h * dec:(h + 1) * dec, :]
            o_ref[pl.ds(s * dec, dec), :] = acc


def _xattn_sample(qx, mk, mv, layer, *, dec):
    nseq, n_mem = mk.shape[1], mk.shape[3]
    sb = SEQ_BLOCK
    mem = lambda i: (layer, i, 0, 0)
    return pl.pallas_call(
        functools.partial(_xattn_sample_kernel, dec=dec),
        grid=(nseq // sb,),
        in_specs=[
            pl.BlockSpec((sb * dec, X_DIM), lambda i: (i, 0)),
            pl.BlockSpec((1, sb, X_DIM, n_mem), mem),
            pl.BlockSpec((1, sb, X_DIM, n_mem), mem),
        ],
        out_specs=pl.BlockSpec((sb * dec, X_DIM), lambda i: (i, 0)),
        out_shape=jax.ShapeDtypeStruct(qx.shape, F32),
        name="xattn_sample",
    )(qx, mk, mv)


def _out_mlp_kernel(x_ref, os_ref, ox_ref, wo_ref, gm_ref, wu_ref, wd_ref, y_ref):
    o = jnp.concatenate([os_ref[...].astype(BF16), ox_ref[...].astype(BF16)], axis=1)
    x1 = x_ref[...] + jnp.dot(o, wo_ref[0], preferred_element_type=F32)
    hm = _rms_rows(x1, gm_ref[0]).astype(BF16)
    acc = x1
    for c in range(D_FF // FF_CHUNK):
        a = jnp.dot(hm, wu_ref[0, :, c * FF_CHUNK:(c + 1) * FF_CHUNK], preferred_element_type=F32)
        a = jnp.square(jnp.maximum(a, 0.0)).astype(BF16)
        acc = acc + jnp.dot(a, wd_ref[0, c * FF_CHUNK:(c + 1) * FF_CHUNK, :], preferred_element_type=F32)
    y_ref[...] = acc


def _out_mlp(x2d, o_self, o_x, layer, wo, gm, wu, wd):
    n = x2d.shape[0]
    tm = TOKEN_TILE
    row = lambda i: (i, 0)
    return pl.pallas_call(
        _out_mlp_kernel,
        grid=(n // tm,),
        in_specs=[
            pl.BlockSpec((tm, D_MODEL), row),
            pl.BlockSpec((tm, CONV_CH), row),
            pl.BlockSpec((tm, X_DIM), row),
            _layer_spec((CONV_CH + X_DIM, D_MODEL), layer),
            _layer_spec((1, D_MODEL), layer),
            _layer_spec((D_MODEL, D_FF), layer),
            _layer_spec((D_FF, D_MODEL), layer),
        ],
        out_specs=pl.BlockSpec((tm, D_MODEL), row),
        out_shape=jax.ShapeDtypeStruct((n, D_MODEL), F32),
        name="out_mlp",
    )(x2d, o_self, o_x, wo, gm, wu, wd)


def _rope_tables(pos):
    half = HEAD_DIM // 2
    inv = ROPE_THETA ** (-jnp.arange(half, dtype=F32) * 2.0 / HEAD_DIM)
    ang = pos[:, None] * inv[None, :]
    return jnp.cos(ang), jnp.sin(ang)


def _rope_lane_tables(cos, sin):
    reps = LANES // (HEAD_DIM // 2)
    return jnp.tile(cos, (1, reps)), jnp.concatenate([-sin, sin] * (reps // 2), axis=1)


def _group_major(w, axis):
    shape = w.shape
    w = w.reshape(shape[:axis] + (N_KV_HEADS, GQA_GROUP, HEAD_DIM) + shape[axis + 1:])
    return jnp.swapaxes(w, axis, axis + 1).reshape(shape)


def _feature_major(cache):
    lead = cache.shape[:-3]
    pos, heads, hd = cache.shape[-3:]
    nd = len(lead)
    perm = tuple(range(nd)) + (nd + 1, nd + 2, nd)
    return jnp.transpose(cache, perm).reshape(lead + (heads * hd, pos))


def _position_major(cache_t, heads):
    lead = cache_t.shape[:-2]
    pos = cache_t.shape[-1]
    nd = len(lead)
    perm = tuple(range(nd)) + (nd + 2, nd, nd + 1)
    return jnp.transpose(cache_t.reshape(lead + (heads, HEAD_DIM, pos)), perm)


def _gain_cols(g):
    return jnp.broadcast_to(g[:, :, None], g.shape + (LANES,))


def kernel(x_prompt, x_sample, mem_prompt, cache_swa_k, cache_swa_v, state_conv, cache_mem_k, cache_mem_v,
           norm_mix, w_in_att, q_norm_att, k_norm_att, sinks, w_in_conv, conv_w, norm_mem, w_mem_kv,
           q_norm_x, k_norm_x, w_out, norm_mlp, w_up, w_down):
    batch, seq, _ = x_prompt.shape
    nseq, dec, _ = x_sample.shape
    n_mem = mem_prompt.shape[1]
    assert seq % TOKEN_TILE == 0 and (nseq * dec) == TOKEN_TILE and nseq % SEQ_BLOCK == 0
    assert LANES % (SEQ_BLOCK * dec) == 0

    kv0, kv1 = CONV_CH, CONV_CH + 2 * KV_DIM
    w_q = jnp.concatenate([_group_major(w_in_att[:, :, :kv0], 2), w_in_att[:, :, kv1:]], axis=2).astype(BF16)
    w_kv_t = jnp.swapaxes(w_in_att[:, :, kv0:kv1], 1, 2).astype(BF16)
    w_conv = w_in_conv.astype(BF16)
    att_layers = jnp.arange(DEPTH) % 2 == 0
    w_o = jnp.concatenate(
        [jnp.where(att_layers[:, None, None], _group_major(w_out[:, :CONV_CH], 1), w_out[:, :CONV_CH]),
         w_out[:, CONV_CH:]], axis=1).astype(BF16)
    w_u = w_up.astype(BF16)
    w_d = w_down.astype(BF16)
    w_mkv_t = jnp.swapaxes(w_mem_kv, 1, 2).astype(BF16)

    g_mix = norm_mix[:, None, :]
    g_mlp = norm_mlp[:, None, :]
    gq = jnp.tile(q_norm_att, (1, N_Q_HEADS))[:, None, :]
    gx = jnp.tile(q_norm_x, (1, N_X_HEADS))[:, None, :]
    gk_cols = _gain_cols(k_norm_att)
    gkx_cols = _gain_cols(k_norm_x)

    cos_p, sin_p = _rope_tables(jnp.arange(seq, dtype=F32))
    cos_s, sin_s = _rope_tables(PAST_LEN + (jnp.arange(nseq * dec) % dec).astype(F32))
    rope_p = _rope_lane_tables(cos_p, sin_p) + (cos_p.T, sin_p.T)
    rope_s = _rope_lane_tables(cos_s, sin_s) + (cos_s.T, sin_s.T)

    mk_p, mv_p = _mem_kv(mem_prompt.reshape(batch * n_mem, D_MODEL), norm_mem[:, None, :], w_mkv_t, gkx_cols,
                         batch=batch, n_mem=n_mem)
    mk_s = _feature_major(cache_mem_k)
    mv_s = _feature_major(cache_mem_v)
    kbuf = _feature_major(cache_swa_k)
    vbuf = _feature_major(cache_swa_v)

    xp = x_prompt.reshape(batch * seq, D_MODEL)
    xs = x_sample.reshape(nseq * dec, D_MODEL)
    tiles_per_seq = seq // TOKEN_TILE
    k_p, v_p, c_p, c_s = [], [], [], []
    cache_s = None
    for i in range(DEPTH):
        j = i // 2
        if i % 2 == 0:
            q, qx, kt, vt, kt32, vt32 = _in_att(xp, i, j, g_mix, w_q, w_kv_t, gq, gx, gk_cols, *rope_p,
                                                 tiles_per_seq=tiles_per_seq, keep_cols=WINDOW, act_dtype=BF16)
            k_p.append(kt32)
            v_p.append(vt32)
            o_self = _swa_prompt(sinks[j], q, kt, vt, batch=batch, seq=seq)
        else:
            o_self, qx, tail = _in_conv(xp, i, j, g_mix, w_conv, gx, conv_w, None,
                                        tiles_per_seq=tiles_per_seq, period=0,
                                        keep_rows=SUBLANES, act_dtype=BF16)
            c_p.append(tail.reshape(batch, SUBLANES, CONV_CH)[:, SUBLANES - (CONV_W - 1):])
        o_x = _xattn_prompt(qx, mk_p, mv_p, i, batch=batch, seq=seq)
        xp = _out_mlp(xp, o_self, o_x, i, w_o, g_mlp, w_u, w_d)

        if i % 2 == 0:
            q, qx, kt, vt = _in_att(xs, i, j, g_mix, w_q, w_kv_t, gq, gx, gk_cols, *rope_s,
                                    tiles_per_seq=1, keep_cols=0, act_dtype=F32)
            o_self, *cache_s = _swa_sample(sinks[j], q, kt, vt, kbuf, vbuf, j, cache_s, dec=dec)
        else:
            prefix = jnp.pad(state_conv[j], ((0, 0), (0, dec - (CONV_W - 1)), (0, 0)))
            o_self, qx, tail = _in_conv(xs, i, j, g_mix, w_conv, gx, conv_w,
                                        prefix.reshape(nseq * dec, CONV_CH),
                                        tiles_per_seq=1, period=dec,
                                        keep_rows=nseq * dec, act_dtype=F32)
            c_s.append(tail.reshape(nseq, dec, CONV_CH)[:, dec - (CONV_W - 1):])
        o_x = _xattn_sample(qx, mk_s, mv_s, i, dec=dec)
        xs = _out_mlp(xs, o_self, o_x, i, w_o, g_mlp, w_u, w_d)

    return (xp.reshape(batch, seq, D_MODEL),
            xs.reshape(nseq, dec, D_MODEL),
            _position_major(jnp.stack(k_p), N_KV_HEADS),
            _position_major(jnp.stack(v_p), N_KV_HEADS),
            jnp.stack(c_p),
            _position_major(mk_p, N_X_HEADS),
            _position_major(mv_p, N_X_HEADS),
            _position_major(cache_s[0], N_KV_HEADS),
            _position_major(cache_s[1], N_KV_HEADS),
            jnp.stack(c_s))
```

```python
import functools

import jax
import jax.numpy as jnp
from jax import lax
from jax.experimental import pallas as pl
from jax.experimental.pallas import tpu as pltpu

F32 = jnp.float32
BF16 = jnp.bfloat16

D_MODEL = 1024
DEPTH = 4
HEAD_DIM = 64
N_Q_HEADS = 12
N_KV_HEADS = 4
GQA_GROUP = N_Q_HEADS // N_KV_HEADS
WINDOW = 128
PAST_LEN = 8192
ROPE_THETA = 10000.0
CONV_CH = N_Q_HEADS * HEAD_DIM
CONV_W = 3
N_X_HEADS = 4
X_DIM = N_X_HEADS * HEAD_DIM
KV_DIM = N_KV_HEADS * HEAD_DIM
D_FF = 4 * D_MODEL
CONV_IN = 3 * CONV_CH + X_DIM
EPS = 1e-6
NEG = -1e30
LOG2E = 1.4426950408889634
Q_SCALE = HEAD_DIM ** -0.5 * LOG2E

LANES = 128
SUBLANES = 8
MXU_DIM = 256
TOKEN_TILE = 512
FF_CHUNK = 2048
SEQ_BLOCK = 16
SEQ_GROUP = 16
PAIR = 2 * HEAD_DIM

_NT = (((1,), (1,)), ((), ()))


def _const_spec(shape):
    nd = len(shape)
    return pl.BlockSpec(shape, lambda *_: (0,) * nd, pipeline_mode=pl.Buffered(1))


def _layer_spec(shape, layer):
    nd = len(shape)
    return pl.BlockSpec((1,) + shape, lambda *_: (layer,) + (0,) * nd, pipeline_mode=pl.Buffered(1))


def _rms_rows(x, g):
    return x * lax.rsqrt(jnp.mean(x * x, axis=-1, keepdims=True) + EPS) * g


def _head_blockdiag():
    r = lax.broadcasted_iota(jnp.int32, (MXU_DIM, MXU_DIM), 0) // HEAD_DIM
    c = lax.broadcasted_iota(jnp.int32, (MXU_DIM, MXU_DIM), 1) // HEAD_DIM
    return jnp.where(r == c, 1.0, 0.0).astype(BF16)


def _head_mean_sq(z, bd):
    sq = z * z
    hi = sq.astype(BF16)
    lo = (sq - hi.astype(F32)).astype(BF16)
    parts = []
    for c in range(z.shape[1] // MXU_DIM):
        sl = slice(c * MXU_DIM, (c + 1) * MXU_DIM)
        parts.append(jnp.dot(hi[:, sl], bd, preferred_element_type=F32)
                     + jnp.dot(lo[:, sl], bd, preferred_element_type=F32))
    ms = parts[0] if len(parts) == 1 else jnp.concatenate(parts, axis=1)
    return ms * (1.0 / HEAD_DIM)


def _rope_rows(x, cos, sin_signed):
    t = x.shape[0]
    first_half = (lax.broadcasted_iota(jnp.int32, (t, LANES), 1) & (HEAD_DIM - 1)) < HEAD_DIM // 2
    parts = []
    for c in range(x.shape[1] // LANES):
        xc = x[:, c * LANES:(c + 1) * LANES]
        partner = jnp.where(first_half,
                            pltpu.roll(xc, LANES - HEAD_DIM // 2, axis=1),
                            pltpu.roll(xc, HEAD_DIM // 2, axis=1))
        parts.append(xc * cos + partner * sin_signed)
    return jnp.concatenate(parts, axis=1)


def _lane_tile(col, n):
    return col if n == LANES else jnp.concatenate([col] * (n // LANES), axis=1)


def _head_norm_cols(xt, g_col):
    t = xt.shape[1]
    g = _lane_tile(g_col, t)
    parts = []
    for h in range(xt.shape[0] // HEAD_DIM):
        blk = xt[h * HEAD_DIM:(h + 1) * HEAD_DIM, :]
        ms = jnp.mean(blk * blk, axis=0, keepdims=True)
        parts.append(blk * lax.rsqrt(ms + EPS) * g)
    return parts


def _softmax_rows(s, sink=None):
    m = jnp.max(s, axis=-1, keepdims=True)
    if sink is not None:
        m = jnp.maximum(m, sink)
    p = jnp.exp2(s - m)
    den = jnp.sum(p, axis=-1, keepdims=True)
    if sink is not None:
        den = den + jnp.exp2(sink - m)
    return p.astype(BF16), 1.0 / den


def _mem_kv_kernel(mem_ref, g_ref, wt_ref, gk_ref, mk_ref, mv_ref):
    x = mem_ref[...]
    xn = x * lax.rsqrt(jnp.mean(x * x, axis=-1, keepdims=True) + EPS)
    for i in range(DEPTH):
        h = (xn * g_ref[i]).astype(BF16)
        kvt = lax.dot_general(wt_ref[i], h, _NT, preferred_element_type=F32)
        mk_ref[i, 0] = jnp.concatenate(_head_norm_cols(kvt[:X_DIM], gk_ref[i]), axis=0)
        mv_ref[i, 0] = kvt[X_DIM:]


def _mem_kv(mem2d, norm_mem, w_mem_kv_t, gk_cols, *, batch, n_mem):
    out = jax.ShapeDtypeStruct((DEPTH, batch, X_DIM, n_mem), F32)
    return pl.pallas_call(
        _mem_kv_kernel,
        grid=(batch,),
        in_specs=[
            pl.BlockSpec((n_mem, D_MODEL), lambda b: (b, 0)),
            _const_spec((DEPTH, 1, D_MODEL)),
            _const_spec((DEPTH, 2 * X_DIM, D_MODEL)),
            _const_spec((DEPTH, HEAD_DIM, LANES)),
        ],
        out_specs=[pl.BlockSpec((DEPTH, 1, X_DIM, n_mem), lambda b: (0, b, 0, 0))] * 2,
        out_shape=[out, out],
        name="mem_kv",
    )(mem2d, norm_mem, w_mem_kv_t, gk_cols)


def _in_att_kernel(*refs, keep_cols):
    (x_ref, g_ref, wq_ref, wkv_ref, gq_ref, gx_ref, gk_ref, cos_ref, sin_ref, cost_ref, sint_ref,
     q_ref, qx_ref, kt_ref, vt_ref) = refs[:15]
    t = x_ref.shape[0]
    h = _rms_rows(x_ref[...], g_ref[0]).astype(BF16)
    bd = _head_blockdiag()
    z = jnp.dot(h, wq_ref[0], preferred_element_type=F32)
    q = z[:, :CONV_CH]
    q = q * lax.rsqrt(_head_mean_sq(q, bd) + EPS) * gq_ref[0]
    q_ref[...] = (_rope_rows(q, cos_ref[...], sin_ref[...]) * Q_SCALE).astype(q_ref.dtype)
    qx = z[:, CONV_CH:]
    qx = qx * lax.rsqrt(_head_mean_sq(qx, bd) + EPS) * gx_ref[0]
    qx_ref[...] = (qx * Q_SCALE).astype(qx_ref.dtype)

    kvt = lax.dot_general(wkv_ref[0], h, _NT, preferred_element_type=F32)
    cos_t, sin_t = cost_ref[...], sint_ref[...]
    half = HEAD_DIM // 2
    k_parts = []
    for blk in _head_norm_cols(kvt[:KV_DIM], gk_ref[0]):
        x1, x2 = blk[:half], blk[half:]
        k_parts += [x1 * cos_t - x2 * sin_t, x2 * cos_t + x1 * sin_t]
    kt = jnp.concatenate(k_parts, axis=0)
    vt = kvt[KV_DIM:]
    kt_ref[...] = kt.astype(kt_ref.dtype)
    vt_ref[...] = vt.astype(vt_ref.dtype)
    if keep_cols:
        kt32_ref, vt32_ref = refs[15:]
        kt32_ref[0] = kt[:, t - keep_cols:]
        vt32_ref[0] = vt[:, t - keep_cols:]


def _in_att(x2d, layer, j, g_mix, wq, wkv_t, gq, gx, gk_cols, cos, sin, cos_t, sin_t,
            *, tiles_per_seq, keep_cols, act_dtype):
    n = x2d.shape[0]
    tm = TOKEN_TILE
    nt = n // tm
    nseq = nt // tiles_per_seq
    row = lambda i: (i, 0)
    col = lambda i: (0, i)
    pos = lambda i: (i % tiles_per_seq, 0)
    pos_t = lambda i: (0, i % tiles_per_seq)
    out_specs = [
        pl.BlockSpec((tm, CONV_CH), row),
        pl.BlockSpec((tm, X_DIM), row),
        pl.BlockSpec((KV_DIM, tm), col),
        pl.BlockSpec((KV_DIM, tm), col),
    ]
    out_shape = [
        jax.ShapeDtypeStruct((n, CONV_CH), act_dtype),
        jax.ShapeDtypeStruct((n, X_DIM), act_dtype),
        jax.ShapeDtypeStruct((KV_DIM, n), act_dtype),
        jax.ShapeDtypeStruct((KV_DIM, n), act_dtype),
    ]
    if keep_cols:
        out_specs += [pl.BlockSpec((1, KV_DIM, keep_cols), lambda i: (i // tiles_per_seq, 0, 0))] * 2
        out_shape += [jax.ShapeDtypeStruct((nseq, KV_DIM, keep_cols), F32)] * 2
    return pl.pallas_call(
        functools.partial(_in_att_kernel, keep_cols=keep_cols),
        grid=(nt,),
        in_specs=[
            pl.BlockSpec((tm, D_MODEL), row),
            _layer_spec((1, D_MODEL), layer),
            _layer_spec((D_MODEL, CONV_CH + X_DIM), j),
            _layer_spec((2 * KV_DIM, D_MODEL), j),
            _layer_spec((1, CONV_CH), j),
            _layer_spec((1, X_DIM), layer),
            _layer_spec((HEAD_DIM, LANES), j),
            pl.BlockSpec((tm, LANES), pos),
            pl.BlockSpec((tm, LANES), pos),
            pl.BlockSpec((HEAD_DIM // 2, tm), pos_t),
            pl.BlockSpec((HEAD_DIM // 2, tm), pos_t),
        ],
        out_specs=out_specs,
        out_shape=out_shape,
        name="in_att",
    )(x2d, g_mix, wq, wkv_t, gq, gx, gk_cols, cos, sin, cos_t, sin_t)


def _in_conv_kernel(*refs, tiles_per_seq, period):
    if period:
        (x_ref, g_ref, w_ref, gx_ref, cw_ref, prefix_ref,
         o_ref, qx_ref, gt_ref, gbuf, pbuf) = refs
    else:
        (x_ref, g_ref, w_ref, gx_ref, cw_ref,
         o_ref, qx_ref, gt_ref, gbuf) = refs
    t = x_ref.shape[0]
    keep = gt_ref.shape[0]
    h = _rms_rows(x_ref[...], g_ref[0]).astype(BF16)

    def proj(lo, width):
        return jnp.dot(h, w_ref[0, :, lo:lo + width], preferred_element_type=F32)

    @pl.when(pl.program_id(0) % tiles_per_seq == 0)
    def _():
        gbuf[0:SUBLANES, :] = jnp.zeros((SUBLANES, CONV_CH), F32)

    if period:
        pbuf[0:t, :] = prefix_ref[...]
        pbuf[t:t + SUBLANES, :] = jnp.zeros((SUBLANES, CONV_CH), F32)
        tok = lax.broadcasted_iota(jnp.int32, (t, MXU_DIM), 0) % period
    cw = cw_ref[0]
    for c in range(CONV_CH // MXU_DIM):
        sl = slice(c * MXU_DIM, (c + 1) * MXU_DIM)
        gate_b = proj(sl.start, MXU_DIM)
        u = proj(CONV_CH + sl.start, MXU_DIM) * proj(2 * CONV_CH + sl.start, MXU_DIM)
        gbuf[SUBLANES:SUBLANES + t, sl] = u
        back1 = gbuf[SUBLANES - 1:SUBLANES - 1 + t, sl]
        back2 = gbuf[SUBLANES - 2:SUBLANES - 2 + t, sl]
        if period:
            back1 = jnp.where(tok >= 1, back1, pbuf[1:1 + t, sl])
            back2 = jnp.where(tok >= 2, back2, pbuf[0:t, sl])
        y = back2 * cw[0:1, sl] + back1 * cw[1:2, sl] + u * cw[2:3, sl]
        o_ref[:, sl] = (gate_b * y).astype(o_ref.dtype)
        gt_ref[:, sl] = u[t - keep:, :]
    gbuf[0:SUBLANES, :] = gbuf[t:t + SUBLANES, :]
    qx = proj(3 * CONV_CH, X_DIM)
    qx = qx * lax.rsqrt(_head_mean_sq(qx, _head_blockdiag()) + EPS) * gx_ref[0]
    qx_ref[...] = (qx * Q_SCALE).astype(qx_ref.dtype)


def _in_conv(x2d, layer, j, g_mix, w, gx, cw, prefix, *, tiles_per_seq, period, keep_rows, act_dtype):
    n = x2d.shape[0]
    tm = TOKEN_TILE
    nt = n // tm
    nseq = nt // tiles_per_seq
    row = lambda i: (i, 0)
    in_specs = [
        pl.BlockSpec((tm, D_MODEL), row),
        _layer_spec((1, D_MODEL), layer),
        _layer_spec((D_MODEL, CONV_IN), j),
        _layer_spec((1, X_DIM), layer),
        _layer_spec((CONV_W, CONV_CH), j),
    ]
    args = [x2d, g_mix, w, gx, cw]
    scratch = [pltpu.VMEM((tm + 2 * SUBLANES, CONV_CH), F32)]
    if period:
        in_specs.append(pl.BlockSpec((tm, CONV_CH), row))
        args.append(prefix)
        scratch.append(pltpu.VMEM((tm + SUBLANES, CONV_CH), F32))
    return pl.pallas_call(
        functools.partial(_in_conv_kernel, tiles_per_seq=tiles_per_seq, period=period),
        grid=(nt,),
        in_specs=in_specs,
        out_specs=[
            pl.BlockSpec((tm, CONV_CH), row),
            pl.BlockSpec((tm, X_DIM), row),
            pl.BlockSpec((keep_rows, CONV_CH), lambda i: (i // tiles_per_seq, 0)),
        ],
        out_shape=[
            jax.ShapeDtypeStruct((n, CONV_CH), act_dtype),
            jax.ShapeDtypeStruct((n, X_DIM), act_dtype),
            jax.ShapeDtypeStruct((nseq * keep_rows, CONV_CH), F32),
        ],
        scratch_shapes=scratch,
        name="in_conv",
    )(*args)


def _pair_lhs(slabs):
    low = lax.broadcasted_iota(jnp.int32, slabs[0].shape, 1) < HEAD_DIM
    zero = jnp.zeros_like(slabs[0])
    parts = []
    for s in slabs:
        parts += [jnp.where(low, s, zero), jnp.where(low, zero, s)]
    return jnp.concatenate(parts, axis=0)


def _swa_prompt_kernel(sinks_ref, q_ref, kc_ref, kp_ref, vc_ref, vp_ref, o_ref, kfull, vfull, bias):
    tq = q_ref.shape[0]

    @pl.when(jnp.logical_and(pl.program_id(0) == 0, pl.program_id(1) == 0))
    def _():
        r = lax.broadcasted_iota(jnp.int32, (WINDOW, 2 * WINDOW), 0)
        c = lax.broadcasted_iota(jnp.int32, (WINDOW, 2 * WINDOW), 1)
        band = (c > r) & (c - WINDOW <= r)
        for kind, visible in enumerate((band, band & (c >= WINDOW))):
            base = jnp.where(visible, 0.0, NEG)
            for h in range(N_Q_HEADS):
                bias[kind, h] = jnp.where(c == 0, sinks_ref[h] * LOG2E, base)

    first_kind = jnp.where(pl.program_id(1) == 0, 1, 0)
    kfull[:, 0:WINDOW] = kp_ref[...]
    kfull[:, WINDOW:] = kc_ref[...]
    vfull[:, 0:WINDOW] = vp_ref[...]
    vfull[:, WINDOW:] = vc_ref[...]
    col0 = lax.broadcasted_iota(jnp.int32, (PAIR, 2 * WINDOW), 1) == 0
    zero_slab = jnp.zeros((PAIR, 2 * WINDOW), BF16)
    low = lax.broadcasted_iota(jnp.int32, (WINDOW, PAIR), 1) < HEAD_DIM
    for qb in range(tq // WINDOW):
        r0 = qb * WINDOW
        kind = first_kind if qb == 0 else 0
        slabs = [None] * (GQA_GROUP * KV_DIM // PAIR)
        for pr in range(KV_DIM // PAIR):
            kslab = jnp.where(col0, zero_slab, kfull[pr * PAIR:(pr + 1) * PAIR, r0:r0 + 2 * WINDOW])
            vslab = jnp.where(col0, zero_slab, vfull[pr * PAIR:(pr + 1) * PAIR, r0:r0 + 2 * WINDOW])
            lhs = _pair_lhs([q_ref[r0:r0 + WINDOW, g * KV_DIM + pr * PAIR:g * KV_DIM + (pr + 1) * PAIR]
                             for g in range(GQA_GROUP)])
            s_all = jnp.dot(lhs, kslab, preferred_element_type=F32)
            probs, inv = [], []
            for g in range(GQA_GROUP):
                for e in range(2):
                    idx = g * 2 + e
                    head = (2 * pr + e) * GQA_GROUP + g
                    p, inv_den = _softmax_rows(s_all[idx * WINDOW:(idx + 1) * WINDOW] + bias[kind, head])
                    probs.append(p)
                    inv.append(inv_den)
            res = lax.dot_general(jnp.concatenate(probs, axis=0), vslab, _NT,
                                  preferred_element_type=F32)
            for g in range(GQA_GROUP):
                lo = res[(2 * g) * WINDOW:(2 * g + 1) * WINDOW] * inv[2 * g]
                hi = res[(2 * g + 1) * WINDOW:(2 * g + 2) * WINDOW] * inv[2 * g + 1]
                slabs[g * (KV_DIM // PAIR) + pr] = jnp.where(low, lo, hi)
        o_ref[r0:r0 + WINDOW, :] = jnp.concatenate(slabs, axis=1).astype(o_ref.dtype)


def _swa_prompt(sinks, q, kt, vt, *, batch, seq):
    tq = TOKEN_TILE
    nt = seq // tq
    per = tq // WINDOW
    rows = lambda b, j: (b * nt + j, 0)
    cur = lambda b, j: (0, b * nt + j)
    prev = lambda b, j: (0, b * nt * per + jnp.maximum(j * per - 1, 0))
    return pl.pallas_call(
        _swa_prompt_kernel,
        grid=(batch, nt),
        in_specs=[
            pl.BlockSpec(memory_space=pltpu.SMEM),
            pl.BlockSpec((tq, CONV_CH), rows),
            pl.BlockSpec((KV_DIM, tq), cur),
            pl.BlockSpec((KV_DIM, WINDOW), prev),
            pl.BlockSpec((KV_DIM, tq), cur),
            pl.BlockSpec((KV_DIM, WINDOW), prev),
        ],
        out_specs=pl.BlockSpec((tq, CONV_CH), rows),
        out_shape=jax.ShapeDtypeStruct(q.shape, BF16),
        scratch_shapes=[pltpu.VMEM((KV_DIM, tq + WINDOW), BF16)] * 2
        + [pltpu.VMEM((2, N_Q_HEADS, WINDOW, 2 * WINDOW), F32)],
        name="swa_prompt",
    )(sinks, q, kt, kt, vt, vt)


def _swa_sample_kernel(*refs, dec, carried, slot):
    if carried:
        sinks_ref, q_ref, kn_ref, vn_ref, kb_ref, vb_ref, _, _, o_ref, ko_ref, vo_ref = refs
    else:
        sinks_ref, q_ref, kn_ref, vn_ref, kb_ref, vb_ref, o_ref, ko_ref, vo_ref = refs
    nseq = kb_ref.shape[1]
    for other in range(ko_ref.shape[0]):
        if other != slot:
            ko_ref[other] = jnp.zeros(ko_ref.shape[1:], F32)
            vo_ref[other] = jnp.zeros(vo_ref.shape[1:], F32)
    steps_per_tile = LANES // (nseq * dec)
    base = (pl.program_id(0) % steps_per_tile) * (nseq * dec)
    grp = N_KV_HEADS * dec
    rows = GQA_GROUP * grp
    ri = lax.broadcasted_iota(jnp.int32, (rows, KV_DIM), 0)
    li = lax.broadcasted_iota(jnp.int32, (rows, KV_DIM), 1)
    head_mask = ((ri % grp) // dec) == (li // HEAD_DIM)
    tok = lax.broadcasted_iota(jnp.int32, (rows, WINDOW + LANES), 0) % dec
    col = lax.broadcasted_iota(jnp.int32, (rows, WINDOW + LANES), 1)
    cache_mask = (col < WINDOW) & (col > tok)
    lane = lax.broadcasted_iota(jnp.int32, (KV_DIM, WINDOW), 1)
    sink = jnp.concatenate(
        [jnp.full((dec, 1), sinks_ref[kv * GQA_GROUP + g] * LOG2E, F32)
         for g in range(GQA_GROUP) for kv in range(N_KV_HEADS)], axis=0)
    kn = kn_ref[...]
    vn = vn_ref[...]
    kn16 = kn.astype(BF16)
    vn16 = vn.astype(BF16)
    for s0 in range(0, nseq, SEQ_GROUP):
        seqs = range(s0, s0 + SEQ_GROUP)
        scores, values = [], []
        for s in seqs:
            off = base + s * dec
            new = col - (WINDOW + off)
            mask = cache_mask | ((new >= 0) & (new <= tok))
            qs = q_ref[pl.ds(s * dec, dec), :]
            qbd = jnp.concatenate(
                [jnp.concatenate([qs[:, g * KV_DIM:(g + 1) * KV_DIM]] * N_KV_HEADS, axis=0)
                 for g in range(GQA_GROUP)], axis=0)
            qbd = jnp.where(head_mask, qbd, 0.0).astype(BF16)
            keys = jnp.concatenate([kb_ref[0, s].astype(BF16), kn16], axis=1)
            values.append(jnp.concatenate([vb_ref[0, s].astype(BF16), vn16], axis=1))
            scores.append(jnp.where(mask, jnp.dot(qbd, keys, preferred_element_type=F32), NEG))
        probs = [_softmax_rows(sc, sink) for sc in scores]
        for s, (p, inv_den), vals in zip(seqs, probs, values):
            o = lax.dot_general(p, vals, _NT, preferred_element_type=F32) * inv_den
            o = jnp.where(head_mask, o, 0.0)
            folded = []
            for g in range(GQA_GROUP):
                og = o[g * grp:g * grp + dec, :]
                for kv in range(1, N_KV_HEADS):
                    og = og + o[g * grp + kv * dec:g * grp + (kv + 1) * dec, :]
                folded.append(og)
            o_ref[pl.ds(s * dec, dec), :] = jnp.concatenate(folded, axis=1)
        for s in seqs:
            shift_new = (WINDOW - dec + LANES - (base + s * dec)) % LANES
            ko_ref[slot, s] = jnp.where(lane < WINDOW - dec, pltpu.roll(kb_ref[0, s], WINDOW - dec, axis=1),
                                     pltpu.roll(kn, shift_new, axis=1))
            vo_ref[slot, s] = jnp.where(lane < WINDOW - dec, pltpu.roll(vb_ref[0, s], WINDOW - dec, axis=1),
                                     pltpu.roll(vn, shift_new, axis=1))


def _swa_sample(sinks, q, kt_new, vt_new, kbuf, vbuf, j, carry, *, dec):
    nseq = kbuf.shape[1]
    sb = SEQ_BLOCK
    steps_per_tile = LANES // (sb * dec)
    row = lambda i: (i, 0)
    tile = lambda i: (0, i // steps_per_tile)
    cache = lambda i: (j, i, 0, 0)
    if carry is None:
        new_cache = pl.BlockSpec((kbuf.shape[0], sb, KV_DIM, WINDOW), lambda i: (0, i, 0, 0))
    else:
        new_cache = pl.BlockSpec((1, sb, KV_DIM, WINDOW), cache)
    in_specs = [
        pl.BlockSpec(memory_space=pltpu.SMEM),
        pl.BlockSpec((sb * dec, CONV_CH), row),
        pl.BlockSpec((KV_DIM, LANES), tile),
        pl.BlockSpec((KV_DIM, LANES), tile),
        pl.BlockSpec((1, sb, KV_DIM, WINDOW), cache),
        pl.BlockSpec((1, sb, KV_DIM, WINDOW), cache),
    ]
    args = [sinks, q, kt_new, vt_new, kbuf, vbuf]
    aliases = {}
    if carry is not None:
        aliases = {len(args): 1, len(args) + 1: 2}
        in_specs += [pl.BlockSpec(memory_space=pl.ANY)] * 2
        args += list(carry)
    return pl.pallas_call(
        functools.partial(_swa_sample_kernel, dec=dec, carried=carry is not None,
                          slot=j if carry is None else 0),
        grid=(nseq // sb,),
        in_specs=in_specs,
        out_specs=[
            pl.BlockSpec((sb * dec, CONV_CH), row),
            new_cache,
            new_cache,
        ],
        out_shape=[
            jax.ShapeDtypeStruct(q.shape, F32),
            jax.ShapeDtypeStruct(kbuf.shape, F32),
            jax.ShapeDtypeStruct(vbuf.shape, F32),
        ],
        input_output_aliases=aliases,
        name="swa_sample",
    )(*args)


def _xattn_tile(q_ref, mk_ref, mv_ref):
    tq = q_ref.shape[0]
    mk = mk_ref[0, 0].astype(BF16)
    mv = mv_ref[0, 0].astype(BF16)
    low = lax.broadcasted_iota(jnp.int32, (WINDOW, PAIR), 1) < HEAD_DIM
    blocks = []
    for rb in range(tq // WINDOW):
        r0 = rb * WINDOW
        slabs = []
        for pr in range(X_DIM // PAIR):
            lhs = _pair_lhs([q_ref[r0:r0 + WINDOW, pr * PAIR:(pr + 1) * PAIR]])
            s_all = jnp.dot(lhs, mk[pr * PAIR:(pr + 1) * PAIR], preferred_element_type=F32)
            p0, inv0 = _softmax_rows(s_all[:WINDOW])
            p1, inv1 = _softmax_rows(s_all[WINDOW:])
            res = lax.dot_general(jnp.concatenate([p0, p1], axis=0), mv[pr * PAIR:(pr + 1) * PAIR], _NT,
                                  preferred_element_type=F32)
            slabs.append(jnp.where(low, res[:WINDOW] * inv0, res[WINDOW:] * inv1))
        blocks.append(jnp.concatenate(slabs, axis=1).astype(BF16))
    return jnp.concatenate(blocks, axis=0)


def _xattn_prompt_kernel(q_ref, mk_ref, mv_ref, o_ref):
    o_ref[...] = _xattn_tile(q_ref, mk_ref, mv_ref)


def _xattn_prompt(qx, mk, mv, layer, *, batch, seq):
    tq = TOKEN_TILE
    nt = seq // tq
    n_mem = mk.shape[3]
    mem = lambda b, j: (layer, b, 0, 0)
    return pl.pallas_call(
        _xattn_prompt_kernel,
        grid=(batch, nt),
        in_specs=[
            pl.BlockSpec((tq, X_DIM), lambda b, j: (b * nt + j, 0)),
            pl.BlockSpec((1, 1, X_DIM, n_mem), mem),
            pl.BlockSpec((1, 1, X_DIM, n_mem), mem),
        ],
        out_specs=pl.BlockSpec((tq, X_DIM), lambda b, j: (b * nt + j, 0)),
        out_shape=jax.ShapeDtypeStruct(qx.shape, BF16),
        name="xattn_prompt",
    )(qx, mk, mv)


def _xattn_sample_kernel(q_ref, mk_ref, mv_ref, o_ref, *, dec):
    nseq = mk_ref.shape[1]
    rows = N_X_HEADS * dec
    head_mask = (lax.broadcasted_iota(jnp.int32, (rows, X_DIM), 0) // dec) == (
        lax.broadcasted_iota(jnp.int32, (rows, X_DIM), 1) // HEAD_DIM)
    for s0 in range(0, nseq, SEQ_GROUP):
        seqs = range(s0, s0 + SEQ_GROUP)
        scores = []
        for s in seqs:
            qs = q_ref[pl.ds(s * dec, dec), :]
            qbd = jnp.where(head_mask, jnp.concatenate([qs] * N_X_HEADS, axis=0), 0.0).astype(BF16)
            scores.append(jnp.dot(qbd, mk_ref[0, s].astype(BF16), preferred_element_type=F32))
        probs = [_softmax_rows(sc) for sc in scores]
        for s, (p, inv_den) in zip(seqs, probs):
            o = lax.dot_general(p, mv_ref[0, s].astype(BF16), _NT, preferred_element_type=F32) * inv_den
            o = jnp.where(head_mask, o, 0.0)
            acc = o[0:dec, :]
            for h in range(1, N_X_HEADS):
                acc = acc + o[h * dec:(h + 1) * dec, :]
            o_ref[pl.ds(s * dec, dec), :] = acc


def _xattn_sample(qx, mk, mv, layer, *, dec):
    nseq, n_mem = mk.shape[1], mk.shape[3]
    sb = SEQ_BLOCK
    mem = lambda i: (layer, i, 0, 0)
    return pl.pallas_call(
        functools.partial(_xattn_sample_kernel, dec=dec),
        grid=(nseq // sb,),
        in_specs=[
            pl.BlockSpec((sb * dec, X_DIM), lambda i: (i, 0)),
            pl.BlockSpec((1, sb, X_DIM, n_mem), mem),
            pl.BlockSpec((1, sb, X_DIM, n_mem), mem),
        ],
        out_specs=pl.BlockSpec((sb * dec, X_DIM), lambda i: (i, 0)),
        out_shape=jax.ShapeDtypeStruct(qx.shape, F32),
        name="xattn_sample",
    )(qx, mk, mv)


def _out_mlp_kernel(*refs, fuse_xattn):
    if fuse_xattn:
        x_ref, os_ref, qx_ref, mk_ref, mv_ref, wo_ref, gm_ref, wu_ref, wd_ref, y_ref = refs
        o_x = _xattn_tile(qx_ref, mk_ref, mv_ref)
    else:
        x_ref, os_ref, ox_ref, wo_ref, gm_ref, wu_ref, wd_ref, y_ref = refs
        o_x = ox_ref[...].astype(BF16)
    o = jnp.concatenate([os_ref[...].astype(BF16), o_x], axis=1)
    x1 = x_ref[...] + jnp.dot(o, wo_ref[0], preferred_element_type=F32)
    hm = _rms_rows(x1, gm_ref[0]).astype(BF16)
    acc = x1
    for c in range(D_FF // FF_CHUNK):
        a = jnp.dot(hm, wu_ref[0, :, c * FF_CHUNK:(c + 1) * FF_CHUNK], preferred_element_type=F32)
        a = jnp.square(jnp.maximum(a, 0.0)).astype(BF16)
        acc = acc + jnp.dot(a, wd_ref[0, c * FF_CHUNK:(c + 1) * FF_CHUNK, :], preferred_element_type=F32)
    y_ref[...] = acc


def _out_mlp(x2d, o_self, o_x, layer, wo, gm, wu, wd, *, memory=None, tiles_per_batch=None):
    n = x2d.shape[0]
    tm = TOKEN_TILE
    row = lambda i: (i, 0)
    in_specs = [
        pl.BlockSpec((tm, D_MODEL), row),
        pl.BlockSpec((tm, CONV_CH), row),
        pl.BlockSpec((tm, X_DIM), row),
    ]
    args = [x2d, o_self, o_x]
    if memory is not None:
        n_mem = memory[0].shape[3]
        mem = lambda i: (layer, i // tiles_per_batch, 0, 0)
        in_specs += [pl.BlockSpec((1, 1, X_DIM, n_mem), mem)] * 2
        args += list(memory)
    in_specs += [
        _layer_spec((CONV_CH + X_DIM, D_MODEL), layer),
        _layer_spec((1, D_MODEL), layer),
        _layer_spec((D_MODEL, D_FF), layer),
        _layer_spec((D_FF, D_MODEL), layer),
    ]
    return pl.pallas_call(
        functools.partial(_out_mlp_kernel, fuse_xattn=memory is not None),
        grid=(n // tm,),
        in_specs=in_specs,
        out_specs=pl.BlockSpec((tm, D_MODEL), row),
        out_shape=jax.ShapeDtypeStruct((n, D_MODEL), F32),
        name="out_mlp",
    )(*args, wo, gm, wu, wd)


def _rope_tables(pos):
    half = HEAD_DIM // 2
    inv = ROPE_THETA ** (-jnp.arange(half, dtype=F32) * 2.0 / HEAD_DIM)
    ang = pos[:, None] * inv[None, :]
    return jnp.cos(ang), jnp.sin(ang)


def _rope_lane_tables(cos, sin):
    reps = LANES // (HEAD_DIM // 2)
    return jnp.tile(cos, (1, reps)), jnp.concatenate([-sin, sin] * (reps // 2), axis=1)


def _group_major(w, axis):
    shape = w.shape
    w = w.reshape(shape[:axis] + (N_KV_HEADS, GQA_GROUP, HEAD_DIM) + shape[axis + 1:])
    return jnp.swapaxes(w, axis, axis + 1).reshape(shape)


def _feature_major(cache):
    lead = cache.shape[:-3]
    pos, heads, hd = cache.shape[-3:]
    nd = len(lead)
    perm = tuple(range(nd)) + (nd + 1, nd + 2, nd)
    return jnp.transpose(cache, perm).reshape(lead + (heads * hd, pos))


def _position_major(cache_t, heads):
    lead = cache_t.shape[:-2]
    pos = cache_t.shape[-1]
    nd = len(lead)
    perm = tuple(range(nd)) + (nd + 2, nd, nd + 1)
    return jnp.transpose(cache_t.reshape(lead + (heads, HEAD_DIM, pos)), perm)


def _gain_cols(g):
    return jnp.broadcast_to(g[:, :, None], g.shape + (LANES,))


def kernel(x_prompt, x_sample, mem_prompt, cache_swa_k, cache_swa_v, state_conv, cache_mem_k, cache_mem_v,
           norm_mix, w_in_att, q_norm_att, k_norm_att, sinks, w_in_conv, conv_w, norm_mem, w_mem_kv,
           q_norm_x, k_norm_x, w_out, norm_mlp, w_up, w_down):
    batch, seq, _ = x_prompt.shape
    nseq, dec, _ = x_sample.shape
    n_mem = mem_prompt.shape[1]
    assert seq % TOKEN_TILE == 0 and (nseq * dec) == TOKEN_TILE and nseq % SEQ_BLOCK == 0
    assert LANES % (SEQ_BLOCK * dec) == 0

    kv0, kv1 = CONV_CH, CONV_CH + 2 * KV_DIM
    w_q = jnp.concatenate([_group_major(w_in_att[:, :, :kv0], 2), w_in_att[:, :, kv1:]], axis=2).astype(BF16)
    w_kv_t = jnp.swapaxes(w_in_att[:, :, kv0:kv1], 1, 2).astype(BF16)
    w_conv = w_in_conv.astype(BF16)
    att_layers = jnp.arange(DEPTH) % 2 == 0
    w_o = jnp.concatenate(
        [jnp.where(att_layers[:, None, None], _group_major(w_out[:, :CONV_CH], 1), w_out[:, :CONV_CH]),
         w_out[:, CONV_CH:]], axis=1).astype(BF16)
    w_u = w_up.astype(BF16)
    w_d = w_down.astype(BF16)
    w_mkv_t = jnp.swapaxes(w_mem_kv, 1, 2).astype(BF16)

    g_mix = norm_mix[:, None, :]
    g_mlp = norm_mlp[:, None, :]
    gq = jnp.tile(q_norm_att, (1, N_Q_HEADS))[:, None, :]
    gx = jnp.tile(q_norm_x, (1, N_X_HEADS))[:, None, :]
    gk_cols = _gain_cols(k_norm_att)
    gkx_cols = _gain_cols(k_norm_x)

    cos_p, sin_p = _rope_tables(jnp.arange(seq, dtype=F32))
    cos_s, sin_s = _rope_tables(PAST_LEN + (jnp.arange(nseq * dec) % dec).astype(F32))
    rope_p = _rope_lane_tables(cos_p, sin_p) + (cos_p.T, sin_p.T)
    rope_s = _rope_lane_tables(cos_s, sin_s) + (cos_s.T, sin_s.T)

    mk_p, mv_p = _mem_kv(mem_prompt.reshape(batch * n_mem, D_MODEL), norm_mem[:, None, :], w_mkv_t, gkx_cols,
                         batch=batch, n_mem=n_mem)
    mk_s = _feature_major(cache_mem_k)
    mv_s = _feature_major(cache_mem_v)
    kbuf = _feature_major(cache_swa_k)
    vbuf = _feature_major(cache_swa_v)

    xp = x_prompt.reshape(batch * seq, D_MODEL)
    xs = x_sample.reshape(nseq * dec, D_MODEL)
    tiles_per_seq = seq // TOKEN_TILE
    k_p, v_p, c_p, c_s = [], [], [], []
    cache_s = None
    for i in range(DEPTH):
        j = i // 2
        if i % 2 == 0:
            q, qx, kt, vt, kt32, vt32 = _in_att(xp, i, j, g_mix, w_q, w_kv_t, gq, gx, gk_cols, *rope_p,
                                                 tiles_per_seq=tiles_per_seq, keep_cols=WINDOW, act_dtype=BF16)
            k_p.append(kt32)
            v_p.append(vt32)
            o_self = _swa_prompt(sinks[j], q, kt, vt, batch=batch, seq=seq)
        else:
            o_self, qx, tail = _in_conv(xp, i, j, g_mix, w_conv, gx, conv_w, None,
                                        tiles_per_seq=tiles_per_seq, period=0,
                                        keep_rows=SUBLANES, act_dtype=BF16)
            c_p.append(tail.reshape(batch, SUBLANES, CONV_CH)[:, SUBLANES - (CONV_W - 1):])
        xp = _out_mlp(xp, o_self, qx, i, w_o, g_mlp, w_u, w_d,
                      memory=(mk_p, mv_p), tiles_per_batch=tiles_per_seq)

        if i % 2 == 0:
            q, qx, kt, vt = _in_att(xs, i, j, g_mix, w_q, w_kv_t, gq, gx, gk_cols, *rope_s,
                                    tiles_per_seq=1, keep_cols=0, act_dtype=F32)
            o_self, *cache_s = _swa_sample(sinks[j], q, kt, vt, kbuf, vbuf, j, cache_s, dec=dec)
        else:
            prefix = jnp.pad(state_conv[j], ((0, 0), (0, dec - (CONV_W - 1)), (0, 0)))
            o_self, qx, tail = _in_conv(xs, i, j, g_mix, w_conv, gx, conv_w,
                                        prefix.reshape(nseq * dec, CONV_CH),
                                        tiles_per_seq=1, period=dec,
                                        keep_rows=nseq * dec, act_dtype=F32)
            c_s.append(tail.reshape(nseq, dec, CONV_CH)[:, dec - (CONV_W - 1):])
        o_x = _xattn_sample(qx, mk_s, mv_s, i, dec=dec)
        xs = _out_mlp(xs, o_self, o_x, i, w_o, g_mlp, w_u, w_d)

    return (xp.reshape(batch, seq, D_MODEL),
            xs.reshape(nseq, dec, D_MODEL),
            _position_major(jnp.stack(k_p), N_KV_HEADS),
            _position_major(jnp.stack(v_p), N_KV_HEADS),
            jnp.stack(c_p),
            _position_major(mk_p, N_X_HEADS),
            _position_major(mv_p, N_X_HEADS),
            _position_major(cache_s[0], N_KV_HEADS),
            _position_major(cache_s[1], N_KV_HEADS),
            jnp.stack(c_s))
```

```python
import functools

import jax
import jax.numpy as jnp
from jax import lax
from jax.experimental import pallas as pl
from jax.experimental.pallas import tpu as pltpu

F32 = jnp.float32
BF16 = jnp.bfloat16

D_MODEL = 1024
DEPTH = 4
HEAD_DIM = 64
N_Q_HEADS = 12
N_KV_HEADS = 4
GQA_GROUP = N_Q_HEADS // N_KV_HEADS
WINDOW = 128
PAST_LEN = 8192
ROPE_THETA = 10000.0
CONV_CH = N_Q_HEADS * HEAD_DIM
CONV_W = 3
N_X_HEADS = 4
X_DIM = N_X_HEADS * HEAD_DIM
KV_DIM = N_KV_HEADS * HEAD_DIM
D_FF = 4 * D_MODEL
CONV_IN = 3 * CONV_CH + X_DIM
EPS = 1e-6
NEG = -1e30
LOG2E = 1.4426950408889634
Q_SCALE = HEAD_DIM ** -0.5 * LOG2E

LANES = 128
SUBLANES = 8
MXU_DIM = 256
TOKEN_TILE = 512
FF_CHUNK = 2048
SEQ_BLOCK = 16
SEQ_GROUP = 16
PAIR = 2 * HEAD_DIM

_NT = (((1,), (1,)), ((), ()))


def _const_spec(shape):
    nd = len(shape)
    return pl.BlockSpec(shape, lambda *_: (0,) * nd, pipeline_mode=pl.Buffered(1))


def _layer_spec(shape, layer):
    nd = len(shape)
    return pl.BlockSpec((1,) + shape, lambda *_: (layer,) + (0,) * nd, pipeline_mode=pl.Buffered(1))


def _rms_rows(x, g):
    return x * lax.rsqrt(jnp.mean(x * x, axis=-1, keepdims=True) + EPS) * g


def _head_blockdiag():
    r = lax.broadcasted_iota(jnp.int32, (MXU_DIM, MXU_DIM), 0) // HEAD_DIM
    c = lax.broadcasted_iota(jnp.int32, (MXU_DIM, MXU_DIM), 1) // HEAD_DIM
    return jnp.where(r == c, 1.0, 0.0).astype(BF16)


def _head_mean_sq(z, bd):
    sq = z * z
    hi = sq.astype(BF16)
    lo = (sq - hi.astype(F32)).astype(BF16)
    parts = []
    for c in range(z.shape[1] // MXU_DIM):
        sl = slice(c * MXU_DIM, (c + 1) * MXU_DIM)
        parts.append(jnp.dot(hi[:, sl], bd, preferred_element_type=F32)
                     + jnp.dot(lo[:, sl], bd, preferred_element_type=F32))
    ms = parts[0] if len(parts) == 1 else jnp.concatenate(parts, axis=1)
    return ms * (1.0 / HEAD_DIM)


def _rope_rows(x, cos, sin_signed):
    t = x.shape[0]
    first_half = (lax.broadcasted_iota(jnp.int32, (t, LANES), 1) & (HEAD_DIM - 1)) < HEAD_DIM // 2
    parts = []
    for c in range(x.shape[1] // LANES):
        xc = x[:, c * LANES:(c + 1) * LANES]
        partner = jnp.where(first_half,
                            pltpu.roll(xc, LANES - HEAD_DIM // 2, axis=1),
                            pltpu.roll(xc, HEAD_DIM // 2, axis=1))
        parts.append(xc * cos + partner * sin_signed)
    return jnp.concatenate(parts, axis=1)


def _lane_tile(col, n):
    return col if n == LANES else jnp.concatenate([col] * (n // LANES), axis=1)


def _head_norm_cols(xt, g_col):
    t = xt.shape[1]
    g = _lane_tile(g_col, t)
    parts = []
    for h in range(xt.shape[0] // HEAD_DIM):
        blk = xt[h * HEAD_DIM:(h + 1) * HEAD_DIM, :]
        ms = jnp.mean(blk * blk, axis=0, keepdims=True)
        parts.append(blk * lax.rsqrt(ms + EPS) * g)
    return parts


def _softmax_rows(s, sink=None):
    m = jnp.max(s, axis=-1, keepdims=True)
    if sink is not None:
        m = jnp.maximum(m, sink)
    p = jnp.exp2(s - m)
    den = jnp.sum(p, axis=-1, keepdims=True)
    if sink is not None:
        den = den + jnp.exp2(sink - m)
    return p.astype(BF16), 1.0 / den


def _mem_kv_kernel(mem_ref, g_ref, wt_ref, gk_ref, mk_ref, mv_ref):
    x = mem_ref[...]
    xn = x * lax.rsqrt(jnp.mean(x * x, axis=-1, keepdims=True) + EPS)
    for i in range(DEPTH):
        h = (xn * g_ref[i]).astype(BF16)
        kvt = lax.dot_general(wt_ref[i], h, _NT, preferred_element_type=F32)
        mk_ref[i, 0] = jnp.concatenate(_head_norm_cols(kvt[:X_DIM], gk_ref[i]), axis=0)
        mv_ref[i, 0] = kvt[X_DIM:]


def _mem_kv(mem2d, norm_mem, w_mem_kv_t, gk_cols, *, batch, n_mem):
    out = jax.ShapeDtypeStruct((DEPTH, batch, X_DIM, n_mem), F32)
    return pl.pallas_call(
        _mem_kv_kernel,
        grid=(batch,),
        in_specs=[
            pl.BlockSpec((n_mem, D_MODEL), lambda b: (b, 0)),
            _const_spec((DEPTH, 1, D_MODEL)),
            _const_spec((DEPTH, 2 * X_DIM, D_MODEL)),
            _const_spec((DEPTH, HEAD_DIM, LANES)),
        ],
        out_specs=[pl.BlockSpec((DEPTH, 1, X_DIM, n_mem), lambda b: (0, b, 0, 0))] * 2,
        out_shape=[out, out],
        name="mem_kv",
    )(mem2d, norm_mem, w_mem_kv_t, gk_cols)


def _in_att_kernel(*refs, keep_cols):
    (x_ref, g_ref, wq_ref, wkv_ref, gq_ref, gx_ref, gk_ref, cos_ref, sin_ref, cost_ref, sint_ref,
     q_ref, qx_ref, kt_ref, vt_ref) = refs[:15]
    t = x_ref.shape[0]
    h = _rms_rows(x_ref[...], g_ref[0]).astype(BF16)
    bd = _head_blockdiag()
    z = jnp.dot(h, wq_ref[0], preferred_element_type=F32)
    q = z[:, :CONV_CH]
    q = q * lax.rsqrt(_head_mean_sq(q, bd) + EPS) * gq_ref[0]
    q_ref[...] = (_rope_rows(q, cos_ref[...], sin_ref[...]) * Q_SCALE).astype(q_ref.dtype)
    qx = z[:, CONV_CH:]
    qx = qx * lax.rsqrt(_head_mean_sq(qx, bd) + EPS) * gx_ref[0]
    qx_ref[...] = (qx * Q_SCALE).astype(qx_ref.dtype)

    kvt = lax.dot_general(wkv_ref[0], h, _NT, preferred_element_type=F32)
    cos_t, sin_t = cost_ref[...], sint_ref[...]
    half = HEAD_DIM // 2
    k_parts = []
    for blk in _head_norm_cols(kvt[:KV_DIM], gk_ref[0]):
        x1, x2 = blk[:half], blk[half:]
        k_parts += [x1 * cos_t - x2 * sin_t, x2 * cos_t + x1 * sin_t]
    kt = jnp.concatenate(k_parts, axis=0)
    vt = kvt[KV_DIM:]
    kt_ref[...] = kt.astype(kt_ref.dtype)
    vt_ref[...] = vt.astype(vt_ref.dtype)
    if keep_cols:
        kt32_ref, vt32_ref = refs[15:]
        kt32_ref[0] = kt[:, t - keep_cols:]
        vt32_ref[0] = vt[:, t - keep_cols:]


def _in_att(x2d, layer, j, g_mix, wq, wkv_t, gq, gx, gk_cols, cos, sin, cos_t, sin_t,
            *, tiles_per_seq, keep_cols, act_dtype):
    n = x2d.shape[0]
    tm = TOKEN_TILE
    nt = n // tm
    nseq = nt // tiles_per_seq
    row = lambda i: (i, 0)
    col = lambda i: (0, i)
    pos = lambda i: (i % tiles_per_seq, 0)
    pos_t = lambda i: (0, i % tiles_per_seq)
    out_specs = [
        pl.BlockSpec((tm, CONV_CH), row),
        pl.BlockSpec((tm, X_DIM), row),
        pl.BlockSpec((KV_DIM, tm), col),
        pl.BlockSpec((KV_DIM, tm), col),
    ]
    out_shape = [
        jax.ShapeDtypeStruct((n, CONV_CH), act_dtype),
        jax.ShapeDtypeStruct((n, X_DIM), act_dtype),
        jax.ShapeDtypeStruct((KV_DIM, n), act_dtype),
        jax.ShapeDtypeStruct((KV_DIM, n), act_dtype),
    ]
    if keep_cols:
        out_specs += [pl.BlockSpec((1, KV_DIM, keep_cols), lambda i: (i // tiles_per_seq, 0, 0))] * 2
        out_shape += [jax.ShapeDtypeStruct((nseq, KV_DIM, keep_cols), F32)] * 2
    return pl.pallas_call(
        functools.partial(_in_att_kernel, keep_cols=keep_cols),
        grid=(nt,),
        in_specs=[
            pl.BlockSpec((tm, D_MODEL), row),
            _layer_spec((1, D_MODEL), layer),
            _layer_spec((D_MODEL, CONV_CH + X_DIM), j),
            _layer_spec((2 * KV_DIM, D_MODEL), j),
            _layer_spec((1, CONV_CH), j),
            _layer_spec((1, X_DIM), layer),
            _layer_spec((HEAD_DIM, LANES), j),
            pl.BlockSpec((tm, LANES), pos),
            pl.BlockSpec((tm, LANES), pos),
            pl.BlockSpec((HEAD_DIM // 2, tm), pos_t),
            pl.BlockSpec((HEAD_DIM // 2, tm), pos_t),
        ],
        out_specs=out_specs,
        out_shape=out_shape,
        name="in_att",
    )(x2d, g_mix, wq, wkv_t, gq, gx, gk_cols, cos, sin, cos_t, sin_t)


def _in_conv_kernel(*refs, tiles_per_seq, period):
    if period:
        (x_ref, g_ref, w_ref, gx_ref, cw_ref, prefix_ref,
         o_ref, qx_ref, gt_ref, gbuf, pbuf) = refs
    else:
        (x_ref, g_ref, w_ref, gx_ref, cw_ref,
         o_ref, qx_ref, gt_ref, gbuf) = refs
    t = x_ref.shape[0]
    keep = gt_ref.shape[0]
    h = _rms_rows(x_ref[...], g_ref[0]).astype(BF16)

    def proj(lo, width):
        return jnp.dot(h, w_ref[:, lo:lo + width], preferred_element_type=F32)

    @pl.when(pl.program_id(0) % tiles_per_seq == 0)
    def _():
        gbuf[0:SUBLANES, :] = jnp.zeros((SUBLANES, CONV_CH), F32)

    if period:
        pbuf[0:t, :] = prefix_ref[...]
        pbuf[t:t + SUBLANES, :] = jnp.zeros((SUBLANES, CONV_CH), F32)
        tok = lax.broadcasted_iota(jnp.int32, (t, MXU_DIM), 0) % period
    cw = cw_ref[0]
    for c in range(CONV_CH // MXU_DIM):
        sl = slice(c * MXU_DIM, (c + 1) * MXU_DIM)
        gate_b = proj(sl.start, MXU_DIM)
        u = proj(CONV_CH + sl.start, MXU_DIM) * proj(2 * CONV_CH + sl.start, MXU_DIM)
        gbuf[SUBLANES:SUBLANES + t, sl] = u
        back1 = gbuf[SUBLANES - 1:SUBLANES - 1 + t, sl]
        back2 = gbuf[SUBLANES - 2:SUBLANES - 2 + t, sl]
        if period:
            back1 = jnp.where(tok >= 1, back1, pbuf[1:1 + t, sl])
            back2 = jnp.where(tok >= 2, back2, pbuf[0:t, sl])
        y = back2 * cw[0:1, sl] + back1 * cw[1:2, sl] + u * cw[2:3, sl]
        o_ref[:, sl] = (gate_b * y).astype(o_ref.dtype)
        gt_ref[:, sl] = u[t - keep:, :]
    gbuf[0:SUBLANES, :] = gbuf[t:t + SUBLANES, :]
    qx = proj(3 * CONV_CH, X_DIM)
    qx = qx * lax.rsqrt(_head_mean_sq(qx, _head_blockdiag()) + EPS) * gx_ref[0]
    qx_ref[...] = (qx * Q_SCALE).astype(qx_ref.dtype)


def _in_conv(x2d, layer, j, g_mix, w, gx, cw, prefix, *, tiles_per_seq, period, keep_rows, act_dtype):
    n = x2d.shape[0]
    tm = TOKEN_TILE
    nt = n // tm
    nseq = nt // tiles_per_seq
    row = lambda i: (i, 0)
    in_specs = [
        pl.BlockSpec((tm, D_MODEL), row),
        _layer_spec((1, D_MODEL), layer),
        _const_spec((D_MODEL, CONV_IN)),
        _layer_spec((1, X_DIM), layer),
        _layer_spec((CONV_W, CONV_CH), j),
    ]
    args = [x2d, g_mix, w, gx, cw]
    scratch = [pltpu.VMEM((tm + 2 * SUBLANES, CONV_CH), F32)]
    if period:
        in_specs.append(pl.BlockSpec((tm, CONV_CH), row))
        args.append(prefix)
        scratch.append(pltpu.VMEM((tm + SUBLANES, CONV_CH), F32))
    return pl.pallas_call(
        functools.partial(_in_conv_kernel, tiles_per_seq=tiles_per_seq, period=period),
        grid=(nt,),
        in_specs=in_specs,
        out_specs=[
            pl.BlockSpec((tm, CONV_CH), row),
            pl.BlockSpec((tm, X_DIM), row),
            pl.BlockSpec((keep_rows, CONV_CH), lambda i: (i // tiles_per_seq, 0)),
        ],
        out_shape=[
            jax.ShapeDtypeStruct((n, CONV_CH), act_dtype),
            jax.ShapeDtypeStruct((n, X_DIM), act_dtype),
            jax.ShapeDtypeStruct((nseq * keep_rows, CONV_CH), F32),
        ],
        scratch_shapes=scratch,
        name="in_conv",
    )(*args)


def _pair_lhs(slabs):
    low = lax.broadcasted_iota(jnp.int32, slabs[0].shape, 1) < HEAD_DIM
    zero = jnp.zeros_like(slabs[0])
    parts = []
    for s in slabs:
        parts += [jnp.where(low, s, zero), jnp.where(low, zero, s)]
    return jnp.concatenate(parts, axis=0)


def _swa_prompt_kernel(sinks_ref, q_ref, kc_ref, kp_ref, vc_ref, vp_ref, o_ref, kfull, vfull, bias):
    tq = q_ref.shape[0]

    @pl.when(jnp.logical_and(pl.program_id(0) == 0, pl.program_id(1) == 0))
    def _():
        r = lax.broadcasted_iota(jnp.int32, (WINDOW, 2 * WINDOW), 0)
        c = lax.broadcasted_iota(jnp.int32, (WINDOW, 2 * WINDOW), 1)
        band = (c > r) & (c - WINDOW <= r)
        for kind, visible in enumerate((band, band & (c >= WINDOW))):
            base = jnp.where(visible, 0.0, NEG)
            for h in range(N_Q_HEADS):
                bias[kind, h] = jnp.where(c == 0, sinks_ref[h] * LOG2E, base)

    first_kind = jnp.where(pl.program_id(1) == 0, 1, 0)
    kfull[:, 0:WINDOW] = kp_ref[...]
    kfull[:, WINDOW:] = kc_ref[...]
    vfull[:, 0:WINDOW] = vp_ref[...]
    vfull[:, WINDOW:] = vc_ref[...]
    col0 = lax.broadcasted_iota(jnp.int32, (PAIR, 2 * WINDOW), 1) == 0
    zero_slab = jnp.zeros((PAIR, 2 * WINDOW), BF16)
    low = lax.broadcasted_iota(jnp.int32, (WINDOW, PAIR), 1) < HEAD_DIM
    for qb in range(tq // WINDOW):
        r0 = qb * WINDOW
        kind = first_kind if qb == 0 else 0
        slabs = [None] * (GQA_GROUP * KV_DIM // PAIR)
        for pr in range(KV_DIM // PAIR):
            kslab = jnp.where(col0, zero_slab, kfull[pr * PAIR:(pr + 1) * PAIR, r0:r0 + 2 * WINDOW])
            vslab = jnp.where(col0, zero_slab, vfull[pr * PAIR:(pr + 1) * PAIR, r0:r0 + 2 * WINDOW])
            lhs = _pair_lhs([q_ref[r0:r0 + WINDOW, g * KV_DIM + pr * PAIR:g * KV_DIM + (pr + 1) * PAIR]
                             for g in range(GQA_GROUP)])
            s_all = jnp.dot(lhs, kslab, preferred_element_type=F32)
            probs, inv = [], []
            for g in range(GQA_GROUP):
                for e in range(2):
                    idx = g * 2 + e
                    head = (2 * pr + e) * GQA_GROUP + g
                    p, inv_den = _softmax_rows(s_all[idx * WINDOW:(idx + 1) * WINDOW] + bias[kind, head])
                    probs.append(p)
                    inv.append(inv_den)
            res = lax.dot_general(jnp.concatenate(probs, axis=0), vslab, _NT,
                                  preferred_element_type=F32)
            for g in range(GQA_GROUP):
                lo = res[(2 * g) * WINDOW:(2 * g + 1) * WINDOW] * inv[2 * g]
                hi = res[(2 * g + 1) * WINDOW:(2 * g + 2) * WINDOW] * inv[2 * g + 1]
                slabs[g * (KV_DIM // PAIR) + pr] = jnp.where(low, lo, hi)
        o_ref[r0:r0 + WINDOW, :] = jnp.concatenate(slabs, axis=1).astype(o_ref.dtype)


def _swa_prompt(sinks, q, kt, vt, *, batch, seq):
    tq = TOKEN_TILE
    nt = seq // tq
    per = tq // WINDOW
    rows = lambda b, j: (b * nt + j, 0)
    cur = lambda b, j: (0, b * nt + j)
    prev = lambda b, j: (0, b * nt * per + jnp.maximum(j * per - 1, 0))
    return pl.pallas_call(
        _swa_prompt_kernel,
        grid=(batch, nt),
        in_specs=[
            pl.BlockSpec(memory_space=pltpu.SMEM),
            pl.BlockSpec((tq, CONV_CH), rows),
            pl.BlockSpec((KV_DIM, tq), cur),
            pl.BlockSpec((KV_DIM, WINDOW), prev),
            pl.BlockSpec((KV_DIM, tq), cur),
            pl.BlockSpec((KV_DIM, WINDOW), prev),
        ],
        out_specs=pl.BlockSpec((tq, CONV_CH), rows),
        out_shape=jax.ShapeDtypeStruct(q.shape, BF16),
        scratch_shapes=[pltpu.VMEM((KV_DIM, tq + WINDOW), BF16)] * 2
        + [pltpu.VMEM((2, N_Q_HEADS, WINDOW, 2 * WINDOW), F32)],
        name="swa_prompt",
    )(sinks, q, kt, kt, vt, vt)


def _swa_sample_kernel(*refs, dec, carried, slot):
    if carried:
        sinks_ref, q_ref, kn_ref, vn_ref, kb_ref, vb_ref, _, _, o_ref, ko_ref, vo_ref = refs
    else:
        sinks_ref, q_ref, kn_ref, vn_ref, kb_ref, vb_ref, o_ref, ko_ref, vo_ref = refs
    nseq = kb_ref.shape[1]
    for other in range(ko_ref.shape[0]):
        if other != slot:
            ko_ref[other] = jnp.zeros(ko_ref.shape[1:], F32)
            vo_ref[other] = jnp.zeros(vo_ref.shape[1:], F32)
    steps_per_tile = LANES // (nseq * dec)
    base = (pl.program_id(0) % steps_per_tile) * (nseq * dec)
    grp = N_KV_HEADS * dec
    rows = GQA_GROUP * grp
    ri = lax.broadcasted_iota(jnp.int32, (rows, KV_DIM), 0)
    li = lax.broadcasted_iota(jnp.int32, (rows, KV_DIM), 1)
    head_mask = ((ri % grp) // dec) == (li // HEAD_DIM)
    tok = lax.broadcasted_iota(jnp.int32, (rows, WINDOW + LANES), 0) % dec
    col = lax.broadcasted_iota(jnp.int32, (rows, WINDOW + LANES), 1)
    cache_mask = (col < WINDOW) & (col > tok)
    lane = lax.broadcasted_iota(jnp.int32, (KV_DIM, WINDOW), 1)
    sink = jnp.concatenate(
        [jnp.full((dec, 1), sinks_ref[kv * GQA_GROUP + g] * LOG2E, F32)
         for g in range(GQA_GROUP) for kv in range(N_KV_HEADS)], axis=0)
    kn = kn_ref[...]
    vn = vn_ref[...]
    kn16 = kn.astype(BF16)
    vn16 = vn.astype(BF16)
    for s0 in range(0, nseq, SEQ_GROUP):
        seqs = range(s0, s0 + SEQ_GROUP)
        scores, values = [], []
        for s in seqs:
            off = base + s * dec
            new = col - (WINDOW + off)
            mask = cache_mask | ((new >= 0) & (new <= tok))
            qs = q_ref[pl.ds(s * dec, dec), :]
            qbd = jnp.concatenate(
                [jnp.concatenate([qs[:, g * KV_DIM:(g + 1) * KV_DIM]] * N_KV_HEADS, axis=0)
                 for g in range(GQA_GROUP)], axis=0)
            qbd = jnp.where(head_mask, qbd, 0.0).astype(BF16)
            keys = jnp.concatenate([kb_ref[0, s].astype(BF16), kn16], axis=1)
            values.append(jnp.concatenate([vb_ref[0, s].astype(BF16), vn16], axis=1))
            scores.append(jnp.where(mask, jnp.dot(qbd, keys, preferred_element_type=F32), NEG))
        probs = [_softmax_rows(sc, sink) for sc in scores]
        for s, (p, inv_den), vals in zip(seqs, probs, values):
            o = lax.dot_general(p, vals, _NT, preferred_element_type=F32) * inv_den
            o = jnp.where(head_mask, o, 0.0)
            folded = []
            for g in range(GQA_GROUP):
                og = o[g * grp:g * grp + dec, :]
                for kv in range(1, N_KV_HEADS):
                    og = og + o[g * grp + kv * dec:g * grp + (kv + 1) * dec, :]
                folded.append(og)
            o_ref[pl.ds(s * dec, dec), :] = jnp.concatenate(folded, axis=1)
        for s in seqs:
            shift_new = (WINDOW - dec + LANES - (base + s * dec)) % LANES
            ko_ref[slot, s] = jnp.where(lane < WINDOW - dec, pltpu.roll(kb_ref[0, s], WINDOW - dec, axis=1),
                                     pltpu.roll(kn, shift_new, axis=1))
            vo_ref[slot, s] = jnp.where(lane < WINDOW - dec, pltpu.roll(vb_ref[0, s], WINDOW - dec, axis=1),
                                     pltpu.roll(vn, shift_new, axis=1))


def _swa_sample(sinks, q, kt_new, vt_new, kbuf, vbuf, j, carry, *, dec):
    nseq = kbuf.shape[1]
    sb = SEQ_BLOCK
    steps_per_tile = LANES // (sb * dec)
    row = lambda i: (i, 0)
    tile = lambda i: (0, i // steps_per_tile)
    cache = lambda i: (j, i, 0, 0)
    if carry is None:
        new_cache = pl.BlockSpec((kbuf.shape[0], sb, KV_DIM, WINDOW), lambda i: (0, i, 0, 0))
    else:
        new_cache = pl.BlockSpec((1, sb, KV_DIM, WINDOW), cache)
    in_specs = [
        pl.BlockSpec(memory_space=pltpu.SMEM),
        pl.BlockSpec((sb * dec, CONV_CH), row),
        pl.BlockSpec((KV_DIM, LANES), tile),
        pl.BlockSpec((KV_DIM, LANES), tile),
        pl.BlockSpec((1, sb, KV_DIM, WINDOW), cache),
        pl.BlockSpec((1, sb, KV_DIM, WINDOW), cache),
    ]
    args = [sinks, q, kt_new, vt_new, kbuf, vbuf]
    aliases = {}
    if carry is not None:
        aliases = {len(args): 1, len(args) + 1: 2}
        in_specs += [pl.BlockSpec(memory_space=pl.ANY)] * 2
        args += list(carry)
    return pl.pallas_call(
        functools.partial(_swa_sample_kernel, dec=dec, carried=carry is not None,
                          slot=j if carry is None else 0),
        grid=(nseq // sb,),
        in_specs=in_specs,
        out_specs=[
            pl.BlockSpec((sb * dec, CONV_CH), row),
            new_cache,
            new_cache,
        ],
        out_shape=[
            jax.ShapeDtypeStruct(q.shape, F32),
            jax.ShapeDtypeStruct(kbuf.shape, F32),
            jax.ShapeDtypeStruct(vbuf.shape, F32),
        ],
        input_output_aliases=aliases,
        name="swa_sample",
    )(*args)


def _xattn_tile(q_ref, mk_ref, mv_ref):
    tq = q_ref.shape[0]
    mk = mk_ref[0, 0].astype(BF16)
    mv = mv_ref[0, 0].astype(BF16)
    low = lax.broadcasted_iota(jnp.int32, (WINDOW, PAIR), 1) < HEAD_DIM
    blocks = []
    for rb in range(tq // WINDOW):
        r0 = rb * WINDOW
        slabs = []
        for pr in range(X_DIM // PAIR):
            lhs = _pair_lhs([q_ref[r0:r0 + WINDOW, pr * PAIR:(pr + 1) * PAIR]])
            s_all = jnp.dot(lhs, mk[pr * PAIR:(pr + 1) * PAIR], preferred_element_type=F32)
            p0, inv0 = _softmax_rows(s_all[:WINDOW])
            p1, inv1 = _softmax_rows(s_all[WINDOW:])
            res = lax.dot_general(jnp.concatenate([p0, p1], axis=0), mv[pr * PAIR:(pr + 1) * PAIR], _NT,
                                  preferred_element_type=F32)
            slabs.append(jnp.where(low, res[:WINDOW] * inv0, res[WINDOW:] * inv1))
        blocks.append(jnp.concatenate(slabs, axis=1).astype(BF16))
    return jnp.concatenate(blocks, axis=0)


def _xattn_sample_kernel(q_ref, mk_ref, mv_ref, o_ref, *, dec):
    nseq = mk_ref.shape[1]
    rows = N_X_HEADS * dec
    head_mask = (lax.broadcasted_iota(jnp.int32, (rows, X_DIM), 0) // dec) == (
        lax.broadcasted_iota(jnp.int32, (rows, X_DIM), 1) // HEAD_DIM)
    for s0 in range(0, nseq, SEQ_GROUP):
        seqs = range(s0, s0 + SEQ_GROUP)
        scores = []
        for s in seqs:
            qs = q_ref[pl.ds(s * dec, dec), :]
            qbd = jnp.where(head_mask, jnp.concatenate([qs] * N_X_HEADS, axis=0), 0.0).astype(BF16)
            scores.append(jnp.dot(qbd, mk_ref[0, s].astype(BF16), preferred_element_type=F32))
        probs = [_softmax_rows(sc) for sc in scores]
        for s, (p, inv_den) in zip(seqs, probs):
            o = lax.dot_general(p, mv_ref[0, s].astype(BF16), _NT, preferred_element_type=F32) * inv_den
            o = jnp.where(head_mask, o, 0.0)
            acc = o[0:dec, :]
            for h in range(1, N_X_HEADS):
                acc = acc + o[h * dec:(h + 1) * dec, :]
            o_ref[pl.ds(s * dec, dec), :] = acc


def _xattn_sample(qx, mk, mv, layer, *, dec):
    nseq, n_mem = mk.shape[1], mk.shape[3]
    sb = SEQ_BLOCK
    mem = lambda i: (layer, i, 0, 0)
    return pl.pallas_call(
        functools.partial(_xattn_sample_kernel, dec=dec),
        grid=(nseq // sb,),
        in_specs=[
            pl.BlockSpec((sb * dec, X_DIM), lambda i: (i, 0)),
            pl.BlockSpec((1, sb, X_DIM, n_mem), mem),
            pl.BlockSpec((1, sb, X_DIM, n_mem), mem),
        ],
        out_specs=pl.BlockSpec((sb * dec, X_DIM), lambda i: (i, 0)),
        out_shape=jax.ShapeDtypeStruct(qx.shape, F32),
        name="xattn_sample",
    )(qx, mk, mv)


def _out_mlp_kernel(*refs, fuse_xattn, n_convert):
    n_in = len(refs) - 1 - n_convert
    for src, dst in zip(refs[n_in - n_convert:n_in], refs[n_in + 1:]):
        dst[...] = src[0].astype(BF16)
    refs = refs[:n_in - n_convert] + (refs[n_in],)
    if fuse_xattn:
        x_ref, os_ref, qx_ref, mk_ref, mv_ref, wo_ref, gm_ref, wu_ref, wd_ref, y_ref = refs
        o_x = _xattn_tile(qx_ref, mk_ref, mv_ref)
    else:
        x_ref, os_ref, ox_ref, wo_ref, gm_ref, wu_ref, wd_ref, y_ref = refs
        o_x = ox_ref[...].astype(BF16)
    o = jnp.concatenate([os_ref[...].astype(BF16), o_x], axis=1)
    x1 = x_ref[...] + jnp.dot(o, wo_ref[0], preferred_element_type=F32)
    hm = _rms_rows(x1, gm_ref[0]).astype(BF16)
    acc = x1
    for c in range(D_FF // FF_CHUNK):
        a = jnp.dot(hm, wu_ref[:, c * FF_CHUNK:(c + 1) * FF_CHUNK], preferred_element_type=F32)
        a = jnp.square(jnp.maximum(a, 0.0)).astype(BF16)
        acc = acc + jnp.dot(a, wd_ref[c * FF_CHUNK:(c + 1) * FF_CHUNK, :], preferred_element_type=F32)
    y_ref[...] = acc


def _out_mlp(x2d, o_self, o_x, layer, wo, gm, wu, wd, *, memory=None, tiles_per_batch=None, convert=()):
    n = x2d.shape[0]
    tm = TOKEN_TILE
    steps = n // tm
    row = lambda i: (i, 0)
    in_specs = [
        pl.BlockSpec((tm, D_MODEL), row),
        pl.BlockSpec((tm, CONV_CH), row),
        pl.BlockSpec((tm, X_DIM), row),
    ]
    args = [x2d, o_self, o_x]
    if memory is not None:
        n_mem = memory[0].shape[3]
        mem = lambda i: (layer, i // tiles_per_batch, 0, 0)
        in_specs += [pl.BlockSpec((1, 1, X_DIM, n_mem), mem)] * 2
        args += list(memory)
    in_specs += [
        _layer_spec((CONV_CH + X_DIM, D_MODEL), layer),
        _layer_spec((1, D_MODEL), layer),
        _const_spec((D_MODEL, D_FF)),
        _const_spec((D_FF, D_MODEL)),
    ]
    args += [wo, gm, wu, wd]
    out_specs = [pl.BlockSpec((tm, D_MODEL), row)]
    out_shape = [jax.ShapeDtypeStruct((n, D_MODEL), F32)]
    for stack, l in convert:
        _, r, c = stack.shape
        in_specs.append(pl.BlockSpec((1, r // steps, c), lambda i, l=l: (l, i, 0)))
        args.append(stack)
        out_specs.append(pl.BlockSpec((r // steps, c), row))
        out_shape.append(jax.ShapeDtypeStruct((r, c), BF16))
    return pl.pallas_call(
        functools.partial(_out_mlp_kernel, fuse_xattn=memory is not None, n_convert=len(convert)),
        grid=(steps,),
        in_specs=in_specs,
        out_specs=out_specs,
        out_shape=out_shape,
        name="out_mlp",
    )(*args)


def _rope_tables(pos):
    half = HEAD_DIM // 2
    inv = ROPE_THETA ** (-jnp.arange(half, dtype=F32) * 2.0 / HEAD_DIM)
    ang = pos[:, None] * inv[None, :]
    return jnp.cos(ang), jnp.sin(ang)


def _rope_lane_tables(cos, sin):
    reps = LANES // (HEAD_DIM // 2)
    return jnp.tile(cos, (1, reps)), jnp.concatenate([-sin, sin] * (reps // 2), axis=1)


def _group_major(w, axis):
    shape = w.shape
    w = w.reshape(shape[:axis] + (N_KV_HEADS, GQA_GROUP, HEAD_DIM) + shape[axis + 1:])
    return jnp.swapaxes(w, axis, axis + 1).reshape(shape)


def _feature_major(cache):
    lead = cache.shape[:-3]
    pos, heads, hd = cache.shape[-3:]
    nd = len(lead)
    perm = tuple(range(nd)) + (nd + 1, nd + 2, nd)
    return jnp.transpose(cache, perm).reshape(lead + (heads * hd, pos))


def _position_major(cache_t, heads):
    lead = cache_t.shape[:-2]
    pos = cache_t.shape[-1]
    nd = len(lead)
    perm = tuple(range(nd)) + (nd + 2, nd, nd + 1)
    return jnp.transpose(cache_t.reshape(lead + (heads, HEAD_DIM, pos)), perm)


def _gain_cols(g):
    return jnp.broadcast_to(g[:, :, None], g.shape + (LANES,))


def kernel(x_prompt, x_sample, mem_prompt, cache_swa_k, cache_swa_v, state_conv, cache_mem_k, cache_mem_v,
           norm_mix, w_in_att, q_norm_att, k_norm_att, sinks, w_in_conv, conv_w, norm_mem, w_mem_kv,
           q_norm_x, k_norm_x, w_out, norm_mlp, w_up, w_down):
    batch, seq, _ = x_prompt.shape
    nseq, dec, _ = x_sample.shape
    n_mem = mem_prompt.shape[1]
    assert seq % TOKEN_TILE == 0 and (nseq * dec) == TOKEN_TILE and nseq % SEQ_BLOCK == 0
    assert LANES % (SEQ_BLOCK * dec) == 0

    kv0, kv1 = CONV_CH, CONV_CH + 2 * KV_DIM
    w_q = jnp.concatenate([_group_major(w_in_att[:, :, :kv0], 2), w_in_att[:, :, kv1:]], axis=2).astype(BF16)
    w_kv_t = jnp.swapaxes(w_in_att[:, :, kv0:kv1], 1, 2).astype(BF16)
    att_layers = jnp.arange(DEPTH) % 2 == 0
    w_o = jnp.concatenate(
        [jnp.where(att_layers[:, None, None], _group_major(w_out[:, :CONV_CH], 1), w_out[:, :CONV_CH]),
         w_out[:, CONV_CH:]], axis=1).astype(BF16)
    w_u, w_d, w_conv = w_up[0].astype(BF16), w_down[0].astype(BF16), None
    w_mkv_t = jnp.swapaxes(w_mem_kv, 1, 2).astype(BF16)

    g_mix = norm_mix[:, None, :]
    g_mlp = norm_mlp[:, None, :]
    gq = jnp.tile(q_norm_att, (1, N_Q_HEADS))[:, None, :]
    gx = jnp.tile(q_norm_x, (1, N_X_HEADS))[:, None, :]
    gk_cols = _gain_cols(k_norm_att)
    gkx_cols = _gain_cols(k_norm_x)

    cos_p, sin_p = _rope_tables(jnp.arange(seq, dtype=F32))
    cos_s, sin_s = _rope_tables(PAST_LEN + (jnp.arange(nseq * dec) % dec).astype(F32))
    rope_p = _rope_lane_tables(cos_p, sin_p) + (cos_p.T, sin_p.T)
    rope_s = _rope_lane_tables(cos_s, sin_s) + (cos_s.T, sin_s.T)

    mk_p, mv_p = _mem_kv(mem_prompt.reshape(batch * n_mem, D_MODEL), norm_mem[:, None, :], w_mkv_t, gkx_cols,
                         batch=batch, n_mem=n_mem)
    mk_s = _feature_major(cache_mem_k)
    mv_s = _feature_major(cache_mem_v)
    kbuf = _feature_major(cache_swa_k)
    vbuf = _feature_major(cache_swa_v)

    xp = x_prompt.reshape(batch * seq, D_MODEL)
    xs = x_sample.reshape(nseq * dec, D_MODEL)
    tiles_per_seq = seq // TOKEN_TILE
    k_p, v_p, c_p, c_s = [], [], [], []
    cache_s = None
    for i in range(DEPTH):
        j = i // 2
        if i % 2 == 0:
            q, qx, kt, vt, kt32, vt32 = _in_att(xp, i, j, g_mix, w_q, w_kv_t, gq, gx, gk_cols, *rope_p,
                                                 tiles_per_seq=tiles_per_seq, keep_cols=WINDOW, act_dtype=BF16)
            k_p.append(kt32)
            v_p.append(vt32)
            o_self = _swa_prompt(sinks[j], q, kt, vt, batch=batch, seq=seq)
        else:
            o_self, qx, tail = _in_conv(xp, i, j, g_mix, w_conv, gx, conv_w, None,
                                        tiles_per_seq=tiles_per_seq, period=0,
                                        keep_rows=SUBLANES, act_dtype=BF16)
            c_p.append(tail.reshape(batch, SUBLANES, CONV_CH)[:, SUBLANES - (CONV_W - 1):])
        convert = []
        if i + 1 < DEPTH:
            convert = [(w_up, i + 1), (w_down, i + 1)] + ([(w_in_conv, (i + 1) // 2)] if i % 2 == 0 else [])
        xp, *w_next = _out_mlp(xp, o_self, qx, i, w_o, g_mlp, w_u, w_d, memory=(mk_p, mv_p),
                               tiles_per_batch=tiles_per_seq, convert=convert)

        if i % 2 == 0:
            q, qx, kt, vt = _in_att(xs, i, j, g_mix, w_q, w_kv_t, gq, gx, gk_cols, *rope_s,
                                    tiles_per_seq=1, keep_cols=0, act_dtype=F32)
            o_self, *cache_s = _swa_sample(sinks[j], q, kt, vt, kbuf, vbuf, j, cache_s, dec=dec)
        else:
            prefix = jnp.pad(state_conv[j], ((0, 0), (0, dec - (CONV_W - 1)), (0, 0)))
            o_self, qx, tail = _in_conv(xs, i, j, g_mix, w_conv, gx, conv_w,
                                        prefix.reshape(nseq * dec, CONV_CH),
                                        tiles_per_seq=1, period=dec,
                                        keep_rows=nseq * dec, act_dtype=F32)
            c_s.append(tail.reshape(nseq, dec, CONV_CH)[:, dec - (CONV_W - 1):])
        o_x = _xattn_sample(qx, mk_s, mv_s, i, dec=dec)
        xs, = _out_mlp(xs, o_self, o_x, i, w_o, g_mlp, w_u, w_d)
        if w_next:
            w_u, w_d = w_next[:2]
            w_conv = w_next[2] if len(w_next) > 2 else None

    return (xp.reshape(batch, seq, D_MODEL),
            xs.reshape(nseq, dec, D_MODEL),
            _position_major(jnp.stack(k_p), N_KV_HEADS),
            _position_major(jnp.stack(v_p), N_KV_HEADS),
            jnp.stack(c_p),
            _position_major(mk_p, N_X_HEADS),
            _position_major(mv_p, N_X_HEADS),
            _position_major(cache_s[0], N_KV_HEADS),
            _position_major(cache_s[1], N_KV_HEADS),
            jnp.stack(c_s))
```

```python
import functools

import jax
import jax.numpy as jnp
from jax import lax
from jax.experimental import pallas as pl
from jax.experimental.pallas import tpu as pltpu

F32 = jnp.float32
BF16 = jnp.bfloat16

D_MODEL = 1024
DEPTH = 4
HEAD_DIM = 64
N_Q_HEADS = 12
N_KV_HEADS = 4
GQA_GROUP = N_Q_HEADS // N_KV_HEADS
WINDOW = 128
PAST_LEN = 8192
ROPE_THETA = 10000.0
CONV_CH = N_Q_HEADS * HEAD_DIM
CONV_W = 3
N_X_HEADS = 4
X_DIM = N_X_HEADS * HEAD_DIM
KV_DIM = N_KV_HEADS * HEAD_DIM
D_FF = 4 * D_MODEL
CONV_IN = 3 * CONV_CH + X_DIM
EPS = 1e-6
NEG = -1e30
LOG2E = 1.4426950408889634
Q_SCALE = HEAD_DIM ** -0.5 * LOG2E

LANES = 128
SUBLANES = 8
MXU_DIM = 256
TOKEN_TILE = 512
FF_CHUNK = 2048
SEQ_BLOCK = 16
SEQ_GROUP = 16
PAIR = 2 * HEAD_DIM

_NT = (((1,), (1,)), ((), ()))


def _const_spec(shape):
    nd = len(shape)
    return pl.BlockSpec(shape, lambda *_: (0,) * nd, pipeline_mode=pl.Buffered(1))


def _layer_spec(shape, layer):
    nd = len(shape)
    return pl.BlockSpec((1,) + shape, lambda *_: (layer,) + (0,) * nd, pipeline_mode=pl.Buffered(1))


def _rms_rows(x, g):
    return x * lax.rsqrt(jnp.mean(x * x, axis=-1, keepdims=True) + EPS) * g


def _head_blockdiag():
    r = lax.broadcasted_iota(jnp.int32, (MXU_DIM, MXU_DIM), 0) // HEAD_DIM
    c = lax.broadcasted_iota(jnp.int32, (MXU_DIM, MXU_DIM), 1) // HEAD_DIM
    return jnp.where(r == c, 1.0, 0.0).astype(BF16)


def _head_mean_sq(z, bd):
    sq = z * z
    hi = sq.astype(BF16)
    lo = (sq - hi.astype(F32)).astype(BF16)
    parts = []
    for c in range(z.shape[1] // MXU_DIM):
        sl = slice(c * MXU_DIM, (c + 1) * MXU_DIM)
        parts.append(jnp.dot(hi[:, sl], bd, preferred_element_type=F32)
                     + jnp.dot(lo[:, sl], bd, preferred_element_type=F32))
    ms = parts[0] if len(parts) == 1 else jnp.concatenate(parts, axis=1)
    return ms * (1.0 / HEAD_DIM)


def _rope_rows(x, cos, sin_signed):
    t = x.shape[0]
    first_half = (lax.broadcasted_iota(jnp.int32, (t, LANES), 1) & (HEAD_DIM - 1)) < HEAD_DIM // 2
    parts = []
    for c in range(x.shape[1] // LANES):
        xc = x[:, c * LANES:(c + 1) * LANES]
        partner = jnp.where(first_half,
                            pltpu.roll(xc, LANES - HEAD_DIM // 2, axis=1),
                            pltpu.roll(xc, HEAD_DIM // 2, axis=1))
        parts.append(xc * cos + partner * sin_signed)
    return jnp.concatenate(parts, axis=1)


def _lane_tile(col, n):
    return col if n == LANES else jnp.concatenate([col] * (n // LANES), axis=1)


def _head_norm_cols(xt, g_col):
    t = xt.shape[1]
    g = _lane_tile(g_col, t)
    parts = []
    for h in range(xt.shape[0] // HEAD_DIM):
        blk = xt[h * HEAD_DIM:(h + 1) * HEAD_DIM, :]
        ms = jnp.mean(blk * blk, axis=0, keepdims=True)
        parts.append(blk * lax.rsqrt(ms + EPS) * g)
    return parts


def _softmax_rows(s, sink=None):
    m = jnp.max(s, axis=-1, keepdims=True)
    if sink is not None:
        m = jnp.maximum(m, sink)
    p = jnp.exp2(s - m)
    den = jnp.sum(p, axis=-1, keepdims=True)
    if sink is not None:
        den = den + jnp.exp2(sink - m)
    return p.astype(BF16), 1.0 / den


def _mem_kv_kernel(mem_ref, g_ref, wt_ref, gk_ref, mk_ref, mv_ref):
    x = mem_ref[...]
    xn = x * lax.rsqrt(jnp.mean(x * x, axis=-1, keepdims=True) + EPS)
    for i in range(DEPTH):
        h = (xn * g_ref[i]).astype(BF16)
        kvt = lax.dot_general(wt_ref[i], h, _NT, preferred_element_type=F32)
        mk_ref[i, 0] = jnp.concatenate(_head_norm_cols(kvt[:X_DIM], gk_ref[i]), axis=0)
        mv_ref[i, 0] = kvt[X_DIM:]


def _mem_kv(mem2d, norm_mem, w_mem_kv_t, gk_cols, *, batch, n_mem):
    out = jax.ShapeDtypeStruct((DEPTH, batch, X_DIM, n_mem), F32)
    return pl.pallas_call(
        _mem_kv_kernel,
        grid=(batch,),
        in_specs=[
            pl.BlockSpec((n_mem, D_MODEL), lambda b: (b, 0)),
            _const_spec((DEPTH, 1, D_MODEL)),
            _const_spec((DEPTH, 2 * X_DIM, D_MODEL)),
            _const_spec((DEPTH, HEAD_DIM, LANES)),
        ],
        out_specs=[pl.BlockSpec((DEPTH, 1, X_DIM, n_mem), lambda b: (0, b, 0, 0))] * 2,
        out_shape=[out, out],
        name="mem_kv",
    )(mem2d, norm_mem, w_mem_kv_t, gk_cols)


def _att_mixer(h, wq, wkv_t, gq, gx, gk_col, cos, sin, cos_t, sin_t):
    bd = _head_blockdiag()
    z = jnp.dot(h, wq, preferred_element_type=F32)
    q = z[:, :CONV_CH]
    q = q * lax.rsqrt(_head_mean_sq(q, bd) + EPS) * gq
    q = _rope_rows(q, cos, sin) * Q_SCALE
    qx = z[:, CONV_CH:]
    qx = qx * lax.rsqrt(_head_mean_sq(qx, bd) + EPS) * gx * Q_SCALE
    kvt = lax.dot_general(wkv_t, h, _NT, preferred_element_type=F32)
    half = HEAD_DIM // 2
    k_parts = []
    for blk in _head_norm_cols(kvt[:KV_DIM], gk_col):
        x1, x2 = blk[:half], blk[half:]
        k_parts += [x1 * cos_t - x2 * sin_t, x2 * cos_t + x1 * sin_t]
    return q, qx, jnp.concatenate(k_parts, axis=0), kvt[KV_DIM:]


def _in_att_kernel(x_ref, g_ref, wq_ref, wkv_ref, gq_ref, gx_ref, gk_ref, cos_ref, sin_ref, cost_ref, sint_ref,
                   q_ref, qx_ref, kt_ref, vt_ref, *keep_refs):
    h = _rms_rows(x_ref[...], g_ref[0]).astype(BF16)
    q, qx, kt, vt = _att_mixer(h, wq_ref[0], wkv_ref[0], gq_ref[0], gx_ref[0], gk_ref[0],
                               cos_ref[...], sin_ref[...], cost_ref[...], sint_ref[...])
    q_ref[...] = q.astype(q_ref.dtype)
    qx_ref[...] = qx.astype(qx_ref.dtype)
    kt_ref[...] = kt.astype(kt_ref.dtype)
    vt_ref[...] = vt.astype(vt_ref.dtype)
    if keep_refs:
        kt32_ref, vt32_ref = keep_refs
        keep = kt32_ref.shape[2]
        kt32_ref[0] = kt[:, kt.shape[1] - keep:]
        vt32_ref[0] = vt[:, vt.shape[1] - keep:]


def _in_att(x2d, layer, j, g_mix, wq, wkv_t, gq, gx, gk_cols, cos, sin, cos_t, sin_t,
            *, tiles_per_seq, keep_cols, act_dtype):
    n = x2d.shape[0]
    tm = TOKEN_TILE
    nt = n // tm
    nseq = nt // tiles_per_seq
    row = lambda i: (i, 0)
    col = lambda i: (0, i)
    pos = lambda i: (i % tiles_per_seq, 0)
    pos_t = lambda i: (0, i % tiles_per_seq)
    out_specs = [
        pl.BlockSpec((tm, CONV_CH), row),
        pl.BlockSpec((tm, X_DIM), row),
        pl.BlockSpec((KV_DIM, tm), col),
        pl.BlockSpec((KV_DIM, tm), col),
    ]
    out_shape = [
        jax.ShapeDtypeStruct((n, CONV_CH), act_dtype),
        jax.ShapeDtypeStruct((n, X_DIM), act_dtype),
        jax.ShapeDtypeStruct((KV_DIM, n), act_dtype),
        jax.ShapeDtypeStruct((KV_DIM, n), act_dtype),
    ]
    if keep_cols:
        out_specs += [pl.BlockSpec((1, KV_DIM, keep_cols), lambda i: (i // tiles_per_seq, 0, 0))] * 2
        out_shape += [jax.ShapeDtypeStruct((nseq, KV_DIM, keep_cols), F32)] * 2
    return pl.pallas_call(
        _in_att_kernel,
        grid=(nt,),
        in_specs=[
            pl.BlockSpec((tm, D_MODEL), row),
            _layer_spec((1, D_MODEL), layer),
            _layer_spec((D_MODEL, CONV_CH + X_DIM), j),
            _layer_spec((2 * KV_DIM, D_MODEL), j),
            _layer_spec((1, CONV_CH), j),
            _layer_spec((1, X_DIM), layer),
            _layer_spec((HEAD_DIM, LANES), j),
            pl.BlockSpec((tm, LANES), pos),
            pl.BlockSpec((tm, LANES), pos),
            pl.BlockSpec((HEAD_DIM // 2, tm), pos_t),
            pl.BlockSpec((HEAD_DIM // 2, tm), pos_t),
        ],
        out_specs=out_specs,
        out_shape=out_shape,
        name="in_att",
    )(x2d, g_mix, wq, wkv_t, gq, gx, gk_cols, cos, sin, cos_t, sin_t)


def _conv_mixer(h, w_ref, cw, gx, gbuf, tail_ref, new_sequence, *, period=0, prefix_ref=None, pbuf=None):
    t = h.shape[0]
    keep = tail_ref.shape[0]

    def proj(lo, width):
        return jnp.dot(h, w_ref[:, lo:lo + width], preferred_element_type=F32)

    @pl.when(new_sequence)
    def _():
        gbuf[0:SUBLANES, :] = jnp.zeros((SUBLANES, CONV_CH), F32)

    if period:
        pbuf[0:t, :] = prefix_ref[...]
        pbuf[t:t + SUBLANES, :] = jnp.zeros((SUBLANES, CONV_CH), F32)
        tok = lax.broadcasted_iota(jnp.int32, (t, MXU_DIM), 0) % period
    chunks = []
    for c in range(CONV_CH // MXU_DIM):
        sl = slice(c * MXU_DIM, (c + 1) * MXU_DIM)
        gate_b = proj(sl.start, MXU_DIM)
        u = proj(CONV_CH + sl.start, MXU_DIM) * proj(2 * CONV_CH + sl.start, MXU_DIM)
        gbuf[SUBLANES:SUBLANES + t, sl] = u
        back1 = gbuf[SUBLANES - 1:SUBLANES - 1 + t, sl]
        back2 = gbuf[SUBLANES - 2:SUBLANES - 2 + t, sl]
        if period:
            back1 = jnp.where(tok >= 1, back1, pbuf[1:1 + t, sl])
            back2 = jnp.where(tok >= 2, back2, pbuf[0:t, sl])
        y = back2 * cw[0:1, sl] + back1 * cw[1:2, sl] + u * cw[2:3, sl]
        chunks.append(gate_b * y)
        tail_ref[:, sl] = u[t - keep:, :]
    gbuf[0:SUBLANES, :] = gbuf[t:t + SUBLANES, :]
    qx = proj(3 * CONV_CH, X_DIM)
    qx = qx * lax.rsqrt(_head_mean_sq(qx, _head_blockdiag()) + EPS) * gx
    return chunks, qx * Q_SCALE


def _in_conv_kernel(x_ref, g_ref, w_ref, gx_ref, cw_ref, prefix_ref, o_ref, qx_ref, gt_ref, gbuf, pbuf, *, period):
    h = _rms_rows(x_ref[...], g_ref[0]).astype(BF16)
    chunks, qx = _conv_mixer(h, w_ref, cw_ref[0], gx_ref[0], gbuf, gt_ref, True,
                             period=period, prefix_ref=prefix_ref, pbuf=pbuf)
    o_ref[...] = jnp.concatenate(chunks, axis=1).astype(o_ref.dtype)
    qx_ref[...] = qx.astype(qx_ref.dtype)


def _in_conv(x2d, layer, j, g_mix, w, gx, cw, prefix, *, period, act_dtype):
    n = x2d.shape[0]
    tm = TOKEN_TILE
    row = lambda i: (i, 0)
    return pl.pallas_call(
        functools.partial(_in_conv_kernel, period=period),
        grid=(n // tm,),
        in_specs=[
            pl.BlockSpec((tm, D_MODEL), row),
            _layer_spec((1, D_MODEL), layer),
            _const_spec((D_MODEL, CONV_IN)),
            _layer_spec((1, X_DIM), layer),
            _layer_spec((CONV_W, CONV_CH), j),
            pl.BlockSpec((tm, CONV_CH), row),
        ],
        out_specs=[
            pl.BlockSpec((tm, CONV_CH), row),
            pl.BlockSpec((tm, X_DIM), row),
            pl.BlockSpec((tm, CONV_CH), row),
        ],
        out_shape=[
            jax.ShapeDtypeStruct((n, CONV_CH), act_dtype),
            jax.ShapeDtypeStruct((n, X_DIM), act_dtype),
            jax.ShapeDtypeStruct((n, CONV_CH), F32),
        ],
        scratch_shapes=[pltpu.VMEM((tm + 2 * SUBLANES, CONV_CH), F32), pltpu.VMEM((tm + SUBLANES, CONV_CH), F32)],
        name="in_conv",
    )(x2d, g_mix, w, gx, cw, prefix)


def _pair_lhs(slabs):
    low = lax.broadcasted_iota(jnp.int32, slabs[0].shape, 1) < HEAD_DIM
    zero = jnp.zeros_like(slabs[0])
    parts = []
    for s in slabs:
        parts += [jnp.where(low, s, zero), jnp.where(low, zero, s)]
    return jnp.concatenate(parts, axis=0)


def _swa_bias(sinks):
    r = lax.broadcasted_iota(jnp.int32, (WINDOW, 2 * WINDOW), 0)
    c = lax.broadcasted_iota(jnp.int32, (WINDOW, 2 * WINDOW), 1)
    band = (c > r) & (c - WINDOW <= r)
    base = jnp.where(jnp.stack([band, band & (c >= WINDOW)]), 0.0, NEG)[:, None]
    return jnp.where(c == 0, (sinks * LOG2E)[None, :, None, None], base)


def _swa_tile(q, kfull, vfull, bias, first_kind):
    tq = q.shape[0]
    col0 = lax.broadcasted_iota(jnp.int32, (PAIR, 2 * WINDOW), 1) == 0
    zero_slab = jnp.zeros((PAIR, 2 * WINDOW), BF16)
    low = lax.broadcasted_iota(jnp.int32, (WINDOW, PAIR), 1) < HEAD_DIM
    blocks = []
    for qb in range(tq // WINDOW):
        r0 = qb * WINDOW
        kind = first_kind if qb == 0 else 0
        slabs = [None] * (GQA_GROUP * KV_DIM // PAIR)
        for pr in range(KV_DIM // PAIR):
            kslab = jnp.where(col0, zero_slab, kfull[pr * PAIR:(pr + 1) * PAIR, r0:r0 + 2 * WINDOW])
            vslab = jnp.where(col0, zero_slab, vfull[pr * PAIR:(pr + 1) * PAIR, r0:r0 + 2 * WINDOW])
            lhs = _pair_lhs([q[r0:r0 + WINDOW, g * KV_DIM + pr * PAIR:g * KV_DIM + (pr + 1) * PAIR]
                             for g in range(GQA_GROUP)])
            s_all = jnp.dot(lhs, kslab, preferred_element_type=F32)
            probs, inv = [], []
            for g in range(GQA_GROUP):
                for e in range(2):
                    idx = g * 2 + e
                    head = (2 * pr + e) * GQA_GROUP + g
                    p, inv_den = _softmax_rows(s_all[idx * WINDOW:(idx + 1) * WINDOW] + bias[kind, head])
                    probs.append(p)
                    inv.append(inv_den)
            res = lax.dot_general(jnp.concatenate(probs, axis=0), vslab, _NT,
                                  preferred_element_type=F32)
            for g in range(GQA_GROUP):
                lo = res[(2 * g) * WINDOW:(2 * g + 1) * WINDOW] * inv[2 * g]
                hi = res[(2 * g + 1) * WINDOW:(2 * g + 2) * WINDOW] * inv[2 * g + 1]
                slabs[g * (KV_DIM // PAIR) + pr] = jnp.where(low, lo, hi)
        blocks.append(jnp.concatenate(slabs, axis=1).astype(BF16))
    return jnp.concatenate(blocks, axis=0)


def _swa_prompt_kernel(bias, q_ref, kc_ref, kp_ref, vc_ref, vp_ref, o_ref, kfull, vfull):
    kfull[:, 0:WINDOW] = kp_ref[...]
    kfull[:, WINDOW:] = kc_ref[...]
    vfull[:, 0:WINDOW] = vp_ref[...]
    vfull[:, WINDOW:] = vc_ref[...]
    first_kind = jnp.where(pl.program_id(1) == 0, 1, 0)
    o_ref[...] = _swa_tile(q_ref, kfull, vfull, bias, first_kind)


def _swa_prompt(sinks, q, kt, vt, *, batch, seq):
    tq = TOKEN_TILE
    nt = seq // tq
    per = tq // WINDOW
    rows = lambda b, j: (b * nt + j, 0)
    cur = lambda b, j: (0, b * nt + j)
    prev = lambda b, j: (0, b * nt * per + jnp.maximum(j * per - 1, 0))
    return pl.pallas_call(
        _swa_prompt_kernel,
        grid=(batch, nt),
        in_specs=[
            _const_spec((2, N_Q_HEADS, WINDOW, 2 * WINDOW)),
            pl.BlockSpec((tq, CONV_CH), rows),
            pl.BlockSpec((KV_DIM, tq), cur),
            pl.BlockSpec((KV_DIM, WINDOW), prev),
            pl.BlockSpec((KV_DIM, tq), cur),
            pl.BlockSpec((KV_DIM, WINDOW), prev),
        ],
        out_specs=pl.BlockSpec((tq, CONV_CH), rows),
        out_shape=jax.ShapeDtypeStruct(q.shape, BF16),
        scratch_shapes=[pltpu.VMEM((KV_DIM, tq + WINDOW), BF16)] * 2,
        name="swa_prompt",
    )(_swa_bias(sinks), q, kt, kt, vt, vt)


def _swa_sample_kernel(*refs, dec, carried, slot):
    if carried:
        sinks_ref, q_ref, kn_ref, vn_ref, kb_ref, vb_ref, _, _, o_ref, ko_ref, vo_ref = refs
    else:
        sinks_ref, q_ref, kn_ref, vn_ref, kb_ref, vb_ref, o_ref, ko_ref, vo_ref = refs
    nseq = kb_ref.shape[1]
    for other in range(ko_ref.shape[0]):
        if other != slot:
            ko_ref[other] = jnp.zeros(ko_ref.shape[1:], F32)
            vo_ref[other] = jnp.zeros(vo_ref.shape[1:], F32)
    steps_per_tile = LANES // (nseq * dec)
    base = (pl.program_id(0) % steps_per_tile) * (nseq * dec)
    grp = N_KV_HEADS * dec
    rows = GQA_GROUP * grp
    ri = lax.broadcasted_iota(jnp.int32, (rows, KV_DIM), 0)
    li = lax.broadcasted_iota(jnp.int32, (rows, KV_DIM), 1)
    head_mask = ((ri % grp) // dec) == (li // HEAD_DIM)
    tok = lax.broadcasted_iota(jnp.int32, (rows, WINDOW + LANES), 0) % dec
    col = lax.broadcasted_iota(jnp.int32, (rows, WINDOW + LANES), 1)
    cache_mask = (col < WINDOW) & (col > tok)
    lane = lax.broadcasted_iota(jnp.int32, (KV_DIM, WINDOW), 1)
    sink = jnp.concatenate(
        [jnp.full((dec, 1), sinks_ref[kv * GQA_GROUP + g] * LOG2E, F32)
         for g in range(GQA_GROUP) for kv in range(N_KV_HEADS)], axis=0)
    kn = kn_ref[...]
    vn = vn_ref[...]
    kn16 = kn.astype(BF16)
    vn16 = vn.astype(BF16)
    for s0 in range(0, nseq, SEQ_GROUP):
        seqs = range(s0, s0 + SEQ_GROUP)
        scores, values = [], []
        for s in seqs:
            off = base + s * dec
            new = col - (WINDOW + off)
            mask = cache_mask | ((new >= 0) & (new <= tok))
            qs = q_ref[pl.ds(s * dec, dec), :]
            qbd = jnp.concatenate(
                [jnp.concatenate([qs[:, g * KV_DIM:(g + 1) * KV_DIM]] * N_KV_HEADS, axis=0)
                 for g in range(GQA_GROUP)], axis=0)
            qbd = jnp.where(head_mask, qbd, 0.0).astype(BF16)
            keys = jnp.concatenate([kb_ref[0, s].astype(BF16), kn16], axis=1)
            values.append(jnp.concatenate([vb_ref[0, s].astype(BF16), vn16], axis=1))
            scores.append(jnp.where(mask, jnp.dot(qbd, keys, preferred_element_type=F32), NEG))
        probs = [_softmax_rows(sc, sink) for sc in scores]
        for s, (p, inv_den), vals in zip(seqs, probs, values):
            o = lax.dot_general(p, vals, _NT, preferred_element_type=F32) * inv_den
            o = jnp.where(head_mask, o, 0.0)
            folded = []
            for g in range(GQA_GROUP):
                og = o[g * grp:g * grp + dec, :]
                for kv in range(1, N_KV_HEADS):
                    og = og + o[g * grp + kv * dec:g * grp + (kv + 1) * dec, :]
                folded.append(og)
            o_ref[pl.ds(s * dec, dec), :] = jnp.concatenate(folded, axis=1)
        for s in seqs:
            shift_new = (WINDOW - dec + LANES - (base + s * dec)) % LANES
            ko_ref[slot, s] = jnp.where(lane < WINDOW - dec, pltpu.roll(kb_ref[0, s], WINDOW - dec, axis=1),
                                     pltpu.roll(kn, shift_new, axis=1))
            vo_ref[slot, s] = jnp.where(lane < WINDOW - dec, pltpu.roll(vb_ref[0, s], WINDOW - dec, axis=1),
                                     pltpu.roll(vn, shift_new, axis=1))


def _swa_sample(sinks, q, kt_new, vt_new, kbuf, vbuf, j, carry, *, dec):
    nseq = kbuf.shape[1]
    sb = SEQ_BLOCK
    steps_per_tile = LANES // (sb * dec)
    row = lambda i: (i, 0)
    tile = lambda i: (0, i // steps_per_tile)
    cache = lambda i: (j, i, 0, 0)
    if carry is None:
        new_cache = pl.BlockSpec((kbuf.shape[0], sb, KV_DIM, WINDOW), lambda i: (0, i, 0, 0))
    else:
        new_cache = pl.BlockSpec((1, sb, KV_DIM, WINDOW), cache)
    in_specs = [
        pl.BlockSpec(memory_space=pltpu.SMEM),
        pl.BlockSpec((sb * dec, CONV_CH), row),
        pl.BlockSpec((KV_DIM, LANES), tile),
        pl.BlockSpec((KV_DIM, LANES), tile),
        pl.BlockSpec((1, sb, KV_DIM, WINDOW), cache),
        pl.BlockSpec((1, sb, KV_DIM, WINDOW), cache),
    ]
    args = [sinks, q, kt_new, vt_new, kbuf, vbuf]
    aliases = {}
    if carry is not None:
        aliases = {len(args): 1, len(args) + 1: 2}
        in_specs += [pl.BlockSpec(memory_space=pl.ANY)] * 2
        args += list(carry)
    return pl.pallas_call(
        functools.partial(_swa_sample_kernel, dec=dec, carried=carry is not None,
                          slot=j if carry is None else 0),
        grid=(nseq // sb,),
        in_specs=in_specs,
        out_specs=[
            pl.BlockSpec((sb * dec, CONV_CH), row),
            new_cache,
            new_cache,
        ],
        out_shape=[
            jax.ShapeDtypeStruct(q.shape, F32),
            jax.ShapeDtypeStruct(kbuf.shape, F32),
            jax.ShapeDtypeStruct(vbuf.shape, F32),
        ],
        input_output_aliases=aliases,
        name="swa_sample",
    )(*args)


def _xattn_tile(q_ref, mk_ref, mv_ref):
    tq = q_ref.shape[0]
    mk = mk_ref[0, 0].astype(BF16)
    mv = mv_ref[0, 0].astype(BF16)
    low = lax.broadcasted_iota(jnp.int32, (WINDOW, PAIR), 1) < HEAD_DIM
    blocks = []
    for rb in range(tq // WINDOW):
        r0 = rb * WINDOW
        slabs = []
        for pr in range(X_DIM // PAIR):
            lhs = _pair_lhs([q_ref[r0:r0 + WINDOW, pr * PAIR:(pr + 1) * PAIR]])
            s_all = jnp.dot(lhs, mk[pr * PAIR:(pr + 1) * PAIR], preferred_element_type=F32)
            p0, inv0 = _softmax_rows(s_all[:WINDOW])
            p1, inv1 = _softmax_rows(s_all[WINDOW:])
            res = lax.dot_general(jnp.concatenate([p0, p1], axis=0), mv[pr * PAIR:(pr + 1) * PAIR], _NT,
                                  preferred_element_type=F32)
            slabs.append(jnp.where(low, res[:WINDOW] * inv0, res[WINDOW:] * inv1))
        blocks.append(jnp.concatenate(slabs, axis=1).astype(BF16))
    return jnp.concatenate(blocks, axis=0)


def _xattn_sample_kernel(q_ref, mk_ref, mv_ref, o_ref, *, dec):
    nseq = mk_ref.shape[1]
    rows = N_X_HEADS * dec
    head_mask = (lax.broadcasted_iota(jnp.int32, (rows, X_DIM), 0) // dec) == (
        lax.broadcasted_iota(jnp.int32, (rows, X_DIM), 1) // HEAD_DIM)
    for s0 in range(0, nseq, SEQ_GROUP):
        seqs = range(s0, s0 + SEQ_GROUP)
        scores = []
        for s in seqs:
            qs = q_ref[pl.ds(s * dec, dec), :]
            qbd = jnp.where(head_mask, jnp.concatenate([qs] * N_X_HEADS, axis=0), 0.0).astype(BF16)
            scores.append(jnp.dot(qbd, mk_ref[0, s].astype(BF16), preferred_element_type=F32))
        probs = [_softmax_rows(sc) for sc in scores]
        for s, (p, inv_den) in zip(seqs, probs):
            o = lax.dot_general(p, mv_ref[0, s].astype(BF16), _NT, preferred_element_type=F32) * inv_den
            o = jnp.where(head_mask, o, 0.0)
            acc = o[0:dec, :]
            for h in range(1, N_X_HEADS):
                acc = acc + o[h * dec:(h + 1) * dec, :]
            o_ref[pl.ds(s * dec, dec), :] = acc


def _xattn_sample(qx, mk, mv, layer, *, dec):
    nseq, n_mem = mk.shape[1], mk.shape[3]
    sb = SEQ_BLOCK
    mem = lambda i: (layer, i, 0, 0)
    return pl.pallas_call(
        functools.partial(_xattn_sample_kernel, dec=dec),
        grid=(nseq // sb,),
        in_specs=[
            pl.BlockSpec((sb * dec, X_DIM), lambda i: (i, 0)),
            pl.BlockSpec((1, sb, X_DIM, n_mem), mem),
            pl.BlockSpec((1, sb, X_DIM, n_mem), mem),
        ],
        out_specs=pl.BlockSpec((sb * dec, X_DIM), lambda i: (i, 0)),
        out_shape=jax.ShapeDtypeStruct(qx.shape, F32),
        name="xattn_sample",
    )(qx, mk, mv)


def _mlp_tile(x, o, wo, gm, wu_ref, wd_ref):
    x1 = x + jnp.dot(o, wo, preferred_element_type=F32)
    hm = _rms_rows(x1, gm).astype(BF16)
    acc = x1
    for c in range(D_FF // FF_CHUNK):
        a = jnp.dot(hm, wu_ref[:, c * FF_CHUNK:(c + 1) * FF_CHUNK], preferred_element_type=F32)
        a = jnp.square(jnp.maximum(a, 0.0)).astype(BF16)
        acc = acc + jnp.dot(a, wd_ref[c * FF_CHUNK:(c + 1) * FF_CHUNK, :], preferred_element_type=F32)
    return acc


def _cast_blocks(srcs, dsts):
    for src, dst in zip(srcs, dsts):
        dst[...] = src[0].astype(BF16)


def _convert_specs(convert, steps):
    in_specs, args, out_specs, out_shape = [], [], [], []
    for stack, l in convert:
        _, r, c = stack.shape
        in_specs.append(pl.BlockSpec((1, r // steps, c), lambda i, l=l: (l, i, 0)))
        args.append(stack)
        out_specs.append(pl.BlockSpec((r // steps, c), lambda i: (i, 0)))
        out_shape.append(jax.ShapeDtypeStruct((r, c), BF16))
    return in_specs, args, out_specs, out_shape


def _out_mlp_kernel(*refs, fuse_xattn, n_convert):
    n_in = len(refs) - 1 - n_convert
    _cast_blocks(refs[n_in - n_convert:n_in], refs[n_in + 1:])
    refs = refs[:n_in - n_convert] + (refs[n_in],)
    if fuse_xattn:
        x_ref, os_ref, qx_ref, mk_ref, mv_ref, wo_ref, gm_ref, wu_ref, wd_ref, y_ref = refs
        o_x = _xattn_tile(qx_ref, mk_ref, mv_ref)
    else:
        x_ref, os_ref, ox_ref, wo_ref, gm_ref, wu_ref, wd_ref, y_ref = refs
        o_x = ox_ref[...].astype(BF16)
    o = jnp.concatenate([os_ref[...].astype(BF16), o_x], axis=1)
    y_ref[...] = _mlp_tile(x_ref[...], o, wo_ref[0], gm_ref[0], wu_ref, wd_ref)


def _out_mlp(x2d, o_self, o_x, layer, wo, gm, wu, wd, *, memory=None, tiles_per_batch=None, convert=()):
    n = x2d.shape[0]
    tm = TOKEN_TILE
    steps = n // tm
    row = lambda i: (i, 0)
    in_specs = [
        pl.BlockSpec((tm, D_MODEL), row),
        pl.BlockSpec((tm, CONV_CH), row),
        pl.BlockSpec((tm, X_DIM), row),
    ]
    args = [x2d, o_self, o_x]
    if memory is not None:
        n_mem = memory[0].shape[3]
        mem = lambda i: (layer, i // tiles_per_batch, 0, 0)
        in_specs += [pl.BlockSpec((1, 1, X_DIM, n_mem), mem)] * 2
        args += list(memory)
    in_specs += [
        _layer_spec((CONV_CH + X_DIM, D_MODEL), layer),
        _layer_spec((1, D_MODEL), layer),
        _const_spec((D_MODEL, D_FF)),
        _const_spec((D_FF, D_MODEL)),
    ]
    args += [wo, gm, wu, wd]
    cv_in, cv_args, cv_out, cv_shape = _convert_specs(convert, steps)
    return pl.pallas_call(
        functools.partial(_out_mlp_kernel, fuse_xattn=memory is not None, n_convert=len(convert)),
        grid=(steps,),
        in_specs=in_specs + cv_in,
        out_specs=[pl.BlockSpec((tm, D_MODEL), row)] + cv_out,
        out_shape=[jax.ShapeDtypeStruct((n, D_MODEL), F32)] + cv_shape,
        name="out_mlp",
    )(*args, *cv_args)


def _conv_layer_kernel(*refs, tiles_per_seq, n_convert):
    (x_ref, g_ref, w_ref, gx_ref, cw_ref, mk_ref, mv_ref, wo_ref, gm_ref, wu_ref, wd_ref) = refs[:11]
    y_ref, tail_ref = refs[11 + n_convert:13 + n_convert]
    gbuf = refs[-1]
    _cast_blocks(refs[11:11 + n_convert], refs[13 + n_convert:-1])
    x = x_ref[...]
    h = _rms_rows(x, g_ref[0]).astype(BF16)
    chunks, qx = _conv_mixer(h, w_ref, cw_ref[0], gx_ref[0], gbuf, tail_ref,
                             pl.program_id(0) % tiles_per_seq == 0)
    o_x = _xattn_tile(qx.astype(BF16), mk_ref, mv_ref)
    o = jnp.concatenate([c.astype(BF16) for c in chunks] + [o_x], axis=1)
    y_ref[...] = _mlp_tile(x, o, wo_ref[0], gm_ref[0], wu_ref, wd_ref)


def _conv_layer(x2d, layer, j, g_mix, w_conv, gx, cw, memory, wo, gm, wu, wd,
                *, tiles_per_seq, keep_rows, convert=()):
    n = x2d.shape[0]
    tm = TOKEN_TILE
    steps = n // tm
    nseq = steps // tiles_per_seq
    n_mem = memory[0].shape[3]
    row = lambda i: (i, 0)
    mem = lambda i: (layer, i // tiles_per_seq, 0, 0)
    cv_in, cv_args, cv_out, cv_shape = _convert_specs(convert, steps)
    return pl.pallas_call(
        functools.partial(_conv_layer_kernel, tiles_per_seq=tiles_per_seq, n_convert=len(convert)),
        grid=(steps,),
        in_specs=[
            pl.BlockSpec((tm, D_MODEL), row),
            _layer_spec((1, D_MODEL), layer),
            _const_spec((D_MODEL, CONV_IN)),
            _layer_spec((1, X_DIM), layer),
            _layer_spec((CONV_W, CONV_CH), j),
            pl.BlockSpec((1, 1, X_DIM, n_mem), mem),
            pl.BlockSpec((1, 1, X_DIM, n_mem), mem),
            _layer_spec((CONV_CH + X_DIM, D_MODEL), layer),
            _layer_spec((1, D_MODEL), layer),
            _const_spec((D_MODEL, D_FF)),
            _const_spec((D_FF, D_MODEL)),
        ] + cv_in,
        out_specs=[
            pl.BlockSpec((tm, D_MODEL), row),
            pl.BlockSpec((keep_rows, CONV_CH), lambda i: (i // tiles_per_seq, 0)),
        ] + cv_out,
        out_shape=[
            jax.ShapeDtypeStruct((n, D_MODEL), F32),
            jax.ShapeDtypeStruct((nseq * keep_rows, CONV_CH), F32),
        ] + cv_shape,
        scratch_shapes=[pltpu.VMEM((tm + 2 * SUBLANES, CONV_CH), F32)],
        name="conv_layer",
    )(x2d, g_mix, w_conv, gx, cw, *memory, wo, gm, wu, wd, *cv_args)


def _rope_tables(pos):
    half = HEAD_DIM // 2
    inv = ROPE_THETA ** (-jnp.arange(half, dtype=F32) * 2.0 / HEAD_DIM)
    ang = pos[:, None] * inv[None, :]
    return jnp.cos(ang), jnp.sin(ang)


def _rope_lane_tables(cos, sin):
    reps = LANES // (HEAD_DIM // 2)
    return jnp.tile(cos, (1, reps)), jnp.concatenate([-sin, sin] * (reps // 2), axis=1)


def _group_major(w, axis):
    shape = w.shape
    w = w.reshape(shape[:axis] + (N_KV_HEADS, GQA_GROUP, HEAD_DIM) + shape[axis + 1:])
    return jnp.swapaxes(w, axis, axis + 1).reshape(shape)


def _feature_major(cache):
    lead = cache.shape[:-3]
    pos, heads, hd = cache.shape[-3:]
    nd = len(lead)
    perm = tuple(range(nd)) + (nd + 1, nd + 2, nd)
    return jnp.transpose(cache, perm).reshape(lead + (heads * hd, pos))


def _position_major(cache_t, heads):
    lead = cache_t.shape[:-2]
    pos = cache_t.shape[-1]
    nd = len(lead)
    perm = tuple(range(nd)) + (nd + 2, nd, nd + 1)
    return jnp.transpose(cache_t.reshape(lead + (heads, HEAD_DIM, pos)), perm)


def _gain_cols(g):
    return jnp.broadcast_to(g[:, :, None], g.shape + (LANES,))


def kernel(x_prompt, x_sample, mem_prompt, cache_swa_k, cache_swa_v, state_conv, cache_mem_k, cache_mem_v,
           norm_mix, w_in_att, q_norm_att, k_norm_att, sinks, w_in_conv, conv_w, norm_mem, w_mem_kv,
           q_norm_x, k_norm_x, w_out, norm_mlp, w_up, w_down):
    batch, seq, _ = x_prompt.shape
    nseq, dec, _ = x_sample.shape
    n_mem = mem_prompt.shape[1]
    assert seq % TOKEN_TILE == 0 and (nseq * dec) == TOKEN_TILE and nseq % SEQ_BLOCK == 0
    assert LANES % (SEQ_BLOCK * dec) == 0

    kv0, kv1 = CONV_CH, CONV_CH + 2 * KV_DIM
    w_q = jnp.concatenate([_group_major(w_in_att[:, :, :kv0], 2), w_in_att[:, :, kv1:]], axis=2).astype(BF16)
    w_kv_t = jnp.swapaxes(w_in_att[:, :, kv0:kv1], 1, 2).astype(BF16)
    att_layers = jnp.arange(DEPTH) % 2 == 0
    w_o = jnp.concatenate(
        [jnp.where(att_layers[:, None, None], _group_major(w_out[:, :CONV_CH], 1), w_out[:, :CONV_CH]),
         w_out[:, CONV_CH:]], axis=1).astype(BF16)
    w_u, w_d, w_conv = w_up[0].astype(BF16), w_down[0].astype(BF16), None
    w_mkv_t = jnp.swapaxes(w_mem_kv, 1, 2).astype(BF16)

    g_mix = norm_mix[:, None, :]
    g_mlp = norm_mlp[:, None, :]
    gq = jnp.tile(q_norm_att, (1, N_Q_HEADS))[:, None, :]
    gx = jnp.tile(q_norm_x, (1, N_X_HEADS))[:, None, :]
    gk_cols = _gain_cols(k_norm_att)
    gkx_cols = _gain_cols(k_norm_x)

    cos_p, sin_p = _rope_tables(jnp.arange(seq, dtype=F32))
    cos_s, sin_s = _rope_tables(PAST_LEN + (jnp.arange(nseq * dec) % dec).astype(F32))
    rope_p = _rope_lane_tables(cos_p, sin_p) + (cos_p.T, sin_p.T)
    rope_s = _rope_lane_tables(cos_s, sin_s) + (cos_s.T, sin_s.T)

    mk_p, mv_p = _mem_kv(mem_prompt.reshape(batch * n_mem, D_MODEL), norm_mem[:, None, :], w_mkv_t, gkx_cols,
                         batch=batch, n_mem=n_mem)
    mk_s = _feature_major(cache_mem_k)
    mv_s = _feature_major(cache_mem_v)
    kbuf = _feature_major(cache_swa_k)
    vbuf = _feature_major(cache_swa_v)

    xp = x_prompt.reshape(batch * seq, D_MODEL)
    xs = x_sample.reshape(nseq * dec, D_MODEL)
    tiles_per_seq = seq // TOKEN_TILE
    k_p, v_p, c_p, c_s = [], [], [], []
    cache_s = None
    for i in range(DEPTH):
        j = i // 2
        convert = []
        if i + 1 < DEPTH:
            convert = [(w_up, i + 1), (w_down, i + 1)] + ([(w_in_conv, (i + 1) // 2)] if i % 2 == 0 else [])
        if i % 2 == 0:
            q, qx, kt, vt, kt32, vt32 = _in_att(xp, i, j, g_mix, w_q, w_kv_t, gq, gx, gk_cols, *rope_p,
                                                 tiles_per_seq=tiles_per_seq, keep_cols=WINDOW, act_dtype=BF16)
            k_p.append(kt32)
            v_p.append(vt32)
            o_self = _swa_prompt(sinks[j], q, kt, vt, batch=batch, seq=seq)
            xp, *w_next = _out_mlp(xp, o_self, qx, i, w_o, g_mlp, w_u, w_d, memory=(mk_p, mv_p),
                                   tiles_per_batch=tiles_per_seq, convert=convert)
            q, qx, kt, vt = _in_att(xs, i, j, g_mix, w_q, w_kv_t, gq, gx, gk_cols, *rope_s,
                                    tiles_per_seq=1, keep_cols=0, act_dtype=F32)
            o_self, *cache_s = _swa_sample(sinks[j], q, kt, vt, kbuf, vbuf, j, cache_s, dec=dec)
        else:
            xp, tail, *w_next = _conv_layer(xp, i, j, g_mix, w_conv, gx, conv_w, (mk_p, mv_p),
                                            w_o, g_mlp, w_u, w_d, tiles_per_seq=tiles_per_seq,
                                            keep_rows=SUBLANES, convert=convert)
            c_p.append(tail.reshape(batch, SUBLANES, CONV_CH)[:, SUBLANES - (CONV_W - 1):])
            prefix = jnp.pad(state_conv[j], ((0, 0), (0, dec - (CONV_W - 1)), (0, 0)))
            o_self, qx, u_s = _in_conv(xs, i, j, g_mix, w_conv, gx, conv_w,
                                       prefix.reshape(nseq * dec, CONV_CH), period=dec, act_dtype=F32)
            c_s.append(u_s.reshape(nseq, dec, CONV_CH)[:, dec - (CONV_W - 1):])
        o_x = _xattn_sample(qx, mk_s, mv_s, i, dec=dec)
        xs, = _out_mlp(xs, o_self, o_x, i, w_o, g_mlp, w_u, w_d)
        if w_next:
            w_u, w_d = w_next[:2]
            w_conv = w_next[2] if len(w_next) > 2 else None

    return (xp.reshape(batch, seq, D_MODEL),
            xs.reshape(nseq, dec, D_MODEL),
            _position_major(jnp.stack(k_p), N_KV_HEADS),
            _position_major(jnp.stack(v_p), N_KV_HEADS),
            jnp.stack(c_p),
            _position_major(mk_p, N_X_HEADS),
            _position_major(mv_p, N_X_HEADS),
            _position_major(cache_s[0], N_KV_HEADS),
            _position_major(cache_s[1], N_KV_HEADS),
            jnp.stack(c_s))
```

```python
import functools

import jax
import jax.numpy as jnp
from jax import lax
from jax.experimental import pallas as pl
from jax.experimental.pallas import tpu as pltpu

F32 = jnp.float32
BF16 = jnp.bfloat16

D_MODEL = 1024
DEPTH = 4
HEAD_DIM = 64
N_Q_HEADS = 12
N_KV_HEADS = 4
GQA_GROUP = N_Q_HEADS // N_KV_HEADS
WINDOW = 128
PAST_LEN = 8192
ROPE_THETA = 10000.0
CONV_CH = N_Q_HEADS * HEAD_DIM
CONV_W = 3
N_X_HEADS = 4
X_DIM = N_X_HEADS * HEAD_DIM
KV_DIM = N_KV_HEADS * HEAD_DIM
D_FF = 4 * D_MODEL
CONV_IN = 3 * CONV_CH + X_DIM
EPS = 1e-6
NEG = -1e30
LOG2E = 1.4426950408889634
Q_SCALE = HEAD_DIM ** -0.5 * LOG2E

LANES = 128
SUBLANES = 8
MXU_DIM = 256
TOKEN_TILE = 512
FF_CHUNK = 2048
SEQ_BLOCK = 16
SEQ_GROUP = 16
PAIR = 2 * HEAD_DIM

_NT = (((1,), (1,)), ((), ()))


def _const_spec(shape):
    nd = len(shape)
    return pl.BlockSpec(shape, lambda *_: (0,) * nd, pipeline_mode=pl.Buffered(1))


def _layer_spec(shape, layer):
    nd = len(shape)
    return pl.BlockSpec((1,) + shape, lambda *_: (layer,) + (0,) * nd, pipeline_mode=pl.Buffered(1))


def _rms_rows(x, g):
    return x * lax.rsqrt(jnp.mean(x * x, axis=-1, keepdims=True) + EPS) * g


def _head_blockdiag():
    r = lax.broadcasted_iota(jnp.int32, (MXU_DIM, MXU_DIM), 0) // HEAD_DIM
    c = lax.broadcasted_iota(jnp.int32, (MXU_DIM, MXU_DIM), 1) // HEAD_DIM
    return jnp.where(r == c, 1.0, 0.0).astype(BF16)


def _head_mean_sq(z, bd):
    sq = z * z
    hi = sq.astype(BF16)
    lo = (sq - hi.astype(F32)).astype(BF16)
    parts = []
    for c in range(z.shape[1] // MXU_DIM):
        sl = slice(c * MXU_DIM, (c + 1) * MXU_DIM)
        parts.append(jnp.dot(hi[:, sl], bd, preferred_element_type=F32)
                     + jnp.dot(lo[:, sl], bd, preferred_element_type=F32))
    ms = parts[0] if len(parts) == 1 else jnp.concatenate(parts, axis=1)
    return ms * (1.0 / HEAD_DIM)


def _rope_rows(x, cos, sin_signed):
    t = x.shape[0]
    first_half = (lax.broadcasted_iota(jnp.int32, (t, LANES), 1) & (HEAD_DIM - 1)) < HEAD_DIM // 2
    parts = []
    for c in range(x.shape[1] // LANES):
        xc = x[:, c * LANES:(c + 1) * LANES]
        partner = jnp.where(first_half,
                            pltpu.roll(xc, LANES - HEAD_DIM // 2, axis=1),
                            pltpu.roll(xc, HEAD_DIM // 2, axis=1))
        parts.append(xc * cos + partner * sin_signed)
    return jnp.concatenate(parts, axis=1)


def _lane_tile(col, n):
    return col if n == LANES else jnp.concatenate([col] * (n // LANES), axis=1)


def _head_norm_cols(xt, g_col):
    t = xt.shape[1]
    g = _lane_tile(g_col, t)
    parts = []
    for h in range(xt.shape[0] // HEAD_DIM):
        blk = xt[h * HEAD_DIM:(h + 1) * HEAD_DIM, :]
        ms = jnp.mean(blk * blk, axis=0, keepdims=True)
        parts.append(blk * lax.rsqrt(ms + EPS) * g)
    return parts


def _softmax_rows(s, sink=None):
    m = jnp.max(s, axis=-1, keepdims=True)
    if sink is not None:
        m = jnp.maximum(m, sink)
    p = jnp.exp2(s - m)
    den = jnp.sum(p, axis=-1, keepdims=True)
    if sink is not None:
        den = den + jnp.exp2(sink - m)
    return p.astype(BF16), 1.0 / den


def _convert_specs(convert, steps):
    in_specs, args, out_specs, out_shape = [], [], [], []
    for stack, l in convert:
        _, r, c = stack.shape
        in_specs.append(pl.BlockSpec((1, r // steps, c), lambda i, l=l: (l, i, 0)))
        args.append(stack)
        out_specs.append(pl.BlockSpec((r // steps, c), lambda i: (i, 0)))
        out_shape.append(jax.ShapeDtypeStruct((r, c), BF16))
    return in_specs, args, out_specs, out_shape


def _mem_kv_kernel(mem_ref, g_ref, wt_ref, gk_ref, mk_ref, mv_ref):
    x = mem_ref[...]
    xn = x * lax.rsqrt(jnp.mean(x * x, axis=-1, keepdims=True) + EPS)
    for i in range(DEPTH):
        h = (xn * g_ref[i]).astype(BF16)
        kvt = lax.dot_general(wt_ref[i], h, _NT, preferred_element_type=F32)
        mk_ref[i, 0] = jnp.concatenate(_head_norm_cols(kvt[:X_DIM], gk_ref[i]), axis=0)
        mv_ref[i, 0] = kvt[X_DIM:]


def _mem_kv(mem2d, norm_mem, w_mem_kv_t, gk_cols, *, batch, n_mem):
    out = jax.ShapeDtypeStruct((DEPTH, batch, X_DIM, n_mem), F32)
    return pl.pallas_call(
        _mem_kv_kernel,
        grid=(batch,),
        in_specs=[
            pl.BlockSpec((n_mem, D_MODEL), lambda b: (b, 0)),
            _const_spec((DEPTH, 1, D_MODEL)),
            _const_spec((DEPTH, 2 * X_DIM, D_MODEL)),
            _const_spec((DEPTH, HEAD_DIM, LANES)),
        ],
        out_specs=[pl.BlockSpec((DEPTH, 1, X_DIM, n_mem), lambda b: (0, b, 0, 0))] * 2,
        out_shape=[out, out],
        name="mem_kv",
    )(mem2d, norm_mem, w_mem_kv_t, gk_cols)


def _in_att_kernel(*refs, keep_cols, n_convert):
    (x_ref, g_ref, wq_ref, wkv_ref, gq_ref, gx_ref, gk_ref, cos_ref, sin_ref, cost_ref, sint_ref) = refs[:11]
    n_out = 6 if keep_cols else 4
    outs = refs[11 + n_convert:11 + n_convert + n_out]
    q_ref, qx_ref, kt_ref, vt_ref = outs[:4]
    for src, dst in zip(refs[11:11 + n_convert], refs[11 + n_convert + n_out:]):
        dst[...] = src[0].astype(BF16)
    t = x_ref.shape[0]
    h = _rms_rows(x_ref[...], g_ref[0]).astype(BF16)
    bd = _head_blockdiag()
    z = jnp.dot(h, wq_ref[0], preferred_element_type=F32)
    q = z[:, :CONV_CH]
    q = q * lax.rsqrt(_head_mean_sq(q, bd) + EPS) * gq_ref[0]
    q_ref[...] = (_rope_rows(q, cos_ref[...], sin_ref[...]) * Q_SCALE).astype(q_ref.dtype)
    qx = z[:, CONV_CH:]
    qx = qx * lax.rsqrt(_head_mean_sq(qx, bd) + EPS) * gx_ref[0]
    qx_ref[...] = (qx * Q_SCALE).astype(qx_ref.dtype)

    kvt = lax.dot_general(wkv_ref[0], h, _NT, preferred_element_type=F32)
    cos_t, sin_t = cost_ref[...], sint_ref[...]
    half = HEAD_DIM // 2
    k_parts = []
    for blk in _head_norm_cols(kvt[:KV_DIM], gk_ref[0]):
        x1, x2 = blk[:half], blk[half:]
        k_parts += [x1 * cos_t - x2 * sin_t, x2 * cos_t + x1 * sin_t]
    kt = jnp.concatenate(k_parts, axis=0)
    vt = kvt[KV_DIM:]
    kt_ref[...] = kt.astype(kt_ref.dtype)
    vt_ref[...] = vt.astype(vt_ref.dtype)
    if keep_cols:
        kt32_ref, vt32_ref = outs[4:]
        kt32_ref[0] = kt[:, t - keep_cols:]
        vt32_ref[0] = vt[:, t - keep_cols:]


def _in_att(x2d, layer, j, g_mix, wq, wkv_t, gq, gx, gk_cols, cos, sin, cos_t, sin_t,
            *, tiles_per_seq, keep_cols, act_dtype, convert=()):
    n = x2d.shape[0]
    tm = TOKEN_TILE
    nt = n // tm
    nseq = nt // tiles_per_seq
    row = lambda i: (i, 0)
    col = lambda i: (0, i)
    pos = lambda i: (i % tiles_per_seq, 0)
    pos_t = lambda i: (0, i % tiles_per_seq)
    out_specs = [
        pl.BlockSpec((tm, CONV_CH), row),
        pl.BlockSpec((tm, X_DIM), row),
        pl.BlockSpec((KV_DIM, tm), col),
        pl.BlockSpec((KV_DIM, tm), col),
    ]
    out_shape = [
        jax.ShapeDtypeStruct((n, CONV_CH), act_dtype),
        jax.ShapeDtypeStruct((n, X_DIM), act_dtype),
        jax.ShapeDtypeStruct((KV_DIM, n), act_dtype),
        jax.ShapeDtypeStruct((KV_DIM, n), act_dtype),
    ]
    if keep_cols:
        out_specs += [pl.BlockSpec((1, KV_DIM, keep_cols), lambda i: (i // tiles_per_seq, 0, 0))] * 2
        out_shape += [jax.ShapeDtypeStruct((nseq, KV_DIM, keep_cols), F32)] * 2
    cv_in, cv_args, cv_out, cv_shape = _convert_specs(convert, nt)
    return pl.pallas_call(
        functools.partial(_in_att_kernel, keep_cols=keep_cols, n_convert=len(convert)),
        grid=(nt,),
        in_specs=[
            pl.BlockSpec((tm, D_MODEL), row),
            _layer_spec((1, D_MODEL), layer),
            _layer_spec((D_MODEL, CONV_CH + X_DIM), j),
            _layer_spec((2 * KV_DIM, D_MODEL), j),
            _layer_spec((1, CONV_CH), j),
            _layer_spec((1, X_DIM), layer),
            _layer_spec((HEAD_DIM, LANES), j),
            pl.BlockSpec((tm, LANES), pos),
            pl.BlockSpec((tm, LANES), pos),
            pl.BlockSpec((HEAD_DIM // 2, tm), pos_t),
            pl.BlockSpec((HEAD_DIM // 2, tm), pos_t),
        ] + cv_in,
        out_specs=out_specs + cv_out,
        out_shape=out_shape + cv_shape,
        name="in_att",
    )(x2d, g_mix, wq, wkv_t, gq, gx, gk_cols, cos, sin, cos_t, sin_t, *cv_args)


def _in_conv_kernel(*refs, tiles_per_seq, period):
    if period:
        (x_ref, g_ref, w_ref, gx_ref, cw_ref, prefix_ref,
         o_ref, qx_ref, gt_ref, gbuf, pbuf) = refs
    else:
        (x_ref, g_ref, w_ref, gx_ref, cw_ref,
         o_ref, qx_ref, gt_ref, gbuf) = refs
    t = x_ref.shape[0]
    keep = gt_ref.shape[0]
    h = _rms_rows(x_ref[...], g_ref[0]).astype(BF16)

    def proj(lo, width):
        return jnp.dot(h, w_ref[:, lo:lo + width], preferred_element_type=F32)

    @pl.when(pl.program_id(0) % tiles_per_seq == 0)
    def _():
        gbuf[0:SUBLANES, :] = jnp.zeros((SUBLANES, CONV_CH), F32)

    if period:
        pbuf[0:t, :] = prefix_ref[...]
        pbuf[t:t + SUBLANES, :] = jnp.zeros((SUBLANES, CONV_CH), F32)
        tok = lax.broadcasted_iota(jnp.int32, (t, MXU_DIM), 0) % period
    cw = cw_ref[0]
    for c in range(CONV_CH // MXU_DIM):
        sl = slice(c * MXU_DIM, (c + 1) * MXU_DIM)
        gate_b = proj(sl.start, MXU_DIM)
        u = proj(CONV_CH + sl.start, MXU_DIM) * proj(2 * CONV_CH + sl.start, MXU_DIM)
        gbuf[SUBLANES:SUBLANES + t, sl] = u
        back1 = gbuf[SUBLANES - 1:SUBLANES - 1 + t, sl]
        back2 = gbuf[SUBLANES - 2:SUBLANES - 2 + t, sl]
        if period:
            back1 = jnp.where(tok >= 1, back1, pbuf[1:1 + t, sl])
            back2 = jnp.where(tok >= 2, back2, pbuf[0:t, sl])
        y = back2 * cw[0:1, sl] + back1 * cw[1:2, sl] + u * cw[2:3, sl]
        o_ref[:, sl] = (gate_b * y).astype(o_ref.dtype)
        gt_ref[:, sl] = u[t - keep:, :]
    gbuf[0:SUBLANES, :] = gbuf[t:t + SUBLANES, :]
    qx = proj(3 * CONV_CH, X_DIM)
    qx = qx * lax.rsqrt(_head_mean_sq(qx, _head_blockdiag()) + EPS) * gx_ref[0]
    qx_ref[...] = (qx * Q_SCALE).astype(qx_ref.dtype)


def _in_conv(x2d, layer, j, g_mix, w, gx, cw, prefix, *, tiles_per_seq, period, keep_rows, act_dtype):
    n = x2d.shape[0]
    tm = TOKEN_TILE
    nt = n // tm
    nseq = nt // tiles_per_seq
    row = lambda i: (i, 0)
    in_specs = [
        pl.BlockSpec((tm, D_MODEL), row),
        _layer_spec((1, D_MODEL), layer),
        _const_spec((D_MODEL, CONV_IN)),
        _layer_spec((1, X_DIM), layer),
        _layer_spec((CONV_W, CONV_CH), j),
    ]
    args = [x2d, g_mix, w, gx, cw]
    scratch = [pltpu.VMEM((tm + 2 * SUBLANES, CONV_CH), F32)]
    if period:
        in_specs.append(pl.BlockSpec((tm, CONV_CH), row))
        args.append(prefix)
        scratch.append(pltpu.VMEM((tm + SUBLANES, CONV_CH), F32))
    return pl.pallas_call(
        functools.partial(_in_conv_kernel, tiles_per_seq=tiles_per_seq, period=period),
        grid=(nt,),
        in_specs=in_specs,
        out_specs=[
            pl.BlockSpec((tm, CONV_CH), row),
            pl.BlockSpec((tm, X_DIM), row),
            pl.BlockSpec((keep_rows, CONV_CH), lambda i: (i // tiles_per_seq, 0)),
        ],
        out_shape=[
            jax.ShapeDtypeStruct((n, CONV_CH), act_dtype),
            jax.ShapeDtypeStruct((n, X_DIM), act_dtype),
            jax.ShapeDtypeStruct((nseq * keep_rows, CONV_CH), F32),
        ],
        scratch_shapes=scratch,
        name="in_conv",
    )(*args)


def _pair_lhs(slabs):
    low = lax.broadcasted_iota(jnp.int32, slabs[0].shape, 1) < HEAD_DIM
    zero = jnp.zeros_like(slabs[0])
    parts = []
    for s in slabs:
        parts += [jnp.where(low, s, zero), jnp.where(low, zero, s)]
    return jnp.concatenate(parts, axis=0)


def _swa_prompt_kernel(sinks_ref, q_ref, kc_ref, kp_ref, vc_ref, vp_ref, o_ref, kfull, vfull, bias):
    tq = q_ref.shape[0]

    @pl.when(jnp.logical_and(pl.program_id(0) == 0, pl.program_id(1) == 0))
    def _():
        r = lax.broadcasted_iota(jnp.int32, (WINDOW, 2 * WINDOW), 0)
        c = lax.broadcasted_iota(jnp.int32, (WINDOW, 2 * WINDOW), 1)
        band = (c > r) & (c - WINDOW <= r)
        for kind, visible in enumerate((band, band & (c >= WINDOW))):
            base = jnp.where(visible, 0.0, NEG)
            for h in range(N_Q_HEADS):
                bias[kind, h] = jnp.where(c == 0, sinks_ref[h] * LOG2E, base)

    first_kind = jnp.where(pl.program_id(1) == 0, 1, 0)
    kfull[:, 0:WINDOW] = kp_ref[...]
    kfull[:, WINDOW:] = kc_ref[...]
    vfull[:, 0:WINDOW] = vp_ref[...]
    vfull[:, WINDOW:] = vc_ref[...]
    col0 = lax.broadcasted_iota(jnp.int32, (PAIR, 2 * WINDOW), 1) == 0
    zero_slab = jnp.zeros((PAIR, 2 * WINDOW), BF16)
    low = lax.broadcasted_iota(jnp.int32, (WINDOW, PAIR), 1) < HEAD_DIM
    for qb in range(tq // WINDOW):
        r0 = qb * WINDOW
        kind = first_kind if qb == 0 else 0
        slabs = [None] * (GQA_GROUP * KV_DIM // PAIR)
        for pr in range(KV_DIM // PAIR):
            kslab = jnp.where(col0, zero_slab, kfull[pr * PAIR:(pr + 1) * PAIR, r0:r0 + 2 * WINDOW])
            vslab = jnp.where(col0, zero_slab, vfull[pr * PAIR:(pr + 1) * PAIR, r0:r0 + 2 * WINDOW])
            lhs = _pair_lhs([q_ref[r0:r0 + WINDOW, g * KV_DIM + pr * PAIR:g * KV_DIM + (pr + 1) * PAIR]
                             for g in range(GQA_GROUP)])
            s_all = jnp.dot(lhs, kslab, preferred_element_type=F32)
            probs, inv = [], []
            for g in range(GQA_GROUP):
                for e in range(2):
                    idx = g * 2 + e
                    head = (2 * pr + e) * GQA_GROUP + g
                    p, inv_den = _softmax_rows(s_all[idx * WINDOW:(idx + 1) * WINDOW] + bias[kind, head])
                    probs.append(p)
                    inv.append(inv_den)
            res = lax.dot_general(jnp.concatenate(probs, axis=0), vslab, _NT,
                                  preferred_element_type=F32)
            for g in range(GQA_GROUP):
                lo = res[(2 * g) * WINDOW:(2 * g + 1) * WINDOW] * inv[2 * g]
                hi = res[(2 * g + 1) * WINDOW:(2 * g + 2) * WINDOW] * inv[2 * g + 1]
                slabs[g * (KV_DIM // PAIR) + pr] = jnp.where(low, lo, hi)
        o_ref[r0:r0 + WINDOW, :] = jnp.concatenate(slabs, axis=1).astype(o_ref.dtype)


def _swa_prompt(sinks, q, kt, vt, *, batch, seq):
    tq = TOKEN_TILE
    nt = seq // tq
    per = tq // WINDOW
    rows = lambda b, j: (b * nt + j, 0)
    cur = lambda b, j: (0, b * nt + j)
    prev = lambda b, j: (0, b * nt * per + jnp.maximum(j * per - 1, 0))
    return pl.pallas_call(
        _swa_prompt_kernel,
        grid=(batch, nt),
        in_specs=[
            pl.BlockSpec(memory_space=pltpu.SMEM),
            pl.BlockSpec((tq, CONV_CH), rows),
            pl.BlockSpec((KV_DIM, tq), cur),
            pl.BlockSpec((KV_DIM, WINDOW), prev),
            pl.BlockSpec((KV_DIM, tq), cur),
            pl.BlockSpec((KV_DIM, WINDOW), prev),
        ],
        out_specs=pl.BlockSpec((tq, CONV_CH), rows),
        out_shape=jax.ShapeDtypeStruct(q.shape, BF16),
        scratch_shapes=[pltpu.VMEM((KV_DIM, tq + WINDOW), BF16)] * 2
        + [pltpu.VMEM((2, N_Q_HEADS, WINDOW, 2 * WINDOW), F32)],
        name="swa_prompt",
    )(sinks, q, kt, kt, vt, vt)


def _swa_sample_kernel(*refs, dec, carried, slot):
    if carried:
        sinks_ref, q_ref, kn_ref, vn_ref, kb_ref, vb_ref, _, _, o_ref, ko_ref, vo_ref = refs
    else:
        sinks_ref, q_ref, kn_ref, vn_ref, kb_ref, vb_ref, o_ref, ko_ref, vo_ref = refs
    nseq = kb_ref.shape[1]
    for other in range(ko_ref.shape[0]):
        if other != slot:
            ko_ref[other] = jnp.zeros(ko_ref.shape[1:], F32)
            vo_ref[other] = jnp.zeros(vo_ref.shape[1:], F32)
    steps_per_tile = LANES // (nseq * dec)
    base = (pl.program_id(0) % steps_per_tile) * (nseq * dec)
    grp = N_KV_HEADS * dec
    rows = GQA_GROUP * grp
    ri = lax.broadcasted_iota(jnp.int32, (rows, KV_DIM), 0)
    li = lax.broadcasted_iota(jnp.int32, (rows, KV_DIM), 1)
    head_mask = ((ri % grp) // dec) == (li // HEAD_DIM)
    tok = lax.broadcasted_iota(jnp.int32, (rows, WINDOW + LANES), 0) % dec
    col = lax.broadcasted_iota(jnp.int32, (rows, WINDOW + LANES), 1)
    cache_mask = (col < WINDOW) & (col > tok)
    lane = lax.broadcasted_iota(jnp.int32, (KV_DIM, WINDOW), 1)
    sink = jnp.concatenate(
        [jnp.full((dec, 1), sinks_ref[kv * GQA_GROUP + g] * LOG2E, F32)
         for g in range(GQA_GROUP) for kv in range(N_KV_HEADS)], axis=0)
    kn = kn_ref[...]
    vn = vn_ref[...]
    kn16 = kn.astype(BF16)
    vn16 = vn.astype(BF16)
    for s0 in range(0, nseq, SEQ_GROUP):
        seqs = range(s0, s0 + SEQ_GROUP)
        scores, values = [], []
        for s in seqs:
            off = base + s * dec
            new = col - (WINDOW + off)
            mask = cache_mask | ((new >= 0) & (new <= tok))
            qs = q_ref[pl.ds(s * dec, dec), :]
            qbd = jnp.concatenate(
                [jnp.concatenate([qs[:, g * KV_DIM:(g + 1) * KV_DIM]] * N_KV_HEADS, axis=0)
                 for g in range(GQA_GROUP)], axis=0)
            qbd = jnp.where(head_mask, qbd, 0.0).astype(BF16)
            keys = jnp.concatenate([kb_ref[0, s].astype(BF16), kn16], axis=1)
            values.append(jnp.concatenate([vb_ref[0, s].astype(BF16), vn16], axis=1))
            scores.append(jnp.where(mask, jnp.dot(qbd, keys, preferred_element_type=F32), NEG))
        probs = [_softmax_rows(sc, sink) for sc in scores]
        for s, (p, inv_den), vals in zip(seqs, probs, values):
            o = lax.dot_general(p, vals, _NT, preferred_element_type=F32) * inv_den
            o = jnp.where(head_mask, o, 0.0)
            folded = []
            for g in range(GQA_GROUP):
                og = o[g * grp:g * grp + dec, :]
                for kv in range(1, N_KV_HEADS):
                    og = og + o[g * grp + kv * dec:g * grp + (kv + 1) * dec, :]
                folded.append(og)
            o_ref[pl.ds(s * dec, dec), :] = jnp.concatenate(folded, axis=1)
        for s in seqs:
            shift_new = (WINDOW - dec + LANES - (base + s * dec)) % LANES
            ko_ref[slot, s] = jnp.where(lane < WINDOW - dec, pltpu.roll(kb_ref[0, s], WINDOW - dec, axis=1),
                                     pltpu.roll(kn, shift_new, axis=1))
            vo_ref[slot, s] = jnp.where(lane < WINDOW - dec, pltpu.roll(vb_ref[0, s], WINDOW - dec, axis=1),
                                     pltpu.roll(vn, shift_new, axis=1))


def _swa_sample(sinks, q, kt_new, vt_new, kbuf, vbuf, j, carry, *, dec):
    nseq = kbuf.shape[1]
    sb = SEQ_BLOCK
    steps_per_tile = LANES // (sb * dec)
    row = lambda i: (i, 0)
    tile = lambda i: (0, i // steps_per_tile)
    cache = lambda i: (j, i, 0, 0)
    if carry is None:
        new_cache = pl.BlockSpec((kbuf.shape[0], sb, KV_DIM, WINDOW), lambda i: (0, i, 0, 0))
    else:
        new_cache = pl.BlockSpec((1, sb, KV_DIM, WINDOW), cache)
    in_specs = [
        pl.BlockSpec(memory_space=pltpu.SMEM),
        pl.BlockSpec((sb * dec, CONV_CH), row),
        pl.BlockSpec((KV_DIM, LANES), tile),
        pl.BlockSpec((KV_DIM, LANES), tile),
        pl.BlockSpec((1, sb, KV_DIM, WINDOW), cache),
        pl.BlockSpec((1, sb, KV_DIM, WINDOW), cache),
    ]
    args = [sinks, q, kt_new, vt_new, kbuf, vbuf]
    aliases = {}
    if carry is not None:
        aliases = {len(args): 1, len(args) + 1: 2}
        in_specs += [pl.BlockSpec(memory_space=pl.ANY)] * 2
        args += list(carry)
    return pl.pallas_call(
        functools.partial(_swa_sample_kernel, dec=dec, carried=carry is not None,
                          slot=j if carry is None else 0),
        grid=(nseq // sb,),
        in_specs=in_specs,
        out_specs=[
            pl.BlockSpec((sb * dec, CONV_CH), row),
            new_cache,
            new_cache,
        ],
        out_shape=[
            jax.ShapeDtypeStruct(q.shape, F32),
            jax.ShapeDtypeStruct(kbuf.shape, F32),
            jax.ShapeDtypeStruct(vbuf.shape, F32),
        ],
        input_output_aliases=aliases,
        name="swa_sample",
    )(*args)


def _xattn_tile(q_ref, mk_ref, mv_ref):
    tq = q_ref.shape[0]
    mk = mk_ref[0, 0].astype(BF16)
    mv = mv_ref[0, 0].astype(BF16)
    low = lax.broadcasted_iota(jnp.int32, (WINDOW, PAIR), 1) < HEAD_DIM
    blocks = []
    for rb in range(tq // WINDOW):
        r0 = rb * WINDOW
        slabs = []
        for pr in range(X_DIM // PAIR):
            lhs = _pair_lhs([q_ref[r0:r0 + WINDOW, pr * PAIR:(pr + 1) * PAIR]])
            s_all = jnp.dot(lhs, mk[pr * PAIR:(pr + 1) * PAIR], preferred_element_type=F32)
            p0, inv0 = _softmax_rows(s_all[:WINDOW])
            p1, inv1 = _softmax_rows(s_all[WINDOW:])
            res = lax.dot_general(jnp.concatenate([p0, p1], axis=0), mv[pr * PAIR:(pr + 1) * PAIR], _NT,
                                  preferred_element_type=F32)
            slabs.append(jnp.where(low, res[:WINDOW] * inv0, res[WINDOW:] * inv1))
        blocks.append(jnp.concatenate(slabs, axis=1).astype(BF16))
    return jnp.concatenate(blocks, axis=0)


def _xattn_sample_kernel(q_ref, mk_ref, mv_ref, o_ref, *, dec):
    nseq = mk_ref.shape[1]
    rows = N_X_HEADS * dec
    head_mask = (lax.broadcasted_iota(jnp.int32, (rows, X_DIM), 0) // dec) == (
        lax.broadcasted_iota(jnp.int32, (rows, X_DIM), 1) // HEAD_DIM)
    for s0 in range(0, nseq, SEQ_GROUP):
        seqs = range(s0, s0 + SEQ_GROUP)
        scores = []
        for s in seqs:
            qs = q_ref[pl.ds(s * dec, dec), :]
            qbd = jnp.where(head_mask, jnp.concatenate([qs] * N_X_HEADS, axis=0), 0.0).astype(BF16)
            scores.append(jnp.dot(qbd, mk_ref[0, s].astype(BF16), preferred_element_type=F32))
        probs = [_softmax_rows(sc) for sc in scores]
        for s, (p, inv_den) in zip(seqs, probs):
            o = lax.dot_general(p, mv_ref[0, s].astype(BF16), _NT, preferred_element_type=F32) * inv_den
            o = jnp.where(head_mask, o, 0.0)
            acc = o[0:dec, :]
            for h in range(1, N_X_HEADS):
                acc = acc + o[h * dec:(h + 1) * dec, :]
            o_ref[pl.ds(s * dec, dec), :] = acc


def _xattn_sample(qx, mk, mv, layer, *, dec):
    nseq, n_mem = mk.shape[1], mk.shape[3]
    sb = SEQ_BLOCK
    mem = lambda i: (layer, i, 0, 0)
    return pl.pallas_call(
        functools.partial(_xattn_sample_kernel, dec=dec),
        grid=(nseq // sb,),
        in_specs=[
            pl.BlockSpec((sb * dec, X_DIM), lambda i: (i, 0)),
            pl.BlockSpec((1, sb, X_DIM, n_mem), mem),
            pl.BlockSpec((1, sb, X_DIM, n_mem), mem),
        ],
        out_specs=pl.BlockSpec((sb * dec, X_DIM), lambda i: (i, 0)),
        out_shape=jax.ShapeDtypeStruct(qx.shape, F32),
        name="xattn_sample",
    )(qx, mk, mv)


def _out_mlp_kernel(*refs, fuse_xattn, n_convert):
    n_in = len(refs) - 1 - n_convert
    for src, dst in zip(refs[n_in - n_convert:n_in], refs[n_in + 1:]):
        dst[...] = src[0].astype(BF16)
    refs = refs[:n_in - n_convert] + (refs[n_in],)
    if fuse_xattn:
        x_ref, os_ref, qx_ref, mk_ref, mv_ref, wo_ref, gm_ref, wu_ref, wd_ref, y_ref = refs
        o_x = _xattn_tile(qx_ref, mk_ref, mv_ref)
    else:
        x_ref, os_ref, ox_ref, wo_ref, gm_ref, wu_ref, wd_ref, y_ref = refs
        o_x = ox_ref[...].astype(BF16)
    o = jnp.concatenate([os_ref[...].astype(BF16), o_x], axis=1)
    x1 = x_ref[...] + jnp.dot(o, wo_ref[...], preferred_element_type=F32)
    hm = _rms_rows(x1, gm_ref[0]).astype(BF16)
    acc = x1
    for c in range(D_FF // FF_CHUNK):
        a = jnp.dot(hm, wu_ref[:, c * FF_CHUNK:(c + 1) * FF_CHUNK], preferred_element_type=F32)
        a = jnp.square(jnp.maximum(a, 0.0)).astype(BF16)
        acc = acc + jnp.dot(a, wd_ref[c * FF_CHUNK:(c + 1) * FF_CHUNK, :], preferred_element_type=F32)
    y_ref[...] = acc


def _out_mlp(x2d, o_self, o_x, layer, wo, gm, wu, wd, *, memory=None, tiles_per_batch=None, convert=()):
    n = x2d.shape[0]
    tm = TOKEN_TILE
    steps = n // tm
    row = lambda i: (i, 0)
    in_specs = [
        pl.BlockSpec((tm, D_MODEL), row),
        pl.BlockSpec((tm, CONV_CH), row),
        pl.BlockSpec((tm, X_DIM), row),
    ]
    args = [x2d, o_self, o_x]
    if memory is not None:
        n_mem = memory[0].shape[3]
        mem = lambda i: (layer, i // tiles_per_batch, 0, 0)
        in_specs += [pl.BlockSpec((1, 1, X_DIM, n_mem), mem)] * 2
        args += list(memory)
    in_specs += [
        _const_spec((CONV_CH + X_DIM, D_MODEL)),
        _layer_spec((1, D_MODEL), layer),
        _const_spec((D_MODEL, D_FF)),
        _const_spec((D_FF, D_MODEL)),
    ]
    args += [wo, gm, wu, wd]
    cv_in, cv_args, cv_out, cv_shape = _convert_specs(convert, steps)
    return pl.pallas_call(
        functools.partial(_out_mlp_kernel, fuse_xattn=memory is not None, n_convert=len(convert)),
        grid=(steps,),
        in_specs=in_specs + cv_in,
        out_specs=[pl.BlockSpec((tm, D_MODEL), row)] + cv_out,
        out_shape=[jax.ShapeDtypeStruct((n, D_MODEL), F32)] + cv_shape,
        name="out_mlp",
    )(*args, *cv_args)


def _rope_tables(pos):
    half = HEAD_DIM // 2
    inv = ROPE_THETA ** (-jnp.arange(half, dtype=F32) * 2.0 / HEAD_DIM)
    ang = pos[:, None] * inv[None, :]
    return jnp.cos(ang), jnp.sin(ang)


def _rope_lane_tables(cos, sin):
    reps = LANES // (HEAD_DIM // 2)
    return jnp.tile(cos, (1, reps)), jnp.concatenate([-sin, sin] * (reps // 2), axis=1)


def _group_major(w, axis):
    shape = w.shape
    w = w.reshape(shape[:axis] + (N_KV_HEADS, GQA_GROUP, HEAD_DIM) + shape[axis + 1:])
    return jnp.swapaxes(w, axis, axis + 1).reshape(shape)


def _feature_major(cache):
    lead = cache.shape[:-3]
    pos, heads, hd = cache.shape[-3:]
    nd = len(lead)
    perm = tuple(range(nd)) + (nd + 1, nd + 2, nd)
    return jnp.transpose(cache, perm).reshape(lead + (heads * hd, pos))


def _position_major(cache_t, heads):
    lead = cache_t.shape[:-2]
    pos = cache_t.shape[-1]
    nd = len(lead)
    perm = tuple(range(nd)) + (nd + 2, nd, nd + 1)
    return jnp.transpose(cache_t.reshape(lead + (heads, HEAD_DIM, pos)), perm)


def _gain_cols(g):
    return jnp.broadcast_to(g[:, :, None], g.shape + (LANES,))


def kernel(x_prompt, x_sample, mem_prompt, cache_swa_k, cache_swa_v, state_conv, cache_mem_k, cache_mem_v,
           norm_mix, w_in_att, q_norm_att, k_norm_att, sinks, w_in_conv, conv_w, norm_mem, w_mem_kv,
           q_norm_x, k_norm_x, w_out, norm_mlp, w_up, w_down):
    batch, seq, _ = x_prompt.shape
    nseq, dec, _ = x_sample.shape
    n_mem = mem_prompt.shape[1]
    assert seq % TOKEN_TILE == 0 and (nseq * dec) == TOKEN_TILE and nseq % SEQ_BLOCK == 0
    assert LANES % (SEQ_BLOCK * dec) == 0

    kv0, kv1 = CONV_CH, CONV_CH + 2 * KV_DIM
    w_q = jnp.concatenate([_group_major(w_in_att[:, :, :kv0], 2), w_in_att[:, :, kv1:]], axis=2).astype(BF16)
    w_kv_t = jnp.swapaxes(w_in_att[:, :, kv0:kv1], 1, 2).astype(BF16)
    w_o = [jnp.concatenate([_group_major(w_out[i, :CONV_CH], 0), w_out[i, CONV_CH:]], axis=0).astype(BF16)
           if i % 2 == 0 else w_out[i].astype(BF16) for i in range(DEPTH)]
    w_u = w_d = w_conv = None
    w_mkv_t = jnp.swapaxes(w_mem_kv, 1, 2).astype(BF16)

    g_mix = norm_mix[:, None, :]
    g_mlp = norm_mlp[:, None, :]
    gq = jnp.tile(q_norm_att, (1, N_Q_HEADS))[:, None, :]
    gx = jnp.tile(q_norm_x, (1, N_X_HEADS))[:, None, :]
    gk_cols = _gain_cols(k_norm_att)
    gkx_cols = _gain_cols(k_norm_x)

    cos_p, sin_p = _rope_tables(jnp.arange(seq, dtype=F32))
    cos_s, sin_s = _rope_tables(PAST_LEN + (jnp.arange(nseq * dec) % dec).astype(F32))
    rope_p = _rope_lane_tables(cos_p, sin_p) + (cos_p.T, sin_p.T)
    rope_s = _rope_lane_tables(cos_s, sin_s) + (cos_s.T, sin_s.T)

    mk_p, mv_p = _mem_kv(mem_prompt.reshape(batch * n_mem, D_MODEL), norm_mem[:, None, :], w_mkv_t, gkx_cols,
                         batch=batch, n_mem=n_mem)
    mk_s = _feature_major(cache_mem_k)
    mv_s = _feature_major(cache_mem_v)
    kbuf = _feature_major(cache_swa_k)
    vbuf = _feature_major(cache_swa_v)

    xp = x_prompt.reshape(batch * seq, D_MODEL)
    xs = x_sample.reshape(nseq * dec, D_MODEL)
    tiles_per_seq = seq // TOKEN_TILE
    k_p, v_p, c_p, c_s = [], [], [], []
    cache_s = None
    for i in range(DEPTH):
        j = i // 2
        if i % 2 == 0:
            first = [(w_up, 0), (w_down, 0)] if i == 0 else []
            q, qx, kt, vt, kt32, vt32, *w_first = _in_att(
                xp, i, j, g_mix, w_q, w_kv_t, gq, gx, gk_cols, *rope_p,
                tiles_per_seq=tiles_per_seq, keep_cols=WINDOW, act_dtype=BF16, convert=first)
            if w_first:
                w_u, w_d = w_first
            k_p.append(kt32)
            v_p.append(vt32)
            o_self = _swa_prompt(sinks[j], q, kt, vt, batch=batch, seq=seq)
        else:
            o_self, qx, tail = _in_conv(xp, i, j, g_mix, w_conv, gx, conv_w, None,
                                        tiles_per_seq=tiles_per_seq, period=0,
                                        keep_rows=SUBLANES, act_dtype=BF16)
            c_p.append(tail.reshape(batch, SUBLANES, CONV_CH)[:, SUBLANES - (CONV_W - 1):])
        convert = []
        if i + 1 < DEPTH:
            convert = [(w_up, i + 1), (w_down, i + 1)] + ([(w_in_conv, (i + 1) // 2)] if i % 2 == 0 else [])
        xp, *w_next = _out_mlp(xp, o_self, qx, i, w_o[i], g_mlp, w_u, w_d, memory=(mk_p, mv_p),
                               tiles_per_batch=tiles_per_seq, convert=convert)

        if i % 2 == 0:
            q, qx, kt, vt = _in_att(xs, i, j, g_mix, w_q, w_kv_t, gq, gx, gk_cols, *rope_s,
                                    tiles_per_seq=1, keep_cols=0, act_dtype=F32)
            o_self, *cache_s = _swa_sample(sinks[j], q, kt, vt, kbuf, vbuf, j, cache_s, dec=dec)
        else:
            prefix = jnp.pad(state_conv[j], ((0, 0), (0, dec - (CONV_W - 1)), (0, 0)))
            o_self, qx, tail = _in_conv(xs, i, j, g_mix, w_conv, gx, conv_w,
                                        prefix.reshape(nseq * dec, CONV_CH),
                                        tiles_per_seq=1, period=dec,
                                        keep_rows=nseq * dec, act_dtype=F32)
            c_s.append(tail.reshape(nseq, dec, CONV_CH)[:, dec - (CONV_W - 1):])
        o_x = _xattn_sample(qx, mk_s, mv_s, i, dec=dec)
        xs, = _out_mlp(xs, o_self, o_x, i, w_o[i], g_mlp, w_u, w_d)
        if w_next:
            w_u, w_d = w_next[:2]
            w_conv = w_next[2] if len(w_next) > 2 else None

    return (xp.reshape(batch, seq, D_MODEL),
            xs.reshape(nseq, dec, D_MODEL),
            _position_major(jnp.stack(k_p), N_KV_HEADS),
            _position_major(jnp.stack(v_p), N_KV_HEADS),
            jnp.stack(c_p),
            _position_major(mk_p, N_X_HEADS),
            _position_major(mv_p, N_X_HEADS),
            _position_major(cache_s[0], N_KV_HEADS),
            _position_major(cache_s[1], N_KV_HEADS),
            jnp.stack(c_s))
```

```python
import functools

import jax
import jax.numpy as jnp
from jax import lax
from jax.experimental import pallas as pl
from jax.experimental.pallas import tpu as pltpu

F32 = jnp.float32
BF16 = jnp.bfloat16

D_MODEL = 1024
DEPTH = 4
HEAD_DIM = 64
N_Q_HEADS = 12
N_KV_HEADS = 4
GQA_GROUP = N_Q_HEADS // N_KV_HEADS
WINDOW = 128
PAST_LEN = 8192
ROPE_THETA = 10000.0
CONV_CH = N_Q_HEADS * HEAD_DIM
CONV_W = 3
N_X_HEADS = 4
X_DIM = N_X_HEADS * HEAD_DIM
KV_DIM = N_KV_HEADS * HEAD_DIM
D_FF = 4 * D_MODEL
CONV_IN = 3 * CONV_CH + X_DIM
EPS = 1e-6
NEG = -1e30
LOG2E = 1.4426950408889634
Q_SCALE = HEAD_DIM ** -0.5 * LOG2E

LANES = 128
SUBLANES = 8
MXU_DIM = 256
TOKEN_TILE = 512
FF_CHUNK = 2048
SEQ_BLOCK = 16
SEQ_GROUP = 16
PAIR = 2 * HEAD_DIM

_NT = (((1,), (1,)), ((), ()))


def _const_spec(shape):
    nd = len(shape)
    return pl.BlockSpec(shape, lambda *_: (0,) * nd, pipeline_mode=pl.Buffered(1))


def _layer_spec(shape, layer):
    nd = len(shape)
    return pl.BlockSpec((1,) + shape, lambda *_: (layer,) + (0,) * nd, pipeline_mode=pl.Buffered(1))


def _rms_rows(x, g):
    return x * lax.rsqrt(jnp.mean(x * x, axis=-1, keepdims=True) + EPS) * g


def _head_blockdiag():
    r = lax.broadcasted_iota(jnp.int32, (MXU_DIM, MXU_DIM), 0) // HEAD_DIM
    c = lax.broadcasted_iota(jnp.int32, (MXU_DIM, MXU_DIM), 1) // HEAD_DIM
    return jnp.where(r == c, 1.0, 0.0).astype(BF16)


def _head_mean_sq(z, bd):
    sq = z * z
    hi = sq.astype(BF16)
    lo = (sq - hi.astype(F32)).astype(BF16)
    parts = []
    for c in range(z.shape[1] // MXU_DIM):
        sl = slice(c * MXU_DIM, (c + 1) * MXU_DIM)
        parts.append(jnp.dot(hi[:, sl], bd, preferred_element_type=F32)
                     + jnp.dot(lo[:, sl], bd, preferred_element_type=F32))
    ms = parts[0] if len(parts) == 1 else jnp.concatenate(parts, axis=1)
    return ms * (1.0 / HEAD_DIM)


def _rope_rows(x, cos, sin_signed):
    t = x.shape[0]
    first_half = (lax.broadcasted_iota(jnp.int32, (t, LANES), 1) & (HEAD_DIM - 1)) < HEAD_DIM // 2
    parts = []
    for c in range(x.shape[1] // LANES):
        xc = x[:, c * LANES:(c + 1) * LANES]
        partner = jnp.where(first_half,
                            pltpu.roll(xc, LANES - HEAD_DIM // 2, axis=1),
                            pltpu.roll(xc, HEAD_DIM // 2, axis=1))
        parts.append(xc * cos + partner * sin_signed)
    return jnp.concatenate(parts, axis=1)


def _lane_tile(col, n):
    return col if n == LANES else jnp.concatenate([col] * (n // LANES), axis=1)


def _head_norm_cols(xt, g_col):
    t = xt.shape[1]
    g = _lane_tile(g_col, t)
    parts = []
    for h in range(xt.shape[0] // HEAD_DIM):
        blk = xt[h * HEAD_DIM:(h + 1) * HEAD_DIM, :]
        ms = jnp.mean(blk * blk, axis=0, keepdims=True)
        parts.append(blk * lax.rsqrt(ms + EPS) * g)
    return parts


def _softmax_rows(s, sink=None):
    m = jnp.max(s, axis=-1, keepdims=True)
    if sink is not None:
        m = jnp.maximum(m, sink)
    p = jnp.exp2(s - m)
    den = jnp.sum(p, axis=-1, keepdims=True)
    if sink is not None:
        den = den + jnp.exp2(sink - m)
    return p.astype(BF16), 1.0 / den


def _convert_specs(convert, steps, last=None):
    last = steps - 1 if last is None else last
    in_specs, args, out_specs, out_shape = [], [], [], []
    for stack, l in convert:
        _, r, c = stack.shape
        in_specs.append(pl.BlockSpec((1, r // steps, c), lambda i, l=l: (l, jnp.minimum(i, last), 0)))
        args.append(stack)
        out_specs.append(pl.BlockSpec((r // steps, c), lambda i: (jnp.minimum(i, last), 0)))
        out_shape.append(jax.ShapeDtypeStruct((r, c), BF16))
    return in_specs, args, out_specs, out_shape


def _mem_kv_kernel(mem_ref, g_ref, wt_ref, gk_ref, mk_ref, mv_ref):
    x = mem_ref[...]
    xn = x * lax.rsqrt(jnp.mean(x * x, axis=-1, keepdims=True) + EPS)
    for i in range(DEPTH):
        h = (xn * g_ref[i]).astype(BF16)
        kvt = lax.dot_general(wt_ref[i], h, _NT, preferred_element_type=F32)
        mk_ref[i, 0] = jnp.concatenate(_head_norm_cols(kvt[:X_DIM], gk_ref[i]), axis=0)
        mv_ref[i, 0] = kvt[X_DIM:]


def _mem_kv(mem2d, norm_mem, w_mem_kv_t, gk_cols, *, batch, n_mem):
    out = jax.ShapeDtypeStruct((DEPTH, batch, X_DIM, n_mem), F32)
    return pl.pallas_call(
        _mem_kv_kernel,
        grid=(batch,),
        in_specs=[
            pl.BlockSpec((n_mem, D_MODEL), lambda b: (b, 0)),
            _const_spec((DEPTH, 1, D_MODEL)),
            _const_spec((DEPTH, 2 * X_DIM, D_MODEL)),
            _const_spec((DEPTH, HEAD_DIM, LANES)),
        ],
        out_specs=[pl.BlockSpec((DEPTH, 1, X_DIM, n_mem), lambda b: (0, b, 0, 0))] * 2,
        out_shape=[out, out],
        name="mem_kv",
    )(mem2d, norm_mem, w_mem_kv_t, gk_cols)


def _in_att_kernel(*refs, keep_cols, n_convert):
    (x_ref, g_ref, wq_ref, wkv_ref, gq_ref, gx_ref, gk_ref, cos_ref, sin_ref, cost_ref, sint_ref) = refs[:11]
    n_out = 6 if keep_cols else 4
    outs = refs[11 + n_convert:11 + n_convert + n_out]
    q_ref, qx_ref, kt_ref, vt_ref = outs[:4]
    for src, dst in zip(refs[11:11 + n_convert], refs[11 + n_convert + n_out:]):
        dst[...] = src[0].astype(BF16)
    t = x_ref.shape[0]
    h = _rms_rows(x_ref[...], g_ref[0]).astype(BF16)
    bd = _head_blockdiag()
    z = jnp.dot(h, wq_ref[0], preferred_element_type=F32)
    q = z[:, :CONV_CH]
    q = q * lax.rsqrt(_head_mean_sq(q, bd) + EPS) * gq_ref[0]
    q_ref[...] = (_rope_rows(q, cos_ref[...], sin_ref[...]) * Q_SCALE).astype(q_ref.dtype)
    qx = z[:, CONV_CH:]
    qx = qx * lax.rsqrt(_head_mean_sq(qx, bd) + EPS) * gx_ref[0]
    qx_ref[...] = (qx * Q_SCALE).astype(qx_ref.dtype)

    kvt = lax.dot_general(wkv_ref[0], h, _NT, preferred_element_type=F32)
    cos_t, sin_t = cost_ref[...], sint_ref[...]
    half = HEAD_DIM // 2
    k_parts = []
    for blk in _head_norm_cols(kvt[:KV_DIM], gk_ref[0]):
        x1, x2 = blk[:half], blk[half:]
        k_parts += [x1 * cos_t - x2 * sin_t, x2 * cos_t + x1 * sin_t]
    kt = jnp.concatenate(k_parts, axis=0)
    vt = kvt[KV_DIM:]
    kt_ref[...] = kt.astype(kt_ref.dtype)
    vt_ref[...] = vt.astype(vt_ref.dtype)
    if keep_cols:
        kt32_ref, vt32_ref = outs[4:]
        kt32_ref[0] = kt[:, t - keep_cols:]
        vt32_ref[0] = vt[:, t - keep_cols:]


def _in_att(x2d, layer, j, g_mix, wq, wkv_t, gq, gx, gk_cols, cos, sin, cos_t, sin_t,
            *, tiles_per_seq, keep_cols, act_dtype, convert=()):
    n = x2d.shape[0]
    tm = TOKEN_TILE
    nt = n // tm
    nseq = nt // tiles_per_seq
    row = lambda i: (i, 0)
    col = lambda i: (0, i)
    pos = lambda i: (i % tiles_per_seq, 0)
    pos_t = lambda i: (0, i % tiles_per_seq)
    out_specs = [
        pl.BlockSpec((tm, CONV_CH), row),
        pl.BlockSpec((tm, X_DIM), row),
        pl.BlockSpec((KV_DIM, tm), col),
        pl.BlockSpec((KV_DIM, tm), col),
    ]
    out_shape = [
        jax.ShapeDtypeStruct((n, CONV_CH), act_dtype),
        jax.ShapeDtypeStruct((n, X_DIM), act_dtype),
        jax.ShapeDtypeStruct((KV_DIM, n), act_dtype),
        jax.ShapeDtypeStruct((KV_DIM, n), act_dtype),
    ]
    if keep_cols:
        out_specs += [pl.BlockSpec((1, KV_DIM, keep_cols), lambda i: (i // tiles_per_seq, 0, 0))] * 2
        out_shape += [jax.ShapeDtypeStruct((nseq, KV_DIM, keep_cols), F32)] * 2
    cv_in, cv_args, cv_out, cv_shape = _convert_specs(convert, nt)
    return pl.pallas_call(
        functools.partial(_in_att_kernel, keep_cols=keep_cols, n_convert=len(convert)),
        grid=(nt,),
        in_specs=[
            pl.BlockSpec((tm, D_MODEL), row),
            _layer_spec((1, D_MODEL), layer),
            _layer_spec((D_MODEL, CONV_CH + X_DIM), j),
            _layer_spec((2 * KV_DIM, D_MODEL), j),
            _layer_spec((1, CONV_CH), j),
            _layer_spec((1, X_DIM), layer),
            _layer_spec((HEAD_DIM, LANES), j),
            pl.BlockSpec((tm, LANES), pos),
            pl.BlockSpec((tm, LANES), pos),
            pl.BlockSpec((HEAD_DIM // 2, tm), pos_t),
            pl.BlockSpec((HEAD_DIM // 2, tm), pos_t),
        ] + cv_in,
        out_specs=out_specs + cv_out,
        out_shape=out_shape + cv_shape,
        name="in_att",
    )(x2d, g_mix, wq, wkv_t, gq, gx, gk_cols, cos, sin, cos_t, sin_t, *cv_args)


def _in_conv_kernel(*refs, tiles_per_seq, period):
    if period:
        (x_ref, g_ref, w_ref, gx_ref, cw_ref, prefix_ref,
         o_ref, qx_ref, gt_ref, gbuf, pbuf) = refs
    else:
        (x_ref, g_ref, w_ref, gx_ref, cw_ref,
         o_ref, qx_ref, gt_ref, gbuf) = refs
    t = x_ref.shape[0]
    keep = gt_ref.shape[0]
    h = _rms_rows(x_ref[...], g_ref[0]).astype(BF16)

    def proj(lo, width):
        return jnp.dot(h, w_ref[:, lo:lo + width], preferred_element_type=F32)

    @pl.when(pl.program_id(0) % tiles_per_seq == 0)
    def _():
        gbuf[0:SUBLANES, :] = jnp.zeros((SUBLANES, CONV_CH), F32)

    if period:
        pbuf[0:t, :] = prefix_ref[...]
        pbuf[t:t + SUBLANES, :] = jnp.zeros((SUBLANES, CONV_CH), F32)
        tok = lax.broadcasted_iota(jnp.int32, (t, MXU_DIM), 0) % period
    cw = cw_ref[0]
    for c in range(CONV_CH // MXU_DIM):
        sl = slice(c * MXU_DIM, (c + 1) * MXU_DIM)
        gate_b = proj(sl.start, MXU_DIM)
        u = proj(CONV_CH + sl.start, MXU_DIM) * proj(2 * CONV_CH + sl.start, MXU_DIM)
        gbuf[SUBLANES:SUBLANES + t, sl] = u
        back1 = gbuf[SUBLANES - 1:SUBLANES - 1 + t, sl]
        back2 = gbuf[SUBLANES - 2:SUBLANES - 2 + t, sl]
        if period:
            back1 = jnp.where(tok >= 1, back1, pbuf[1:1 + t, sl])
            back2 = jnp.where(tok >= 2, back2, pbuf[0:t, sl])
        y = back2 * cw[0:1, sl] + back1 * cw[1:2, sl] + u * cw[2:3, sl]
        o_ref[:, sl] = (gate_b * y).astype(o_ref.dtype)
        gt_ref[:, sl] = u[t - keep:, :]
    gbuf[0:SUBLANES, :] = gbuf[t:t + SUBLANES, :]
    qx = proj(3 * CONV_CH, X_DIM)
    qx = qx * lax.rsqrt(_head_mean_sq(qx, _head_blockdiag()) + EPS) * gx_ref[0]
    qx_ref[...] = (qx * Q_SCALE).astype(qx_ref.dtype)


def _in_conv(x2d, layer, j, g_mix, w, gx, cw, prefix, *, tiles_per_seq, period, keep_rows, act_dtype):
    n = x2d.shape[0]
    tm = TOKEN_TILE
    nt = n // tm
    nseq = nt // tiles_per_seq
    row = lambda i: (i, 0)
    in_specs = [
        pl.BlockSpec((tm, D_MODEL), row),
        _layer_spec((1, D_MODEL), layer),
        _const_spec((D_MODEL, CONV_IN)),
        _layer_spec((1, X_DIM), layer),
        _layer_spec((CONV_W, CONV_CH), j),
    ]
    args = [x2d, g_mix, w, gx, cw]
    scratch = [pltpu.VMEM((tm + 2 * SUBLANES, CONV_CH), F32)]
    if period:
        in_specs.append(pl.BlockSpec((tm, CONV_CH), row))
        args.append(prefix)
        scratch.append(pltpu.VMEM((tm + SUBLANES, CONV_CH), F32))
    return pl.pallas_call(
        functools.partial(_in_conv_kernel, tiles_per_seq=tiles_per_seq, period=period),
        grid=(nt,),
        in_specs=in_specs,
        out_specs=[
            pl.BlockSpec((tm, CONV_CH), row),
            pl.BlockSpec((tm, X_DIM), row),
            pl.BlockSpec((keep_rows, CONV_CH), lambda i: (i // tiles_per_seq, 0)),
        ],
        out_shape=[
            jax.ShapeDtypeStruct((n, CONV_CH), act_dtype),
            jax.ShapeDtypeStruct((n, X_DIM), act_dtype),
            jax.ShapeDtypeStruct((nseq * keep_rows, CONV_CH), F32),
        ],
        scratch_shapes=scratch,
        name="in_conv",
    )(*args)


def _pair_lhs(slabs):
    low = lax.broadcasted_iota(jnp.int32, slabs[0].shape, 1) < HEAD_DIM
    zero = jnp.zeros_like(slabs[0])
    parts = []
    for s in slabs:
        parts += [jnp.where(low, s, zero), jnp.where(low, zero, s)]
    return jnp.concatenate(parts, axis=0)


def _swa_prompt_kernel(sinks_ref, q_ref, kc_ref, kp_ref, vc_ref, vp_ref, o_ref, kfull, vfull, bias):
    tq = q_ref.shape[0]

    @pl.when(jnp.logical_and(pl.program_id(0) == 0, pl.program_id(1) == 0))
    def _():
        r = lax.broadcasted_iota(jnp.int32, (WINDOW, 2 * WINDOW), 0)
        c = lax.broadcasted_iota(jnp.int32, (WINDOW, 2 * WINDOW), 1)
        band = (c > r) & (c - WINDOW <= r)
        for kind, visible in enumerate((band, band & (c >= WINDOW))):
            base = jnp.where(visible, 0.0, NEG)
            for h in range(N_Q_HEADS):
                bias[kind, h] = jnp.where(c == 0, sinks_ref[h] * LOG2E, base)

    first_kind = jnp.where(pl.program_id(1) == 0, 1, 0)
    kfull[:, 0:WINDOW] = kp_ref[...]
    kfull[:, WINDOW:] = kc_ref[...]
    vfull[:, 0:WINDOW] = vp_ref[...]
    vfull[:, WINDOW:] = vc_ref[...]
    col0 = lax.broadcasted_iota(jnp.int32, (PAIR, 2 * WINDOW), 1) == 0
    zero_slab = jnp.zeros((PAIR, 2 * WINDOW), BF16)
    low = lax.broadcasted_iota(jnp.int32, (WINDOW, PAIR), 1) < HEAD_DIM
    for qb in range(tq // WINDOW):
        r0 = qb * WINDOW
        kind = first_kind if qb == 0 else 0
        slabs = [None] * (GQA_GROUP * KV_DIM // PAIR)
        for pr in range(KV_DIM // PAIR):
            kslab = jnp.where(col0, zero_slab, kfull[pr * PAIR:(pr + 1) * PAIR, r0:r0 + 2 * WINDOW])
            vslab = jnp.where(col0, zero_slab, vfull[pr * PAIR:(pr + 1) * PAIR, r0:r0 + 2 * WINDOW])
            lhs = _pair_lhs([q_ref[r0:r0 + WINDOW, g * KV_DIM + pr * PAIR:g * KV_DIM + (pr + 1) * PAIR]
                             for g in range(GQA_GROUP)])
            s_all = jnp.dot(lhs, kslab, preferred_element_type=F32)
            probs, inv = [], []
            for g in range(GQA_GROUP):
                for e in range(2):
                    idx = g * 2 + e
                    head = (2 * pr + e) * GQA_GROUP + g
                    p, inv_den = _softmax_rows(s_all[idx * WINDOW:(idx + 1) * WINDOW] + bias[kind, head])
                    probs.append(p)
                    inv.append(inv_den)
            res = lax.dot_general(jnp.concatenate(probs, axis=0), vslab, _NT,
                                  preferred_element_type=F32)
            for g in range(GQA_GROUP):
                lo = res[(2 * g) * WINDOW:(2 * g + 1) * WINDOW] * inv[2 * g]
                hi = res[(2 * g + 1) * WINDOW:(2 * g + 2) * WINDOW] * inv[2 * g + 1]
                slabs[g * (KV_DIM // PAIR) + pr] = jnp.where(low, lo, hi)
        o_ref[r0:r0 + WINDOW, :] = jnp.concatenate(slabs, axis=1).astype(o_ref.dtype)


def _swa_prompt(sinks, q, kt, vt, *, batch, seq):
    tq = TOKEN_TILE
    nt = seq // tq
    per = tq // WINDOW
    rows = lambda b, j: (b * nt + j, 0)
    cur = lambda b, j: (0, b * nt + j)
    prev = lambda b, j: (0, b * nt * per + jnp.maximum(j * per - 1, 0))
    return pl.pallas_call(
        _swa_prompt_kernel,
        grid=(batch, nt),
        in_specs=[
            pl.BlockSpec(memory_space=pltpu.SMEM),
            pl.BlockSpec((tq, CONV_CH), rows),
            pl.BlockSpec((KV_DIM, tq), cur),
            pl.BlockSpec((KV_DIM, WINDOW), prev),
            pl.BlockSpec((KV_DIM, tq), cur),
            pl.BlockSpec((KV_DIM, WINDOW), prev),
        ],
        out_specs=pl.BlockSpec((tq, CONV_CH), rows),
        out_shape=jax.ShapeDtypeStruct(q.shape, BF16),
        scratch_shapes=[pltpu.VMEM((KV_DIM, tq + WINDOW), BF16)] * 2
        + [pltpu.VMEM((2, N_Q_HEADS, WINDOW, 2 * WINDOW), F32)],
        name="swa_prompt",
    )(sinks, q, kt, kt, vt, vt)


def _swa_sample_kernel(*refs, dec, carried, slot):
    if carried:
        sinks_ref, q_ref, kn_ref, vn_ref, kb_ref, vb_ref, _, _, o_ref, ko_ref, vo_ref = refs
    else:
        sinks_ref, q_ref, kn_ref, vn_ref, kb_ref, vb_ref, o_ref, ko_ref, vo_ref = refs
    nseq = kb_ref.shape[1]
    for other in range(ko_ref.shape[0]):
        if other != slot:
            ko_ref[other] = jnp.zeros(ko_ref.shape[1:], F32)
            vo_ref[other] = jnp.zeros(vo_ref.shape[1:], F32)
    steps_per_tile = LANES // (nseq * dec)
    base = (pl.program_id(0) % steps_per_tile) * (nseq * dec)
    grp = N_KV_HEADS * dec
    rows = GQA_GROUP * grp
    ri = lax.broadcasted_iota(jnp.int32, (rows, KV_DIM), 0)
    li = lax.broadcasted_iota(jnp.int32, (rows, KV_DIM), 1)
    head_mask = ((ri % grp) // dec) == (li // HEAD_DIM)
    tok = lax.broadcasted_iota(jnp.int32, (rows, WINDOW + LANES), 0) % dec
    col = lax.broadcasted_iota(jnp.int32, (rows, WINDOW + LANES), 1)
    cache_mask = (col < WINDOW) & (col > tok)
    lane = lax.broadcasted_iota(jnp.int32, (KV_DIM, WINDOW), 1)
    sink = jnp.concatenate(
        [jnp.full((dec, 1), sinks_ref[kv * GQA_GROUP + g] * LOG2E, F32)
         for g in range(GQA_GROUP) for kv in range(N_KV_HEADS)], axis=0)
    kn = kn_ref[...]
    vn = vn_ref[...]
    kn16 = kn.astype(BF16)
    vn16 = vn.astype(BF16)
    for s0 in range(0, nseq, SEQ_GROUP):
        seqs = range(s0, s0 + SEQ_GROUP)
        scores, values = [], []
        for s in seqs:
            off = base + s * dec
            new = col - (WINDOW + off)
            mask = cache_mask | ((new >= 0) & (new <= tok))
            qs = q_ref[pl.ds(s * dec, dec), :]
            qbd = jnp.concatenate(
                [jnp.concatenate([qs[:, g * KV_DIM:(g + 1) * KV_DIM]] * N_KV_HEADS, axis=0)
                 for g in range(GQA_GROUP)], axis=0)
            qbd = jnp.where(head_mask, qbd, 0.0).astype(BF16)
            keys = jnp.concatenate([kb_ref[0, s].astype(BF16), kn16], axis=1)
            values.append(jnp.concatenate([vb_ref[0, s].astype(BF16), vn16], axis=1))
            scores.append(jnp.where(mask, jnp.dot(qbd, keys, preferred_element_type=F32), NEG))
        probs = [_softmax_rows(sc, sink) for sc in scores]
        for s, (p, inv_den), vals in zip(seqs, probs, values):
            o = lax.dot_general(p, vals, _NT, preferred_element_type=F32) * inv_den
            o = jnp.where(head_mask, o, 0.0)
            folded = []
            for g in range(GQA_GROUP):
                og = o[g * grp:g * grp + dec, :]
                for kv in range(1, N_KV_HEADS):
                    og = og + o[g * grp + kv * dec:g * grp + (kv + 1) * dec, :]
                folded.append(og)
            o_ref[pl.ds(s * dec, dec), :] = jnp.concatenate(folded, axis=1)
        for s in seqs:
            shift_new = (WINDOW - dec + LANES - (base + s * dec)) % LANES
            ko_ref[slot, s] = jnp.where(lane < WINDOW - dec, pltpu.roll(kb_ref[0, s], WINDOW - dec, axis=1),
                                     pltpu.roll(kn, shift_new, axis=1))
            vo_ref[slot, s] = jnp.where(lane < WINDOW - dec, pltpu.roll(vb_ref[0, s], WINDOW - dec, axis=1),
                                     pltpu.roll(vn, shift_new, axis=1))


def _swa_sample(sinks, q, kt_new, vt_new, kbuf, vbuf, j, carry, *, dec):
    nseq = kbuf.shape[1]
    sb = SEQ_BLOCK
    steps_per_tile = LANES // (sb * dec)
    row = lambda i: (i, 0)
    tile = lambda i: (0, i // steps_per_tile)
    cache = lambda i: (j, i, 0, 0)
    if carry is None:
        new_cache = pl.BlockSpec((kbuf.shape[0], sb, KV_DIM, WINDOW), lambda i: (0, i, 0, 0))
    else:
        new_cache = pl.BlockSpec((1, sb, KV_DIM, WINDOW), cache)
    in_specs = [
        pl.BlockSpec(memory_space=pltpu.SMEM),
        pl.BlockSpec((sb * dec, CONV_CH), row),
        pl.BlockSpec((KV_DIM, LANES), tile),
        pl.BlockSpec((KV_DIM, LANES), tile),
        pl.BlockSpec((1, sb, KV_DIM, WINDOW), cache),
        pl.BlockSpec((1, sb, KV_DIM, WINDOW), cache),
    ]
    args = [sinks, q, kt_new, vt_new, kbuf, vbuf]
    aliases = {}
    if carry is not None:
        aliases = {len(args): 1, len(args) + 1: 2}
        in_specs += [pl.BlockSpec(memory_space=pl.ANY)] * 2
        args += list(carry)
    return pl.pallas_call(
        functools.partial(_swa_sample_kernel, dec=dec, carried=carry is not None,
                          slot=j if carry is None else 0),
        grid=(nseq // sb,),
        in_specs=in_specs,
        out_specs=[
            pl.BlockSpec((sb * dec, CONV_CH), row),
            new_cache,
            new_cache,
        ],
        out_shape=[
            jax.ShapeDtypeStruct(q.shape, F32),
            jax.ShapeDtypeStruct(kbuf.shape, F32),
            jax.ShapeDtypeStruct(vbuf.shape, F32),
        ],
        input_output_aliases=aliases,
        name="swa_sample",
    )(*args)


def _xattn_tile(q_ref, mk_ref, mv_ref):
    tq = q_ref.shape[0]
    mk = mk_ref[0, 0].astype(BF16)
    mv = mv_ref[0, 0].astype(BF16)
    low = lax.broadcasted_iota(jnp.int32, (WINDOW, PAIR), 1) < HEAD_DIM
    blocks = []
    for rb in range(tq // WINDOW):
        r0 = rb * WINDOW
        slabs = []
        for pr in range(X_DIM // PAIR):
            lhs = _pair_lhs([q_ref[r0:r0 + WINDOW, pr * PAIR:(pr + 1) * PAIR]])
            s_all = jnp.dot(lhs, mk[pr * PAIR:(pr + 1) * PAIR], preferred_element_type=F32)
            p0, inv0 = _softmax_rows(s_all[:WINDOW])
            p1, inv1 = _softmax_rows(s_all[WINDOW:])
            res = lax.dot_general(jnp.concatenate([p0, p1], axis=0), mv[pr * PAIR:(pr + 1) * PAIR], _NT,
                                  preferred_element_type=F32)
            slabs.append(jnp.where(low, res[:WINDOW] * inv0, res[WINDOW:] * inv1))
        blocks.append(jnp.concatenate(slabs, axis=1).astype(BF16))
    return jnp.concatenate(blocks, axis=0)


def _xattn_sample_kernel(q_ref, mk_ref, mv_ref, o_ref, *, dec):
    nseq = mk_ref.shape[1]
    rows = N_X_HEADS * dec
    head_mask = (lax.broadcasted_iota(jnp.int32, (rows, X_DIM), 0) // dec) == (
        lax.broadcasted_iota(jnp.int32, (rows, X_DIM), 1) // HEAD_DIM)
    for s0 in range(0, nseq, SEQ_GROUP):
        seqs = range(s0, s0 + SEQ_GROUP)
        scores = []
        for s in seqs:
            qs = q_ref[pl.ds(s * dec, dec), :]
            qbd = jnp.where(head_mask, jnp.concatenate([qs] * N_X_HEADS, axis=0), 0.0).astype(BF16)
            scores.append(jnp.dot(qbd, mk_ref[0, s].astype(BF16), preferred_element_type=F32))
        probs = [_softmax_rows(sc) for sc in scores]
        for s, (p, inv_den) in zip(seqs, probs):
            o = lax.dot_general(p, mv_ref[0, s].astype(BF16), _NT, preferred_element_type=F32) * inv_den
            o = jnp.where(head_mask, o, 0.0)
            acc = o[0:dec, :]
            for h in range(1, N_X_HEADS):
                acc = acc + o[h * dec:(h + 1) * dec, :]
            o_ref[pl.ds(s * dec, dec), :] = acc


def _xattn_sample(qx, mk, mv, layer, *, dec):
    nseq, n_mem = mk.shape[1], mk.shape[3]
    sb = SEQ_BLOCK
    mem = lambda i: (layer, i, 0, 0)
    return pl.pallas_call(
        functools.partial(_xattn_sample_kernel, dec=dec),
        grid=(nseq // sb,),
        in_specs=[
            pl.BlockSpec((sb * dec, X_DIM), lambda i: (i, 0)),
            pl.BlockSpec((1, sb, X_DIM, n_mem), mem),
            pl.BlockSpec((1, sb, X_DIM, n_mem), mem),
        ],
        out_specs=pl.BlockSpec((sb * dec, X_DIM), lambda i: (i, 0)),
        out_shape=jax.ShapeDtypeStruct(qx.shape, F32),
        name="xattn_sample",
    )(qx, mk, mv)


def _mlp_tile(x, o, wo, gm, wu_ref, wd_ref):
    x1 = x + jnp.dot(o, wo, preferred_element_type=F32)
    hm = _rms_rows(x1, gm).astype(BF16)
    acc = x1
    for c in range(D_FF // FF_CHUNK):
        a = jnp.dot(hm, wu_ref[:, c * FF_CHUNK:(c + 1) * FF_CHUNK], preferred_element_type=F32)
        a = jnp.square(jnp.maximum(a, 0.0)).astype(BF16)
        acc = acc + jnp.dot(a, wd_ref[c * FF_CHUNK:(c + 1) * FF_CHUNK, :], preferred_element_type=F32)
    return acc


def _out_mlp_kernel(*refs, n_prompt_tiles, n_convert):
    (xp_ref, osp_ref, qxp_ref, mk_ref, mv_ref, xs_ref, oss_ref, oxs_ref,
     wo_ref, gm_ref, wu_ref, wd_ref) = refs[:12]
    yp_ref, ys_ref = refs[12 + n_convert:14 + n_convert]
    for src, dst in zip(refs[12:12 + n_convert], refs[14 + n_convert:]):
        dst[...] = src[0].astype(BF16)
    is_sample = pl.program_id(0) == n_prompt_tiles

    @pl.when(jnp.logical_not(is_sample))
    def _():
        o = jnp.concatenate([osp_ref[...], _xattn_tile(qxp_ref, mk_ref, mv_ref)], axis=1)
        yp_ref[...] = _mlp_tile(xp_ref[...], o, wo_ref[...], gm_ref[0], wu_ref, wd_ref)

    @pl.when(is_sample)
    def _():
        o = jnp.concatenate([oss_ref[...].astype(BF16), oxs_ref[...].astype(BF16)], axis=1)
        ys_ref[...] = _mlp_tile(xs_ref[...], o, wo_ref[...], gm_ref[0], wu_ref, wd_ref)


def _out_mlp(xp, os_p, qx_p, memory, xs, os_s, ox_s, layer, wo, gm, wu, wd, *, tiles_per_batch, convert=()):
    n = xp.shape[0]
    tm = TOKEN_TILE
    assert xs.shape[0] == tm
    npt = n // tm
    n_mem = memory[0].shape[3]
    last = npt - 1
    prow = lambda i: (jnp.minimum(i, last), 0)
    mem = lambda i: (layer, jnp.minimum(i, last) // tiles_per_batch, 0, 0)
    cv_in, cv_args, cv_out, cv_shape = _convert_specs(convert, npt, last)
    return pl.pallas_call(
        functools.partial(_out_mlp_kernel, n_prompt_tiles=npt, n_convert=len(convert)),
        grid=(npt + 1,),
        in_specs=[
            pl.BlockSpec((tm, D_MODEL), prow),
            pl.BlockSpec((tm, CONV_CH), prow),
            pl.BlockSpec((tm, X_DIM), prow),
            pl.BlockSpec((1, 1, X_DIM, n_mem), mem),
            pl.BlockSpec((1, 1, X_DIM, n_mem), mem),
            _const_spec((tm, D_MODEL)),
            _const_spec((tm, CONV_CH)),
            _const_spec((tm, X_DIM)),
            _const_spec((CONV_CH + X_DIM, D_MODEL)),
            _layer_spec((1, D_MODEL), layer),
            _const_spec((D_MODEL, D_FF)),
            _const_spec((D_FF, D_MODEL)),
        ] + cv_in,
        out_specs=[pl.BlockSpec((tm, D_MODEL), prow), pl.BlockSpec((tm, D_MODEL), lambda i: (0, 0))] + cv_out,
        out_shape=[jax.ShapeDtypeStruct((n, D_MODEL), F32), jax.ShapeDtypeStruct((tm, D_MODEL), F32)] + cv_shape,
        name="out_mlp",
    )(xp, os_p, qx_p, *memory, xs, os_s, ox_s, wo, gm, wu, wd, *cv_args)


def _rope_tables(pos):
    half = HEAD_DIM // 2
    inv = ROPE_THETA ** (-jnp.arange(half, dtype=F32) * 2.0 / HEAD_DIM)
    ang = pos[:, None] * inv[None, :]
    return jnp.cos(ang), jnp.sin(ang)


def _rope_lane_tables(cos, sin):
    reps = LANES // (HEAD_DIM // 2)
    return jnp.tile(cos, (1, reps)), jnp.concatenate([-sin, sin] * (reps // 2), axis=1)


def _group_major(w, axis):
    shape = w.shape
    w = w.reshape(shape[:axis] + (N_KV_HEADS, GQA_GROUP, HEAD_DIM) + shape[axis + 1:])
    return jnp.swapaxes(w, axis, axis + 1).reshape(shape)


def _feature_major(cache):
    lead = cache.shape[:-3]
    pos, heads, hd = cache.shape[-3:]
    nd = len(lead)
    perm = tuple(range(nd)) + (nd + 1, nd + 2, nd)
    return jnp.transpose(cache, perm).reshape(lead + (heads * hd, pos))


def _position_major(cache_t, heads):
    lead = cache_t.shape[:-2]
    pos = cache_t.shape[-1]
    nd = len(lead)
    perm = tuple(range(nd)) + (nd + 2, nd, nd + 1)
    return jnp.transpose(cache_t.reshape(lead + (heads, HEAD_DIM, pos)), perm)


def _gain_cols(g):
    return jnp.broadcast_to(g[:, :, None], g.shape + (LANES,))


def kernel(x_prompt, x_sample, mem_prompt, cache_swa_k, cache_swa_v, state_conv, cache_mem_k, cache_mem_v,
           norm_mix, w_in_att, q_norm_att, k_norm_att, sinks, w_in_conv, conv_w, norm_mem, w_mem_kv,
           q_norm_x, k_norm_x, w_out, norm_mlp, w_up, w_down):
    batch, seq, _ = x_prompt.shape
    nseq, dec, _ = x_sample.shape
    n_mem = mem_prompt.shape[1]
    assert seq % TOKEN_TILE == 0 and (nseq * dec) == TOKEN_TILE and nseq % SEQ_BLOCK == 0
    assert LANES % (SEQ_BLOCK * dec) == 0

    kv0, kv1 = CONV_CH, CONV_CH + 2 * KV_DIM
    w_q = jnp.concatenate([_group_major(w_in_att[:, :, :kv0], 2), w_in_att[:, :, kv1:]], axis=2).astype(BF16)
    w_kv_t = jnp.swapaxes(w_in_att[:, :, kv0:kv1], 1, 2).astype(BF16)
    w_o = [jnp.concatenate([_group_major(w_out[i, :CONV_CH], 0), w_out[i, CONV_CH:]], axis=0).astype(BF16)
           if i % 2 == 0 else w_out[i].astype(BF16) for i in range(DEPTH)]
    w_u = w_d = w_conv = None
    w_mkv_t = jnp.swapaxes(w_mem_kv, 1, 2).astype(BF16)

    g_mix = norm_mix[:, None, :]
    g_mlp = norm_mlp[:, None, :]
    gq = jnp.tile(q_norm_att, (1, N_Q_HEADS))[:, None, :]
    gx = jnp.tile(q_norm_x, (1, N_X_HEADS))[:, None, :]
    gk_cols = _gain_cols(k_norm_att)
    gkx_cols = _gain_cols(k_norm_x)

    cos_p, sin_p = _rope_tables(jnp.arange(seq, dtype=F32))
    cos_s, sin_s = _rope_tables(PAST_LEN + (jnp.arange(nseq * dec) % dec).astype(F32))
    rope_p = _rope_lane_tables(cos_p, sin_p) + (cos_p.T, sin_p.T)
    rope_s = _rope_lane_tables(cos_s, sin_s) + (cos_s.T, sin_s.T)

    mk_p, mv_p = _mem_kv(mem_prompt.reshape(batch * n_mem, D_MODEL), norm_mem[:, None, :], w_mkv_t, gkx_cols,
                         batch=batch, n_mem=n_mem)
    mk_s = _feature_major(cache_mem_k)
    mv_s = _feature_major(cache_mem_v)
    kbuf = _feature_major(cache_swa_k)
    vbuf = _feature_major(cache_swa_v)

    xp = x_prompt.reshape(batch * seq, D_MODEL)
    xs = x_sample.reshape(nseq * dec, D_MODEL)
    tiles_per_seq = seq // TOKEN_TILE
    k_p, v_p, c_p, c_s = [], [], [], []
    cache_s = None
    for i in range(DEPTH):
        j = i // 2
        if i % 2 == 0:
            first = [(w_up, 0), (w_down, 0)] if i == 0 else []
            q, qx_p, kt, vt, kt32, vt32, *w_first = _in_att(
                xp, i, j, g_mix, w_q, w_kv_t, gq, gx, gk_cols, *rope_p,
                tiles_per_seq=tiles_per_seq, keep_cols=WINDOW, act_dtype=BF16, convert=first)
            if w_first:
                w_u, w_d = w_first
            k_p.append(kt32)
            v_p.append(vt32)
            os_p = _swa_prompt(sinks[j], q, kt, vt, batch=batch, seq=seq)
            q, qx_s, kt, vt = _in_att(xs, i, j, g_mix, w_q, w_kv_t, gq, gx, gk_cols, *rope_s,
                                      tiles_per_seq=1, keep_cols=0, act_dtype=F32)
            os_s, *cache_s = _swa_sample(sinks[j], q, kt, vt, kbuf, vbuf, j, cache_s, dec=dec)
        else:
            os_p, qx_p, tail = _in_conv(xp, i, j, g_mix, w_conv, gx, conv_w, None,
                                        tiles_per_seq=tiles_per_seq, period=0,
                                        keep_rows=SUBLANES, act_dtype=BF16)
            c_p.append(tail.reshape(batch, SUBLANES, CONV_CH)[:, SUBLANES - (CONV_W - 1):])
            prefix = jnp.pad(state_conv[j], ((0, 0), (0, dec - (CONV_W - 1)), (0, 0)))
            os_s, qx_s, tail = _in_conv(xs, i, j, g_mix, w_conv, gx, conv_w,
                                        prefix.reshape(nseq * dec, CONV_CH),
                                        tiles_per_seq=1, period=dec,
                                        keep_rows=nseq * dec, act_dtype=F32)
            c_s.append(tail.reshape(nseq, dec, CONV_CH)[:, dec - (CONV_W - 1):])
        ox_s = _xattn_sample(qx_s, mk_s, mv_s, i, dec=dec)
        convert = []
        if i + 1 < DEPTH:
            convert = [(w_up, i + 1), (w_down, i + 1)] + ([(w_in_conv, (i + 1) // 2)] if i % 2 == 0 else [])
        xp, xs, *w_next = _out_mlp(xp, os_p, qx_p, (mk_p, mv_p), xs, os_s, ox_s, i, w_o[i], g_mlp, w_u, w_d,
                                   tiles_per_batch=tiles_per_seq, convert=convert)
        if w_next:
            w_u, w_d = w_next[:2]
            w_conv = w_next[2] if len(w_next) > 2 else None

    return (xp.reshape(batch, seq, D_MODEL),
            xs.reshape(nseq, dec, D_MODEL),
            _position_major(jnp.stack(k_p), N_KV_HEADS),
            _position_major(jnp.stack(v_p), N_KV_HEADS),
            jnp.stack(c_p),
            _position_major(mk_p, N_X_HEADS),
            _position_major(mv_p, N_X_HEADS),
            _position_major(cache_s[0], N_KV_HEADS),
            _position_major(cache_s[1], N_KV_HEADS),
            jnp.stack(c_s))
```

```python
import functools

import jax
import jax.numpy as jnp
from jax import lax
from jax.experimental import pallas as pl
from jax.experimental.pallas import tpu as pltpu

F32 = jnp.float32
BF16 = jnp.bfloat16

D_MODEL = 1024
DEPTH = 4
HEAD_DIM = 64
N_Q_HEADS = 12
N_KV_HEADS = 4
GQA_GROUP = N_Q_HEADS // N_KV_HEADS
WINDOW = 128
PAST_LEN = 8192
ROPE_THETA = 10000.0
CONV_CH = N_Q_HEADS * HEAD_DIM
CONV_W = 3
N_X_HEADS = 4
X_DIM = N_X_HEADS * HEAD_DIM
KV_DIM = N_KV_HEADS * HEAD_DIM
D_FF = 4 * D_MODEL
CONV_IN = 3 * CONV_CH + X_DIM
EPS = 1e-6
NEG = -1e30
LOG2E = 1.4426950408889634
Q_SCALE = HEAD_DIM ** -0.5 * LOG2E

LANES = 128
SUBLANES = 8
MXU_DIM = 256
TOKEN_TILE = 512
MIX_TILE = 1024
FF_CHUNK = 2048
SEQ_BLOCK = 16
SEQ_GROUP = 16
PAIR = 2 * HEAD_DIM

_NT = (((1,), (1,)), ((), ()))


def _const_spec(shape):
    nd = len(shape)
    return pl.BlockSpec(shape, lambda *_: (0,) * nd, pipeline_mode=pl.Buffered(1))


def _layer_spec(shape, layer):
    nd = len(shape)
    return pl.BlockSpec((1,) + shape, lambda *_: (layer,) + (0,) * nd, pipeline_mode=pl.Buffered(1))


def _rms_rows(x, g):
    return x * lax.rsqrt(jnp.mean(x * x, axis=-1, keepdims=True) + EPS) * g


def _head_blockdiag():
    r = lax.broadcasted_iota(jnp.int32, (MXU_DIM, MXU_DIM), 0) // HEAD_DIM
    c = lax.broadcasted_iota(jnp.int32, (MXU_DIM, MXU_DIM), 1) // HEAD_DIM
    return jnp.where(r == c, 1.0, 0.0).astype(BF16)


def _head_mean_sq(z, bd):
    sq = z * z
    hi = sq.astype(BF16)
    lo = (sq - hi.astype(F32)).astype(BF16)
    parts = []
    for c in range(z.shape[1] // MXU_DIM):
        sl = slice(c * MXU_DIM, (c + 1) * MXU_DIM)
        parts.append(jnp.dot(hi[:, sl], bd, preferred_element_type=F32)
                     + jnp.dot(lo[:, sl], bd, preferred_element_type=F32))
    ms = parts[0] if len(parts) == 1 else jnp.concatenate(parts, axis=1)
    return ms * (1.0 / HEAD_DIM)


def _rope_rows(x, cos, sin_signed):
    t = x.shape[0]
    first_half = (lax.broadcasted_iota(jnp.int32, (t, LANES), 1) & (HEAD_DIM - 1)) < HEAD_DIM // 2
    parts = []
    for c in range(x.shape[1] // LANES):
        xc = x[:, c * LANES:(c + 1) * LANES]
        partner = jnp.where(first_half,
                            pltpu.roll(xc, LANES - HEAD_DIM // 2, axis=1),
                            pltpu.roll(xc, HEAD_DIM // 2, axis=1))
        parts.append(xc * cos + partner * sin_signed)
    return jnp.concatenate(parts, axis=1)


def _lane_tile(col, n):
    return col if n == LANES else jnp.concatenate([col] * (n // LANES), axis=1)


def _head_norm_cols(xt, g_col):
    t = xt.shape[1]
    g = _lane_tile(g_col, t)
    parts = []
    for h in range(xt.shape[0] // HEAD_DIM):
        blk = xt[h * HEAD_DIM:(h + 1) * HEAD_DIM, :]
        ms = jnp.mean(blk * blk, axis=0, keepdims=True)
        parts.append(blk * lax.rsqrt(ms + EPS) * g)
    return parts


def _softmax_rows(s, sink=None):
    m = jnp.max(s, axis=-1, keepdims=True)
    if sink is not None:
        m = jnp.maximum(m, sink)
    p = jnp.exp2(s - m)
    den = jnp.sum(p, axis=-1, keepdims=True)
    if sink is not None:
        den = den + jnp.exp2(sink - m)
    return p.astype(BF16), 1.0 / den


def _convert_specs(convert, steps, last=None):
    last = steps - 1 if last is None else last
    in_specs, args, out_specs, out_shape = [], [], [], []
    for stack, l in convert:
        _, r, c = stack.shape
        in_specs.append(pl.BlockSpec((1, r // steps, c), lambda i, l=l: (l, jnp.minimum(i, last), 0)))
        args.append(stack)
        out_specs.append(pl.BlockSpec((r // steps, c), lambda i: (jnp.minimum(i, last), 0)))
        out_shape.append(jax.ShapeDtypeStruct((r, c), BF16))
    return in_specs, args, out_specs, out_shape


def _mem_kv_kernel(mem_ref, g_ref, wt_ref, gk_ref, mk_ref, mv_ref):
    x = mem_ref[...]
    xn = x * lax.rsqrt(jnp.mean(x * x, axis=-1, keepdims=True) + EPS)
    for i in range(DEPTH):
        h = (xn * g_ref[i]).astype(BF16)
        kvt = lax.dot_general(wt_ref[i], h, _NT, preferred_element_type=F32)
        mk_ref[i, 0] = jnp.concatenate(_head_norm_cols(kvt[:X_DIM], gk_ref[i]), axis=0)
        mv_ref[i, 0] = kvt[X_DIM:]


def _mem_kv(mem2d, norm_mem, w_mem_kv_t, gk_cols, *, batch, n_mem):
    out = jax.ShapeDtypeStruct((DEPTH, batch, X_DIM, n_mem), F32)
    return pl.pallas_call(
        _mem_kv_kernel,
        grid=(batch,),
        in_specs=[
            pl.BlockSpec((n_mem, D_MODEL), lambda b: (b, 0)),
            _const_spec((DEPTH, 1, D_MODEL)),
            _const_spec((DEPTH, 2 * X_DIM, D_MODEL)),
            _const_spec((DEPTH, HEAD_DIM, LANES)),
        ],
        out_specs=[pl.BlockSpec((DEPTH, 1, X_DIM, n_mem), lambda b: (0, b, 0, 0))] * 2,
        out_shape=[out, out],
        name="mem_kv",
    )(mem2d, norm_mem, w_mem_kv_t, gk_cols)


def _in_att_kernel(*refs, keep_cols, n_convert):
    (x_ref, g_ref, wq_ref, wkv_ref, gq_ref, gx_ref, gk_ref, cos_ref, sin_ref, cost_ref, sint_ref) = refs[:11]
    n_out = 6 if keep_cols else 4
    outs = refs[11 + n_convert:11 + n_convert + n_out]
    q_ref, qx_ref, kt_ref, vt_ref = outs[:4]
    for src, dst in zip(refs[11:11 + n_convert], refs[11 + n_convert + n_out:]):
        dst[...] = src[0].astype(BF16)
    t = x_ref.shape[0]
    h = _rms_rows(x_ref[...], g_ref[0]).astype(BF16)
    bd = _head_blockdiag()
    z = jnp.dot(h, wq_ref[0], preferred_element_type=F32)
    q = z[:, :CONV_CH]
    q = q * lax.rsqrt(_head_mean_sq(q, bd) + EPS) * gq_ref[0]
    q_ref[...] = (_rope_rows(q, cos_ref[...], sin_ref[...]) * Q_SCALE).astype(q_ref.dtype)
    qx = z[:, CONV_CH:]
    qx = qx * lax.rsqrt(_head_mean_sq(qx, bd) + EPS) * gx_ref[0]
    qx_ref[...] = (qx * Q_SCALE).astype(qx_ref.dtype)

    kvt = lax.dot_general(wkv_ref[0], h, _NT, preferred_element_type=F32)
    cos_t, sin_t = cost_ref[...], sint_ref[...]
    half = HEAD_DIM // 2
    k_parts = []
    for blk in _head_norm_cols(kvt[:KV_DIM], gk_ref[0]):
        x1, x2 = blk[:half], blk[half:]
        k_parts += [x1 * cos_t - x2 * sin_t, x2 * cos_t + x1 * sin_t]
    kt = jnp.concatenate(k_parts, axis=0)
    vt = kvt[KV_DIM:]
    kt_ref[...] = kt.astype(kt_ref.dtype)
    vt_ref[...] = vt.astype(vt_ref.dtype)
    if keep_cols:
        kt32_ref, vt32_ref = outs[4:]
        kt32_ref[0] = kt[:, t - keep_cols:]
        vt32_ref[0] = vt[:, t - keep_cols:]


def _in_att(x2d, layer, j, g_mix, wq, wkv_t, gq, gx, gk_cols, cos, sin, cos_t, sin_t,
            *, tiles_per_seq, keep_cols, act_dtype, convert=()):
    n = x2d.shape[0]
    tm = min(MIX_TILE, n)
    nt = n // tm
    nseq = nt // tiles_per_seq
    row = lambda i: (i, 0)
    col = lambda i: (0, i)
    pos = lambda i: (i % tiles_per_seq, 0)
    pos_t = lambda i: (0, i % tiles_per_seq)
    out_specs = [
        pl.BlockSpec((tm, CONV_CH), row),
        pl.BlockSpec((tm, X_DIM), row),
        pl.BlockSpec((KV_DIM, tm), col),
        pl.BlockSpec((KV_DIM, tm), col),
    ]
    out_shape = [
        jax.ShapeDtypeStruct((n, CONV_CH), act_dtype),
        jax.ShapeDtypeStruct((n, X_DIM), act_dtype),
        jax.ShapeDtypeStruct((KV_DIM, n), act_dtype),
        jax.ShapeDtypeStruct((KV_DIM, n), act_dtype),
    ]
    if keep_cols:
        out_specs += [pl.BlockSpec((1, KV_DIM, keep_cols), lambda i: (i // tiles_per_seq, 0, 0))] * 2
        out_shape += [jax.ShapeDtypeStruct((nseq, KV_DIM, keep_cols), F32)] * 2
    cv_in, cv_args, cv_out, cv_shape = _convert_specs(convert, nt)
    return pl.pallas_call(
        functools.partial(_in_att_kernel, keep_cols=keep_cols, n_convert=len(convert)),
        grid=(nt,),
        in_specs=[
            pl.BlockSpec((tm, D_MODEL), row),
            _layer_spec((1, D_MODEL), layer),
            _layer_spec((D_MODEL, CONV_CH + X_DIM), j),
            _layer_spec((2 * KV_DIM, D_MODEL), j),
            _layer_spec((1, CONV_CH), j),
            _layer_spec((1, X_DIM), layer),
            _layer_spec((HEAD_DIM, LANES), j),
            pl.BlockSpec((tm, LANES), pos),
            pl.BlockSpec((tm, LANES), pos),
            pl.BlockSpec((HEAD_DIM // 2, tm), pos_t),
            pl.BlockSpec((HEAD_DIM // 2, tm), pos_t),
        ] + cv_in,
        out_specs=out_specs + cv_out,
        out_shape=out_shape + cv_shape,
        name="in_att",
    )(x2d, g_mix, wq, wkv_t, gq, gx, gk_cols, cos, sin, cos_t, sin_t, *cv_args)


def _in_conv_kernel(*refs, tiles_per_seq, period):
    if period:
        (x_ref, g_ref, w_ref, gx_ref, cw_ref, prefix_ref,
         o_ref, qx_ref, gt_ref, gbuf, pbuf) = refs
    else:
        (x_ref, g_ref, w_ref, gx_ref, cw_ref,
         o_ref, qx_ref, gt_ref, gbuf) = refs
    t = x_ref.shape[0]
    keep = gt_ref.shape[0]
    h = _rms_rows(x_ref[...], g_ref[0]).astype(BF16)

    def proj(lo, width):
        return jnp.dot(h, w_ref[:, lo:lo + width], preferred_element_type=F32)

    @pl.when(pl.program_id(0) % tiles_per_seq == 0)
    def _():
        gbuf[0:SUBLANES, :] = jnp.zeros((SUBLANES, CONV_CH), F32)

    if period:
        pbuf[0:t, :] = prefix_ref[...]
        pbuf[t:t + SUBLANES, :] = jnp.zeros((SUBLANES, CONV_CH), F32)
        tok = lax.broadcasted_iota(jnp.int32, (t, MXU_DIM), 0) % period
    cw = cw_ref[0]
    for c in range(CONV_CH // MXU_DIM):
        sl = slice(c * MXU_DIM, (c + 1) * MXU_DIM)
        gate_b = proj(sl.start, MXU_DIM)
        u = proj(CONV_CH + sl.start, MXU_DIM) * proj(2 * CONV_CH + sl.start, MXU_DIM)
        gbuf[SUBLANES:SUBLANES + t, sl] = u
        back1 = gbuf[SUBLANES - 1:SUBLANES - 1 + t, sl]
        back2 = gbuf[SUBLANES - 2:SUBLANES - 2 + t, sl]
        if period:
            back1 = jnp.where(tok >= 1, back1, pbuf[1:1 + t, sl])
            back2 = jnp.where(tok >= 2, back2, pbuf[0:t, sl])
        y = back2 * cw[0:1, sl] + back1 * cw[1:2, sl] + u * cw[2:3, sl]
        o_ref[:, sl] = (gate_b * y).astype(o_ref.dtype)
        gt_ref[:, sl] = u[t - keep:, :]
    gbuf[0:SUBLANES, :] = gbuf[t:t + SUBLANES, :]
    qx = proj(3 * CONV_CH, X_DIM)
    qx = qx * lax.rsqrt(_head_mean_sq(qx, _head_blockdiag()) + EPS) * gx_ref[0]
    qx_ref[...] = (qx * Q_SCALE).astype(qx_ref.dtype)


def _in_conv(x2d, layer, j, g_mix, w, gx, cw, prefix, *, tiles_per_seq, period, keep_rows, act_dtype):
    n = x2d.shape[0]
    tm = min(MIX_TILE, n)
    nt = n // tm
    nseq = nt // tiles_per_seq
    row = lambda i: (i, 0)
    in_specs = [
        pl.BlockSpec((tm, D_MODEL), row),
        _layer_spec((1, D_MODEL), layer),
        _const_spec((D_MODEL, CONV_IN)),
        _layer_spec((1, X_DIM), layer),
        _layer_spec((CONV_W, CONV_CH), j),
    ]
    args = [x2d, g_mix, w, gx, cw]
    scratch = [pltpu.VMEM((tm + 2 * SUBLANES, CONV_CH), F32)]
    if period:
        in_specs.append(pl.BlockSpec((tm, CONV_CH), row))
        args.append(prefix)
        scratch.append(pltpu.VMEM((tm + SUBLANES, CONV_CH), F32))
    return pl.pallas_call(
        functools.partial(_in_conv_kernel, tiles_per_seq=tiles_per_seq, period=period),
        grid=(nt,),
        in_specs=in_specs,
        out_specs=[
            pl.BlockSpec((tm, CONV_CH), row),
            pl.BlockSpec((tm, X_DIM), row),
            pl.BlockSpec((keep_rows, CONV_CH), lambda i: (i // tiles_per_seq, 0)),
        ],
        out_shape=[
            jax.ShapeDtypeStruct((n, CONV_CH), act_dtype),
            jax.ShapeDtypeStruct((n, X_DIM), act_dtype),
            jax.ShapeDtypeStruct((nseq * keep_rows, CONV_CH), F32),
        ],
        scratch_shapes=scratch,
        name="in_conv",
    )(*args)


def _pair_lhs(slabs):
    low = lax.broadcasted_iota(jnp.int32, slabs[0].shape, 1) < HEAD_DIM
    zero = jnp.zeros_like(slabs[0])
    parts = []
    for s in slabs:
        parts += [jnp.where(low, s, zero), jnp.where(low, zero, s)]
    return jnp.concatenate(parts, axis=0)


def _swa_prompt_kernel(sinks_ref, q_ref, kc_ref, kp_ref, vc_ref, vp_ref, o_ref, kfull, vfull, bias):
    tq = q_ref.shape[0]

    @pl.when(jnp.logical_and(pl.program_id(0) == 0, pl.program_id(1) == 0))
    def _():
        r = lax.broadcasted_iota(jnp.int32, (WINDOW, 2 * WINDOW), 0)
        c = lax.broadcasted_iota(jnp.int32, (WINDOW, 2 * WINDOW), 1)
        band = (c > r) & (c - WINDOW <= r)
        for kind, visible in enumerate((band, band & (c >= WINDOW))):
            base = jnp.where(visible, 0.0, NEG)
            for h in range(N_Q_HEADS):
                bias[kind, h] = jnp.where(c == 0, sinks_ref[h] * LOG2E, base)

    first_kind = jnp.where(pl.program_id(1) == 0, 1, 0)
    kfull[:, 0:WINDOW] = kp_ref[...]
    kfull[:, WINDOW:] = kc_ref[...]
    vfull[:, 0:WINDOW] = vp_ref[...]
    vfull[:, WINDOW:] = vc_ref[...]
    col0 = lax.broadcasted_iota(jnp.int32, (PAIR, 2 * WINDOW), 1) == 0
    zero_slab = jnp.zeros((PAIR, 2 * WINDOW), BF16)
    low = lax.broadcasted_iota(jnp.int32, (WINDOW, PAIR), 1) < HEAD_DIM
    for qb in range(tq // WINDOW):
        r0 = qb * WINDOW
        kind = first_kind if qb == 0 else 0
        slabs = [None] * (GQA_GROUP * KV_DIM // PAIR)
        for pr in range(KV_DIM // PAIR):
            kslab = jnp.where(col0, zero_slab, kfull[pr * PAIR:(pr + 1) * PAIR, r0:r0 + 2 * WINDOW])
            vslab = jnp.where(col0, zero_slab, vfull[pr * PAIR:(pr + 1) * PAIR, r0:r0 + 2 * WINDOW])
            lhs = _pair_lhs([q_ref[r0:r0 + WINDOW, g * KV_DIM + pr * PAIR:g * KV_DIM + (pr + 1) * PAIR]
                             for g in range(GQA_GROUP)])
            s_all = jnp.dot(lhs, kslab, preferred_element_type=F32)
            probs, inv = [], []
            for g in range(GQA_GROUP):
                for e in range(2):
                    idx = g * 2 + e
                    head = (2 * pr + e) * GQA_GROUP + g
                    p, inv_den = _softmax_rows(s_all[idx * WINDOW:(idx + 1) * WINDOW] + bias[kind, head])
                    probs.append(p)
                    inv.append(inv_den)
            res = lax.dot_general(jnp.concatenate(probs, axis=0), vslab, _NT,
                                  preferred_element_type=F32)
            for g in range(GQA_GROUP):
                lo = res[(2 * g) * WINDOW:(2 * g + 1) * WINDOW] * inv[2 * g]
                hi = res[(2 * g + 1) * WINDOW:(2 * g + 2) * WINDOW] * inv[2 * g + 1]
                slabs[g * (KV_DIM // PAIR) + pr] = jnp.where(low, lo, hi)
        o_ref[r0:r0 + WINDOW, :] = jnp.concatenate(slabs, axis=1).astype(o_ref.dtype)


def _swa_prompt(sinks, q, kt, vt, *, batch, seq):
    tq = MIX_TILE
    nt = seq // tq
    per = tq // WINDOW
    rows = lambda b, j: (b * nt + j, 0)
    cur = lambda b, j: (0, b * nt + j)
    prev = lambda b, j: (0, b * nt * per + jnp.maximum(j * per - 1, 0))
    return pl.pallas_call(
        _swa_prompt_kernel,
        grid=(batch, nt),
        in_specs=[
            pl.BlockSpec(memory_space=pltpu.SMEM),
            pl.BlockSpec((tq, CONV_CH), rows),
            pl.BlockSpec((KV_DIM, tq), cur),
            pl.BlockSpec((KV_DIM, WINDOW), prev),
            pl.BlockSpec((KV_DIM, tq), cur),
            pl.BlockSpec((KV_DIM, WINDOW), prev),
        ],
        out_specs=pl.BlockSpec((tq, CONV_CH), rows),
        out_shape=jax.ShapeDtypeStruct(q.shape, BF16),
        scratch_shapes=[pltpu.VMEM((KV_DIM, tq + WINDOW), BF16)] * 2
        + [pltpu.VMEM((2, N_Q_HEADS, WINDOW, 2 * WINDOW), F32)],
        name="swa_prompt",
    )(sinks, q, kt, kt, vt, vt)


def _swa_sample_kernel(*refs, dec, carried, slot):
    if carried:
        sinks_ref, q_ref, kn_ref, vn_ref, kb_ref, vb_ref, _, _, o_ref, ko_ref, vo_ref = refs
    else:
        sinks_ref, q_ref, kn_ref, vn_ref, kb_ref, vb_ref, o_ref, ko_ref, vo_ref = refs
    nseq = kb_ref.shape[1]
    for other in range(ko_ref.shape[0]):
        if other != slot:
            ko_ref[other] = jnp.zeros(ko_ref.shape[1:], F32)
            vo_ref[other] = jnp.zeros(vo_ref.shape[1:], F32)
    steps_per_tile = LANES // (nseq * dec)
    base = (pl.program_id(0) % steps_per_tile) * (nseq * dec)
    grp = N_KV_HEADS * dec
    rows = GQA_GROUP * grp
    ri = lax.broadcasted_iota(jnp.int32, (rows, KV_DIM), 0)
    li = lax.broadcasted_iota(jnp.int32, (rows, KV_DIM), 1)
    head_mask = ((ri % grp) // dec) == (li // HEAD_DIM)
    tok = lax.broadcasted_iota(jnp.int32, (rows, WINDOW + LANES), 0) % dec
    col = lax.broadcasted_iota(jnp.int32, (rows, WINDOW + LANES), 1)
    cache_mask = (col < WINDOW) & (col > tok)
    lane = lax.broadcasted_iota(jnp.int32, (KV_DIM, WINDOW), 1)
    sink = jnp.concatenate(
        [jnp.full((dec, 1), sinks_ref[kv * GQA_GROUP + g] * LOG2E, F32)
         for g in range(GQA_GROUP) for kv in range(N_KV_HEADS)], axis=0)
    kn = kn_ref[...]
    vn = vn_ref[...]
    kn16 = kn.astype(BF16)
    vn16 = vn.astype(BF16)
    for s0 in range(0, nseq, SEQ_GROUP):
        seqs = range(s0, s0 + SEQ_GROUP)
        scores, values = [], []
        for s in seqs:
            off = base + s * dec
            new = col - (WINDOW + off)
            mask = cache_mask | ((new >= 0) & (new <= tok))
            qs = q_ref[pl.ds(s * dec, dec), :]
            qbd = jnp.concatenate(
                [jnp.concatenate([qs[:, g * KV_DIM:(g + 1) * KV_DIM]] * N_KV_HEADS, axis=0)
                 for g in range(GQA_GROUP)], axis=0)
            qbd = jnp.where(head_mask, qbd, 0.0).astype(BF16)
            keys = jnp.concatenate([kb_ref[0, s].astype(BF16), kn16], axis=1)
            values.append(jnp.concatenate([vb_ref[0, s].astype(BF16), vn16], axis=1))
            scores.append(jnp.where(mask, jnp.dot(qbd, keys, preferred_element_type=F32), NEG))
        probs = [_softmax_rows(sc, sink) for sc in scores]
        for s, (p, inv_den), vals in zip(seqs, probs, values):
            o = lax.dot_general(p, vals, _NT, preferred_element_type=F32) * inv_den
            o = jnp.where(head_mask, o, 0.0)
            folded = []
            for g in range(GQA_GROUP):
                og = o[g * grp:g * grp + dec, :]
                for kv in range(1, N_KV_HEADS):
                    og = og + o[g * grp + kv * dec:g * grp + (kv + 1) * dec, :]
                folded.append(og)
            o_ref[pl.ds(s * dec, dec), :] = jnp.concatenate(folded, axis=1)
        for s in seqs:
            shift_new = (WINDOW - dec + LANES - (base + s * dec)) % LANES
            ko_ref[slot, s] = jnp.where(lane < WINDOW - dec, pltpu.roll(kb_ref[0, s], WINDOW - dec, axis=1),
                                     pltpu.roll(kn, shift_new, axis=1))
            vo_ref[slot, s] = jnp.where(lane < WINDOW - dec, pltpu.roll(vb_ref[0, s], WINDOW - dec, axis=1),
                                     pltpu.roll(vn, shift_new, axis=1))


def _swa_sample(sinks, q, kt_new, vt_new, kbuf, vbuf, j, carry, *, dec):
    nseq = kbuf.shape[1]
    sb = SEQ_BLOCK
    steps_per_tile = LANES // (sb * dec)
    row = lambda i: (i, 0)
    tile = lambda i: (0, i // steps_per_tile)
    cache = lambda i: (j, i, 0, 0)
    if carry is None:
        new_cache = pl.BlockSpec((kbuf.shape[0], sb, KV_DIM, WINDOW), lambda i: (0, i, 0, 0))
    else:
        new_cache = pl.BlockSpec((1, sb, KV_DIM, WINDOW), cache)
    in_specs = [
        pl.BlockSpec(memory_space=pltpu.SMEM),
        pl.BlockSpec((sb * dec, CONV_CH), row),
        pl.BlockSpec((KV_DIM, LANES), tile),
        pl.BlockSpec((KV_DIM, LANES), tile),
        pl.BlockSpec((1, sb, KV_DIM, WINDOW), cache),
        pl.BlockSpec((1, sb, KV_DIM, WINDOW), cache),
    ]
    args = [sinks, q, kt_new, vt_new, kbuf, vbuf]
    aliases = {}
    if carry is not None:
        aliases = {len(args): 1, len(args) + 1: 2}
        in_specs += [pl.BlockSpec(memory_space=pl.ANY)] * 2
        args += list(carry)
    return pl.pallas_call(
        functools.partial(_swa_sample_kernel, dec=dec, carried=carry is not None,
                          slot=j if carry is None else 0),
        grid=(nseq // sb,),
        in_specs=in_specs,
        out_specs=[
            pl.BlockSpec((sb * dec, CONV_CH), row),
            new_cache,
            new_cache,
        ],
        out_shape=[
            jax.ShapeDtypeStruct(q.shape, F32),
            jax.ShapeDtypeStruct(kbuf.shape, F32),
            jax.ShapeDtypeStruct(vbuf.shape, F32),
        ],
        input_output_aliases=aliases,
        name="swa_sample",
    )(*args)


def _xattn_tile(q_ref, mk_ref, mv_ref):
    tq = q_ref.shape[0]
    mk = mk_ref[0, 0].astype(BF16)
    mv = mv_ref[0, 0].astype(BF16)
    low = lax.broadcasted_iota(jnp.int32, (WINDOW, PAIR), 1) < HEAD_DIM
    blocks = []
    for rb in range(tq // WINDOW):
        r0 = rb * WINDOW
        slabs = []
        for pr in range(X_DIM // PAIR):
            lhs = _pair_lhs([q_ref[r0:r0 + WINDOW, pr * PAIR:(pr + 1) * PAIR]])
            s_all = jnp.dot(lhs, mk[pr * PAIR:(pr + 1) * PAIR], preferred_element_type=F32)
            p0, inv0 = _softmax_rows(s_all[:WINDOW])
            p1, inv1 = _softmax_rows(s_all[WINDOW:])
            res = lax.dot_general(jnp.concatenate([p0, p1], axis=0), mv[pr * PAIR:(pr + 1) * PAIR], _NT,
                                  preferred_element_type=F32)
            slabs.append(jnp.where(low, res[:WINDOW] * inv0, res[WINDOW:] * inv1))
        blocks.append(jnp.concatenate(slabs, axis=1).astype(BF16))
    return jnp.concatenate(blocks, axis=0)


def _xattn_sample_kernel(q_ref, mk_ref, mv_ref, o_ref, *, dec):
    nseq = mk_ref.shape[1]
    rows = N_X_HEADS * dec
    head_mask = (lax.broadcasted_iota(jnp.int32, (rows, X_DIM), 0) // dec) == (
        lax.broadcasted_iota(jnp.int32, (rows, X_DIM), 1) // HEAD_DIM)
    for s0 in range(0, nseq, SEQ_GROUP):
        seqs = range(s0, s0 + SEQ_GROUP)
        scores = []
        for s in seqs:
            qs = q_ref[pl.ds(s * dec, dec), :]
            qbd = jnp.where(head_mask, jnp.concatenate([qs] * N_X_HEADS, axis=0), 0.0).astype(BF16)
            scores.append(jnp.dot(qbd, mk_ref[0, s].astype(BF16), preferred_element_type=F32))
        probs = [_softmax_rows(sc) for sc in scores]
        for s, (p, inv_den) in zip(seqs, probs):
            o = lax.dot_general(p, mv_ref[0, s].astype(BF16), _NT, preferred_element_type=F32) * inv_den
            o = jnp.where(head_mask, o, 0.0)
            acc = o[0:dec, :]
            for h in range(1, N_X_HEADS):
                acc = acc + o[h * dec:(h + 1) * dec, :]
            o_ref[pl.ds(s * dec, dec), :] = acc


def _xattn_sample(qx, mk, mv, layer, *, dec):
    nseq, n_mem = mk.shape[1], mk.shape[3]
    sb = SEQ_BLOCK
    mem = lambda i: (layer, i, 0, 0)
    return pl.pallas_call(
        functools.partial(_xattn_sample_kernel, dec=dec),
        grid=(nseq // sb,),
        in_specs=[
            pl.BlockSpec((sb * dec, X_DIM), lambda i: (i, 0)),
            pl.BlockSpec((1, sb, X_DIM, n_mem), mem),
            pl.BlockSpec((1, sb, X_DIM, n_mem), mem),
        ],
        out_specs=pl.BlockSpec((sb * dec, X_DIM), lambda i: (i, 0)),
        out_shape=jax.ShapeDtypeStruct(qx.shape, F32),
        name="xattn_sample",
    )(qx, mk, mv)


def _mlp_tile(x, o, wo, gm, wu_ref, wd_ref):
    x1 = x + jnp.dot(o, wo, preferred_element_type=F32)
    hm = _rms_rows(x1, gm).astype(BF16)
    acc = x1
    for c in range(D_FF // FF_CHUNK):
        a = jnp.dot(hm, wu_ref[:, c * FF_CHUNK:(c + 1) * FF_CHUNK], preferred_element_type=F32)
        a = jnp.square(jnp.maximum(a, 0.0)).astype(BF16)
        acc = acc + jnp.dot(a, wd_ref[c * FF_CHUNK:(c + 1) * FF_CHUNK, :], preferred_element_type=F32)
    return acc


def _out_mlp_kernel(*refs, n_prompt_tiles, n_convert):
    (xp_ref, osp_ref, qxp_ref, mk_ref, mv_ref, xs_ref, oss_ref, oxs_ref,
     wo_ref, gm_ref, wu_ref, wd_ref) = refs[:12]
    yp_ref, ys_ref = refs[12 + n_convert:14 + n_convert]
    for src, dst in zip(refs[12:12 + n_convert], refs[14 + n_convert:]):
        dst[...] = src[0].astype(BF16)
    is_sample = pl.program_id(0) == n_prompt_tiles

    @pl.when(jnp.logical_not(is_sample))
    def _():
        o = jnp.concatenate([osp_ref[...], _xattn_tile(qxp_ref, mk_ref, mv_ref)], axis=1)
        yp_ref[...] = _mlp_tile(xp_ref[...], o, wo_ref[...], gm_ref[0], wu_ref, wd_ref)

    @pl.when(is_sample)
    def _():
        o = jnp.concatenate([oss_ref[...].astype(BF16), oxs_ref[...].astype(BF16)], axis=1)
        ys_ref[...] = _mlp_tile(xs_ref[...], o, wo_ref[...], gm_ref[0], wu_ref, wd_ref)


def _out_mlp(xp, os_p, qx_p, memory, xs, os_s, ox_s, layer, wo, gm, wu, wd, *, tiles_per_batch, convert=()):
    n = xp.shape[0]
    tm = TOKEN_TILE
    assert xs.shape[0] == tm
    npt = n // tm
    n_mem = memory[0].shape[3]
    last = npt - 1
    prow = lambda i: (jnp.minimum(i, last), 0)
    mem = lambda i: (layer, jnp.minimum(i, last) // tiles_per_batch, 0, 0)
    cv_in, cv_args, cv_out, cv_shape = _convert_specs(convert, npt, last)
    return pl.pallas_call(
        functools.partial(_out_mlp_kernel, n_prompt_tiles=npt, n_convert=len(convert)),
        grid=(npt + 1,),
        in_specs=[
            pl.BlockSpec((tm, D_MODEL), prow),
            pl.BlockSpec((tm, CONV_CH), prow),
            pl.BlockSpec((tm, X_DIM), prow),
            pl.BlockSpec((1, 1, X_DIM, n_mem), mem),
            pl.BlockSpec((1, 1, X_DIM, n_mem), mem),
            _const_spec((tm, D_MODEL)),
            _const_spec((tm, CONV_CH)),
            _const_spec((tm, X_DIM)),
            _const_spec((CONV_CH + X_DIM, D_MODEL)),
            _layer_spec((1, D_MODEL), layer),
            _const_spec((D_MODEL, D_FF)),
            _const_spec((D_FF, D_MODEL)),
        ] + cv_in,
        out_specs=[pl.BlockSpec((tm, D_MODEL), prow), pl.BlockSpec((tm, D_MODEL), lambda i: (0, 0))] + cv_out,
        out_shape=[jax.ShapeDtypeStruct((n, D_MODEL), F32), jax.ShapeDtypeStruct((tm, D_MODEL), F32)] + cv_shape,
        name="out_mlp",
    )(xp, os_p, qx_p, *memory, xs, os_s, ox_s, wo, gm, wu, wd, *cv_args)


def _rope_tables(pos):
    half = HEAD_DIM // 2
    inv = ROPE_THETA ** (-jnp.arange(half, dtype=F32) * 2.0 / HEAD_DIM)
    ang = pos[:, None] * inv[None, :]
    return jnp.cos(ang), jnp.sin(ang)


def _rope_lane_tables(cos, sin):
    reps = LANES // (HEAD_DIM // 2)
    return jnp.tile(cos, (1, reps)), jnp.concatenate([-sin, sin] * (reps // 2), axis=1)


def _group_major(w, axis):
    shape = w.shape
    w = w.reshape(shape[:axis] + (N_KV_HEADS, GQA_GROUP, HEAD_DIM) + shape[axis + 1:])
    return jnp.swapaxes(w, axis, axis + 1).reshape(shape)


def _feature_major(cache):
    lead = cache.shape[:-3]
    pos, heads, hd = cache.shape[-3:]
    nd = len(lead)
    perm = tuple(range(nd)) + (nd + 1, nd + 2, nd)
    return jnp.transpose(cache, perm).reshape(lead + (heads * hd, pos))


def _position_major(cache_t, heads):
    lead = cache_t.shape[:-2]
    pos = cache_t.shape[-1]
    nd = len(lead)
    perm = tuple(range(nd)) + (nd + 2, nd, nd + 1)
    return jnp.transpose(cache_t.reshape(lead + (heads, HEAD_DIM, pos)), perm)


def _gain_cols(g):
    return jnp.broadcast_to(g[:, :, None], g.shape + (LANES,))


def kernel(x_prompt, x_sample, mem_prompt, cache_swa_k, cache_swa_v, state_conv, cache_mem_k, cache_mem_v,
           norm_mix, w_in_att, q_norm_att, k_norm_att, sinks, w_in_conv, conv_w, norm_mem, w_mem_kv,
           q_norm_x, k_norm_x, w_out, norm_mlp, w_up, w_down):
    batch, seq, _ = x_prompt.shape
    nseq, dec, _ = x_sample.shape
    n_mem = mem_prompt.shape[1]
    assert seq % MIX_TILE == 0 and MIX_TILE % TOKEN_TILE == 0
    assert (nseq * dec) == TOKEN_TILE and nseq % SEQ_BLOCK == 0
    assert LANES % (SEQ_BLOCK * dec) == 0

    kv0, kv1 = CONV_CH, CONV_CH + 2 * KV_DIM
    w_q = jnp.concatenate([_group_major(w_in_att[:, :, :kv0], 2), w_in_att[:, :, kv1:]], axis=2).astype(BF16)
    w_kv_t = jnp.swapaxes(w_in_att[:, :, kv0:kv1], 1, 2).astype(BF16)
    w_o = [jnp.concatenate([_group_major(w_out[i, :CONV_CH], 0), w_out[i, CONV_CH:]], axis=0).astype(BF16)
           if i % 2 == 0 else w_out[i].astype(BF16) for i in range(DEPTH)]
    w_u = w_d = w_conv = None
    w_mkv_t = jnp.swapaxes(w_mem_kv, 1, 2).astype(BF16)

    g_mix = norm_mix[:, None, :]
    g_mlp = norm_mlp[:, None, :]
    gq = jnp.tile(q_norm_att, (1, N_Q_HEADS))[:, None, :]
    gx = jnp.tile(q_norm_x, (1, N_X_HEADS))[:, None, :]
    gk_cols = _gain_cols(k_norm_att)
    gkx_cols = _gain_cols(k_norm_x)

    cos_p, sin_p = _rope_tables(jnp.arange(seq, dtype=F32))
    cos_s, sin_s = _rope_tables(PAST_LEN + (jnp.arange(nseq * dec) % dec).astype(F32))
    rope_p = _rope_lane_tables(cos_p, sin_p) + (cos_p.T, sin_p.T)
    rope_s = _rope_lane_tables(cos_s, sin_s) + (cos_s.T, sin_s.T)

    mk_p, mv_p = _mem_kv(mem_prompt.reshape(batch * n_mem, D_MODEL), norm_mem[:, None, :], w_mkv_t, gkx_cols,
                         batch=batch, n_mem=n_mem)
    mk_s = _feature_major(cache_mem_k)
    mv_s = _feature_major(cache_mem_v)
    kbuf = _feature_major(cache_swa_k)
    vbuf = _feature_major(cache_swa_v)

    xp = x_prompt.reshape(batch * seq, D_MODEL)
    xs = x_sample.reshape(nseq * dec, D_MODEL)
    tiles_per_seq = seq // MIX_TILE
    k_p, v_p, c_p, c_s = [], [], [], []
    cache_s = None
    for i in range(DEPTH):
        j = i // 2
        if i % 2 == 0:
            first = [(w_up, 0), (w_down, 0)] if i == 0 else []
            q, qx_p, kt, vt, kt32, vt32, *w_first = _in_att(
                xp, i, j, g_mix, w_q, w_kv_t, gq, gx, gk_cols, *rope_p,
                tiles_per_seq=tiles_per_seq, keep_cols=WINDOW, act_dtype=BF16, convert=first)
            if w_first:
                w_u, w_d = w_first
            k_p.append(kt32)
            v_p.append(vt32)
            os_p = _swa_prompt(sinks[j], q, kt, vt, batch=batch, seq=seq)
            q, qx_s, kt, vt = _in_att(xs, i, j, g_mix, w_q, w_kv_t, gq, gx, gk_cols, *rope_s,
                                      tiles_per_seq=1, keep_cols=0, act_dtype=F32)
            os_s, *cache_s = _swa_sample(sinks[j], q, kt, vt, kbuf, vbuf, j, cache_s, dec=dec)
        else:
            os_p, qx_p, tail = _in_conv(xp, i, j, g_mix, w_conv, gx, conv_w, None,
                                        tiles_per_seq=tiles_per_seq, period=0,
                                        keep_rows=SUBLANES, act_dtype=BF16)
            c_p.append(tail.reshape(batch, SUBLANES, CONV_CH)[:, SUBLANES - (CONV_W - 1):])
            prefix = jnp.pad(state_conv[j], ((0, 0), (0, dec - (CONV_W - 1)), (0, 0)))
            os_s, qx_s, tail = _in_conv(xs, i, j, g_mix, w_conv, gx, conv_w,
                                        prefix.reshape(nseq * dec, CONV_CH),
                                        tiles_per_seq=1, period=dec,
                                        keep_rows=nseq * dec, act_dtype=F32)
            c_s.append(tail.reshape(nseq, dec, CONV_CH)[:, dec - (CONV_W - 1):])
        ox_s = _xattn_sample(qx_s, mk_s, mv_s, i, dec=dec)
        convert = []
        if i + 1 < DEPTH:
            convert = [(w_up, i + 1), (w_down, i + 1)] + ([(w_in_conv, (i + 1) // 2)] if i % 2 == 0 else [])
        xp, xs, *w_next = _out_mlp(xp, os_p, qx_p, (mk_p, mv_p), xs, os_s, ox_s, i, w_o[i], g_mlp, w_u, w_d,
                                   tiles_per_batch=seq // TOKEN_TILE, convert=convert)
        if w_next:
            w_u, w_d = w_next[:2]
            w_conv = w_next[2] if len(w_next) > 2 else None

    return (xp.reshape(batch, seq, D_MODEL),
            xs.reshape(nseq, dec, D_MODEL),
            _position_major(jnp.stack(k_p), N_KV_HEADS),
            _position_major(jnp.stack(v_p), N_KV_HEADS),
            jnp.stack(c_p),
            _position_major(mk_p, N_X_HEADS),
            _position_major(mv_p, N_X_HEADS),
            _position_major(cache_s[0], N_KV_HEADS),
            _position_major(cache_s[1], N_KV_HEADS),
            jnp.stack(c_s))
```

```python
import functools

import jax
import jax.numpy as jnp
from jax import lax
from jax.experimental import pallas as pl
from jax.experimental.pallas import tpu as pltpu

F32 = jnp.float32
BF16 = jnp.bfloat16

D_MODEL = 1024
DEPTH = 4
HEAD_DIM = 64
N_Q_HEADS = 12
N_KV_HEADS = 4
GQA_GROUP = N_Q_HEADS // N_KV_HEADS
WINDOW = 128
PAST_LEN = 8192
ROPE_THETA = 10000.0
CONV_CH = N_Q_HEADS * HEAD_DIM
CONV_W = 3
N_X_HEADS = 4
X_DIM = N_X_HEADS * HEAD_DIM
KV_DIM = N_KV_HEADS * HEAD_DIM
D_FF = 4 * D_MODEL
CONV_IN = 3 * CONV_CH + X_DIM
EPS = 1e-6
NEG = -1e30
LOG2E = 1.4426950408889634
Q_SCALE = HEAD_DIM ** -0.5 * LOG2E

LANES = 128
SUBLANES = 8
MXU_DIM = 256
TOKEN_TILE = 512
ATT_TILE = 1024
MIX_TILE = 2048
FF_CHUNK = 2048
SEQ_BLOCK = 16
SEQ_GROUP = 16
PAIR = 2 * HEAD_DIM

_NT = (((1,), (1,)), ((), ()))


def _const_spec(shape):
    nd = len(shape)
    return pl.BlockSpec(shape, lambda *_: (0,) * nd, pipeline_mode=pl.Buffered(1))


def _layer_spec(shape, layer):
    nd = len(shape)
    return pl.BlockSpec((1,) + shape, lambda *_: (layer,) + (0,) * nd, pipeline_mode=pl.Buffered(1))


def _rms_rows(x, g):
    return x * lax.rsqrt(jnp.mean(x * x, axis=-1, keepdims=True) + EPS) * g


def _head_blockdiag():
    r = lax.broadcasted_iota(jnp.int32, (MXU_DIM, MXU_DIM), 0) // HEAD_DIM
    c = lax.broadcasted_iota(jnp.int32, (MXU_DIM, MXU_DIM), 1) // HEAD_DIM
    return jnp.where(r == c, 1.0, 0.0).astype(BF16)


def _head_mean_sq(z, bd):
    sq = z * z
    hi = sq.astype(BF16)
    lo = (sq - hi.astype(F32)).astype(BF16)
    parts = []
    for c in range(z.shape[1] // MXU_DIM):
        sl = slice(c * MXU_DIM, (c + 1) * MXU_DIM)
        parts.append(jnp.dot(hi[:, sl], bd, preferred_element_type=F32)
                     + jnp.dot(lo[:, sl], bd, preferred_element_type=F32))
    ms = parts[0] if len(parts) == 1 else jnp.concatenate(parts, axis=1)
    return ms * (1.0 / HEAD_DIM)


def _rope_rows(x, cos, sin_signed):
    t = x.shape[0]
    first_half = (lax.broadcasted_iota(jnp.int32, (t, LANES), 1) & (HEAD_DIM - 1)) < HEAD_DIM // 2
    parts = []
    for c in range(x.shape[1] // LANES):
        xc = x[:, c * LANES:(c + 1) * LANES]
        partner = jnp.where(first_half,
                            pltpu.roll(xc, LANES - HEAD_DIM // 2, axis=1),
                            pltpu.roll(xc, HEAD_DIM // 2, axis=1))
        parts.append(xc * cos + partner * sin_signed)
    return jnp.concatenate(parts, axis=1)


def _lane_tile(col, n):
    return col if n == LANES else jnp.concatenate([col] * (n // LANES), axis=1)


def _head_norm_cols(xt, g_col):
    t = xt.shape[1]
    g = _lane_tile(g_col, t)
    parts = []
    for h in range(xt.shape[0] // HEAD_DIM):
        blk = xt[h * HEAD_DIM:(h + 1) * HEAD_DIM, :]
        ms = jnp.mean(blk * blk, axis=0, keepdims=True)
        parts.append(blk * lax.rsqrt(ms + EPS) * g)
    return parts


def _softmax_rows(s, sink=None):
    m = jnp.max(s, axis=-1, keepdims=True)
    if sink is not None:
        m = jnp.maximum(m, sink)
    p = jnp.exp2(s - m)
    den = jnp.sum(p, axis=-1, keepdims=True)
    if sink is not None:
        den = den + jnp.exp2(sink - m)
    return p.astype(BF16), 1.0 / den


def _convert_specs(convert, steps, last=None):
    last = steps - 1 if last is None else last
    in_specs, args, out_specs, out_shape = [], [], [], []
    for stack, l in convert:
        _, r, c = stack.shape
        in_specs.append(pl.BlockSpec((1, r // steps, c), lambda i, l=l: (l, jnp.minimum(i, last), 0)))
        args.append(stack)
        out_specs.append(pl.BlockSpec((r // steps, c), lambda i: (jnp.minimum(i, last), 0)))
        out_shape.append(jax.ShapeDtypeStruct((r, c), BF16))
    return in_specs, args, out_specs, out_shape


def _mem_kv_kernel(mem_ref, g_ref, wt_ref, gk_ref, mk_ref, mv_ref):
    x = mem_ref[...]
    xn = x * lax.rsqrt(jnp.mean(x * x, axis=-1, keepdims=True) + EPS)
    for i in range(DEPTH):
        h = (xn * g_ref[i]).astype(BF16)
        kvt = lax.dot_general(wt_ref[i], h, _NT, preferred_element_type=F32)
        mk_ref[i, 0] = jnp.concatenate(_head_norm_cols(kvt[:X_DIM], gk_ref[i]), axis=0)
        mv_ref[i, 0] = kvt[X_DIM:]


def _mem_kv(mem2d, norm_mem, w_mem_kv_t, gk_cols, *, batch, n_mem):
    out = jax.ShapeDtypeStruct((DEPTH, batch, X_DIM, n_mem), F32)
    return pl.pallas_call(
        _mem_kv_kernel,
        grid=(batch,),
        in_specs=[
            pl.BlockSpec((n_mem, D_MODEL), lambda b: (b, 0)),
            _const_spec((DEPTH, 1, D_MODEL)),
            _const_spec((DEPTH, 2 * X_DIM, D_MODEL)),
            _const_spec((DEPTH, HEAD_DIM, LANES)),
        ],
        out_specs=[pl.BlockSpec((DEPTH, 1, X_DIM, n_mem), lambda b: (0, b, 0, 0))] * 2,
        out_shape=[out, out],
        name="mem_kv",
    )(mem2d, norm_mem, w_mem_kv_t, gk_cols)


def _in_att_kernel(*refs, keep_cols, n_convert):
    (x_ref, g_ref, wq_ref, wkv_ref, gq_ref, gx_ref, gk_ref, cos_ref, sin_ref, cost_ref, sint_ref) = refs[:11]
    n_out = 6 if keep_cols else 4
    outs = refs[11 + n_convert:11 + n_convert + n_out]
    q_ref, qx_ref, kt_ref, vt_ref = outs[:4]
    for src, dst in zip(refs[11:11 + n_convert], refs[11 + n_convert + n_out:]):
        dst[...] = src[0].astype(BF16)
    t = x_ref.shape[0]
    h = _rms_rows(x_ref[...], g_ref[0]).astype(BF16)
    bd = _head_blockdiag()
    z = jnp.dot(h, wq_ref[0], preferred_element_type=F32)
    q = z[:, :CONV_CH]
    q = q * lax.rsqrt(_head_mean_sq(q, bd) + EPS) * gq_ref[0]
    q_ref[...] = (_rope_rows(q, cos_ref[...], sin_ref[...]) * Q_SCALE).astype(q_ref.dtype)
    qx = z[:, CONV_CH:]
    qx = qx * lax.rsqrt(_head_mean_sq(qx, bd) + EPS) * gx_ref[0]
    qx_ref[...] = (qx * Q_SCALE).astype(qx_ref.dtype)

    kvt = lax.dot_general(wkv_ref[0], h, _NT, preferred_element_type=F32)
    cos_t, sin_t = cost_ref[...], sint_ref[...]
    half = HEAD_DIM // 2
    k_parts = []
    for blk in _head_norm_cols(kvt[:KV_DIM], gk_ref[0]):
        x1, x2 = blk[:half], blk[half:]
        k_parts += [x1 * cos_t - x2 * sin_t, x2 * cos_t + x1 * sin_t]
    kt = jnp.concatenate(k_parts, axis=0)
    vt = kvt[KV_DIM:]
    kt_ref[...] = kt.astype(kt_ref.dtype)
    vt_ref[...] = vt.astype(vt_ref.dtype)
    if keep_cols:
        kt32_ref, vt32_ref = outs[4:]
        kt32_ref[0] = kt[:, t - keep_cols:]
        vt32_ref[0] = vt[:, t - keep_cols:]


def _in_att(x2d, layer, j, g_mix, wq, wkv_t, gq, gx, gk_cols, cos, sin, cos_t, sin_t,
            *, tiles_per_seq, keep_cols, act_dtype, convert=()):
    n = x2d.shape[0]
    tm = min(ATT_TILE, n)
    nt = n // tm
    nseq = nt // tiles_per_seq
    row = lambda i: (i, 0)
    col = lambda i: (0, i)
    pos = lambda i: (i % tiles_per_seq, 0)
    pos_t = lambda i: (0, i % tiles_per_seq)
    out_specs = [
        pl.BlockSpec((tm, CONV_CH), row),
        pl.BlockSpec((tm, X_DIM), row),
        pl.BlockSpec((KV_DIM, tm), col),
        pl.BlockSpec((KV_DIM, tm), col),
    ]
    out_shape = [
        jax.ShapeDtypeStruct((n, CONV_CH), act_dtype),
        jax.ShapeDtypeStruct((n, X_DIM), act_dtype),
        jax.ShapeDtypeStruct((KV_DIM, n), act_dtype),
        jax.ShapeDtypeStruct((KV_DIM, n), act_dtype),
    ]
    if keep_cols:
        out_specs += [pl.BlockSpec((1, KV_DIM, keep_cols), lambda i: (i // tiles_per_seq, 0, 0))] * 2
        out_shape += [jax.ShapeDtypeStruct((nseq, KV_DIM, keep_cols), F32)] * 2
    cv_in, cv_args, cv_out, cv_shape = _convert_specs(convert, nt)
    return pl.pallas_call(
        functools.partial(_in_att_kernel, keep_cols=keep_cols, n_convert=len(convert)),
        grid=(nt,),
        in_specs=[
            pl.BlockSpec((tm, D_MODEL), row),
            _layer_spec((1, D_MODEL), layer),
            _layer_spec((D_MODEL, CONV_CH + X_DIM), j),
            _layer_spec((2 * KV_DIM, D_MODEL), j),
            _layer_spec((1, CONV_CH), j),
            _layer_spec((1, X_DIM), layer),
            _layer_spec((HEAD_DIM, LANES), j),
            pl.BlockSpec((tm, LANES), pos),
            pl.BlockSpec((tm, LANES), pos),
            pl.BlockSpec((HEAD_DIM // 2, tm), pos_t),
            pl.BlockSpec((HEAD_DIM // 2, tm), pos_t),
        ] + cv_in,
        out_specs=out_specs + cv_out,
        out_shape=out_shape + cv_shape,
        name="in_att",
    )(x2d, g_mix, wq, wkv_t, gq, gx, gk_cols, cos, sin, cos_t, sin_t, *cv_args)


def _in_conv_kernel(*refs, tiles_per_seq, period):
    if period:
        (x_ref, g_ref, w_ref, gx_ref, cw_ref, prefix_ref,
         o_ref, qx_ref, gt_ref, gbuf, pbuf) = refs
    else:
        (x_ref, g_ref, w_ref, gx_ref, cw_ref,
         o_ref, qx_ref, gt_ref, gbuf) = refs
    t = x_ref.shape[0]
    keep = gt_ref.shape[0]
    h = _rms_rows(x_ref[...], g_ref[0]).astype(BF16)

    def proj(lo, width):
        return jnp.dot(h, w_ref[:, lo:lo + width], preferred_element_type=F32)

    @pl.when(pl.program_id(0) % tiles_per_seq == 0)
    def _():
        gbuf[0:SUBLANES, :] = jnp.zeros((SUBLANES, CONV_CH), F32)

    if period:
        pbuf[0:t, :] = prefix_ref[...]
        pbuf[t:t + SUBLANES, :] = jnp.zeros((SUBLANES, CONV_CH), F32)
        tok = lax.broadcasted_iota(jnp.int32, (t, MXU_DIM), 0) % period
    cw = cw_ref[0]
    for c in range(CONV_CH // MXU_DIM):
        sl = slice(c * MXU_DIM, (c + 1) * MXU_DIM)
        gate_b = proj(sl.start, MXU_DIM)
        u = proj(CONV_CH + sl.start, MXU_DIM) * proj(2 * CONV_CH + sl.start, MXU_DIM)
        gbuf[SUBLANES:SUBLANES + t, sl] = u
        back1 = gbuf[SUBLANES - 1:SUBLANES - 1 + t, sl]
        back2 = gbuf[SUBLANES - 2:SUBLANES - 2 + t, sl]
        if period:
            back1 = jnp.where(tok >= 1, back1, pbuf[1:1 + t, sl])
            back2 = jnp.where(tok >= 2, back2, pbuf[0:t, sl])
        y = back2 * cw[0:1, sl] + back1 * cw[1:2, sl] + u * cw[2:3, sl]
        o_ref[:, sl] = (gate_b * y).astype(o_ref.dtype)
        gt_ref[:, sl] = u[t - keep:, :]
    gbuf[0:SUBLANES, :] = gbuf[t:t + SUBLANES, :]
    qx = proj(3 * CONV_CH, X_DIM)
    qx = qx * lax.rsqrt(_head_mean_sq(qx, _head_blockdiag()) + EPS) * gx_ref[0]
    qx_ref[...] = (qx * Q_SCALE).astype(qx_ref.dtype)


def _in_conv(x2d, layer, j, g_mix, w, gx, cw, prefix, *, tiles_per_seq, period, keep_rows, act_dtype):
    n = x2d.shape[0]
    tm = min(MIX_TILE, n)
    nt = n // tm
    nseq = nt // tiles_per_seq
    row = lambda i: (i, 0)
    in_specs = [
        pl.BlockSpec((tm, D_MODEL), row),
        _layer_spec((1, D_MODEL), layer),
        _const_spec((D_MODEL, CONV_IN)),
        _layer_spec((1, X_DIM), layer),
        _layer_spec((CONV_W, CONV_CH), j),
    ]
    args = [x2d, g_mix, w, gx, cw]
    scratch = [pltpu.VMEM((tm + 2 * SUBLANES, CONV_CH), F32)]
    if period:
        in_specs.append(pl.BlockSpec((tm, CONV_CH), row))
        args.append(prefix)
        scratch.append(pltpu.VMEM((tm + SUBLANES, CONV_CH), F32))
    return pl.pallas_call(
        functools.partial(_in_conv_kernel, tiles_per_seq=tiles_per_seq, period=period),
        grid=(nt,),
        in_specs=in_specs,
        out_specs=[
            pl.BlockSpec((tm, CONV_CH), row),
            pl.BlockSpec((tm, X_DIM), row),
            pl.BlockSpec((keep_rows, CONV_CH), lambda i: (i // tiles_per_seq, 0)),
        ],
        out_shape=[
            jax.ShapeDtypeStruct((n, CONV_CH), act_dtype),
            jax.ShapeDtypeStruct((n, X_DIM), act_dtype),
            jax.ShapeDtypeStruct((nseq * keep_rows, CONV_CH), F32),
        ],
        scratch_shapes=scratch,
        name="in_conv",
    )(*args)


def _pair_lhs(slabs):
    low = lax.broadcasted_iota(jnp.int32, slabs[0].shape, 1) < HEAD_DIM
    zero = jnp.zeros_like(slabs[0])
    parts = []
    for s in slabs:
        parts += [jnp.where(low, s, zero), jnp.where(low, zero, s)]
    return jnp.concatenate(parts, axis=0)


def _swa_prompt_kernel(sinks_ref, q_ref, kc_ref, kp_ref, vc_ref, vp_ref, o_ref, kfull, vfull, bias):
    tq = q_ref.shape[0]

    @pl.when(jnp.logical_and(pl.program_id(0) == 0, pl.program_id(1) == 0))
    def _():
        r = lax.broadcasted_iota(jnp.int32, (WINDOW, 2 * WINDOW), 0)
        c = lax.broadcasted_iota(jnp.int32, (WINDOW, 2 * WINDOW), 1)
        band = (c > r) & (c - WINDOW <= r)
        for kind, visible in enumerate((band, band & (c >= WINDOW))):
            base = jnp.where(visible, 0.0, NEG)
            for h in range(N_Q_HEADS):
                bias[kind, h] = jnp.where(c == 0, sinks_ref[h] * LOG2E, base)

    first_kind = jnp.where(pl.program_id(1) == 0, 1, 0)
    kfull[:, 0:WINDOW] = kp_ref[...]
    kfull[:, WINDOW:] = kc_ref[...]
    vfull[:, 0:WINDOW] = vp_ref[...]
    vfull[:, WINDOW:] = vc_ref[...]
    col0 = lax.broadcasted_iota(jnp.int32, (PAIR, 2 * WINDOW), 1) == 0
    zero_slab = jnp.zeros((PAIR, 2 * WINDOW), BF16)
    low = lax.broadcasted_iota(jnp.int32, (WINDOW, PAIR), 1) < HEAD_DIM
    for qb in range(tq // WINDOW):
        r0 = qb * WINDOW
        kind = first_kind if qb == 0 else 0
        slabs = [None] * (GQA_GROUP * KV_DIM // PAIR)
        for pr in range(KV_DIM // PAIR):
            kslab = jnp.where(col0, zero_slab, kfull[pr * PAIR:(pr + 1) * PAIR, r0:r0 + 2 * WINDOW])
            vslab = jnp.where(col0, zero_slab, vfull[pr * PAIR:(pr + 1) * PAIR, r0:r0 + 2 * WINDOW])
            lhs = _pair_lhs([q_ref[r0:r0 + WINDOW, g * KV_DIM + pr * PAIR:g * KV_DIM + (pr + 1) * PAIR]
                             for g in range(GQA_GROUP)])
            s_all = jnp.dot(lhs, kslab, preferred_element_type=F32)
            probs, inv = [], []
            for g in range(GQA_GROUP):
                for e in range(2):
                    idx = g * 2 + e
                    head = (2 * pr + e) * GQA_GROUP + g
                    p, inv_den = _softmax_rows(s_all[idx * WINDOW:(idx + 1) * WINDOW] + bias[kind, head])
                    probs.append(p)
                    inv.append(inv_den)
            res = lax.dot_general(jnp.concatenate(probs, axis=0), vslab, _NT,
                                  preferred_element_type=F32)
            for g in range(GQA_GROUP):
                lo = res[(2 * g) * WINDOW:(2 * g + 1) * WINDOW] * inv[2 * g]
                hi = res[(2 * g + 1) * WINDOW:(2 * g + 2) * WINDOW] * inv[2 * g + 1]
                slabs[g * (KV_DIM // PAIR) + pr] = jnp.where(low, lo, hi)
        o_ref[r0:r0 + WINDOW, :] = jnp.concatenate(slabs, axis=1).astype(o_ref.dtype)


def _swa_prompt(sinks, q, kt, vt, *, batch, seq):
    tq = MIX_TILE
    nt = seq // tq
    per = tq // WINDOW
    rows = lambda b, j: (b * nt + j, 0)
    cur = lambda b, j: (0, b * nt + j)
    prev = lambda b, j: (0, b * nt * per + jnp.maximum(j * per - 1, 0))
    return pl.pallas_call(
        _swa_prompt_kernel,
        grid=(batch, nt),
        in_specs=[
            pl.BlockSpec(memory_space=pltpu.SMEM),
            pl.BlockSpec((tq, CONV_CH), rows),
            pl.BlockSpec((KV_DIM, tq), cur),
            pl.BlockSpec((KV_DIM, WINDOW), prev),
            pl.BlockSpec((KV_DIM, tq), cur),
            pl.BlockSpec((KV_DIM, WINDOW), prev),
        ],
        out_specs=pl.BlockSpec((tq, CONV_CH), rows),
        out_shape=jax.ShapeDtypeStruct(q.shape, BF16),
        scratch_shapes=[pltpu.VMEM((KV_DIM, tq + WINDOW), BF16)] * 2
        + [pltpu.VMEM((2, N_Q_HEADS, WINDOW, 2 * WINDOW), F32)],
        name="swa_prompt",
    )(sinks, q, kt, kt, vt, vt)


def _swa_sample_kernel(*refs, dec, carried, slot):
    if carried:
        sinks_ref, q_ref, kn_ref, vn_ref, kb_ref, vb_ref, _, _, o_ref, ko_ref, vo_ref = refs
    else:
        sinks_ref, q_ref, kn_ref, vn_ref, kb_ref, vb_ref, o_ref, ko_ref, vo_ref = refs
    nseq = kb_ref.shape[1]
    for other in range(ko_ref.shape[0]):
        if other != slot:
            ko_ref[other] = jnp.zeros(ko_ref.shape[1:], F32)
            vo_ref[other] = jnp.zeros(vo_ref.shape[1:], F32)
    steps_per_tile = LANES // (nseq * dec)
    base = (pl.program_id(0) % steps_per_tile) * (nseq * dec)
    grp = N_KV_HEADS * dec
    rows = GQA_GROUP * grp
    ri = lax.broadcasted_iota(jnp.int32, (rows, KV_DIM), 0)
    li = lax.broadcasted_iota(jnp.int32, (rows, KV_DIM), 1)
    head_mask = ((ri % grp) // dec) == (li // HEAD_DIM)
    tok = lax.broadcasted_iota(jnp.int32, (rows, WINDOW + LANES), 0) % dec
    col = lax.broadcasted_iota(jnp.int32, (rows, WINDOW + LANES), 1)
    cache_mask = (col < WINDOW) & (col > tok)
    lane = lax.broadcasted_iota(jnp.int32, (KV_DIM, WINDOW), 1)
    sink = jnp.concatenate(
        [jnp.full((dec, 1), sinks_ref[kv * GQA_GROUP + g] * LOG2E, F32)
         for g in range(GQA_GROUP) for kv in range(N_KV_HEADS)], axis=0)
    kn = kn_ref[...]
    vn = vn_ref[...]
    kn16 = kn.astype(BF16)
    vn16 = vn.astype(BF16)
    for s0 in range(0, nseq, SEQ_GROUP):
        seqs = range(s0, s0 + SEQ_GROUP)
        scores, values = [], []
        for s in seqs:
            off = base + s * dec
            new = col - (WINDOW + off)
            mask = cache_mask | ((new >= 0) & (new <= tok))
            qs = q_ref[pl.ds(s * dec, dec), :]
            qbd = jnp.concatenate(
                [jnp.concatenate([qs[:, g * KV_DIM:(g + 1) * KV_DIM]] * N_KV_HEADS, axis=0)
                 for g in range(GQA_GROUP)], axis=0)
            qbd = jnp.where(head_mask, qbd, 0.0).astype(BF16)
            keys = jnp.concatenate([kb_ref[0, s].astype(BF16), kn16], axis=1)
            values.append(jnp.concatenate([vb_ref[0, s].astype(BF16), vn16], axis=1))
            scores.append(jnp.where(mask, jnp.dot(qbd, keys, preferred_element_type=F32), NEG))
        probs = [_softmax_rows(sc, sink) for sc in scores]
        for s, (p, inv_den), vals in zip(seqs, probs, values):
            o = lax.dot_general(p, vals, _NT, preferred_element_type=F32) * inv_den
            o = jnp.where(head_mask, o, 0.0)
            folded = []
            for g in range(GQA_GROUP):
                og = o[g * grp:g * grp + dec, :]
                for kv in range(1, N_KV_HEADS):
                    og = og + o[g * grp + kv * dec:g * grp + (kv + 1) * dec, :]
                folded.append(og)
            o_ref[pl.ds(s * dec, dec), :] = jnp.concatenate(folded, axis=1)
        for s in seqs:
            shift_new = (WINDOW - dec + LANES - (base + s * dec)) % LANES
            ko_ref[slot, s] = jnp.where(lane < WINDOW - dec, pltpu.roll(kb_ref[0, s], WINDOW - dec, axis=1),
                                     pltpu.roll(kn, shift_new, axis=1))
            vo_ref[slot, s] = jnp.where(lane < WINDOW - dec, pltpu.roll(vb_ref[0, s], WINDOW - dec, axis=1),
                                     pltpu.roll(vn, shift_new, axis=1))


def _swa_sample(sinks, q, kt_new, vt_new, kbuf, vbuf, j, carry, *, dec):
    nseq = kbuf.shape[1]
    sb = SEQ_BLOCK
    steps_per_tile = LANES // (sb * dec)
    row = lambda i: (i, 0)
    tile = lambda i: (0, i // steps_per_tile)
    cache = lambda i: (j, i, 0, 0)
    if carry is None:
        new_cache = pl.BlockSpec((kbuf.shape[0], sb, KV_DIM, WINDOW), lambda i: (0, i, 0, 0))
    else:
        new_cache = pl.BlockSpec((1, sb, KV_DIM, WINDOW), cache)
    in_specs = [
        pl.BlockSpec(memory_space=pltpu.SMEM),
        pl.BlockSpec((sb * dec, CONV_CH), row),
        pl.BlockSpec((KV_DIM, LANES), tile),
        pl.BlockSpec((KV_DIM, LANES), tile),
        pl.BlockSpec((1, sb, KV_DIM, WINDOW), cache),
        pl.BlockSpec((1, sb, KV_DIM, WINDOW), cache),
    ]
    args = [sinks, q, kt_new, vt_new, kbuf, vbuf]
    aliases = {}
    if carry is not None:
        aliases = {len(args): 1, len(args) + 1: 2}
        in_specs += [pl.BlockSpec(memory_space=pl.ANY)] * 2
        args += list(carry)
    return pl.pallas_call(
        functools.partial(_swa_sample_kernel, dec=dec, carried=carry is not None,
                          slot=j if carry is None else 0),
        grid=(nseq // sb,),
        in_specs=in_specs,
        out_specs=[
            pl.BlockSpec((sb * dec, CONV_CH), row),
            new_cache,
            new_cache,
        ],
        out_shape=[
            jax.ShapeDtypeStruct(q.shape, F32),
            jax.ShapeDtypeStruct(kbuf.shape, F32),
            jax.ShapeDtypeStruct(vbuf.shape, F32),
        ],
        input_output_aliases=aliases,
        name="swa_sample",
    )(*args)


def _xattn_tile(q_ref, mk_ref, mv_ref):
    tq = q_ref.shape[0]
    mk = mk_ref[0, 0].astype(BF16)
    mv = mv_ref[0, 0].astype(BF16)
    low = lax.broadcasted_iota(jnp.int32, (WINDOW, PAIR), 1) < HEAD_DIM
    blocks = []
    for rb in range(tq // WINDOW):
        r0 = rb * WINDOW
        slabs = []
        for pr in range(X_DIM // PAIR):
            lhs = _pair_lhs([q_ref[r0:r0 + WINDOW, pr * PAIR:(pr + 1) * PAIR]])
            s_all = jnp.dot(lhs, mk[pr * PAIR:(pr + 1) * PAIR], preferred_element_type=F32)
            p0, inv0 = _softmax_rows(s_all[:WINDOW])
            p1, inv1 = _softmax_rows(s_all[WINDOW:])
            res = lax.dot_general(jnp.concatenate([p0, p1], axis=0), mv[pr * PAIR:(pr + 1) * PAIR], _NT,
                                  preferred_element_type=F32)
            slabs.append(jnp.where(low, res[:WINDOW] * inv0, res[WINDOW:] * inv1))
        blocks.append(jnp.concatenate(slabs, axis=1).astype(BF16))
    return jnp.concatenate(blocks, axis=0)


def _xattn_sample_kernel(q_ref, mk_ref, mv_ref, o_ref, *, dec):
    nseq = mk_ref.shape[1]
    rows = N_X_HEADS * dec
    head_mask = (lax.broadcasted_iota(jnp.int32, (rows, X_DIM), 0) // dec) == (
        lax.broadcasted_iota(jnp.int32, (rows, X_DIM), 1) // HEAD_DIM)
    for s0 in range(0, nseq, SEQ_GROUP):
        seqs = range(s0, s0 + SEQ_GROUP)
        scores = []
        for s in seqs:
            qs = q_ref[pl.ds(s * dec, dec), :]
            qbd = jnp.where(head_mask, jnp.concatenate([qs] * N_X_HEADS, axis=0), 0.0).astype(BF16)
            scores.append(jnp.dot(qbd, mk_ref[0, s].astype(BF16), preferred_element_type=F32))
        probs = [_softmax_rows(sc) for sc in scores]
        for s, (p, inv_den) in zip(seqs, probs):
            o = lax.dot_general(p, mv_ref[0, s].astype(BF16), _NT, preferred_element_type=F32) * inv_den
            o = jnp.where(head_mask, o, 0.0)
            acc = o[0:dec, :]
            for h in range(1, N_X_HEADS):
                acc = acc + o[h * dec:(h + 1) * dec, :]
            o_ref[pl.ds(s * dec, dec), :] = acc


def _xattn_sample(qx, mk, mv, layer, *, dec):
    nseq, n_mem = mk.shape[1], mk.shape[3]
    sb = SEQ_BLOCK
    mem = lambda i: (layer, i, 0, 0)
    return pl.pallas_call(
        functools.partial(_xattn_sample_kernel, dec=dec),
        grid=(nseq // sb,),
        in_specs=[
            pl.BlockSpec((sb * dec, X_DIM), lambda i: (i, 0)),
            pl.BlockSpec((1, sb, X_DIM, n_mem), mem),
            pl.BlockSpec((1, sb, X_DIM, n_mem), mem),
        ],
        out_specs=pl.BlockSpec((sb * dec, X_DIM), lambda i: (i, 0)),
        out_shape=jax.ShapeDtypeStruct(qx.shape, F32),
        name="xattn_sample",
    )(qx, mk, mv)


def _mlp_tile(x, o, wo, gm, wu_ref, wd_ref):
    x1 = x + jnp.dot(o, wo, preferred_element_type=F32)
    hm = _rms_rows(x1, gm).astype(BF16)
    acc = x1
    for c in range(D_FF // FF_CHUNK):
        a = jnp.dot(hm, wu_ref[:, c * FF_CHUNK:(c + 1) * FF_CHUNK], preferred_element_type=F32)
        a = jnp.square(jnp.maximum(a, 0.0)).astype(BF16)
        acc = acc + jnp.dot(a, wd_ref[c * FF_CHUNK:(c + 1) * FF_CHUNK, :], preferred_element_type=F32)
    return acc


def _out_mlp_kernel(*refs, n_prompt_tiles, n_convert):
    (xp_ref, osp_ref, qxp_ref, mk_ref, mv_ref, xs_ref, oss_ref, oxs_ref,
     wo_ref, gm_ref, wu_ref, wd_ref) = refs[:12]
    yp_ref, ys_ref = refs[12 + n_convert:14 + n_convert]
    for src, dst in zip(refs[12:12 + n_convert], refs[14 + n_convert:]):
        dst[...] = src[0].astype(BF16)
    is_sample = pl.program_id(0) == n_prompt_tiles

    @pl.when(jnp.logical_not(is_sample))
    def _():
        o = jnp.concatenate([osp_ref[...], _xattn_tile(qxp_ref, mk_ref, mv_ref)], axis=1)
        yp_ref[...] = _mlp_tile(xp_ref[...], o, wo_ref[...], gm_ref[0], wu_ref, wd_ref)

    @pl.when(is_sample)
    def _():
        o = jnp.concatenate([oss_ref[...].astype(BF16), oxs_ref[...].astype(BF16)], axis=1)
        ys_ref[...] = _mlp_tile(xs_ref[...], o, wo_ref[...], gm_ref[0], wu_ref, wd_ref)


def _out_mlp(xp, os_p, qx_p, memory, xs, os_s, ox_s, layer, wo, gm, wu, wd, *, tiles_per_batch, convert=()):
    n = xp.shape[0]
    tm = TOKEN_TILE
    assert xs.shape[0] == tm
    npt = n // tm
    n_mem = memory[0].shape[3]
    last = npt - 1
    prow = lambda i: (jnp.minimum(i, last), 0)
    mem = lambda i: (layer, jnp.minimum(i, last) // tiles_per_batch, 0, 0)
    cv_in, cv_args, cv_out, cv_shape = _convert_specs(convert, npt, last)
    return pl.pallas_call(
        functools.partial(_out_mlp_kernel, n_prompt_tiles=npt, n_convert=len(convert)),
        grid=(npt + 1,),
        in_specs=[
            pl.BlockSpec((tm, D_MODEL), prow),
            pl.BlockSpec((tm, CONV_CH), prow),
            pl.BlockSpec((tm, X_DIM), prow),
            pl.BlockSpec((1, 1, X_DIM, n_mem), mem),
            pl.BlockSpec((1, 1, X_DIM, n_mem), mem),
            _const_spec((tm, D_MODEL)),
            _const_spec((tm, CONV_CH)),
            _const_spec((tm, X_DIM)),
            _const_spec((CONV_CH + X_DIM, D_MODEL)),
            _layer_spec((1, D_MODEL), layer),
            _const_spec((D_MODEL, D_FF)),
            _const_spec((D_FF, D_MODEL)),
        ] + cv_in,
        out_specs=[pl.BlockSpec((tm, D_MODEL), prow), pl.BlockSpec((tm, D_MODEL), lambda i: (0, 0))] + cv_out,
        out_shape=[jax.ShapeDtypeStruct((n, D_MODEL), F32), jax.ShapeDtypeStruct((tm, D_MODEL), F32)] + cv_shape,
        name="out_mlp",
    )(xp, os_p, qx_p, *memory, xs, os_s, ox_s, wo, gm, wu, wd, *cv_args)


def _rope_tables(pos):
    half = HEAD_DIM // 2
    inv = ROPE_THETA ** (-jnp.arange(half, dtype=F32) * 2.0 / HEAD_DIM)
    ang = pos[:, None] * inv[None, :]
    return jnp.cos(ang), jnp.sin(ang)


def _rope_lane_tables(cos, sin):
    reps = LANES // (HEAD_DIM // 2)
    return jnp.tile(cos, (1, reps)), jnp.concatenate([-sin, sin] * (reps // 2), axis=1)


def _group_major(w, axis):
    shape = w.shape
    w = w.reshape(shape[:axis] + (N_KV_HEADS, GQA_GROUP, HEAD_DIM) + shape[axis + 1:])
    return jnp.swapaxes(w, axis, axis + 1).reshape(shape)


def _feature_major(cache):
    lead = cache.shape[:-3]
    pos, heads, hd = cache.shape[-3:]
    nd = len(lead)
    perm = tuple(range(nd)) + (nd + 1, nd + 2, nd)
    return jnp.transpose(cache, perm).reshape(lead + (heads * hd, pos))


def _position_major(cache_t, heads):
    lead = cache_t.shape[:-2]
    pos = cache_t.shape[-1]
    nd = len(lead)
    perm = tuple(range(nd)) + (nd + 2, nd, nd + 1)
    return jnp.transpose(cache_t.reshape(lead + (heads, HEAD_DIM, pos)), perm)


def _gain_cols(g):
    return jnp.broadcast_to(g[:, :, None], g.shape + (LANES,))


def kernel(x_prompt, x_sample, mem_prompt, cache_swa_k, cache_swa_v, state_conv, cache_mem_k, cache_mem_v,
           norm_mix, w_in_att, q_norm_att, k_norm_att, sinks, w_in_conv, conv_w, norm_mem, w_mem_kv,
           q_norm_x, k_norm_x, w_out, norm_mlp, w_up, w_down):
    batch, seq, _ = x_prompt.shape
    nseq, dec, _ = x_sample.shape
    n_mem = mem_prompt.shape[1]
    assert seq % MIX_TILE == 0 and seq % ATT_TILE == 0
    assert (nseq * dec) == TOKEN_TILE and nseq % SEQ_BLOCK == 0
    assert LANES % (SEQ_BLOCK * dec) == 0

    kv0, kv1 = CONV_CH, CONV_CH + 2 * KV_DIM
    w_q = jnp.concatenate([_group_major(w_in_att[:, :, :kv0], 2), w_in_att[:, :, kv1:]], axis=2).astype(BF16)
    w_kv_t = jnp.swapaxes(w_in_att[:, :, kv0:kv1], 1, 2).astype(BF16)
    w_o = [jnp.concatenate([_group_major(w_out[i, :CONV_CH], 0), w_out[i, CONV_CH:]], axis=0).astype(BF16)
           if i % 2 == 0 else w_out[i].astype(BF16) for i in range(DEPTH)]
    w_u = w_d = w_conv = None
    w_mkv_t = jnp.swapaxes(w_mem_kv, 1, 2).astype(BF16)

    g_mix = norm_mix[:, None, :]
    g_mlp = norm_mlp[:, None, :]
    gq = jnp.tile(q_norm_att, (1, N_Q_HEADS))[:, None, :]
    gx = jnp.tile(q_norm_x, (1, N_X_HEADS))[:, None, :]
    gk_cols = _gain_cols(k_norm_att)
    gkx_cols = _gain_cols(k_norm_x)

    cos_p, sin_p = _rope_tables(jnp.arange(seq, dtype=F32))
    cos_s, sin_s = _rope_tables(PAST_LEN + (jnp.arange(nseq * dec) % dec).astype(F32))
    rope_p = _rope_lane_tables(cos_p, sin_p) + (cos_p.T, sin_p.T)
    rope_s = _rope_lane_tables(cos_s, sin_s) + (cos_s.T, sin_s.T)

    mk_p, mv_p = _mem_kv(mem_prompt.reshape(batch * n_mem, D_MODEL), norm_mem[:, None, :], w_mkv_t, gkx_cols,
                         batch=batch, n_mem=n_mem)
    mk_s = _feature_major(cache_mem_k)
    mv_s = _feature_major(cache_mem_v)
    kbuf = _feature_major(cache_swa_k)
    vbuf = _feature_major(cache_swa_v)

    xp = x_prompt.reshape(batch * seq, D_MODEL)
    xs = x_sample.reshape(nseq * dec, D_MODEL)
    k_p, v_p, c_p, c_s = [], [], [], []
    cache_s = None
    for i in range(DEPTH):
        j = i // 2
        if i % 2 == 0:
            first = [(w_up, 0), (w_down, 0)] if i == 0 else []
            q, qx_p, kt, vt, kt32, vt32, *w_first = _in_att(
                xp, i, j, g_mix, w_q, w_kv_t, gq, gx, gk_cols, *rope_p,
                tiles_per_seq=seq // ATT_TILE, keep_cols=WINDOW, act_dtype=BF16, convert=first)
            if w_first:
                w_u, w_d = w_first
            k_p.append(kt32)
            v_p.append(vt32)
            os_p = _swa_prompt(sinks[j], q, kt, vt, batch=batch, seq=seq)
            q, qx_s, kt, vt = _in_att(xs, i, j, g_mix, w_q, w_kv_t, gq, gx, gk_cols, *rope_s,
                                      tiles_per_seq=1, keep_cols=0, act_dtype=F32)
            os_s, *cache_s = _swa_sample(sinks[j], q, kt, vt, kbuf, vbuf, j, cache_s, dec=dec)
        else:
            os_p, qx_p, tail = _in_conv(xp, i, j, g_mix, w_conv, gx, conv_w, None,
                                        tiles_per_seq=seq // MIX_TILE, period=0,
                                        keep_rows=SUBLANES, act_dtype=BF16)
            c_p.append(tail.reshape(batch, SUBLANES, CONV_CH)[:, SUBLANES - (CONV_W - 1):])
            prefix = jnp.pad(state_conv[j], ((0, 0), (0, dec - (CONV_W - 1)), (0, 0)))
            os_s, qx_s, tail = _in_conv(xs, i, j, g_mix, w_conv, gx, conv_w,
                                        prefix.reshape(nseq * dec, CONV_CH),
                                        tiles_per_seq=1, period=dec,
                                        keep_rows=nseq * dec, act_dtype=F32)
            c_s.append(tail.reshape(nseq, dec, CONV_CH)[:, dec - (CONV_W - 1):])
        ox_s = _xattn_sample(qx_s, mk_s, mv_s, i, dec=dec)
        convert = []
        if i + 1 < DEPTH:
            convert = [(w_up, i + 1), (w_down, i + 1)] + ([(w_in_conv, (i + 1) // 2)] if i % 2 == 0 else [])
        xp, xs, *w_next = _out_mlp(xp, os_p, qx_p, (mk_p, mv_p), xs, os_s, ox_s, i, w_o[i], g_mlp, w_u, w_d,
                                   tiles_per_batch=seq // TOKEN_TILE, convert=convert)
        if w_next:
            w_u, w_d = w_next[:2]
            w_conv = w_next[2] if len(w_next) > 2 else None

    return (xp.reshape(batch, seq, D_MODEL),
            xs.reshape(nseq, dec, D_MODEL),
            _position_major(jnp.stack(k_p), N_KV_HEADS),
            _position_major(jnp.stack(v_p), N_KV_HEADS),
            jnp.stack(c_p),
            _position_major(mk_p, N_X_HEADS),
            _position_major(mv_p, N_X_HEADS),
            _position_major(cache_s[0], N_KV_HEADS),
            _position_major(cache_s[1], N_KV_HEADS),
            jnp.stack(c_s))
```

```python
import functools

import jax
import jax.numpy as jnp
from jax import lax
from jax.experimental import pallas as pl
from jax.experimental.pallas import tpu as pltpu

F32 = jnp.float32
BF16 = jnp.bfloat16

D_MODEL = 1024
DEPTH = 4
HEAD_DIM = 64
N_Q_HEADS = 12
N_KV_HEADS = 4
GQA_GROUP = N_Q_HEADS // N_KV_HEADS
WINDOW = 128
PAST_LEN = 8192
ROPE_THETA = 10000.0
CONV_CH = N_Q_HEADS * HEAD_DIM
CONV_W = 3
N_X_HEADS = 4
X_DIM = N_X_HEADS * HEAD_DIM
KV_DIM = N_KV_HEADS * HEAD_DIM
D_FF = 4 * D_MODEL
CONV_IN = 3 * CONV_CH + X_DIM
EPS = 1e-6
NEG = -1e30
LOG2E = 1.4426950408889634
Q_SCALE = HEAD_DIM ** -0.5 * LOG2E

LANES = 128
SUBLANES = 8
MXU_DIM = 256
TOKEN_TILE = 512
ATT_TILE = 1024
MIX_TILE = 2048
FF_CHUNK = 2048
SEQ_BLOCK = 16
SEQ_GROUP = 16
PAIR = 2 * HEAD_DIM

_NT = (((1,), (1,)), ((), ()))


def _const_spec(shape):
    nd = len(shape)
    return pl.BlockSpec(shape, lambda *_: (0,) * nd, pipeline_mode=pl.Buffered(1))


def _layer_spec(shape, layer):
    nd = len(shape)
    return pl.BlockSpec((1,) + shape, lambda *_: (layer,) + (0,) * nd, pipeline_mode=pl.Buffered(1))


def _rms_rows(x, g):
    return x * lax.rsqrt(jnp.mean(x * x, axis=-1, keepdims=True) + EPS) * g


def _head_blockdiag():
    r = lax.broadcasted_iota(jnp.int32, (MXU_DIM, MXU_DIM), 0) // HEAD_DIM
    c = lax.broadcasted_iota(jnp.int32, (MXU_DIM, MXU_DIM), 1) // HEAD_DIM
    return jnp.where(r == c, 1.0, 0.0).astype(BF16)


def _head_mean_sq(z, bd):
    sq = z * z
    hi = sq.astype(BF16)
    lo = (sq - hi.astype(F32)).astype(BF16)
    parts = []
    for c in range(z.shape[1] // MXU_DIM):
        sl = slice(c * MXU_DIM, (c + 1) * MXU_DIM)
        parts.append(jnp.dot(hi[:, sl], bd, preferred_element_type=F32)
                     + jnp.dot(lo[:, sl], bd, preferred_element_type=F32))
    ms = parts[0] if len(parts) == 1 else jnp.concatenate(parts, axis=1)
    return ms * (1.0 / HEAD_DIM)


def _rope_rows(x, cos, sin_signed):
    t = x.shape[0]
    first_half = (lax.broadcasted_iota(jnp.int32, (t, LANES), 1) & (HEAD_DIM - 1)) < HEAD_DIM // 2
    parts = []
    for c in range(x.shape[1] // LANES):
        xc = x[:, c * LANES:(c + 1) * LANES]
        partner = jnp.where(first_half,
                            pltpu.roll(xc, LANES - HEAD_DIM // 2, axis=1),
                            pltpu.roll(xc, HEAD_DIM // 2, axis=1))
        parts.append(xc * cos + partner * sin_signed)
    return jnp.concatenate(parts, axis=1)


def _lane_tile(col, n):
    return col if n == LANES else jnp.concatenate([col] * (n // LANES), axis=1)


def _head_norm_cols(xt, g_col):
    t = xt.shape[1]
    g = _lane_tile(g_col, t)
    parts = []
    for h in range(xt.shape[0] // HEAD_DIM):
        blk = xt[h * HEAD_DIM:(h + 1) * HEAD_DIM, :]
        ms = jnp.mean(blk * blk, axis=0, keepdims=True)
        parts.append(blk * lax.rsqrt(ms + EPS) * g)
    return parts


def _softmax_rows(s, sink=None):
    m = jnp.max(s, axis=-1, keepdims=True)
    if sink is not None:
        m = jnp.maximum(m, sink)
    p = jnp.exp2(s - m)
    den = jnp.sum(p, axis=-1, keepdims=True)
    if sink is not None:
        den = den + jnp.exp2(sink - m)
    return p.astype(BF16), den


def _pair_normalise(low, res_lo, res_hi, den_lo, den_hi):
    return jnp.where(low, res_lo, res_hi) / jnp.where(low, den_lo, den_hi)


def _convert_specs(convert, steps, last=None):
    last = steps - 1 if last is None else last
    in_specs, args, out_specs, out_shape = [], [], [], []
    for stack, l in convert:
        _, r, c = stack.shape
        in_specs.append(pl.BlockSpec((1, r // steps, c), lambda i, l=l: (l, jnp.minimum(i, last), 0)))
        args.append(stack)
        out_specs.append(pl.BlockSpec((r // steps, c), lambda i: (jnp.minimum(i, last), 0)))
        out_shape.append(jax.ShapeDtypeStruct((r, c), BF16))
    return in_specs, args, out_specs, out_shape


def _mem_kv_kernel(mem_ref, g_ref, wt_ref, gk_ref, mk_ref, mv_ref):
    x = mem_ref[...]
    xn = x * lax.rsqrt(jnp.mean(x * x, axis=-1, keepdims=True) + EPS)
    for i in range(DEPTH):
        h = (xn * g_ref[i]).astype(BF16)
        kvt = lax.dot_general(wt_ref[i], h, _NT, preferred_element_type=F32)
        mk_ref[i, 0] = jnp.concatenate(_head_norm_cols(kvt[:X_DIM], gk_ref[i]), axis=0)
        mv_ref[i, 0] = kvt[X_DIM:]


def _mem_kv(mem2d, norm_mem, w_mem_kv_t, gk_cols, *, batch, n_mem):
    out = jax.ShapeDtypeStruct((DEPTH, batch, X_DIM, n_mem), F32)
    return pl.pallas_call(
        _mem_kv_kernel,
        grid=(batch,),
        in_specs=[
            pl.BlockSpec((n_mem, D_MODEL), lambda b: (b, 0)),
            _const_spec((DEPTH, 1, D_MODEL)),
            _const_spec((DEPTH, 2 * X_DIM, D_MODEL)),
            _const_spec((DEPTH, HEAD_DIM, LANES)),
        ],
        out_specs=[pl.BlockSpec((DEPTH, 1, X_DIM, n_mem), lambda b: (0, b, 0, 0))] * 2,
        out_shape=[out, out],
        name="mem_kv",
    )(mem2d, norm_mem, w_mem_kv_t, gk_cols)


def _in_att_kernel(*refs, keep_cols, n_convert):
    (x_ref, g_ref, wq_ref, wkv_ref, gq_ref, gx_ref, gk_ref, cos_ref, sin_ref, cost_ref, sint_ref) = refs[:11]
    n_out = 6 if keep_cols else 4
    outs = refs[11 + n_convert:11 + n_convert + n_out]
    q_ref, qx_ref, kt_ref, vt_ref = outs[:4]
    for src, dst in zip(refs[11:11 + n_convert], refs[11 + n_convert + n_out:]):
        dst[...] = src[0].astype(BF16)
    t = x_ref.shape[0]
    h = _rms_rows(x_ref[...], g_ref[0]).astype(BF16)
    bd = _head_blockdiag()
    z = jnp.dot(h, wq_ref[0], preferred_element_type=F32)
    q = z[:, :CONV_CH]
    q = q * lax.rsqrt(_head_mean_sq(q, bd) + EPS) * gq_ref[0]
    q_ref[...] = (_rope_rows(q, cos_ref[...], sin_ref[...]) * Q_SCALE).astype(q_ref.dtype)
    qx = z[:, CONV_CH:]
    qx = qx * lax.rsqrt(_head_mean_sq(qx, bd) + EPS) * gx_ref[0]
    qx_ref[...] = (qx * Q_SCALE).astype(qx_ref.dtype)

    kvt = lax.dot_general(wkv_ref[0], h, _NT, preferred_element_type=F32)
    cos_t, sin_t = cost_ref[...], sint_ref[...]
    half = HEAD_DIM // 2
    k_parts = []
    for blk in _head_norm_cols(kvt[:KV_DIM], gk_ref[0]):
        x1, x2 = blk[:half], blk[half:]
        k_parts += [x1 * cos_t - x2 * sin_t, x2 * cos_t + x1 * sin_t]
    kt = jnp.concatenate(k_parts, axis=0)
    vt = kvt[KV_DIM:]
    kt_ref[...] = kt.astype(kt_ref.dtype)
    vt_ref[...] = vt.astype(vt_ref.dtype)
    if keep_cols:
        kt32_ref, vt32_ref = outs[4:]
        kt32_ref[0] = kt[:, t - keep_cols:]
        vt32_ref[0] = vt[:, t - keep_cols:]


def _in_att(x2d, layer, j, g_mix, wq, wkv_t, gq, gx, gk_cols, cos, sin, cos_t, sin_t,
            *, tiles_per_seq, keep_cols, act_dtype, convert=()):
    n = x2d.shape[0]
    tm = min(ATT_TILE, n)
    nt = n // tm
    nseq = nt // tiles_per_seq
    row = lambda i: (i, 0)
    col = lambda i: (0, i)
    pos = lambda i: (i % tiles_per_seq, 0)
    pos_t = lambda i: (0, i % tiles_per_seq)
    out_specs = [
        pl.BlockSpec((tm, CONV_CH), row),
        pl.BlockSpec((tm, X_DIM), row),
        pl.BlockSpec((KV_DIM, tm), col),
        pl.BlockSpec((KV_DIM, tm), col),
    ]
    out_shape = [
        jax.ShapeDtypeStruct((n, CONV_CH), act_dtype),
        jax.ShapeDtypeStruct((n, X_DIM), act_dtype),
        jax.ShapeDtypeStruct((KV_DIM, n), act_dtype),
        jax.ShapeDtypeStruct((KV_DIM, n), act_dtype),
    ]
    if keep_cols:
        out_specs += [pl.BlockSpec((1, KV_DIM, keep_cols), lambda i: (i // tiles_per_seq, 0, 0))] * 2
        out_shape += [jax.ShapeDtypeStruct((nseq, KV_DIM, keep_cols), F32)] * 2
    cv_in, cv_args, cv_out, cv_shape = _convert_specs(convert, nt)
    return pl.pallas_call(
        functools.partial(_in_att_kernel, keep_cols=keep_cols, n_convert=len(convert)),
        grid=(nt,),
        in_specs=[
            pl.BlockSpec((tm, D_MODEL), row),
            _layer_spec((1, D_MODEL), layer),
            _layer_spec((D_MODEL, CONV_CH + X_DIM), j),
            _layer_spec((2 * KV_DIM, D_MODEL), j),
            _layer_spec((1, CONV_CH), j),
            _layer_spec((1, X_DIM), layer),
            _layer_spec((HEAD_DIM, LANES), j),
            pl.BlockSpec((tm, LANES), pos),
            pl.BlockSpec((tm, LANES), pos),
            pl.BlockSpec((HEAD_DIM // 2, tm), pos_t),
            pl.BlockSpec((HEAD_DIM // 2, tm), pos_t),
        ] + cv_in,
        out_specs=out_specs + cv_out,
        out_shape=out_shape + cv_shape,
        name="in_att",
    )(x2d, g_mix, wq, wkv_t, gq, gx, gk_cols, cos, sin, cos_t, sin_t, *cv_args)


def _in_conv_kernel(*refs, tiles_per_seq, period):
    if period:
        (x_ref, g_ref, w_ref, gx_ref, cw_ref, prefix_ref,
         o_ref, qx_ref, gt_ref, gbuf, pbuf) = refs
    else:
        (x_ref, g_ref, w_ref, gx_ref, cw_ref,
         o_ref, qx_ref, gt_ref, gbuf) = refs
    t = x_ref.shape[0]
    keep = gt_ref.shape[0]
    h = _rms_rows(x_ref[...], g_ref[0]).astype(BF16)

    def proj(lo, width):
        return jnp.dot(h, w_ref[:, lo:lo + width], preferred_element_type=F32)

    @pl.when(pl.program_id(0) % tiles_per_seq == 0)
    def _():
        gbuf[0:SUBLANES, :] = jnp.zeros((SUBLANES, CONV_CH), F32)

    if period:
        pbuf[0:t, :] = prefix_ref[...]
        pbuf[t:t + SUBLANES, :] = jnp.zeros((SUBLANES, CONV_CH), F32)
        tok = lax.broadcasted_iota(jnp.int32, (t, MXU_DIM), 0) % period
    cw = cw_ref[0]
    for c in range(CONV_CH // MXU_DIM):
        sl = slice(c * MXU_DIM, (c + 1) * MXU_DIM)
        gate_b = proj(sl.start, MXU_DIM)
        u = proj(CONV_CH + sl.start, MXU_DIM) * proj(2 * CONV_CH + sl.start, MXU_DIM)
        gbuf[SUBLANES:SUBLANES + t, sl] = u
        back1 = gbuf[SUBLANES - 1:SUBLANES - 1 + t, sl]
        back2 = gbuf[SUBLANES - 2:SUBLANES - 2 + t, sl]
        if period:
            back1 = jnp.where(tok >= 1, back1, pbuf[1:1 + t, sl])
            back2 = jnp.where(tok >= 2, back2, pbuf[0:t, sl])
        y = back2 * cw[0:1, sl] + back1 * cw[1:2, sl] + u * cw[2:3, sl]
        o_ref[:, sl] = (gate_b * y).astype(o_ref.dtype)
        gt_ref[:, sl] = u[t - keep:, :]
    gbuf[0:SUBLANES, :] = gbuf[t:t + SUBLANES, :]
    qx = proj(3 * CONV_CH, X_DIM)
    qx = qx * lax.rsqrt(_head_mean_sq(qx, _head_blockdiag()) + EPS) * gx_ref[0]
    qx_ref[...] = (qx * Q_SCALE).astype(qx_ref.dtype)


def _in_conv(x2d, layer, j, g_mix, w, gx, cw, prefix, *, tiles_per_seq, period, keep_rows, act_dtype):
    n = x2d.shape[0]
    tm = min(MIX_TILE, n)
    nt = n // tm
    nseq = nt // tiles_per_seq
    row = lambda i: (i, 0)
    in_specs = [
        pl.BlockSpec((tm, D_MODEL), row),
        _layer_spec((1, D_MODEL), layer),
        _const_spec((D_MODEL, CONV_IN)),
        _layer_spec((1, X_DIM), layer),
        _layer_spec((CONV_W, CONV_CH), j),
    ]
    args = [x2d, g_mix, w, gx, cw]
    scratch = [pltpu.VMEM((tm + 2 * SUBLANES, CONV_CH), F32)]
    if period:
        in_specs.append(pl.BlockSpec((tm, CONV_CH), row))
        args.append(prefix)
        scratch.append(pltpu.VMEM((tm + SUBLANES, CONV_CH), F32))
    return pl.pallas_call(
        functools.partial(_in_conv_kernel, tiles_per_seq=tiles_per_seq, period=period),
        grid=(nt,),
        in_specs=in_specs,
        out_specs=[
            pl.BlockSpec((tm, CONV_CH), row),
            pl.BlockSpec((tm, X_DIM), row),
            pl.BlockSpec((keep_rows, CONV_CH), lambda i: (i // tiles_per_seq, 0)),
        ],
        out_shape=[
            jax.ShapeDtypeStruct((n, CONV_CH), act_dtype),
            jax.ShapeDtypeStruct((n, X_DIM), act_dtype),
            jax.ShapeDtypeStruct((nseq * keep_rows, CONV_CH), F32),
        ],
        scratch_shapes=scratch,
        name="in_conv",
    )(*args)


def _pair_lhs(slabs):
    low = lax.broadcasted_iota(jnp.int32, slabs[0].shape, 1) < HEAD_DIM
    zero = jnp.zeros_like(slabs[0])
    parts = []
    for s in slabs:
        parts += [jnp.where(low, s, zero), jnp.where(low, zero, s)]
    return jnp.concatenate(parts, axis=0)


def _swa_prompt_kernel(sinks_ref, q_ref, kc_ref, kp_ref, vc_ref, vp_ref, o_ref, kfull, vfull, bias):
    tq = q_ref.shape[0]

    @pl.when(jnp.logical_and(pl.program_id(0) == 0, pl.program_id(1) == 0))
    def _():
        r = lax.broadcasted_iota(jnp.int32, (WINDOW, 2 * WINDOW), 0)
        c = lax.broadcasted_iota(jnp.int32, (WINDOW, 2 * WINDOW), 1)
        band = (c > r) & (c - WINDOW <= r)
        for kind, visible in enumerate((band, band & (c >= WINDOW))):
            base = jnp.where(visible, 0.0, NEG)
            for h in range(N_Q_HEADS):
                bias[kind, h] = jnp.where(c == 0, sinks_ref[h] * LOG2E, base)

    first_kind = jnp.where(pl.program_id(1) == 0, 1, 0)
    kfull[:, 0:WINDOW] = kp_ref[...]
    kfull[:, WINDOW:] = kc_ref[...]
    vfull[:, 0:WINDOW] = vp_ref[...]
    vfull[:, WINDOW:] = vc_ref[...]
    col0 = lax.broadcasted_iota(jnp.int32, (PAIR, 2 * WINDOW), 1) == 0
    zero_slab = jnp.zeros((PAIR, 2 * WINDOW), BF16)
    low = lax.broadcasted_iota(jnp.int32, (WINDOW, PAIR), 1) < HEAD_DIM
    for qb in range(tq // WINDOW):
        r0 = qb * WINDOW
        kind = first_kind if qb == 0 else 0
        slabs = [None] * (GQA_GROUP * KV_DIM // PAIR)
        for pr in range(KV_DIM // PAIR):
            kslab = jnp.where(col0, zero_slab, kfull[pr * PAIR:(pr + 1) * PAIR, r0:r0 + 2 * WINDOW])
            vslab = jnp.where(col0, zero_slab, vfull[pr * PAIR:(pr + 1) * PAIR, r0:r0 + 2 * WINDOW])
            lhs = _pair_lhs([q_ref[r0:r0 + WINDOW, g * KV_DIM + pr * PAIR:g * KV_DIM + (pr + 1) * PAIR]
                             for g in range(GQA_GROUP)])
            s_all = jnp.dot(lhs, kslab, preferred_element_type=F32)
            probs, dens = [], []
            for g in range(GQA_GROUP):
                for e in range(2):
                    idx = g * 2 + e
                    head = (2 * pr + e) * GQA_GROUP + g
                    p, den = _softmax_rows(s_all[idx * WINDOW:(idx + 1) * WINDOW] + bias[kind, head])
                    probs.append(p)
                    dens.append(den)
            res = lax.dot_general(jnp.concatenate(probs, axis=0), vslab, _NT,
                                  preferred_element_type=F32)
            for g in range(GQA_GROUP):
                slabs[g * (KV_DIM // PAIR) + pr] = _pair_normalise(
                    low, res[(2 * g) * WINDOW:(2 * g + 1) * WINDOW], res[(2 * g + 1) * WINDOW:(2 * g + 2) * WINDOW],
                    dens[2 * g], dens[2 * g + 1])
        o_ref[r0:r0 + WINDOW, :] = jnp.concatenate(slabs, axis=1).astype(o_ref.dtype)


def _swa_prompt(sinks, q, kt, vt, *, batch, seq):
    tq = MIX_TILE
    nt = seq // tq
    per = tq // WINDOW
    rows = lambda b, j: (b * nt + j, 0)
    cur = lambda b, j: (0, b * nt + j)
    prev = lambda b, j: (0, b * nt * per + jnp.maximum(j * per - 1, 0))
    return pl.pallas_call(
        _swa_prompt_kernel,
        grid=(batch, nt),
        in_specs=[
            pl.BlockSpec(memory_space=pltpu.SMEM),
            pl.BlockSpec((tq, CONV_CH), rows),
            pl.BlockSpec((KV_DIM, tq), cur),
            pl.BlockSpec((KV_DIM, WINDOW), prev),
            pl.BlockSpec((KV_DIM, tq), cur),
            pl.BlockSpec((KV_DIM, WINDOW), prev),
        ],
        out_specs=pl.BlockSpec((tq, CONV_CH), rows),
        out_shape=jax.ShapeDtypeStruct(q.shape, BF16),
        scratch_shapes=[pltpu.VMEM((KV_DIM, tq + WINDOW), BF16)] * 2
        + [pltpu.VMEM((2, N_Q_HEADS, WINDOW, 2 * WINDOW), F32)],
        name="swa_prompt",
    )(sinks, q, kt, kt, vt, vt)


def _swa_sample_kernel(*refs, dec, carried, slot):
    if carried:
        sinks_ref, q_ref, kn_ref, vn_ref, kb_ref, vb_ref, _, _, o_ref, ko_ref, vo_ref = refs
    else:
        sinks_ref, q_ref, kn_ref, vn_ref, kb_ref, vb_ref, o_ref, ko_ref, vo_ref = refs
    nseq = kb_ref.shape[1]
    for other in range(ko_ref.shape[0]):
        if other != slot:
            ko_ref[other] = jnp.zeros(ko_ref.shape[1:], F32)
            vo_ref[other] = jnp.zeros(vo_ref.shape[1:], F32)
    steps_per_tile = LANES // (nseq * dec)
    base = (pl.program_id(0) % steps_per_tile) * (nseq * dec)
    grp = N_KV_HEADS * dec
    rows = GQA_GROUP * grp
    ri = lax.broadcasted_iota(jnp.int32, (rows, KV_DIM), 0)
    li = lax.broadcasted_iota(jnp.int32, (rows, KV_DIM), 1)
    head_mask = ((ri % grp) // dec) == (li // HEAD_DIM)
    tok = lax.broadcasted_iota(jnp.int32, (rows, WINDOW + LANES), 0) % dec
    col = lax.broadcasted_iota(jnp.int32, (rows, WINDOW + LANES), 1)
    cache_mask = (col < WINDOW) & (col > tok)
    lane = lax.broadcasted_iota(jnp.int32, (KV_DIM, WINDOW), 1)
    sink = jnp.concatenate(
        [jnp.full((dec, 1), sinks_ref[kv * GQA_GROUP + g] * LOG2E, F32)
         for g in range(GQA_GROUP) for kv in range(N_KV_HEADS)], axis=0)
    kn = kn_ref[...]
    vn = vn_ref[...]
    kn16 = kn.astype(BF16)
    vn16 = vn.astype(BF16)
    for s0 in range(0, nseq, SEQ_GROUP):
        seqs = range(s0, s0 + SEQ_GROUP)
        scores, values = [], []
        for s in seqs:
            off = base + s * dec
            new = col - (WINDOW + off)
            mask = cache_mask | ((new >= 0) & (new <= tok))
            qs = q_ref[pl.ds(s * dec, dec), :]
            qbd = jnp.concatenate(
                [jnp.concatenate([qs[:, g * KV_DIM:(g + 1) * KV_DIM]] * N_KV_HEADS, axis=0)
                 for g in range(GQA_GROUP)], axis=0)
            qbd = jnp.where(head_mask, qbd, 0.0).astype(BF16)
            keys = jnp.concatenate([kb_ref[0, s].astype(BF16), kn16], axis=1)
            values.append(jnp.concatenate([vb_ref[0, s].astype(BF16), vn16], axis=1))
            scores.append(jnp.where(mask, jnp.dot(qbd, keys, preferred_element_type=F32), NEG))
        probs = [_softmax_rows(sc, sink) for sc in scores]
        for s, (p, den), vals in zip(seqs, probs, values):
            o = lax.dot_general(p, vals, _NT, preferred_element_type=F32) * (1.0 / den)
            o = jnp.where(head_mask, o, 0.0)
            folded = []
            for g in range(GQA_GROUP):
                og = o[g * grp:g * grp + dec, :]
                for kv in range(1, N_KV_HEADS):
                    og = og + o[g * grp + kv * dec:g * grp + (kv + 1) * dec, :]
                folded.append(og)
            o_ref[pl.ds(s * dec, dec), :] = jnp.concatenate(folded, axis=1)
        for s in seqs:
            shift_new = (WINDOW - dec + LANES - (base + s * dec)) % LANES
            ko_ref[slot, s] = jnp.where(lane < WINDOW - dec, pltpu.roll(kb_ref[0, s], WINDOW - dec, axis=1),
                                     pltpu.roll(kn, shift_new, axis=1))
            vo_ref[slot, s] = jnp.where(lane < WINDOW - dec, pltpu.roll(vb_ref[0, s], WINDOW - dec, axis=1),
                                     pltpu.roll(vn, shift_new, axis=1))


def _swa_sample(sinks, q, kt_new, vt_new, kbuf, vbuf, j, carry, *, dec):
    nseq = kbuf.shape[1]
    sb = SEQ_BLOCK
    steps_per_tile = LANES // (sb * dec)
    row = lambda i: (i, 0)
    tile = lambda i: (0, i // steps_per_tile)
    cache = lambda i: (j, i, 0, 0)
    if carry is None:
        new_cache = pl.BlockSpec((kbuf.shape[0], sb, KV_DIM, WINDOW), lambda i: (0, i, 0, 0))
    else:
        new_cache = pl.BlockSpec((1, sb, KV_DIM, WINDOW), cache)
    in_specs = [
        pl.BlockSpec(memory_space=pltpu.SMEM),
        pl.BlockSpec((sb * dec, CONV_CH), row),
        pl.BlockSpec((KV_DIM, LANES), tile),
        pl.BlockSpec((KV_DIM, LANES), tile),
        pl.BlockSpec((1, sb, KV_DIM, WINDOW), cache),
        pl.BlockSpec((1, sb, KV_DIM, WINDOW), cache),
    ]
    args = [sinks, q, kt_new, vt_new, kbuf, vbuf]
    aliases = {}
    if carry is not None:
        aliases = {len(args): 1, len(args) + 1: 2}
        in_specs += [pl.BlockSpec(memory_space=pl.ANY)] * 2
        args += list(carry)
    return pl.pallas_call(
        functools.partial(_swa_sample_kernel, dec=dec, carried=carry is not None,
                          slot=j if carry is None else 0),
        grid=(nseq // sb,),
        in_specs=in_specs,
        out_specs=[
            pl.BlockSpec((sb * dec, CONV_CH), row),
            new_cache,
            new_cache,
        ],
        out_shape=[
            jax.ShapeDtypeStruct(q.shape, F32),
            jax.ShapeDtypeStruct(kbuf.shape, F32),
            jax.ShapeDtypeStruct(vbuf.shape, F32),
        ],
        input_output_aliases=aliases,
        name="swa_sample",
    )(*args)


def _xattn_tile(q_ref, mk_ref, mv_ref):
    tq = q_ref.shape[0]
    mk = mk_ref[0, 0].astype(BF16)
    mv = mv_ref[0, 0].astype(BF16)
    low = lax.broadcasted_iota(jnp.int32, (WINDOW, PAIR), 1) < HEAD_DIM
    blocks = []
    for rb in range(tq // WINDOW):
        r0 = rb * WINDOW
        slabs = []
        for pr in range(X_DIM // PAIR):
            lhs = _pair_lhs([q_ref[r0:r0 + WINDOW, pr * PAIR:(pr + 1) * PAIR]])
            s_all = jnp.dot(lhs, mk[pr * PAIR:(pr + 1) * PAIR], preferred_element_type=F32)
            p0, den0 = _softmax_rows(s_all[:WINDOW])
            p1, den1 = _softmax_rows(s_all[WINDOW:])
            res = lax.dot_general(jnp.concatenate([p0, p1], axis=0), mv[pr * PAIR:(pr + 1) * PAIR], _NT,
                                  preferred_element_type=F32)
            slabs.append(_pair_normalise(low, res[:WINDOW], res[WINDOW:], den0, den1))
        blocks.append(jnp.concatenate(slabs, axis=1).astype(BF16))
    return jnp.concatenate(blocks, axis=0)


def _xattn_sample_kernel(q_ref, mk_ref, mv_ref, o_ref, *, dec):
    nseq = mk_ref.shape[1]
    rows = N_X_HEADS * dec
    head_mask = (lax.broadcasted_iota(jnp.int32, (rows, X_DIM), 0) // dec) == (
        lax.broadcasted_iota(jnp.int32, (rows, X_DIM), 1) // HEAD_DIM)
    for s0 in range(0, nseq, SEQ_GROUP):
        seqs = range(s0, s0 + SEQ_GROUP)
        scores = []
        for s in seqs:
            qs = q_ref[pl.ds(s * dec, dec), :]
            qbd = jnp.where(head_mask, jnp.concatenate([qs] * N_X_HEADS, axis=0), 0.0).astype(BF16)
            scores.append(jnp.dot(qbd, mk_ref[0, s].astype(BF16), preferred_element_type=F32))
        probs = [_softmax_rows(sc) for sc in scores]
        for s, (p, den) in zip(seqs, probs):
            o = lax.dot_general(p, mv_ref[0, s].astype(BF16), _NT, preferred_element_type=F32) * (1.0 / den)
            o = jnp.where(head_mask, o, 0.0)
            acc = o[0:dec, :]
            for h in range(1, N_X_HEADS):
                acc = acc + o[h * dec:(h + 1) * dec, :]
            o_ref[pl.ds(s * dec, dec), :] = acc


def _xattn_sample(qx, mk, mv, layer, *, dec):
    nseq, n_mem = mk.shape[1], mk.shape[3]
    sb = 2 * SEQ_BLOCK
    mem = lambda i: (layer, i, 0, 0)
    return pl.pallas_call(
        functools.partial(_xattn_sample_kernel, dec=dec),
        grid=(nseq // sb,),
        in_specs=[
            pl.BlockSpec((sb * dec, X_DIM), lambda i: (i, 0)),
            pl.BlockSpec((1, sb, X_DIM, n_mem), mem),
            pl.BlockSpec((1, sb, X_DIM, n_mem), mem),
        ],
        out_specs=pl.BlockSpec((sb * dec, X_DIM), lambda i: (i, 0)),
        out_shape=jax.ShapeDtypeStruct(qx.shape, F32),
        name="xattn_sample",
    )(qx, mk, mv)


def _mlp_tile(x, o, wo, gm, wu_ref, wd_ref):
    x1 = x + jnp.dot(o, wo, preferred_element_type=F32)
    hm = _rms_rows(x1, gm).astype(BF16)
    acc = x1
    for c in range(D_FF // FF_CHUNK):
        a = jnp.dot(hm, wu_ref[:, c * FF_CHUNK:(c + 1) * FF_CHUNK], preferred_element_type=F32)
        a = jnp.square(jnp.maximum(a, 0.0)).astype(BF16)
        acc = acc + jnp.dot(a, wd_ref[c * FF_CHUNK:(c + 1) * FF_CHUNK, :], preferred_element_type=F32)
    return acc


def _out_mlp_kernel(*refs, n_prompt_tiles, n_convert):
    (xp_ref, osp_ref, qxp_ref, mk_ref, mv_ref, xs_ref, oss_ref, oxs_ref,
     wo_ref, gm_ref, wu_ref, wd_ref) = refs[:12]
    yp_ref, ys_ref = refs[12 + n_convert:14 + n_convert]
    for src, dst in zip(refs[12:12 + n_convert], refs[14 + n_convert:]):
        dst[...] = src[0].astype(BF16)
    is_sample = pl.program_id(0) == n_prompt_tiles

    @pl.when(jnp.logical_not(is_sample))
    def _():
        o = jnp.concatenate([osp_ref[...], _xattn_tile(qxp_ref, mk_ref, mv_ref)], axis=1)
        yp_ref[...] = _mlp_tile(xp_ref[...], o, wo_ref[...], gm_ref[0], wu_ref, wd_ref)

    @pl.when(is_sample)
    def _():
        o = jnp.concatenate([oss_ref[...].astype(BF16), oxs_ref[...].astype(BF16)], axis=1)
        ys_ref[...] = _mlp_tile(xs_ref[...], o, wo_ref[...], gm_ref[0], wu_ref, wd_ref)


def _out_mlp(xp, os_p, qx_p, memory, xs, os_s, ox_s, layer, wo, gm, wu, wd, *, tiles_per_batch, convert=()):
    n = xp.shape[0]
    tm = TOKEN_TILE
    assert xs.shape[0] == tm
    npt = n // tm
    n_mem = memory[0].shape[3]
    last = npt - 1
    prow = lambda i: (jnp.minimum(i, last), 0)
    mem = lambda i: (layer, jnp.minimum(i, last) // tiles_per_batch, 0, 0)
    cv_in, cv_args, cv_out, cv_shape = _convert_specs(convert, npt, last)
    return pl.pallas_call(
        functools.partial(_out_mlp_kernel, n_prompt_tiles=npt, n_convert=len(convert)),
        grid=(npt + 1,),
        in_specs=[
            pl.BlockSpec((tm, D_MODEL), prow),
            pl.BlockSpec((tm, CONV_CH), prow),
            pl.BlockSpec((tm, X_DIM), prow),
            pl.BlockSpec((1, 1, X_DIM, n_mem), mem),
            pl.BlockSpec((1, 1, X_DIM, n_mem), mem),
            _const_spec((tm, D_MODEL)),
            _const_spec((tm, CONV_CH)),
            _const_spec((tm, X_DIM)),
            _const_spec((CONV_CH + X_DIM, D_MODEL)),
            _layer_spec((1, D_MODEL), layer),
            _const_spec((D_MODEL, D_FF)),
            _const_spec((D_FF, D_MODEL)),
        ] + cv_in,
        out_specs=[pl.BlockSpec((tm, D_MODEL), prow), pl.BlockSpec((tm, D_MODEL), lambda i: (0, 0))] + cv_out,
        out_shape=[jax.ShapeDtypeStruct((n, D_MODEL), F32), jax.ShapeDtypeStruct((tm, D_MODEL), F32)] + cv_shape,
        name="out_mlp",
    )(xp, os_p, qx_p, *memory, xs, os_s, ox_s, wo, gm, wu, wd, *cv_args)


def _rope_tables(pos):
    half = HEAD_DIM // 2
    inv = ROPE_THETA ** (-jnp.arange(half, dtype=F32) * 2.0 / HEAD_DIM)
    ang = pos[:, None] * inv[None, :]
    return jnp.cos(ang), jnp.sin(ang)


def _rope_lane_tables(cos, sin):
    reps = LANES // (HEAD_DIM // 2)
    return jnp.tile(cos, (1, reps)), jnp.concatenate([-sin, sin] * (reps // 2), axis=1)


def _group_major(w, axis):
    shape = w.shape
    w = w.reshape(shape[:axis] + (N_KV_HEADS, GQA_GROUP, HEAD_DIM) + shape[axis + 1:])
    return jnp.swapaxes(w, axis, axis + 1).reshape(shape)


def _feature_major(cache):
    lead = cache.shape[:-3]
    pos, heads, hd = cache.shape[-3:]
    nd = len(lead)
    perm = tuple(range(nd)) + (nd + 1, nd + 2, nd)
    return jnp.transpose(cache, perm).reshape(lead + (heads * hd, pos))


def _position_major(cache_t, heads):
    lead = cache_t.shape[:-2]
    pos = cache_t.shape[-1]
    nd = len(lead)
    perm = tuple(range(nd)) + (nd + 2, nd, nd + 1)
    return jnp.transpose(cache_t.reshape(lead + (heads, HEAD_DIM, pos)), perm)


def _gain_cols(g):
    return jnp.broadcast_to(g[:, :, None], g.shape + (LANES,))


def kernel(x_prompt, x_sample, mem_prompt, cache_swa_k, cache_swa_v, state_conv, cache_mem_k, cache_mem_v,
           norm_mix, w_in_att, q_norm_att, k_norm_att, sinks, w_in_conv, conv_w, norm_mem, w_mem_kv,
           q_norm_x, k_norm_x, w_out, norm_mlp, w_up, w_down):
    batch, seq, _ = x_prompt.shape
    nseq, dec, _ = x_sample.shape
    n_mem = mem_prompt.shape[1]
    assert seq % MIX_TILE == 0 and seq % ATT_TILE == 0
    assert (nseq * dec) == TOKEN_TILE and nseq % SEQ_BLOCK == 0
    assert LANES % (SEQ_BLOCK * dec) == 0

    kv0, kv1 = CONV_CH, CONV_CH + 2 * KV_DIM
    w_q = jnp.concatenate([_group_major(w_in_att[:, :, :kv0], 2), w_in_att[:, :, kv1:]], axis=2).astype(BF16)
    w_kv_t = jnp.swapaxes(w_in_att[:, :, kv0:kv1], 1, 2).astype(BF16)
    w_o = [jnp.concatenate([_group_major(w_out[i, :CONV_CH], 0), w_out[i, CONV_CH:]], axis=0).astype(BF16)
           if i % 2 == 0 else w_out[i].astype(BF16) for i in range(DEPTH)]
    w_u = w_d = w_conv = None
    w_mkv_t = jnp.swapaxes(w_mem_kv, 1, 2).astype(BF16)

    g_mix = norm_mix[:, None, :]
    g_mlp = norm_mlp[:, None, :]
    gq = jnp.tile(q_norm_att, (1, N_Q_HEADS))[:, None, :]
    gx = jnp.tile(q_norm_x, (1, N_X_HEADS))[:, None, :]
    gk_cols = _gain_cols(k_norm_att)
    gkx_cols = _gain_cols(k_norm_x)

    cos_p, sin_p = _rope_tables(jnp.arange(seq, dtype=F32))
    cos_s, sin_s = _rope_tables(PAST_LEN + (jnp.arange(nseq * dec) % dec).astype(F32))
    rope_p = _rope_lane_tables(cos_p, sin_p) + (cos_p.T, sin_p.T)
    rope_s = _rope_lane_tables(cos_s, sin_s) + (cos_s.T, sin_s.T)

    mk_p, mv_p = _mem_kv(mem_prompt.reshape(batch * n_mem, D_MODEL), norm_mem[:, None, :], w_mkv_t, gkx_cols,
                         batch=batch, n_mem=n_mem)
    mk_s = _feature_major(cache_mem_k)
    mv_s = _feature_major(cache_mem_v)
    kbuf = _feature_major(cache_swa_k)
    vbuf = _feature_major(cache_swa_v)

    xp = x_prompt.reshape(batch * seq, D_MODEL)
    xs = x_sample.reshape(nseq * dec, D_MODEL)
    k_p, v_p, c_p, c_s = [], [], [], []
    cache_s = None
    for i in range(DEPTH):
        j = i // 2
        if i % 2 == 0:
            first = [(w_up, 0), (w_down, 0)] if i == 0 else []
            q, qx_p, kt, vt, kt32, vt32, *w_first = _in_att(
                xp, i, j, g_mix, w_q, w_kv_t, gq, gx, gk_cols, *rope_p,
                tiles_per_seq=seq // ATT_TILE, keep_cols=WINDOW, act_dtype=BF16, convert=first)
            if w_first:
                w_u, w_d = w_first
            k_p.append(kt32)
            v_p.append(vt32)
            os_p = _swa_prompt(sinks[j], q, kt, vt, batch=batch, seq=seq)
            q, qx_s, kt, vt = _in_att(xs, i, j, g_mix, w_q, w_kv_t, gq, gx, gk_cols, *rope_s,
                                      tiles_per_seq=1, keep_cols=0, act_dtype=F32)
            os_s, *cache_s = _swa_sample(sinks[j], q, kt, vt, kbuf, vbuf, j, cache_s, dec=dec)
        else:
            os_p, qx_p, tail = _in_conv(xp, i, j, g_mix, w_conv, gx, conv_w, None,
                                        tiles_per_seq=seq // MIX_TILE, period=0,
                                        keep_rows=SUBLANES, act_dtype=BF16)
            c_p.append(tail.reshape(batch, SUBLANES, CONV_CH)[:, SUBLANES - (CONV_W - 1):])
            prefix = jnp.pad(state_conv[j], ((0, 0), (0, dec - (CONV_W - 1)), (0, 0)))
            os_s, qx_s, tail = _in_conv(xs, i, j, g_mix, w_conv, gx, conv_w,
                                        prefix.reshape(nseq * dec, CONV_CH),
                                        tiles_per_seq=1, period=dec,
                                        keep_rows=nseq * dec, act_dtype=F32)
            c_s.append(tail.reshape(nseq, dec, CONV_CH)[:, dec - (CONV_W - 1):])
        ox_s = _xattn_sample(qx_s, mk_s, mv_s, i, dec=dec)
        convert = []
        if i + 1 < DEPTH:
            convert = [(w_up, i + 1), (w_down, i + 1)] + ([(w_in_conv, (i + 1) // 2)] if i % 2 == 0 else [])
        xp, xs, *w_next = _out_mlp(xp, os_p, qx_p, (mk_p, mv_p), xs, os_s, ox_s, i, w_o[i], g_mlp, w_u, w_d,
                                   tiles_per_batch=seq // TOKEN_TILE, convert=convert)
        if w_next:
            w_u, w_d = w_next[:2]
            w_conv = w_next[2] if len(w_next) > 2 else None

    return (xp.reshape(batch, seq, D_MODEL),
            xs.reshape(nseq, dec, D_MODEL),
            _position_major(jnp.stack(k_p), N_KV_HEADS),
            _position_major(jnp.stack(v_p), N_KV_HEADS),
            jnp.stack(c_p),
            _position_major(mk_p, N_X_HEADS),
            _position_major(mv_p, N_X_HEADS),
            _position_major(cache_s[0], N_KV_HEADS),
            _position_major(cache_s[1], N_KV_HEADS),
            jnp.stack(c_s))
```

```python
import functools

import jax
import jax.numpy as jnp
from jax import lax
from jax.experimental import pallas as pl
from jax.experimental.pallas import tpu as pltpu

F32 = jnp.float32
BF16 = jnp.bfloat16

D_MODEL = 1024
DEPTH = 4
HEAD_DIM = 64
N_Q_HEADS = 12
N_KV_HEADS = 4
GQA_GROUP = N_Q_HEADS // N_KV_HEADS
WINDOW = 128
PAST_LEN = 8192
ROPE_THETA = 10000.0
CONV_CH = N_Q_HEADS * HEAD_DIM
CONV_W = 3
N_X_HEADS = 4
X_DIM = N_X_HEADS * HEAD_DIM
KV_DIM = N_KV_HEADS * HEAD_DIM
D_FF = 4 * D_MODEL
CONV_IN = 3 * CONV_CH + X_DIM
EPS = 1e-6
NEG = -1e30
LOG2E = 1.4426950408889634
Q_SCALE = HEAD_DIM ** -0.5 * LOG2E

LANES = 128
SUBLANES = 8
MXU_DIM = 256
TOKEN_TILE = 512
ATT_TILE = 1024
MIX_TILE = 2048
FF_CHUNK = 2048
SEQ_BLOCK = 16
SEQ_GROUP = 16
PAIR = 2 * HEAD_DIM

_NT = (((1,), (1,)), ((), ()))


def _const_spec(shape):
    nd = len(shape)
    return pl.BlockSpec(shape, lambda *_: (0,) * nd, pipeline_mode=pl.Buffered(1))


def _layer_spec(shape, layer):
    nd = len(shape)
    return pl.BlockSpec((1,) + shape, lambda *_: (layer,) + (0,) * nd, pipeline_mode=pl.Buffered(1))


def _rms_rows(x, g):
    return x * lax.rsqrt(jnp.mean(x * x, axis=-1, keepdims=True) + EPS) * g


def _head_blockdiag():
    r = lax.broadcasted_iota(jnp.int32, (MXU_DIM, MXU_DIM), 0) // HEAD_DIM
    c = lax.broadcasted_iota(jnp.int32, (MXU_DIM, MXU_DIM), 1) // HEAD_DIM
    return jnp.where(r == c, 1.0, 0.0).astype(BF16)


def _head_mean_sq(z, bd):
    sq = z * z
    hi = sq.astype(BF16)
    lo = (sq - hi.astype(F32)).astype(BF16)
    parts = []
    for c in range(z.shape[1] // MXU_DIM):
        sl = slice(c * MXU_DIM, (c + 1) * MXU_DIM)
        parts.append(jnp.dot(hi[:, sl], bd, preferred_element_type=F32)
                     + jnp.dot(lo[:, sl], bd, preferred_element_type=F32))
    ms = parts[0] if len(parts) == 1 else jnp.concatenate(parts, axis=1)
    return ms * (1.0 / HEAD_DIM)


def _rope_rows(x, cos, sin_signed):
    t = x.shape[0]
    first_half = (lax.broadcasted_iota(jnp.int32, (t, LANES), 1) & (HEAD_DIM - 1)) < HEAD_DIM // 2
    parts = []
    for c in range(x.shape[1] // LANES):
        xc = x[:, c * LANES:(c + 1) * LANES]
        partner = jnp.where(first_half,
                            pltpu.roll(xc, LANES - HEAD_DIM // 2, axis=1),
                            pltpu.roll(xc, HEAD_DIM // 2, axis=1))
        parts.append(xc * cos + partner * sin_signed)
    return jnp.concatenate(parts, axis=1)


def _lane_tile(col, n):
    return col if n == LANES else jnp.concatenate([col] * (n // LANES), axis=1)


def _head_norm_cols(xt, g_col):
    t = xt.shape[1]
    g = _lane_tile(g_col, t)
    parts = []
    for h in range(xt.shape[0] // HEAD_DIM):
        blk = xt[h * HEAD_DIM:(h + 1) * HEAD_DIM, :]
        ms = jnp.mean(blk * blk, axis=0, keepdims=True)
        parts.append(blk * lax.rsqrt(ms + EPS) * g)
    return parts


def _softmax_rows(s, sink=None):
    m = jnp.max(s, axis=-1, keepdims=True)
    if sink is not None:
        m = jnp.maximum(m, sink)
    p = jnp.exp2(s - m)
    den = jnp.sum(p, axis=-1, keepdims=True)
    if sink is not None:
        den = den + jnp.exp2(sink - m)
    return p.astype(BF16), den


def _pair_normalise(low, res_lo, res_hi, den_lo, den_hi):
    return jnp.where(low, res_lo, res_hi) / jnp.where(low, den_lo, den_hi)


def _convert_specs(convert, steps, last=None):
    last = steps - 1 if last is None else last
    in_specs, args, out_specs, out_shape = [], [], [], []
    for stack, l in convert:
        _, r, c = stack.shape
        in_specs.append(pl.BlockSpec((1, r // steps, c), lambda i, l=l: (l, jnp.minimum(i, last), 0)))
        args.append(stack)
        out_specs.append(pl.BlockSpec((r // steps, c), lambda i: (jnp.minimum(i, last), 0)))
        out_shape.append(jax.ShapeDtypeStruct((r, c), BF16))
    return in_specs, args, out_specs, out_shape


def _mem_kv_kernel(mem_ref, g_ref, wt_ref, gk_ref, mk_ref, mv_ref):
    x = mem_ref[...]
    xn = x * lax.rsqrt(jnp.mean(x * x, axis=-1, keepdims=True) + EPS)
    for i in range(DEPTH):
        h = (xn * g_ref[i]).astype(BF16)
        kvt = lax.dot_general(wt_ref[i], h, _NT, preferred_element_type=F32)
        mk_ref[i, 0] = jnp.concatenate(_head_norm_cols(kvt[:X_DIM], gk_ref[i]), axis=0)
        mv_ref[i, 0] = kvt[X_DIM:]


def _mem_kv(mem2d, norm_mem, w_mem_kv_t, gk_cols, *, batch, n_mem):
    out = jax.ShapeDtypeStruct((DEPTH, batch, X_DIM, n_mem), F32)
    return pl.pallas_call(
        _mem_kv_kernel,
        grid=(batch,),
        in_specs=[
            pl.BlockSpec((n_mem, D_MODEL), lambda b: (b, 0)),
            _const_spec((DEPTH, 1, D_MODEL)),
            _const_spec((DEPTH, 2 * X_DIM, D_MODEL)),
            _const_spec((DEPTH, HEAD_DIM, LANES)),
        ],
        out_specs=[pl.BlockSpec((DEPTH, 1, X_DIM, n_mem), lambda b: (0, b, 0, 0))] * 2,
        out_shape=[out, out],
        name="mem_kv",
    )(mem2d, norm_mem, w_mem_kv_t, gk_cols)


def _in_att_kernel(*refs, keep_cols, n_convert):
    (x_ref, g_ref, wq_ref, wkv_ref, gq_ref, gx_ref, gk_ref, cos_ref, sin_ref, cost_ref, sint_ref) = refs[:11]
    n_out = 6 if keep_cols else 4
    outs = refs[11 + n_convert:11 + n_convert + n_out]
    q_ref, qx_ref, kt_ref, vt_ref = outs[:4]
    for src, dst in zip(refs[11:11 + n_convert], refs[11 + n_convert + n_out:]):
        dst[...] = src[0].astype(BF16)
    t = x_ref.shape[0]
    h = _rms_rows(x_ref[...], g_ref[0]).astype(BF16)
    bd = _head_blockdiag()
    z = jnp.dot(h, wq_ref[0], preferred_element_type=F32)
    q = z[:, :CONV_CH]
    q = q * lax.rsqrt(_head_mean_sq(q, bd) + EPS) * gq_ref[0]
    q_ref[...] = (_rope_rows(q, cos_ref[...], sin_ref[...]) * Q_SCALE).astype(q_ref.dtype)
    qx = z[:, CONV_CH:]
    qx = qx * lax.rsqrt(_head_mean_sq(qx, bd) + EPS) * gx_ref[0]
    qx_ref[...] = (qx * Q_SCALE).astype(qx_ref.dtype)

    kvt = lax.dot_general(wkv_ref[0], h, _NT, preferred_element_type=F32)
    cos_t, sin_t = cost_ref[...], sint_ref[...]
    half = HEAD_DIM // 2
    k_parts = []
    for blk in _head_norm_cols(kvt[:KV_DIM], gk_ref[0]):
        x1, x2 = blk[:half], blk[half:]
        k_parts += [x1 * cos_t - x2 * sin_t, x2 * cos_t + x1 * sin_t]
    kt = jnp.concatenate(k_parts, axis=0)
    vt = kvt[KV_DIM:]
    kt_ref[...] = kt.astype(kt_ref.dtype)
    vt_ref[...] = vt.astype(vt_ref.dtype)
    if keep_cols:
        kt32_ref, vt32_ref = outs[4:]
        kt32_ref[0] = kt[:, t - keep_cols:]
        vt32_ref[0] = vt[:, t - keep_cols:]


def _in_att(x2d, layer, j, g_mix, wq, wkv_t, gq, gx, gk_cols, cos, sin, cos_t, sin_t,
            *, tiles_per_seq, keep_cols, act_dtype, convert=()):
    n = x2d.shape[0]
    tm = min(ATT_TILE, n)
    nt = n // tm
    nseq = nt // tiles_per_seq
    row = lambda i: (i, 0)
    col = lambda i: (0, i)
    pos = lambda i: (i % tiles_per_seq, 0)
    pos_t = lambda i: (0, i % tiles_per_seq)
    out_specs = [
        pl.BlockSpec((tm, CONV_CH), row),
        pl.BlockSpec((tm, X_DIM), row),
        pl.BlockSpec((KV_DIM, tm), col),
        pl.BlockSpec((KV_DIM, tm), col),
    ]
    out_shape = [
        jax.ShapeDtypeStruct((n, CONV_CH), act_dtype),
        jax.ShapeDtypeStruct((n, X_DIM), act_dtype),
        jax.ShapeDtypeStruct((KV_DIM, n), act_dtype),
        jax.ShapeDtypeStruct((KV_DIM, n), act_dtype),
    ]
    if keep_cols:
        out_specs += [pl.BlockSpec((1, KV_DIM, keep_cols), lambda i: (i // tiles_per_seq, 0, 0))] * 2
        out_shape += [jax.ShapeDtypeStruct((nseq, KV_DIM, keep_cols), F32)] * 2
    cv_in, cv_args, cv_out, cv_shape = _convert_specs(convert, nt)
    return pl.pallas_call(
        functools.partial(_in_att_kernel, keep_cols=keep_cols, n_convert=len(convert)),
        grid=(nt,),
        in_specs=[
            pl.BlockSpec((tm, D_MODEL), row),
            _layer_spec((1, D_MODEL), layer),
            _layer_spec((D_MODEL, CONV_CH + X_DIM), j),
            _layer_spec((2 * KV_DIM, D_MODEL), j),
            _layer_spec((1, CONV_CH), j),
            _layer_spec((1, X_DIM), layer),
            _layer_spec((HEAD_DIM, LANES), j),
            pl.BlockSpec((tm, LANES), pos),
            pl.BlockSpec((tm, LANES), pos),
            pl.BlockSpec((HEAD_DIM // 2, tm), pos_t),
            pl.BlockSpec((HEAD_DIM // 2, tm), pos_t),
        ] + cv_in,
        out_specs=out_specs + cv_out,
        out_shape=out_shape + cv_shape,
        name="in_att",
    )(x2d, g_mix, wq, wkv_t, gq, gx, gk_cols, cos, sin, cos_t, sin_t, *cv_args)


def _in_conv_kernel(*refs, tiles_per_seq, period):
    if period:
        (x_ref, g_ref, w_ref, gx_ref, cw_ref, prefix_ref,
         o_ref, qx_ref, gt_ref, gbuf, pbuf) = refs
    else:
        (x_ref, g_ref, w_ref, gx_ref, cw_ref,
         o_ref, qx_ref, gt_ref, gbuf) = refs
    t = x_ref.shape[0]
    keep = gt_ref.shape[0]
    h = _rms_rows(x_ref[...], g_ref[0]).astype(BF16)

    def proj(lo, width):
        return jnp.dot(h, w_ref[:, lo:lo + width], preferred_element_type=F32)

    @pl.when(pl.program_id(0) % tiles_per_seq == 0)
    def _():
        gbuf[0:SUBLANES, :] = jnp.zeros((SUBLANES, CONV_CH), F32)

    if period:
        pbuf[0:t, :] = prefix_ref[...]
        pbuf[t:t + SUBLANES, :] = jnp.zeros((SUBLANES, CONV_CH), F32)
        tok = lax.broadcasted_iota(jnp.int32, (t, MXU_DIM), 0) % period
    cw = cw_ref[0]
    for c in range(CONV_CH // MXU_DIM):
        sl = slice(c * MXU_DIM, (c + 1) * MXU_DIM)
        gate_b = proj(sl.start, MXU_DIM)
        u = proj(CONV_CH + sl.start, MXU_DIM) * proj(2 * CONV_CH + sl.start, MXU_DIM)
        gbuf[SUBLANES:SUBLANES + t, sl] = u
        back1 = gbuf[SUBLANES - 1:SUBLANES - 1 + t, sl]
        back2 = gbuf[SUBLANES - 2:SUBLANES - 2 + t, sl]
        if period:
            back1 = jnp.where(tok >= 1, back1, pbuf[1:1 + t, sl])
            back2 = jnp.where(tok >= 2, back2, pbuf[0:t, sl])
        y = back2 * cw[0:1, sl] + back1 * cw[1:2, sl] + u * cw[2:3, sl]
        o_ref[:, sl] = (gate_b * y).astype(o_ref.dtype)
        gt_ref[:, sl] = u[t - keep:, :]
    gbuf[0:SUBLANES, :] = gbuf[t:t + SUBLANES, :]
    qx = proj(3 * CONV_CH, X_DIM)
    qx = qx * lax.rsqrt(_head_mean_sq(qx, _head_blockdiag()) + EPS) * gx_ref[0]
    qx_ref[...] = (qx * Q_SCALE).astype(qx_ref.dtype)


def _in_conv(x2d, layer, j, g_mix, w, gx, cw, prefix, *, tiles_per_seq, period, keep_rows, act_dtype):
    n = x2d.shape[0]
    tm = min(MIX_TILE, n)
    nt = n // tm
    nseq = nt // tiles_per_seq
    row = lambda i: (i, 0)
    in_specs = [
        pl.BlockSpec((tm, D_MODEL), row),
        _layer_spec((1, D_MODEL), layer),
        _const_spec((D_MODEL, CONV_IN)),
        _layer_spec((1, X_DIM), layer),
        _layer_spec((CONV_W, CONV_CH), j),
    ]
    args = [x2d, g_mix, w, gx, cw]
    scratch = [pltpu.VMEM((tm + 2 * SUBLANES, CONV_CH), F32)]
    if period:
        in_specs.append(pl.BlockSpec((tm, CONV_CH), row))
        args.append(prefix)
        scratch.append(pltpu.VMEM((tm + SUBLANES, CONV_CH), F32))
    return pl.pallas_call(
        functools.partial(_in_conv_kernel, tiles_per_seq=tiles_per_seq, period=period),
        grid=(nt,),
        in_specs=in_specs,
        out_specs=[
            pl.BlockSpec((tm, CONV_CH), row),
            pl.BlockSpec((tm, X_DIM), row),
            pl.BlockSpec((keep_rows, CONV_CH), lambda i: (i // tiles_per_seq, 0)),
        ],
        out_shape=[
            jax.ShapeDtypeStruct((n, CONV_CH), act_dtype),
            jax.ShapeDtypeStruct((n, X_DIM), act_dtype),
            jax.ShapeDtypeStruct((nseq * keep_rows, CONV_CH), F32),
        ],
        scratch_shapes=scratch,
        name="in_conv",
    )(*args)


def _pair_lhs(slabs):
    low = lax.broadcasted_iota(jnp.int32, slabs[0].shape, 1) < HEAD_DIM
    zero = jnp.zeros_like(slabs[0])
    parts = []
    for s in slabs:
        parts += [jnp.where(low, s, zero), jnp.where(low, zero, s)]
    return jnp.concatenate(parts, axis=0)


def _swa_prompt_kernel(sinks_ref, q_ref, kc_ref, kp_ref, vc_ref, vp_ref, o_ref, kfull, vfull, bias):
    tq = q_ref.shape[0]

    @pl.when(jnp.logical_and(pl.program_id(0) == 0, pl.program_id(1) == 0))
    def _():
        r = lax.broadcasted_iota(jnp.int32, (WINDOW, 2 * WINDOW), 0)
        c = lax.broadcasted_iota(jnp.int32, (WINDOW, 2 * WINDOW), 1)
        band = (c > r) & (c - WINDOW <= r)
        for kind, visible in enumerate((band, band & (c >= WINDOW))):
            base = jnp.where(visible, 0.0, NEG)
            for h in range(N_Q_HEADS):
                bias[kind, h] = jnp.where(c == 0, sinks_ref[h] * LOG2E, base)

    first_kind = jnp.where(pl.program_id(1) == 0, 1, 0)
    kfull[:, 0:WINDOW] = kp_ref[...]
    kfull[:, WINDOW:] = kc_ref[...]
    vfull[:, 0:WINDOW] = vp_ref[...]
    vfull[:, WINDOW:] = vc_ref[...]
    col0 = lax.broadcasted_iota(jnp.int32, (PAIR, 2 * WINDOW), 1) == 0
    zero_slab = jnp.zeros((PAIR, 2 * WINDOW), BF16)
    low = lax.broadcasted_iota(jnp.int32, (WINDOW, PAIR), 1) < HEAD_DIM
    for qb in range(tq // WINDOW):
        r0 = qb * WINDOW
        kind = first_kind if qb == 0 else 0
        slabs = [None] * (GQA_GROUP * KV_DIM // PAIR)
        for pr in range(KV_DIM // PAIR):
            kslab = jnp.where(col0, zero_slab, kfull[pr * PAIR:(pr + 1) * PAIR, r0:r0 + 2 * WINDOW])
            vslab = jnp.where(col0, zero_slab, vfull[pr * PAIR:(pr + 1) * PAIR, r0:r0 + 2 * WINDOW])
            lhs = _pair_lhs([q_ref[r0:r0 + WINDOW, g * KV_DIM + pr * PAIR:g * KV_DIM + (pr + 1) * PAIR]
                             for g in range(GQA_GROUP)])
            s_all = jnp.dot(lhs, kslab, preferred_element_type=F32)
            probs, dens = [], []
            for g in range(GQA_GROUP):
                for e in range(2):
                    idx = g * 2 + e
                    head = (2 * pr + e) * GQA_GROUP + g
                    p, den = _softmax_rows(s_all[idx * WINDOW:(idx + 1) * WINDOW] + bias[kind, head])
                    probs.append(p)
                    dens.append(den)
            res = lax.dot_general(jnp.concatenate(probs, axis=0), vslab, _NT,
                                  preferred_element_type=F32)
            for g in range(GQA_GROUP):
                slabs[g * (KV_DIM // PAIR) + pr] = _pair_normalise(
                    low, res[(2 * g) * WINDOW:(2 * g + 1) * WINDOW], res[(2 * g + 1) * WINDOW:(2 * g + 2) * WINDOW],
                    dens[2 * g], dens[2 * g + 1])
        o_ref[r0:r0 + WINDOW, :] = jnp.concatenate(slabs, axis=1).astype(o_ref.dtype)


def _swa_prompt(sinks, q, kt, vt, *, batch, seq):
    tq = MIX_TILE
    nt = seq // tq
    per = tq // WINDOW
    rows = lambda b, j: (b * nt + j, 0)
    cur = lambda b, j: (0, b * nt + j)
    prev = lambda b, j: (0, b * nt * per + jnp.maximum(j * per - 1, 0))
    return pl.pallas_call(
        _swa_prompt_kernel,
        grid=(batch, nt),
        in_specs=[
            pl.BlockSpec(memory_space=pltpu.SMEM),
            pl.BlockSpec((tq, CONV_CH), rows),
            pl.BlockSpec((KV_DIM, tq), cur),
            pl.BlockSpec((KV_DIM, WINDOW), prev),
            pl.BlockSpec((KV_DIM, tq), cur),
            pl.BlockSpec((KV_DIM, WINDOW), prev),
        ],
        out_specs=pl.BlockSpec((tq, CONV_CH), rows),
        out_shape=jax.ShapeDtypeStruct(q.shape, BF16),
        scratch_shapes=[pltpu.VMEM((KV_DIM, tq + WINDOW), BF16)] * 2
        + [pltpu.VMEM((2, N_Q_HEADS, WINDOW, 2 * WINDOW), F32)],
        name="swa_prompt",
    )(sinks, q, kt, kt, vt, vt)


def _swa_sample_kernel(*refs, dec, carried, slot):
    if carried:
        sinks_ref, q_ref, kn_ref, vn_ref, kb_ref, vb_ref, _, _, o_ref, ko_ref, vo_ref = refs
    else:
        sinks_ref, q_ref, kn_ref, vn_ref, kb_ref, vb_ref, o_ref, ko_ref, vo_ref = refs
    nseq = kb_ref.shape[1]
    for other in range(ko_ref.shape[0]):
        if other != slot:
            ko_ref[other] = jnp.zeros(ko_ref.shape[1:], F32)
            vo_ref[other] = jnp.zeros(vo_ref.shape[1:], F32)
    steps_per_tile = LANES // (nseq * dec)
    base = (pl.program_id(0) % steps_per_tile) * (nseq * dec)
    grp = N_KV_HEADS * dec
    rows = GQA_GROUP * grp
    ri = lax.broadcasted_iota(jnp.int32, (rows, KV_DIM), 0)
    li = lax.broadcasted_iota(jnp.int32, (rows, KV_DIM), 1)
    head_mask = ((ri % grp) // dec) == (li // HEAD_DIM)
    tok = lax.broadcasted_iota(jnp.int32, (rows, WINDOW + LANES), 0) % dec
    col = lax.broadcasted_iota(jnp.int32, (rows, WINDOW + LANES), 1)
    cache_mask = (col < WINDOW) & (col > tok)
    lane = lax.broadcasted_iota(jnp.int32, (KV_DIM, WINDOW), 1)
    sink = jnp.concatenate(
        [jnp.full((dec, 1), sinks_ref[kv * GQA_GROUP + g] * LOG2E, F32)
         for g in range(GQA_GROUP) for kv in range(N_KV_HEADS)], axis=0)
    kn = kn_ref[...]
    vn = vn_ref[...]
    kn16 = kn.astype(BF16)
    vn16 = vn.astype(BF16)
    for s0 in range(0, nseq, SEQ_GROUP):
        seqs = range(s0, s0 + SEQ_GROUP)
        scores, values = [], []
        for s in seqs:
            off = base + s * dec
            new = col - (WINDOW + off)
            mask = cache_mask | ((new >= 0) & (new <= tok))
            qs = q_ref[pl.ds(s * dec, dec), :]
            qbd = jnp.concatenate(
                [jnp.concatenate([qs[:, g * KV_DIM:(g + 1) * KV_DIM]] * N_KV_HEADS, axis=0)
                 for g in range(GQA_GROUP)], axis=0)
            qbd = jnp.where(head_mask, qbd, 0.0).astype(BF16)
            keys = jnp.concatenate([kb_ref[0, s].astype(BF16), kn16], axis=1)
            values.append(jnp.concatenate([vb_ref[0, s].astype(BF16), vn16], axis=1))
            scores.append(jnp.where(mask, jnp.dot(qbd, keys, preferred_element_type=F32), NEG))
        probs = [_softmax_rows(sc, sink) for sc in scores]
        for s, (p, den), vals in zip(seqs, probs, values):
            o = lax.dot_general(p, vals, _NT, preferred_element_type=F32) * (1.0 / den)
            o = jnp.where(head_mask, o, 0.0)
            folded = []
            for g in range(GQA_GROUP):
                og = o[g * grp:g * grp + dec, :]
                for kv in range(1, N_KV_HEADS):
                    og = og + o[g * grp + kv * dec:g * grp + (kv + 1) * dec, :]
                folded.append(og)
            o_ref[pl.ds(s * dec, dec), :] = jnp.concatenate(folded, axis=1)
        for s in seqs:
            shift_new = (WINDOW - dec + LANES - (base + s * dec)) % LANES
            ko_ref[slot, s] = jnp.where(lane < WINDOW - dec, pltpu.roll(kb_ref[0, s], WINDOW - dec, axis=1),
                                     pltpu.roll(kn, shift_new, axis=1))
            vo_ref[slot, s] = jnp.where(lane < WINDOW - dec, pltpu.roll(vb_ref[0, s], WINDOW - dec, axis=1),
                                     pltpu.roll(vn, shift_new, axis=1))


def _swa_sample(sinks, q, kt_new, vt_new, kbuf, vbuf, j, carry, *, dec):
    nseq = kbuf.shape[1]
    sb = SEQ_BLOCK
    steps_per_tile = LANES // (sb * dec)
    row = lambda i: (i, 0)
    tile = lambda i: (0, i // steps_per_tile)
    cache = lambda i: (j, i, 0, 0)
    if carry is None:
        new_cache = pl.BlockSpec((kbuf.shape[0], sb, KV_DIM, WINDOW), lambda i: (0, i, 0, 0))
    else:
        new_cache = pl.BlockSpec((1, sb, KV_DIM, WINDOW), cache)
    in_specs = [
        pl.BlockSpec(memory_space=pltpu.SMEM),
        pl.BlockSpec((sb * dec, CONV_CH), row),
        pl.BlockSpec((KV_DIM, LANES), tile),
        pl.BlockSpec((KV_DIM, LANES), tile),
        pl.BlockSpec((1, sb, KV_DIM, WINDOW), cache),
        pl.BlockSpec((1, sb, KV_DIM, WINDOW), cache),
    ]
    args = [sinks, q, kt_new, vt_new, kbuf, vbuf]
    aliases = {}
    if carry is not None:
        aliases = {len(args): 1, len(args) + 1: 2}
        in_specs += [pl.BlockSpec(memory_space=pl.ANY)] * 2
        args += list(carry)
    return pl.pallas_call(
        functools.partial(_swa_sample_kernel, dec=dec, carried=carry is not None,
                          slot=j if carry is None else 0),
        grid=(nseq // sb,),
        in_specs=in_specs,
        out_specs=[
            pl.BlockSpec((sb * dec, CONV_CH), row),
            new_cache,
            new_cache,
        ],
        out_shape=[
            jax.ShapeDtypeStruct(q.shape, F32),
            jax.ShapeDtypeStruct(kbuf.shape, F32),
            jax.ShapeDtypeStruct(vbuf.shape, F32),
        ],
        input_output_aliases=aliases,
        name="swa_sample",
    )(*args)


def _xattn_tile(q_ref, mk_ref, mv_ref):
    tq = q_ref.shape[0]
    mk = mk_ref[0, 0].astype(BF16)
    mv = mv_ref[0, 0].astype(BF16)
    low = lax.broadcasted_iota(jnp.int32, (WINDOW, PAIR), 1) < HEAD_DIM
    blocks = []
    for rb in range(tq // WINDOW):
        r0 = rb * WINDOW
        slabs = []
        for pr in range(X_DIM // PAIR):
            lhs = _pair_lhs([q_ref[r0:r0 + WINDOW, pr * PAIR:(pr + 1) * PAIR]])
            s_all = jnp.dot(lhs, mk[pr * PAIR:(pr + 1) * PAIR], preferred_element_type=F32)
            p0, den0 = _softmax_rows(s_all[:WINDOW])
            p1, den1 = _softmax_rows(s_all[WINDOW:])
            res = lax.dot_general(jnp.concatenate([p0, p1], axis=0), mv[pr * PAIR:(pr + 1) * PAIR], _NT,
                                  preferred_element_type=F32)
            slabs.append(_pair_normalise(low, res[:WINDOW], res[WINDOW:], den0, den1))
        blocks.append(jnp.concatenate(slabs, axis=1).astype(BF16))
    return jnp.concatenate(blocks, axis=0)


def _xattn_sample_kernel(q_ref, mk_ref, mv_ref, o_ref, *, dec):
    nseq = mk_ref.shape[1]
    rows = N_X_HEADS * dec
    head_mask = (lax.broadcasted_iota(jnp.int32, (rows, X_DIM), 0) // dec) == (
        lax.broadcasted_iota(jnp.int32, (rows, X_DIM), 1) // HEAD_DIM)
    for s0 in range(0, nseq, SEQ_GROUP):
        seqs = range(s0, s0 + SEQ_GROUP)
        scores = []
        for s in seqs:
            qs = q_ref[pl.ds(s * dec, dec), :]
            qbd = jnp.where(head_mask, jnp.concatenate([qs] * N_X_HEADS, axis=0), 0.0).astype(BF16)
            scores.append(jnp.dot(qbd, mk_ref[0, s].astype(BF16), preferred_element_type=F32))
        probs = [_softmax_rows(sc) for sc in scores]
        for s, (p, den) in zip(seqs, probs):
            o = lax.dot_general(p, mv_ref[0, s].astype(BF16), _NT, preferred_element_type=F32) * (1.0 / den)
            o = jnp.where(head_mask, o, 0.0)
            acc = o[0:dec, :]
            for h in range(1, N_X_HEADS):
                acc = acc + o[h * dec:(h + 1) * dec, :]
            o_ref[pl.ds(s * dec, dec), :] = acc


def _xattn_sample(qx, mk, mv, layer, *, dec):
    nseq, n_mem = mk.shape[1], mk.shape[3]
    sb = SEQ_BLOCK
    mem = lambda i: (layer, i, 0, 0)
    return pl.pallas_call(
        functools.partial(_xattn_sample_kernel, dec=dec),
        grid=(nseq // sb,),
        in_specs=[
            pl.BlockSpec((sb * dec, X_DIM), lambda i: (i, 0)),
            pl.BlockSpec((1, sb, X_DIM, n_mem), mem),
            pl.BlockSpec((1, sb, X_DIM, n_mem), mem),
        ],
        out_specs=pl.BlockSpec((sb * dec, X_DIM), lambda i: (i, 0)),
        out_shape=jax.ShapeDtypeStruct(qx.shape, F32),
        name="xattn_sample",
    )(qx, mk, mv)


def _mlp_tile(x, o, wo, gm, wu_ref, wd_ref):
    x1 = x + jnp.dot(o, wo, preferred_element_type=F32)
    hm = _rms_rows(x1, gm).astype(BF16)
    acc = x1
    for c in range(D_FF // FF_CHUNK):
        a = jnp.dot(hm, wu_ref[:, c * FF_CHUNK:(c + 1) * FF_CHUNK], preferred_element_type=F32)
        a = jnp.square(jnp.maximum(a, 0.0)).astype(BF16)
        acc = acc + jnp.dot(a, wd_ref[c * FF_CHUNK:(c + 1) * FF_CHUNK, :], preferred_element_type=F32)
    return acc


def _out_mlp_kernel(*refs, n_prompt_tiles, n_convert):
    (xp_ref, osp_ref, qxp_ref, mk_ref, mv_ref, xs_ref, oss_ref, oxs_ref,
     wo_ref, gm_ref, wu_ref, wd_ref) = refs[:12]
    yp_ref, ys_ref = refs[12 + n_convert:14 + n_convert]
    for src, dst in zip(refs[12:12 + n_convert], refs[14 + n_convert:]):
        dst[...] = src[0].astype(BF16)
    is_sample = pl.program_id(0) == n_prompt_tiles

    @pl.when(jnp.logical_not(is_sample))
    def _():
        o = jnp.concatenate([osp_ref[...], _xattn_tile(qxp_ref, mk_ref, mv_ref)], axis=1)
        yp_ref[...] = _mlp_tile(xp_ref[...], o, wo_ref[...], gm_ref[0], wu_ref, wd_ref)

    @pl.when(is_sample)
    def _():
        o = jnp.concatenate([oss_ref[...].astype(BF16), oxs_ref[...].astype(BF16)], axis=1)
        ys_ref[...] = _mlp_tile(xs_ref[...], o, wo_ref[...], gm_ref[0], wu_ref, wd_ref)


def _out_mlp(xp, os_p, qx_p, memory, xs, os_s, ox_s, layer, wo, gm, wu, wd, *, tiles_per_batch, convert=()):
    n = xp.shape[0]
    tm = TOKEN_TILE
    assert xs.shape[0] == tm
    npt = n // tm
    n_mem = memory[0].shape[3]
    last = npt - 1
    prow = lambda i: (jnp.minimum(i, last), 0)
    mem = lambda i: (layer, jnp.minimum(i, last) // tiles_per_batch, 0, 0)
    cv_in, cv_args, cv_out, cv_shape = _convert_specs(convert, npt, last)
    return pl.pallas_call(
        functools.partial(_out_mlp_kernel, n_prompt_tiles=npt, n_convert=len(convert)),
        grid=(npt + 1,),
        in_specs=[
            pl.BlockSpec((tm, D_MODEL), prow),
            pl.BlockSpec((tm, CONV_CH), prow),
            pl.BlockSpec((tm, X_DIM), prow),
            pl.BlockSpec((1, 1, X_DIM, n_mem), mem),
            pl.BlockSpec((1, 1, X_DIM, n_mem), mem),
            _const_spec((tm, D_MODEL)),
            _const_spec((tm, CONV_CH)),
            _const_spec((tm, X_DIM)),
            _const_spec((CONV_CH + X_DIM, D_MODEL)),
            _layer_spec((1, D_MODEL), layer),
            _const_spec((D_MODEL, D_FF)),
            _const_spec((D_FF, D_MODEL)),
        ] + cv_in,
        out_specs=[pl.BlockSpec((tm, D_MODEL), prow), pl.BlockSpec((tm, D_MODEL), lambda i: (0, 0))] + cv_out,
        out_shape=[jax.ShapeDtypeStruct((n, D_MODEL), F32), jax.ShapeDtypeStruct((tm, D_MODEL), F32)] + cv_shape,
        name="out_mlp",
    )(xp, os_p, qx_p, *memory, xs, os_s, ox_s, wo, gm, wu, wd, *cv_args)


def _rope_tables(pos):
    half = HEAD_DIM // 2
    inv = ROPE_THETA ** (-jnp.arange(half, dtype=F32) * 2.0 / HEAD_DIM)
    ang = pos[:, None] * inv[None, :]
    return jnp.cos(ang), jnp.sin(ang)


def _rope_lane_tables(cos, sin):
    reps = LANES // (HEAD_DIM // 2)
    return jnp.tile(cos, (1, reps)), jnp.concatenate([-sin, sin] * (reps // 2), axis=1)


def _group_major(w, axis):
    shape = w.shape
    w = w.reshape(shape[:axis] + (N_KV_HEADS, GQA_GROUP, HEAD_DIM) + shape[axis + 1:])
    return jnp.swapaxes(w, axis, axis + 1).reshape(shape)


def _feature_major(cache):
    lead = cache.shape[:-3]
    pos, heads, hd = cache.shape[-3:]
    nd = len(lead)
    perm = tuple(range(nd)) + (nd + 1, nd + 2, nd)
    return jnp.transpose(cache, perm).reshape(lead + (heads * hd, pos))


def _position_major(cache_t, heads):
    lead = cache_t.shape[:-2]
    pos = cache_t.shape[-1]
    nd = len(lead)
    perm = tuple(range(nd)) + (nd + 2, nd, nd + 1)
    return jnp.transpose(cache_t.reshape(lead + (heads, HEAD_DIM, pos)), perm)


def _gain_cols(g):
    return jnp.broadcast_to(g[:, :, None], g.shape + (LANES,))


def kernel(x_prompt, x_sample, mem_prompt, cache_swa_k, cache_swa_v, state_conv, cache_mem_k, cache_mem_v,
           norm_mix, w_in_att, q_norm_att, k_norm_att, sinks, w_in_conv, conv_w, norm_mem, w_mem_kv,
           q_norm_x, k_norm_x, w_out, norm_mlp, w_up, w_down):
    batch, seq, _ = x_prompt.shape
    nseq, dec, _ = x_sample.shape
    n_mem = mem_prompt.shape[1]
    assert seq % MIX_TILE == 0 and seq % ATT_TILE == 0
    assert (nseq * dec) == TOKEN_TILE and nseq % SEQ_BLOCK == 0
    assert LANES % (SEQ_BLOCK * dec) == 0

    kv0, kv1 = CONV_CH, CONV_CH + 2 * KV_DIM
    w_q = jnp.concatenate([_group_major(w_in_att[:, :, :kv0], 2), w_in_att[:, :, kv1:]], axis=2).astype(BF16)
    w_kv_t = jnp.swapaxes(w_in_att[:, :, kv0:kv1], 1, 2).astype(BF16)
    w_o = [jnp.concatenate([_group_major(w_out[i, :CONV_CH], 0), w_out[i, CONV_CH:]], axis=0).astype(BF16)
           if i % 2 == 0 else w_out[i].astype(BF16) for i in range(DEPTH)]
    w_u = w_d = w_conv = None
    w_mkv_t = jnp.swapaxes(w_mem_kv, 1, 2).astype(BF16)

    g_mix = norm_mix[:, None, :]
    g_mlp = norm_mlp[:, None, :]
    gq = jnp.tile(q_norm_att, (1, N_Q_HEADS))[:, None, :]
    gx = jnp.tile(q_norm_x, (1, N_X_HEADS))[:, None, :]
    gk_cols = _gain_cols(k_norm_att)
    gkx_cols = _gain_cols(k_norm_x)

    cos_p, sin_p = _rope_tables(jnp.arange(seq, dtype=F32))
    cos_s, sin_s = _rope_tables(PAST_LEN + (jnp.arange(nseq * dec) % dec).astype(F32))
    rope_p = _rope_lane_tables(cos_p, sin_p) + (cos_p.T, sin_p.T)
    rope_s = _rope_lane_tables(cos_s, sin_s) + (cos_s.T, sin_s.T)

    mk_p, mv_p = _mem_kv(mem_prompt.reshape(batch * n_mem, D_MODEL), norm_mem[:, None, :], w_mkv_t, gkx_cols,
                         batch=batch, n_mem=n_mem)
    mk_s = _feature_major(cache_mem_k)
    mv_s = _feature_major(cache_mem_v)
    kbuf = _feature_major(cache_swa_k)
    vbuf = _feature_major(cache_swa_v)

    xp = x_prompt.reshape(batch * seq, D_MODEL)
    xs = x_sample.reshape(nseq * dec, D_MODEL)
    k_p, v_p, c_p, c_s = [], [], [], []
    cache_s = None
    for i in range(DEPTH):
        j = i // 2
        if i % 2 == 0:
            first = [(w_up, 0), (w_down, 0)] if i == 0 else []
            q, qx_p, kt, vt, kt32, vt32, *w_first = _in_att(
                xp, i, j, g_mix, w_q, w_kv_t, gq, gx, gk_cols, *rope_p,
                tiles_per_seq=seq // ATT_TILE, keep_cols=WINDOW, act_dtype=BF16, convert=first)
            if w_first:
                w_u, w_d = w_first
            k_p.append(kt32)
            v_p.append(vt32)
            os_p = _swa_prompt(sinks[j], q, kt, vt, batch=batch, seq=seq)
            q, qx_s, kt, vt = _in_att(xs, i, j, g_mix, w_q, w_kv_t, gq, gx, gk_cols, *rope_s,
                                      tiles_per_seq=1, keep_cols=0, act_dtype=F32)
            os_s, *cache_s = _swa_sample(sinks[j], q, kt, vt, kbuf, vbuf, j, cache_s, dec=dec)
        else:
            os_p, qx_p, tail = _in_conv(xp, i, j, g_mix, w_conv, gx, conv_w, None,
                                        tiles_per_seq=seq // MIX_TILE, period=0,
                                        keep_rows=SUBLANES, act_dtype=BF16)
            c_p.append(tail.reshape(batch, SUBLANES, CONV_CH)[:, SUBLANES - (CONV_W - 1):])
            prefix = jnp.pad(state_conv[j], ((0, 0), (0, dec - (CONV_W - 1)), (0, 0)))
            os_s, qx_s, tail = _in_conv(xs, i, j, g_mix, w_conv, gx, conv_w,
                                        prefix.reshape(nseq * dec, CONV_CH),
                                        tiles_per_seq=1, period=dec,
                                        keep_rows=nseq * dec, act_dtype=F32)
            c_s.append(tail.reshape(nseq, dec, CONV_CH)[:, dec - (CONV_W - 1):])
        ox_s = _xattn_sample(qx_s, mk_s, mv_s, i, dec=dec)
        convert = []
        if i + 1 < DEPTH:
            convert = [(w_up, i + 1), (w_down, i + 1)] + ([(w_in_conv, (i + 1) // 2)] if i % 2 == 0 else [])
        xp, xs, *w_next = _out_mlp(xp, os_p, qx_p, (mk_p, mv_p), xs, os_s, ox_s, i, w_o[i], g_mlp, w_u, w_d,
                                   tiles_per_batch=seq // TOKEN_TILE, convert=convert)
        if w_next:
            w_u, w_d = w_next[:2]
            w_conv = w_next[2] if len(w_next) > 2 else None

    return (xp.reshape(batch, seq, D_MODEL),
            xs.reshape(nseq, dec, D_MODEL),
            _position_major(jnp.stack(k_p), N_KV_HEADS),
            _position_major(jnp.stack(v_p), N_KV_HEADS),
            jnp.stack(c_p),
            _position_major(mk_p, N_X_HEADS),
            _position_major(mv_p, N_X_HEADS),
            _position_major(cache_s[0], N_KV_HEADS),
            _position_major(cache_s[1], N_KV_HEADS),
            jnp.stack(c_s))
```

```python
import functools

import jax
import jax.numpy as jnp
from jax import lax
from jax.experimental import pallas as pl
from jax.experimental.pallas import tpu as pltpu

F32 = jnp.float32
BF16 = jnp.bfloat16

D_MODEL = 1024
DEPTH = 4
HEAD_DIM = 64
N_Q_HEADS = 12
N_KV_HEADS = 4
GQA_GROUP = N_Q_HEADS // N_KV_HEADS
WINDOW = 128
PAST_LEN = 8192
ROPE_THETA = 10000.0
CONV_CH = N_Q_HEADS * HEAD_DIM
CONV_W = 3
N_X_HEADS = 4
X_DIM = N_X_HEADS * HEAD_DIM
KV_DIM = N_KV_HEADS * HEAD_DIM
D_FF = 4 * D_MODEL
CONV_IN = 3 * CONV_CH + X_DIM
EPS = 1e-6
NEG = -1e30
LOG2E = 1.4426950408889634
Q_SCALE = HEAD_DIM ** -0.5 * LOG2E

LANES = 128
SUBLANES = 8
MXU_DIM = 256
TOKEN_TILE = 512
ATT_TILE = 1024
MIX_TILE = 2048
FF_CHUNK = 2048
SEQ_BLOCK = 16
SEQ_GROUP = 16
PAIR = 2 * HEAD_DIM

_NT = (((1,), (1,)), ((), ()))


def _const_spec(shape):
    nd = len(shape)
    return pl.BlockSpec(shape, lambda *_: (0,) * nd, pipeline_mode=pl.Buffered(1))


def _layer_spec(shape, layer):
    nd = len(shape)
    return pl.BlockSpec((1,) + shape, lambda *_: (layer,) + (0,) * nd, pipeline_mode=pl.Buffered(1))


def _rms_rows(x, g):
    return x * lax.rsqrt(jnp.mean(x * x, axis=-1, keepdims=True) + EPS) * g


def _head_blockdiag():
    r = lax.broadcasted_iota(jnp.int32, (MXU_DIM, MXU_DIM), 0) // HEAD_DIM
    c = lax.broadcasted_iota(jnp.int32, (MXU_DIM, MXU_DIM), 1) // HEAD_DIM
    return jnp.where(r == c, 1.0, 0.0).astype(BF16)


def _head_mean_sq(z, bd):
    sq = z * z
    hi = sq.astype(BF16)
    lo = (sq - hi.astype(F32)).astype(BF16)
    parts = []
    for c in range(z.shape[1] // MXU_DIM):
        sl = slice(c * MXU_DIM, (c + 1) * MXU_DIM)
        parts.append(jnp.dot(hi[:, sl], bd, preferred_element_type=F32)
                     + jnp.dot(lo[:, sl], bd, preferred_element_type=F32))
    ms = parts[0] if len(parts) == 1 else jnp.concatenate(parts, axis=1)
    return ms * (1.0 / HEAD_DIM)


def _rope_rows(x, cos, sin_signed):
    t = x.shape[0]
    first_half = (lax.broadcasted_iota(jnp.int32, (t, LANES), 1) & (HEAD_DIM - 1)) < HEAD_DIM // 2
    parts = []
    for c in range(x.shape[1] // LANES):
        xc = x[:, c * LANES:(c + 1) * LANES]
        partner = jnp.where(first_half,
                            pltpu.roll(xc, LANES - HEAD_DIM // 2, axis=1),
                            pltpu.roll(xc, HEAD_DIM // 2, axis=1))
        parts.append(xc * cos + partner * sin_signed)
    return jnp.concatenate(parts, axis=1)


def _lane_tile(col, n):
    return col if n == LANES else jnp.concatenate([col] * (n // LANES), axis=1)


def _head_norm_cols(xt, g_col):
    t = xt.shape[1]
    g = _lane_tile(g_col, t)
    parts = []
    for h in range(xt.shape[0] // HEAD_DIM):
        blk = xt[h * HEAD_DIM:(h + 1) * HEAD_DIM, :]
        ms = jnp.mean(blk * blk, axis=0, keepdims=True)
        parts.append(blk * lax.rsqrt(ms + EPS) * g)
    return parts


def _softmax_rows(s, sink=None):
    m = jnp.max(s, axis=-1, keepdims=True)
    if sink is not None:
        m = jnp.maximum(m, sink)
    p = jnp.exp2(s - m)
    den = jnp.sum(p, axis=-1, keepdims=True)
    if sink is not None:
        den = den + jnp.exp2(sink - m)
    return p.astype(BF16), den


def _pair_normalise(low, res_lo, res_hi, den_lo, den_hi):
    return jnp.where(low, res_lo, res_hi) / jnp.where(low, den_lo, den_hi)


def _convert_specs(convert, steps, last=None):
    last = steps - 1 if last is None else last
    in_specs, args, out_specs, out_shape = [], [], [], []
    for stack, l in convert:
        _, r, c = stack.shape
        in_specs.append(pl.BlockSpec((1, r // steps, c), lambda i, l=l: (l, jnp.minimum(i, last), 0)))
        args.append(stack)
        out_specs.append(pl.BlockSpec((r // steps, c), lambda i: (jnp.minimum(i, last), 0)))
        out_shape.append(jax.ShapeDtypeStruct((r, c), BF16))
    return in_specs, args, out_specs, out_shape


def _mem_kv_kernel(mem_ref, g_ref, wt_ref, gk_ref, mk_ref, mv_ref):
    x = mem_ref[...]
    xn = x * lax.rsqrt(jnp.mean(x * x, axis=-1, keepdims=True) + EPS)
    for i in range(DEPTH):
        h = (xn * g_ref[i]).astype(BF16)
        kvt = lax.dot_general(wt_ref[i], h, _NT, preferred_element_type=F32)
        mk_ref[i, 0] = jnp.concatenate(_head_norm_cols(kvt[:X_DIM], gk_ref[i]), axis=0)
        mv_ref[i, 0] = kvt[X_DIM:]


def _mem_kv(mem2d, norm_mem, w_mem_kv_t, gk_cols, *, batch, n_mem):
    out = jax.ShapeDtypeStruct((DEPTH, batch, X_DIM, n_mem), F32)
    return pl.pallas_call(
        _mem_kv_kernel,
        grid=(batch,),
        in_specs=[
            pl.BlockSpec((n_mem, D_MODEL), lambda b: (b, 0)),
            _const_spec((DEPTH, 1, D_MODEL)),
            _const_spec((DEPTH, 2 * X_DIM, D_MODEL)),
            _const_spec((DEPTH, HEAD_DIM, LANES)),
        ],
        out_specs=[pl.BlockSpec((DEPTH, 1, X_DIM, n_mem), lambda b: (0, b, 0, 0))] * 2,
        out_shape=[out, out],
        name="mem_kv",
    )(mem2d, norm_mem, w_mem_kv_t, gk_cols)


def _in_att_kernel(*refs, keep_cols, n_convert):
    (x_ref, g_ref, wq_ref, wkv_ref, gq_ref, gx_ref, gk_ref, cos_ref, sin_ref, cost_ref, sint_ref) = refs[:11]
    n_out = 6 if keep_cols else 4
    outs = refs[11 + n_convert:11 + n_convert + n_out]
    q_ref, qx_ref, kt_ref, vt_ref = outs[:4]
    for src, dst in zip(refs[11:11 + n_convert], refs[11 + n_convert + n_out:]):
        dst[...] = src[0].astype(BF16)
    t = x_ref.shape[0]
    h = _rms_rows(x_ref[...], g_ref[0]).astype(BF16)
    bd = _head_blockdiag()
    z = jnp.dot(h, wq_ref[0], preferred_element_type=F32)
    q = z[:, :CONV_CH]
    q = q * lax.rsqrt(_head_mean_sq(q, bd) + EPS) * gq_ref[0]
    q_ref[...] = (_rope_rows(q, cos_ref[...], sin_ref[...]) * Q_SCALE).astype(q_ref.dtype)
    qx = z[:, CONV_CH:]
    qx = qx * lax.rsqrt(_head_mean_sq(qx, bd) + EPS) * gx_ref[0]
    qx_ref[...] = (qx * Q_SCALE).astype(qx_ref.dtype)

    kvt = lax.dot_general(wkv_ref[0], h, _NT, preferred_element_type=F32)
    cos_t, sin_t = cost_ref[...], sint_ref[...]
    half = HEAD_DIM // 2
    k_parts = []
    for blk in _head_norm_cols(kvt[:KV_DIM], gk_ref[0]):
        x1, x2 = blk[:half], blk[half:]
        k_parts += [x1 * cos_t - x2 * sin_t, x2 * cos_t + x1 * sin_t]
    kt = jnp.concatenate(k_parts, axis=0)
    vt = kvt[KV_DIM:]
    kt_ref[...] = kt.astype(kt_ref.dtype)
    vt_ref[...] = vt.astype(vt_ref.dtype)
    if keep_cols:
        kt32_ref, vt32_ref = outs[4:]
        kt32_ref[0] = kt[:, t - keep_cols:]
        vt32_ref[0] = vt[:, t - keep_cols:]


def _in_att(x2d, layer, j, g_mix, wq, wkv_t, gq, gx, gk_cols, cos, sin, cos_t, sin_t,
            *, tiles_per_seq, keep_cols, act_dtype, convert=()):
    n = x2d.shape[0]
    tm = min(ATT_TILE, n)
    nt = n // tm
    nseq = nt // tiles_per_seq
    row = lambda i: (i, 0)
    col = lambda i: (0, i)
    pos = lambda i: (i % tiles_per_seq, 0)
    pos_t = lambda i: (0, i % tiles_per_seq)
    out_specs = [
        pl.BlockSpec((tm, CONV_CH), row),
        pl.BlockSpec((tm, X_DIM), row),
        pl.BlockSpec((KV_DIM, tm), col),
        pl.BlockSpec((KV_DIM, tm), col),
    ]
    out_shape = [
        jax.ShapeDtypeStruct((n, CONV_CH), act_dtype),
        jax.ShapeDtypeStruct((n, X_DIM), act_dtype),
        jax.ShapeDtypeStruct((KV_DIM, n), act_dtype),
        jax.ShapeDtypeStruct((KV_DIM, n), act_dtype),
    ]
    if keep_cols:
        out_specs += [pl.BlockSpec((1, KV_DIM, keep_cols), lambda i: (i // tiles_per_seq, 0, 0))] * 2
        out_shape += [jax.ShapeDtypeStruct((nseq, KV_DIM, keep_cols), F32)] * 2
    cv_in, cv_args, cv_out, cv_shape = _convert_specs(convert, nt)
    return pl.pallas_call(
        functools.partial(_in_att_kernel, keep_cols=keep_cols, n_convert=len(convert)),
        grid=(nt,),
        in_specs=[
            pl.BlockSpec((tm, D_MODEL), row),
            _layer_spec((1, D_MODEL), layer),
            _layer_spec((D_MODEL, CONV_CH + X_DIM), j),
            _layer_spec((2 * KV_DIM, D_MODEL), j),
            _layer_spec((1, CONV_CH), j),
            _layer_spec((1, X_DIM), layer),
            _layer_spec((HEAD_DIM, LANES), j),
            pl.BlockSpec((tm, LANES), pos),
            pl.BlockSpec((tm, LANES), pos),
            pl.BlockSpec((HEAD_DIM // 2, tm), pos_t),
            pl.BlockSpec((HEAD_DIM // 2, tm), pos_t),
        ] + cv_in,
        out_specs=out_specs + cv_out,
        out_shape=out_shape + cv_shape,
        name="in_att",
    )(x2d, g_mix, wq, wkv_t, gq, gx, gk_cols, cos, sin, cos_t, sin_t, *cv_args)


def _in_conv_kernel(*refs, tiles_per_seq, period):
    if period:
        (x_ref, g_ref, w_ref, gx_ref, cw_ref, prefix_ref,
         o_ref, qx_ref, gt_ref, gbuf, pbuf) = refs
    else:
        (x_ref, g_ref, w_ref, gx_ref, cw_ref,
         o_ref, qx_ref, gt_ref, gbuf) = refs
    t = x_ref.shape[0]
    keep = gt_ref.shape[0]
    h = _rms_rows(x_ref[...], g_ref[0]).astype(BF16)

    def proj(lo, width):
        return jnp.dot(h, w_ref[:, lo:lo + width], preferred_element_type=F32)

    @pl.when(pl.program_id(0) % tiles_per_seq == 0)
    def _():
        gbuf[0:SUBLANES, :] = jnp.zeros((SUBLANES, CONV_CH), F32)

    if period:
        pbuf[0:t, :] = prefix_ref[...]
        pbuf[t:t + SUBLANES, :] = jnp.zeros((SUBLANES, CONV_CH), F32)
        tok = lax.broadcasted_iota(jnp.int32, (t, MXU_DIM), 0) % period
    cw = cw_ref[0]
    for c in range(CONV_CH // MXU_DIM):
        sl = slice(c * MXU_DIM, (c + 1) * MXU_DIM)
        gate_b = proj(sl.start, MXU_DIM)
        u = proj(CONV_CH + sl.start, MXU_DIM) * proj(2 * CONV_CH + sl.start, MXU_DIM)
        gbuf[SUBLANES:SUBLANES + t, sl] = u
        back1 = gbuf[SUBLANES - 1:SUBLANES - 1 + t, sl]
        back2 = gbuf[SUBLANES - 2:SUBLANES - 2 + t, sl]
        if period:
            back1 = jnp.where(tok >= 1, back1, pbuf[1:1 + t, sl])
            back2 = jnp.where(tok >= 2, back2, pbuf[0:t, sl])
        y = back2 * cw[0:1, sl] + back1 * cw[1:2, sl] + u * cw[2:3, sl]
        o_ref[:, sl] = (gate_b * y).astype(o_ref.dtype)
        gt_ref[:, sl] = u[t - keep:, :]
    gbuf[0:SUBLANES, :] = gbuf[t:t + SUBLANES, :]
    qx = proj(3 * CONV_CH, X_DIM)
    qx = qx * lax.rsqrt(_head_mean_sq(qx, _head_blockdiag()) + EPS) * gx_ref[0]
    qx_ref[...] = (qx * Q_SCALE).astype(qx_ref.dtype)


def _in_conv(x2d, layer, j, g_mix, w, gx, cw, prefix, *, tiles_per_seq, period, keep_rows, act_dtype):
    n = x2d.shape[0]
    tm = min(MIX_TILE, n)
    nt = n // tm
    nseq = nt // tiles_per_seq
    row = lambda i: (i, 0)
    in_specs = [
        pl.BlockSpec((tm, D_MODEL), row),
        _layer_spec((1, D_MODEL), layer),
        _const_spec((D_MODEL, CONV_IN)),
        _layer_spec((1, X_DIM), layer),
        _layer_spec((CONV_W, CONV_CH), j),
    ]
    args = [x2d, g_mix, w, gx, cw]
    scratch = [pltpu.VMEM((tm + 2 * SUBLANES, CONV_CH), F32)]
    if period:
        in_specs.append(pl.BlockSpec((tm, CONV_CH), row))
        args.append(prefix)
        scratch.append(pltpu.VMEM((tm + SUBLANES, CONV_CH), F32))
    return pl.pallas_call(
        functools.partial(_in_conv_kernel, tiles_per_seq=tiles_per_seq, period=period),
        grid=(nt,),
        in_specs=in_specs,
        out_specs=[
            pl.BlockSpec((tm, CONV_CH), row),
            pl.BlockSpec((tm, X_DIM), row),
            pl.BlockSpec((keep_rows, CONV_CH), lambda i: (i // tiles_per_seq, 0)),
        ],
        out_shape=[
            jax.ShapeDtypeStruct((n, CONV_CH), act_dtype),
            jax.ShapeDtypeStruct((n, X_DIM), act_dtype),
            jax.ShapeDtypeStruct((nseq * keep_rows, CONV_CH), F32),
        ],
        scratch_shapes=scratch,
        name="in_conv",
    )(*args)


def _pair_lhs(slabs):
    low = lax.broadcasted_iota(jnp.int32, slabs[0].shape, 1) < HEAD_DIM
    zero = jnp.zeros_like(slabs[0])
    parts = []
    for s in slabs:
        parts += [jnp.where(low, s, zero), jnp.where(low, zero, s)]
    return jnp.concatenate(parts, axis=0)


def _swa_prompt_kernel(sinks_ref, q_ref, kc_ref, kp_ref, vc_ref, vp_ref, o_ref, kfull, vfull, bias):
    tq = q_ref.shape[0]

    @pl.when(jnp.logical_and(pl.program_id(0) == 0, pl.program_id(1) == 0))
    def _():
        r = lax.broadcasted_iota(jnp.int32, (WINDOW, 2 * WINDOW), 0)
        c = lax.broadcasted_iota(jnp.int32, (WINDOW, 2 * WINDOW), 1)
        band = (c > r) & (c - WINDOW <= r)
        for kind, visible in enumerate((band, band & (c >= WINDOW))):
            base = jnp.where(visible, 0.0, NEG)
            for h in range(N_Q_HEADS):
                bias[kind, h] = jnp.where(c == 0, sinks_ref[h] * LOG2E, base)

    first_kind = jnp.where(pl.program_id(1) == 0, 1, 0)
    kfull[:, 0:WINDOW] = kp_ref[...]
    kfull[:, WINDOW:] = kc_ref[...]
    vfull[:, 0:WINDOW] = vp_ref[...]
    vfull[:, WINDOW:] = vc_ref[...]
    col0 = lax.broadcasted_iota(jnp.int32, (PAIR, 2 * WINDOW), 1) == 0
    zero_slab = jnp.zeros((PAIR, 2 * WINDOW), BF16)
    low = lax.broadcasted_iota(jnp.int32, (WINDOW, PAIR), 1) < HEAD_DIM
    for qb in range(tq // WINDOW):
        r0 = qb * WINDOW
        kind = first_kind if qb == 0 else 0
        slabs = [None] * (GQA_GROUP * KV_DIM // PAIR)
        for pr in range(KV_DIM // PAIR):
            kslab = jnp.where(col0, zero_slab, kfull[pr * PAIR:(pr + 1) * PAIR, r0:r0 + 2 * WINDOW])
            vslab = jnp.where(col0, zero_slab, vfull[pr * PAIR:(pr + 1) * PAIR, r0:r0 + 2 * WINDOW])
            lhs = _pair_lhs([q_ref[r0:r0 + WINDOW, g * KV_DIM + pr * PAIR:g * KV_DIM + (pr + 1) * PAIR]
                             for g in range(GQA_GROUP)])
            s_all = jnp.dot(lhs, kslab, preferred_element_type=F32)
            probs, dens = [], []
            for g in range(GQA_GROUP):
                for e in range(2):
                    idx = g * 2 + e
                    head = (2 * pr + e) * GQA_GROUP + g
                    p, den = _softmax_rows(s_all[idx * WINDOW:(idx + 1) * WINDOW] + bias[kind, head])
                    probs.append(p)
                    dens.append(den)
            res = lax.dot_general(jnp.concatenate(probs, axis=0), vslab, _NT,
                                  preferred_element_type=F32)
            for g in range(GQA_GROUP):
                slabs[g * (KV_DIM // PAIR) + pr] = _pair_normalise(
                    low, res[(2 * g) * WINDOW:(2 * g + 1) * WINDOW], res[(2 * g + 1) * WINDOW:(2 * g + 2) * WINDOW],
                    dens[2 * g], dens[2 * g + 1])
        o_ref[r0:r0 + WINDOW, :] = jnp.concatenate(slabs, axis=1).astype(o_ref.dtype)


def _swa_prompt(sinks, q, kt, vt, *, batch, seq):
    tq = MIX_TILE
    nt = seq // tq
    per = tq // WINDOW
    rows = lambda b, j: (b * nt + j, 0)
    cur = lambda b, j: (0, b * nt + j)
    prev = lambda b, j: (0, b * nt * per + jnp.maximum(j * per - 1, 0))
    return pl.pallas_call(
        _swa_prompt_kernel,
        grid=(batch, nt),
        in_specs=[
            pl.BlockSpec(memory_space=pltpu.SMEM),
            pl.BlockSpec((tq, CONV_CH), rows),
            pl.BlockSpec((KV_DIM, tq), cur),
            pl.BlockSpec((KV_DIM, WINDOW), prev),
            pl.BlockSpec((KV_DIM, tq), cur),
            pl.BlockSpec((KV_DIM, WINDOW), prev),
        ],
        out_specs=pl.BlockSpec((tq, CONV_CH), rows),
        out_shape=jax.ShapeDtypeStruct(q.shape, BF16),
        scratch_shapes=[pltpu.VMEM((KV_DIM, tq + WINDOW), BF16)] * 2
        + [pltpu.VMEM((2, N_Q_HEADS, WINDOW, 2 * WINDOW), F32)],
        name="swa_prompt",
    )(sinks, q, kt, kt, vt, vt)


def _swa_sample_kernel(*refs, dec, carried, slot):
    if carried:
        sinks_ref, q_ref, kn_ref, vn_ref, kb_ref, vb_ref, _, _, o_ref, ko_ref, vo_ref = refs
    else:
        sinks_ref, q_ref, kn_ref, vn_ref, kb_ref, vb_ref, o_ref, ko_ref, vo_ref = refs
    nseq = kb_ref.shape[1]
    for other in range(ko_ref.shape[0]):
        if other != slot:
            ko_ref[other] = jnp.zeros(ko_ref.shape[1:], F32)
            vo_ref[other] = jnp.zeros(vo_ref.shape[1:], F32)
    steps_per_tile = LANES // (nseq * dec)
    base = (pl.program_id(0) % steps_per_tile) * (nseq * dec)
    grp = N_KV_HEADS * dec
    rows = GQA_GROUP * grp
    ri = lax.broadcasted_iota(jnp.int32, (rows, KV_DIM), 0)
    li = lax.broadcasted_iota(jnp.int32, (rows, KV_DIM), 1)
    head_mask = ((ri % grp) // dec) == (li // HEAD_DIM)
    tok = lax.broadcasted_iota(jnp.int32, (rows, WINDOW + LANES), 0) % dec
    col = lax.broadcasted_iota(jnp.int32, (rows, WINDOW + LANES), 1)
    cache_mask = (col < WINDOW) & (col > tok)
    lane = lax.broadcasted_iota(jnp.int32, (KV_DIM, WINDOW), 1)
    sink = jnp.concatenate(
        [jnp.full((dec, 1), sinks_ref[kv * GQA_GROUP + g] * LOG2E, F32)
         for g in range(GQA_GROUP) for kv in range(N_KV_HEADS)], axis=0)
    kn = kn_ref[...]
    vn = vn_ref[...]
    kn16 = kn.astype(BF16)
    vn16 = vn.astype(BF16)
    for s0 in range(0, nseq, SEQ_GROUP):
        seqs = range(s0, s0 + SEQ_GROUP)
        scores, values = [], []
        for s in seqs:
            off = base + s * dec
            new = col - (WINDOW + off)
            mask = cache_mask | ((new >= 0) & (new <= tok))
            qs = q_ref[pl.ds(s * dec, dec), :]
            qbd = jnp.concatenate(
                [jnp.concatenate([qs[:, g * KV_DIM:(g + 1) * KV_DIM]] * N_KV_HEADS, axis=0)
                 for g in range(GQA_GROUP)], axis=0)
            qbd = jnp.where(head_mask, qbd, 0.0).astype(BF16)
            keys = jnp.concatenate([kb_ref[0, s].astype(BF16), kn16], axis=1)
            values.append(jnp.concatenate([vb_ref[0, s].astype(BF16), vn16], axis=1))
            scores.append(jnp.where(mask, jnp.dot(qbd, keys, preferred_element_type=F32), NEG))
        probs = [_softmax_rows(sc, sink) for sc in scores]
        for s, (p, den), vals in zip(seqs, probs, values):
            o = lax.dot_general(p, vals, _NT, preferred_element_type=F32) * (1.0 / den)
            o = jnp.where(head_mask, o, 0.0)
            folded = []
            for g in range(GQA_GROUP):
                og = o[g * grp:g * grp + dec, :]
                for kv in range(1, N_KV_HEADS):
                    og = og + o[g * grp + kv * dec:g * grp + (kv + 1) * dec, :]
                folded.append(og)
            o_ref[pl.ds(s * dec, dec), :] = jnp.concatenate(folded, axis=1)
        for s in seqs:
            shift_new = (WINDOW - dec + LANES - (base + s * dec)) % LANES
            ko_ref[slot, s] = jnp.where(lane < WINDOW - dec, pltpu.roll(kb_ref[0, s], WINDOW - dec, axis=1),
                                     pltpu.roll(kn, shift_new, axis=1))
            vo_ref[slot, s] = jnp.where(lane < WINDOW - dec, pltpu.roll(vb_ref[0, s], WINDOW - dec, axis=1),
                                     pltpu.roll(vn, shift_new, axis=1))


def _att_sample_kernel(*refs, dec, carried, slot):
    n_in = 8 if carried else 6
    qx_ref, mk_ref, mv_ref = refs[n_in:n_in + 3]
    o_ref, ko_ref, vo_ref, ox_ref = refs[n_in + 3:]
    _xattn_sample_kernel(qx_ref, mk_ref, mv_ref, ox_ref, dec=dec)
    _swa_sample_kernel(*refs[:n_in], o_ref, ko_ref, vo_ref, dec=dec, carried=carried, slot=slot)


def _swa_sample(sinks, q, kt_new, vt_new, kbuf, vbuf, j, carry, *, dec, xattn=None):
    nseq = kbuf.shape[1]
    sb = SEQ_BLOCK
    steps_per_tile = LANES // (sb * dec)
    row = lambda i: (i, 0)
    tile = lambda i: (0, i // steps_per_tile)
    cache = lambda i: (j, i, 0, 0)
    if carry is None:
        new_cache = pl.BlockSpec((kbuf.shape[0], sb, KV_DIM, WINDOW), lambda i: (0, i, 0, 0))
    else:
        new_cache = pl.BlockSpec((1, sb, KV_DIM, WINDOW), cache)
    in_specs = [
        pl.BlockSpec(memory_space=pltpu.SMEM),
        pl.BlockSpec((sb * dec, CONV_CH), row),
        pl.BlockSpec((KV_DIM, LANES), tile),
        pl.BlockSpec((KV_DIM, LANES), tile),
        pl.BlockSpec((1, sb, KV_DIM, WINDOW), cache),
        pl.BlockSpec((1, sb, KV_DIM, WINDOW), cache),
    ]
    args = [sinks, q, kt_new, vt_new, kbuf, vbuf]
    aliases = {}
    if carry is not None:
        aliases = {len(args): 1, len(args) + 1: 2}
        in_specs += [pl.BlockSpec(memory_space=pl.ANY)] * 2
        args += list(carry)
    out_specs = [pl.BlockSpec((sb * dec, CONV_CH), row), new_cache, new_cache]
    out_shape = [jax.ShapeDtypeStruct(q.shape, F32), jax.ShapeDtypeStruct(kbuf.shape, F32),
                 jax.ShapeDtypeStruct(vbuf.shape, F32)]
    body = _swa_sample_kernel
    if xattn is not None:
        qx, mk, mv, layer = xattn
        mem = lambda i: (layer, i, 0, 0)
        in_specs += [pl.BlockSpec((sb * dec, X_DIM), row)] + [pl.BlockSpec((1, sb, X_DIM, mk.shape[3]), mem)] * 2
        args += [qx, mk, mv]
        out_specs.append(pl.BlockSpec((sb * dec, X_DIM), row))
        out_shape.append(jax.ShapeDtypeStruct(qx.shape, F32))
        body = _att_sample_kernel
    return pl.pallas_call(
        functools.partial(body, dec=dec, carried=carry is not None, slot=j if carry is None else 0),
        grid=(nseq // sb,),
        in_specs=in_specs,
        out_specs=out_specs,
        out_shape=out_shape,
        input_output_aliases=aliases,
        name="swa_sample",
    )(*args)


def _xattn_tile(q_ref, mk_ref, mv_ref):
    tq = q_ref.shape[0]
    mk = mk_ref[0, 0].astype(BF16)
    mv = mv_ref[0, 0].astype(BF16)
    low = lax.broadcasted_iota(jnp.int32, (WINDOW, PAIR), 1) < HEAD_DIM
    blocks = []
    for rb in range(tq // WINDOW):
        r0 = rb * WINDOW
        slabs = []
        for pr in range(X_DIM // PAIR):
            lhs = _pair_lhs([q_ref[r0:r0 + WINDOW, pr * PAIR:(pr + 1) * PAIR]])
            s_all = jnp.dot(lhs, mk[pr * PAIR:(pr + 1) * PAIR], preferred_element_type=F32)
            p0, den0 = _softmax_rows(s_all[:WINDOW])
            p1, den1 = _softmax_rows(s_all[WINDOW:])
            res = lax.dot_general(jnp.concatenate([p0, p1], axis=0), mv[pr * PAIR:(pr + 1) * PAIR], _NT,
                                  preferred_element_type=F32)
            slabs.append(_pair_normalise(low, res[:WINDOW], res[WINDOW:], den0, den1))
        blocks.append(jnp.concatenate(slabs, axis=1).astype(BF16))
    return jnp.concatenate(blocks, axis=0)


def _xattn_sample_kernel(q_ref, mk_ref, mv_ref, o_ref, *, dec):
    nseq = mk_ref.shape[1]
    rows = N_X_HEADS * dec
    head_mask = (lax.broadcasted_iota(jnp.int32, (rows, X_DIM), 0) // dec) == (
        lax.broadcasted_iota(jnp.int32, (rows, X_DIM), 1) // HEAD_DIM)
    for s0 in range(0, nseq, SEQ_GROUP):
        seqs = range(s0, s0 + SEQ_GROUP)
        scores = []
        for s in seqs:
            qs = q_ref[pl.ds(s * dec, dec), :]
            qbd = jnp.where(head_mask, jnp.concatenate([qs] * N_X_HEADS, axis=0), 0.0).astype(BF16)
            scores.append(jnp.dot(qbd, mk_ref[0, s].astype(BF16), preferred_element_type=F32))
        probs = [_softmax_rows(sc) for sc in scores]
        for s, (p, den) in zip(seqs, probs):
            o = lax.dot_general(p, mv_ref[0, s].astype(BF16), _NT, preferred_element_type=F32) * (1.0 / den)
            o = jnp.where(head_mask, o, 0.0)
            acc = o[0:dec, :]
            for h in range(1, N_X_HEADS):
                acc = acc + o[h * dec:(h + 1) * dec, :]
            o_ref[pl.ds(s * dec, dec), :] = acc


def _xattn_sample(qx, mk, mv, layer, *, dec):
    nseq, n_mem = mk.shape[1], mk.shape[3]
    sb = SEQ_BLOCK
    mem = lambda i: (layer, i, 0, 0)
    return pl.pallas_call(
        functools.partial(_xattn_sample_kernel, dec=dec),
        grid=(nseq // sb,),
        in_specs=[
            pl.BlockSpec((sb * dec, X_DIM), lambda i: (i, 0)),
            pl.BlockSpec((1, sb, X_DIM, n_mem), mem),
            pl.BlockSpec((1, sb, X_DIM, n_mem), mem),
        ],
        out_specs=pl.BlockSpec((sb * dec, X_DIM), lambda i: (i, 0)),
        out_shape=jax.ShapeDtypeStruct(qx.shape, F32),
        name="xattn_sample",
    )(qx, mk, mv)


def _mlp_tile(x, o, wo, gm, wu_ref, wd_ref):
    x1 = x + jnp.dot(o, wo, preferred_element_type=F32)
    hm = _rms_rows(x1, gm).astype(BF16)
    acc = x1
    for c in range(D_FF // FF_CHUNK):
        a = jnp.dot(hm, wu_ref[:, c * FF_CHUNK:(c + 1) * FF_CHUNK], preferred_element_type=F32)
        a = jnp.square(jnp.maximum(a, 0.0)).astype(BF16)
        acc = acc + jnp.dot(a, wd_ref[c * FF_CHUNK:(c + 1) * FF_CHUNK, :], preferred_element_type=F32)
    return acc


def _out_mlp_kernel(*refs, n_prompt_tiles, n_convert):
    (xp_ref, osp_ref, qxp_ref, mk_ref, mv_ref, xs_ref, oss_ref, oxs_ref,
     wo_ref, gm_ref, wu_ref, wd_ref) = refs[:12]
    yp_ref, ys_ref = refs[12 + n_convert:14 + n_convert]
    for src, dst in zip(refs[12:12 + n_convert], refs[14 + n_convert:]):
        dst[...] = src[0].astype(BF16)
    is_sample = pl.program_id(0) == n_prompt_tiles

    @pl.when(jnp.logical_not(is_sample))
    def _():
        o = jnp.concatenate([osp_ref[...], _xattn_tile(qxp_ref, mk_ref, mv_ref)], axis=1)
        yp_ref[...] = _mlp_tile(xp_ref[...], o, wo_ref[...], gm_ref[0], wu_ref, wd_ref)

    @pl.when(is_sample)
    def _():
        o = jnp.concatenate([oss_ref[...].astype(BF16), oxs_ref[...].astype(BF16)], axis=1)
        ys_ref[...] = _mlp_tile(xs_ref[...], o, wo_ref[...], gm_ref[0], wu_ref, wd_ref)


def _out_mlp(xp, os_p, qx_p, memory, xs, os_s, ox_s, layer, wo, gm, wu, wd, *, tiles_per_batch, convert=()):
    n = xp.shape[0]
    tm = TOKEN_TILE
    assert xs.shape[0] == tm
    npt = n // tm
    n_mem = memory[0].shape[3]
    last = npt - 1
    prow = lambda i: (jnp.minimum(i, last), 0)
    mem = lambda i: (layer, jnp.minimum(i, last) // tiles_per_batch, 0, 0)
    cv_in, cv_args, cv_out, cv_shape = _convert_specs(convert, npt, last)
    return pl.pallas_call(
        functools.partial(_out_mlp_kernel, n_prompt_tiles=npt, n_convert=len(convert)),
        grid=(npt + 1,),
        in_specs=[
            pl.BlockSpec((tm, D_MODEL), prow),
            pl.BlockSpec((tm, CONV_CH), prow),
            pl.BlockSpec((tm, X_DIM), prow),
            pl.BlockSpec((1, 1, X_DIM, n_mem), mem),
            pl.BlockSpec((1, 1, X_DIM, n_mem), mem),
            _const_spec((tm, D_MODEL)),
            _const_spec((tm, CONV_CH)),
            _const_spec((tm, X_DIM)),
            _const_spec((CONV_CH + X_DIM, D_MODEL)),
            _layer_spec((1, D_MODEL), layer),
            _const_spec((D_MODEL, D_FF)),
            _const_spec((D_FF, D_MODEL)),
        ] + cv_in,
        out_specs=[pl.BlockSpec((tm, D_MODEL), prow), pl.BlockSpec((tm, D_MODEL), lambda i: (0, 0))] + cv_out,
        out_shape=[jax.ShapeDtypeStruct((n, D_MODEL), F32), jax.ShapeDtypeStruct((tm, D_MODEL), F32)] + cv_shape,
        name="out_mlp",
    )(xp, os_p, qx_p, *memory, xs, os_s, ox_s, wo, gm, wu, wd, *cv_args)


def _rope_tables(pos):
    half = HEAD_DIM // 2
    inv = ROPE_THETA ** (-jnp.arange(half, dtype=F32) * 2.0 / HEAD_DIM)
    ang = pos[:, None] * inv[None, :]
    return jnp.cos(ang), jnp.sin(ang)


def _rope_lane_tables(cos, sin):
    reps = LANES // (HEAD_DIM // 2)
    return jnp.tile(cos, (1, reps)), jnp.concatenate([-sin, sin] * (reps // 2), axis=1)


def _group_major(w, axis):
    shape = w.shape
    w = w.reshape(shape[:axis] + (N_KV_HEADS, GQA_GROUP, HEAD_DIM) + shape[axis + 1:])
    return jnp.swapaxes(w, axis, axis + 1).reshape(shape)


def _feature_major(cache):
    lead = cache.shape[:-3]
    pos, heads, hd = cache.shape[-3:]
    nd = len(lead)
    perm = tuple(range(nd)) + (nd + 1, nd + 2, nd)
    return jnp.transpose(cache, perm).reshape(lead + (heads * hd, pos))


def _position_major(cache_t, heads):
    lead = cache_t.shape[:-2]
    pos = cache_t.shape[-1]
    nd = len(lead)
    perm = tuple(range(nd)) + (nd + 2, nd, nd + 1)
    return jnp.transpose(cache_t.reshape(lead + (heads, HEAD_DIM, pos)), perm)


def _gain_cols(g):
    return jnp.broadcast_to(g[:, :, None], g.shape + (LANES,))


def kernel(x_prompt, x_sample, mem_prompt, cache_swa_k, cache_swa_v, state_conv, cache_mem_k, cache_mem_v,
           norm_mix, w_in_att, q_norm_att, k_norm_att, sinks, w_in_conv, conv_w, norm_mem, w_mem_kv,
           q_norm_x, k_norm_x, w_out, norm_mlp, w_up, w_down):
    batch, seq, _ = x_prompt.shape
    nseq, dec, _ = x_sample.shape
    n_mem = mem_prompt.shape[1]
    assert seq % MIX_TILE == 0 and seq % ATT_TILE == 0
    assert (nseq * dec) == TOKEN_TILE and nseq % SEQ_BLOCK == 0
    assert LANES % (SEQ_BLOCK * dec) == 0

    kv0, kv1 = CONV_CH, CONV_CH + 2 * KV_DIM
    w_q = jnp.concatenate([_group_major(w_in_att[:, :, :kv0], 2), w_in_att[:, :, kv1:]], axis=2).astype(BF16)
    w_kv_t = jnp.swapaxes(w_in_att[:, :, kv0:kv1], 1, 2).astype(BF16)
    w_o = [jnp.concatenate([_group_major(w_out[i, :CONV_CH], 0), w_out[i, CONV_CH:]], axis=0).astype(BF16)
           if i % 2 == 0 else w_out[i].astype(BF16) for i in range(DEPTH)]
    w_u = w_d = w_conv = None
    w_mkv_t = jnp.swapaxes(w_mem_kv, 1, 2).astype(BF16)

    g_mix = norm_mix[:, None, :]
    g_mlp = norm_mlp[:, None, :]
    gq = jnp.tile(q_norm_att, (1, N_Q_HEADS))[:, None, :]
    gx = jnp.tile(q_norm_x, (1, N_X_HEADS))[:, None, :]
    gk_cols = _gain_cols(k_norm_att)
    gkx_cols = _gain_cols(k_norm_x)

    cos_p, sin_p = _rope_tables(jnp.arange(seq, dtype=F32))
    cos_s, sin_s = _rope_tables(PAST_LEN + (jnp.arange(nseq * dec) % dec).astype(F32))
    rope_p = _rope_lane_tables(cos_p, sin_p) + (cos_p.T, sin_p.T)
    rope_s = _rope_lane_tables(cos_s, sin_s) + (cos_s.T, sin_s.T)

    mk_p, mv_p = _mem_kv(mem_prompt.reshape(batch * n_mem, D_MODEL), norm_mem[:, None, :], w_mkv_t, gkx_cols,
                         batch=batch, n_mem=n_mem)
    mk_s = _feature_major(cache_mem_k)
    mv_s = _feature_major(cache_mem_v)
    kbuf = _feature_major(cache_swa_k)
    vbuf = _feature_major(cache_swa_v)

    xp = x_prompt.reshape(batch * seq, D_MODEL)
    xs = x_sample.reshape(nseq * dec, D_MODEL)
    k_p, v_p, c_p, c_s = [], [], [], []
    cache_s = None
    for i in range(DEPTH):
        j = i // 2
        if i % 2 == 0:
            first = [(w_up, 0), (w_down, 0)] if i == 0 else []
            q, qx_p, kt, vt, kt32, vt32, *w_first = _in_att(
                xp, i, j, g_mix, w_q, w_kv_t, gq, gx, gk_cols, *rope_p,
                tiles_per_seq=seq // ATT_TILE, keep_cols=WINDOW, act_dtype=BF16, convert=first)
            if w_first:
                w_u, w_d = w_first
            k_p.append(kt32)
            v_p.append(vt32)
            os_p = _swa_prompt(sinks[j], q, kt, vt, batch=batch, seq=seq)
            q, qx_s, kt, vt = _in_att(xs, i, j, g_mix, w_q, w_kv_t, gq, gx, gk_cols, *rope_s,
                                      tiles_per_seq=1, keep_cols=0, act_dtype=F32)
            os_s, *cache_s, ox_s = _swa_sample(sinks[j], q, kt, vt, kbuf, vbuf, j, cache_s, dec=dec,
                                               xattn=(qx_s, mk_s, mv_s, i))
        else:
            os_p, qx_p, tail = _in_conv(xp, i, j, g_mix, w_conv, gx, conv_w, None,
                                        tiles_per_seq=seq // MIX_TILE, period=0,
                                        keep_rows=SUBLANES, act_dtype=BF16)
            c_p.append(tail.reshape(batch, SUBLANES, CONV_CH)[:, SUBLANES - (CONV_W - 1):])
            prefix = jnp.pad(state_conv[j], ((0, 0), (0, dec - (CONV_W - 1)), (0, 0)))
            os_s, qx_s, tail = _in_conv(xs, i, j, g_mix, w_conv, gx, conv_w,
                                        prefix.reshape(nseq * dec, CONV_CH),
                                        tiles_per_seq=1, period=dec,
                                        keep_rows=nseq * dec, act_dtype=F32)
            c_s.append(tail.reshape(nseq, dec, CONV_CH)[:, dec - (CONV_W - 1):])
            ox_s = _xattn_sample(qx_s, mk_s, mv_s, i, dec=dec)
        convert = []
        if i + 1 < DEPTH:
            convert = [(w_up, i + 1), (w_down, i + 1)] + ([(w_in_conv, (i + 1) // 2)] if i % 2 == 0 else [])
        xp, xs, *w_next = _out_mlp(xp, os_p, qx_p, (mk_p, mv_p), xs, os_s, ox_s, i, w_o[i], g_mlp, w_u, w_d,
                                   tiles_per_batch=seq // TOKEN_TILE, convert=convert)
        if w_next:
            w_u, w_d = w_next[:2]
            w_conv = w_next[2] if len(w_next) > 2 else None

    return (xp.reshape(batch, seq, D_MODEL),
            xs.reshape(nseq, dec, D_MODEL),
            _position_major(jnp.stack(k_p), N_KV_HEADS),
            _position_major(jnp.stack(v_p), N_KV_HEADS),
            jnp.stack(c_p),
            _position_major(mk_p, N_X_HEADS),
            _position_major(mv_p, N_X_HEADS),
            _position_major(cache_s[0], N_KV_HEADS),
            _position_major(cache_s[1], N_KV_HEADS),
            jnp.stack(c_s))
```

```python
import functools

import jax
import jax.numpy as jnp
from jax import lax
from jax.experimental import pallas as pl
from jax.experimental.pallas import tpu as pltpu

F32 = jnp.float32
BF16 = jnp.bfloat16

D_MODEL = 1024
DEPTH = 4
HEAD_DIM = 64
N_Q_HEADS = 12
N_KV_HEADS = 4
GQA_GROUP = N_Q_HEADS // N_KV_HEADS
WINDOW = 128
PAST_LEN = 8192
ROPE_THETA = 10000.0
CONV_CH = N_Q_HEADS * HEAD_DIM
CONV_W = 3
N_X_HEADS = 4
X_DIM = N_X_HEADS * HEAD_DIM
KV_DIM = N_KV_HEADS * HEAD_DIM
D_FF = 4 * D_MODEL
CONV_IN = 3 * CONV_CH + X_DIM
EPS = 1e-6
NEG = -1e30
LOG2E = 1.4426950408889634
Q_SCALE = HEAD_DIM ** -0.5 * LOG2E

LANES = 128
SUBLANES = 8
MXU_DIM = 256
TOKEN_TILE = 512
ATT_TILE = 1024
MIX_TILE = 2048
FF_CHUNK = 2048
SEQ_BLOCK = 16
SEQ_GROUP = 16
PAIR = 2 * HEAD_DIM

_NT = (((1,), (1,)), ((), ()))


def _const_spec(shape):
    nd = len(shape)
    return pl.BlockSpec(shape, lambda *_: (0,) * nd, pipeline_mode=pl.Buffered(1))


def _layer_spec(shape, layer):
    nd = len(shape)
    return pl.BlockSpec((1,) + shape, lambda *_: (layer,) + (0,) * nd, pipeline_mode=pl.Buffered(1))


def _rms_rows(x, g):
    return x * lax.rsqrt(jnp.mean(x * x, axis=-1, keepdims=True) + EPS) * g


def _head_blockdiag():
    r = lax.broadcasted_iota(jnp.int32, (MXU_DIM, MXU_DIM), 0) // HEAD_DIM
    c = lax.broadcasted_iota(jnp.int32, (MXU_DIM, MXU_DIM), 1) // HEAD_DIM
    return jnp.where(r == c, 1.0, 0.0).astype(BF16)


def _head_mean_sq(z, bd):
    sq = z * z
    hi = sq.astype(BF16)
    lo = (sq - hi.astype(F32)).astype(BF16)
    parts = []
    for c in range(z.shape[1] // MXU_DIM):
        sl = slice(c * MXU_DIM, (c + 1) * MXU_DIM)
        parts.append(jnp.dot(hi[:, sl], bd, preferred_element_type=F32)
                     + jnp.dot(lo[:, sl], bd, preferred_element_type=F32))
    ms = parts[0] if len(parts) == 1 else jnp.concatenate(parts, axis=1)
    return ms * (1.0 / HEAD_DIM)


def _rope_rows(x, cos, sin_signed):
    t = x.shape[0]
    first_half = (lax.broadcasted_iota(jnp.int32, (t, LANES), 1) & (HEAD_DIM - 1)) < HEAD_DIM // 2
    parts = []
    for c in range(x.shape[1] // LANES):
        xc = x[:, c * LANES:(c + 1) * LANES]
        partner = jnp.where(first_half,
                            pltpu.roll(xc, LANES - HEAD_DIM // 2, axis=1),
                            pltpu.roll(xc, HEAD_DIM // 2, axis=1))
        parts.append(xc * cos + partner * sin_signed)
    return jnp.concatenate(parts, axis=1)


def _lane_tile(col, n):
    return col if n == LANES else jnp.concatenate([col] * (n // LANES), axis=1)


def _head_norm_cols(xt, g_col):
    t = xt.shape[1]
    g = _lane_tile(g_col, t)
    parts = []
    for h in range(xt.shape[0] // HEAD_DIM):
        blk = xt[h * HEAD_DIM:(h + 1) * HEAD_DIM, :]
        ms = jnp.mean(blk * blk, axis=0, keepdims=True)
        parts.append(blk * lax.rsqrt(ms + EPS) * g)
    return parts


def _softmax_rows(s, sink=None):
    m = jnp.max(s, axis=-1, keepdims=True)
    if sink is not None:
        m = jnp.maximum(m, sink)
    p = jnp.exp2(s - m)
    den = jnp.sum(p, axis=-1, keepdims=True)
    if sink is not None:
        den = den + jnp.exp2(sink - m)
    return p.astype(BF16), den


def _pair_normalise(low, res_lo, res_hi, den_lo, den_hi):
    return jnp.where(low, res_lo, res_hi) / jnp.where(low, den_lo, den_hi)


def _convert_specs(convert, steps, last=None):
    last = steps - 1 if last is None else last
    in_specs, args, out_specs, out_shape = [], [], [], []
    for stack, l in convert:
        _, r, c = stack.shape
        in_specs.append(pl.BlockSpec((1, r // steps, c), lambda i, l=l: (l, jnp.minimum(i, last), 0)))
        args.append(stack)
        out_specs.append(pl.BlockSpec((r // steps, c), lambda i: (jnp.minimum(i, last), 0)))
        out_shape.append(jax.ShapeDtypeStruct((r, c), BF16))
    return in_specs, args, out_specs, out_shape


def _mem_kv_kernel(mem_ref, g_ref, wt_ref, gk_ref, mk_ref, mv_ref):
    x = mem_ref[...]
    xn = x * lax.rsqrt(jnp.mean(x * x, axis=-1, keepdims=True) + EPS)
    for i in range(DEPTH):
        h = (xn * g_ref[i]).astype(BF16)
        kvt = lax.dot_general(wt_ref[i], h, _NT, preferred_element_type=F32)
        mk_ref[i, 0] = jnp.concatenate(_head_norm_cols(kvt[:X_DIM], gk_ref[i]), axis=0)
        mv_ref[i, 0] = kvt[X_DIM:]


def _mem_kv(mem2d, norm_mem, w_mem_kv_t, gk_cols, *, batch, n_mem):
    out = jax.ShapeDtypeStruct((DEPTH, batch, X_DIM, n_mem), F32)
    return pl.pallas_call(
        _mem_kv_kernel,
        grid=(batch,),
        in_specs=[
            pl.BlockSpec((n_mem, D_MODEL), lambda b: (b, 0)),
            _const_spec((DEPTH, 1, D_MODEL)),
            _const_spec((DEPTH, 2 * X_DIM, D_MODEL)),
            _const_spec((DEPTH, HEAD_DIM, LANES)),
        ],
        out_specs=[pl.BlockSpec((DEPTH, 1, X_DIM, n_mem), lambda b: (0, b, 0, 0))] * 2,
        out_shape=[out, out],
        name="mem_kv",
    )(mem2d, norm_mem, w_mem_kv_t, gk_cols)


def _in_att_kernel(*refs, keep_cols, n_convert):
    (x_ref, g_ref, wq_ref, wkv_ref, gq_ref, gx_ref, gk_ref, cos_ref, sin_ref, cost_ref, sint_ref) = refs[:11]
    n_out = 6 if keep_cols else 4
    outs = refs[11 + n_convert:11 + n_convert + n_out]
    q_ref, qx_ref, kt_ref, vt_ref = outs[:4]
    for src, dst in zip(refs[11:11 + n_convert], refs[11 + n_convert + n_out:]):
        dst[...] = src[0].astype(BF16)
    t = x_ref.shape[0]
    h = _rms_rows(x_ref[...], g_ref[0]).astype(BF16)
    bd = _head_blockdiag()
    z = jnp.dot(h, wq_ref[0], preferred_element_type=F32)
    q = z[:, :CONV_CH]
    q = q * lax.rsqrt(_head_mean_sq(q, bd) + EPS) * gq_ref[0]
    q_ref[...] = (_rope_rows(q, cos_ref[...], sin_ref[...]) * Q_SCALE).astype(q_ref.dtype)
    qx = z[:, CONV_CH:]
    qx = qx * lax.rsqrt(_head_mean_sq(qx, bd) + EPS) * gx_ref[0]
    qx_ref[...] = (qx * Q_SCALE).astype(qx_ref.dtype)

    kvt = lax.dot_general(wkv_ref[0], h, _NT, preferred_element_type=F32)
    cos_t, sin_t = cost_ref[...], sint_ref[...]
    half = HEAD_DIM // 2
    k_parts = []
    for blk in _head_norm_cols(kvt[:KV_DIM], gk_ref[0]):
        x1, x2 = blk[:half], blk[half:]
        k_parts += [x1 * cos_t - x2 * sin_t, x2 * cos_t + x1 * sin_t]
    kt = jnp.concatenate(k_parts, axis=0)
    vt = kvt[KV_DIM:]
    kt_ref[...] = kt.astype(kt_ref.dtype)
    vt_ref[...] = vt.astype(vt_ref.dtype)
    if keep_cols:
        kt32_ref, vt32_ref = outs[4:]
        kt32_ref[0] = kt[:, t - keep_cols:]
        vt32_ref[0] = vt[:, t - keep_cols:]


def _in_att(x2d, layer, j, g_mix, wq, wkv_t, gq, gx, gk_cols, cos, sin, cos_t, sin_t,
            *, tiles_per_seq, keep_cols, act_dtype, convert=()):
    n = x2d.shape[0]
    tm = min(ATT_TILE, n)
    nt = n // tm
    nseq = nt // tiles_per_seq
    row = lambda i: (i, 0)
    col = lambda i: (0, i)
    pos = lambda i: (i % tiles_per_seq, 0)
    pos_t = lambda i: (0, i % tiles_per_seq)
    out_specs = [
        pl.BlockSpec((tm, CONV_CH), row),
        pl.BlockSpec((tm, X_DIM), row),
        pl.BlockSpec((KV_DIM, tm), col),
        pl.BlockSpec((KV_DIM, tm), col),
    ]
    out_shape = [
        jax.ShapeDtypeStruct((n, CONV_CH), act_dtype),
        jax.ShapeDtypeStruct((n, X_DIM), act_dtype),
        jax.ShapeDtypeStruct((KV_DIM, n), act_dtype),
        jax.ShapeDtypeStruct((KV_DIM, n), act_dtype),
    ]
    if keep_cols:
        out_specs += [pl.BlockSpec((1, KV_DIM, keep_cols), lambda i: (i // tiles_per_seq, 0, 0))] * 2
        out_shape += [jax.ShapeDtypeStruct((nseq, KV_DIM, keep_cols), F32)] * 2
    cv_in, cv_args, cv_out, cv_shape = _convert_specs(convert, nt)
    return pl.pallas_call(
        functools.partial(_in_att_kernel, keep_cols=keep_cols, n_convert=len(convert)),
        grid=(nt,),
        in_specs=[
            pl.BlockSpec((tm, D_MODEL), row),
            _layer_spec((1, D_MODEL), layer),
            _layer_spec((D_MODEL, CONV_CH + X_DIM), j),
            _layer_spec((2 * KV_DIM, D_MODEL), j),
            _layer_spec((1, CONV_CH), j),
            _layer_spec((1, X_DIM), layer),
            _layer_spec((HEAD_DIM, LANES), j),
            pl.BlockSpec((tm, LANES), pos),
            pl.BlockSpec((tm, LANES), pos),
            pl.BlockSpec((HEAD_DIM // 2, tm), pos_t),
            pl.BlockSpec((HEAD_DIM // 2, tm), pos_t),
        ] + cv_in,
        out_specs=out_specs + cv_out,
        out_shape=out_shape + cv_shape,
        name="in_att",
    )(x2d, g_mix, wq, wkv_t, gq, gx, gk_cols, cos, sin, cos_t, sin_t, *cv_args)


def _in_conv_kernel(*refs, tiles_per_seq, period, xattn):
    if xattn:
        (x_ref, g_ref, w_ref, gx_ref, cw_ref, prefix_ref, mk_ref, mv_ref,
         o_ref, qx_ref, gt_ref, ox_ref, gbuf, pbuf) = refs
    elif period:
        (x_ref, g_ref, w_ref, gx_ref, cw_ref, prefix_ref,
         o_ref, qx_ref, gt_ref, gbuf, pbuf) = refs
    else:
        (x_ref, g_ref, w_ref, gx_ref, cw_ref,
         o_ref, qx_ref, gt_ref, gbuf) = refs
    t = x_ref.shape[0]
    keep = gt_ref.shape[0]
    h = _rms_rows(x_ref[...], g_ref[0]).astype(BF16)

    def proj(lo, width):
        return jnp.dot(h, w_ref[:, lo:lo + width], preferred_element_type=F32)

    @pl.when(pl.program_id(0) % tiles_per_seq == 0)
    def _():
        gbuf[0:SUBLANES, :] = jnp.zeros((SUBLANES, CONV_CH), F32)

    if period:
        pbuf[0:t, :] = prefix_ref[...]
        pbuf[t:t + SUBLANES, :] = jnp.zeros((SUBLANES, CONV_CH), F32)
        tok = lax.broadcasted_iota(jnp.int32, (t, MXU_DIM), 0) % period
    cw = cw_ref[0]
    for c in range(CONV_CH // MXU_DIM):
        sl = slice(c * MXU_DIM, (c + 1) * MXU_DIM)
        gate_b = proj(sl.start, MXU_DIM)
        u = proj(CONV_CH + sl.start, MXU_DIM) * proj(2 * CONV_CH + sl.start, MXU_DIM)
        gbuf[SUBLANES:SUBLANES + t, sl] = u
        back1 = gbuf[SUBLANES - 1:SUBLANES - 1 + t, sl]
        back2 = gbuf[SUBLANES - 2:SUBLANES - 2 + t, sl]
        if period:
            back1 = jnp.where(tok >= 1, back1, pbuf[1:1 + t, sl])
            back2 = jnp.where(tok >= 2, back2, pbuf[0:t, sl])
        y = back2 * cw[0:1, sl] + back1 * cw[1:2, sl] + u * cw[2:3, sl]
        o_ref[:, sl] = (gate_b * y).astype(o_ref.dtype)
        gt_ref[:, sl] = u[t - keep:, :]
    gbuf[0:SUBLANES, :] = gbuf[t:t + SUBLANES, :]
    qx = proj(3 * CONV_CH, X_DIM)
    qx = qx * lax.rsqrt(_head_mean_sq(qx, _head_blockdiag()) + EPS) * gx_ref[0]
    qx_ref[...] = (qx * Q_SCALE).astype(qx_ref.dtype)
    if xattn:
        _xattn_sample_kernel(qx_ref, mk_ref, mv_ref, ox_ref, dec=period)


def _in_conv(x2d, layer, j, g_mix, w, gx, cw, prefix, *, tiles_per_seq, period, keep_rows, act_dtype, xattn=None):
    n = x2d.shape[0]
    tm = SEQ_BLOCK * period if xattn else min(MIX_TILE, n)
    nt = n // tm
    nseq = nt // tiles_per_seq
    row = lambda i: (i, 0)
    in_specs = [
        pl.BlockSpec((tm, D_MODEL), row),
        _layer_spec((1, D_MODEL), layer),
        _const_spec((D_MODEL, CONV_IN)),
        _layer_spec((1, X_DIM), layer),
        _layer_spec((CONV_W, CONV_CH), j),
    ]
    args = [x2d, g_mix, w, gx, cw]
    scratch = [pltpu.VMEM((tm + 2 * SUBLANES, CONV_CH), F32)]
    if period:
        in_specs.append(pl.BlockSpec((tm, CONV_CH), row))
        args.append(prefix)
        scratch.append(pltpu.VMEM((tm + SUBLANES, CONV_CH), F32))
    out_specs = [
        pl.BlockSpec((tm, CONV_CH), row),
        pl.BlockSpec((tm, X_DIM), row),
        pl.BlockSpec((keep_rows, CONV_CH), lambda i: (i // tiles_per_seq, 0)),
    ]
    out_shape = [
        jax.ShapeDtypeStruct((n, CONV_CH), act_dtype),
        jax.ShapeDtypeStruct((n, X_DIM), act_dtype),
        jax.ShapeDtypeStruct((nseq * keep_rows, CONV_CH), F32),
    ]
    if xattn:
        mk, mv = xattn
        mem = lambda i: (layer, i, 0, 0)
        in_specs += [pl.BlockSpec((1, SEQ_BLOCK, X_DIM, mk.shape[3]), mem)] * 2
        args += [mk, mv]
        out_specs.append(pl.BlockSpec((tm, X_DIM), row))
        out_shape.append(jax.ShapeDtypeStruct((n, X_DIM), F32))
    return pl.pallas_call(
        functools.partial(_in_conv_kernel, tiles_per_seq=tiles_per_seq, period=period, xattn=xattn is not None),
        grid=(nt,),
        in_specs=in_specs,
        out_specs=out_specs,
        out_shape=out_shape,
        scratch_shapes=scratch,
        name="in_conv",
    )(*args)


def _pair_lhs(slabs):
    low = lax.broadcasted_iota(jnp.int32, slabs[0].shape, 1) < HEAD_DIM
    zero = jnp.zeros_like(slabs[0])
    parts = []
    for s in slabs:
        parts += [jnp.where(low, s, zero), jnp.where(low, zero, s)]
    return jnp.concatenate(parts, axis=0)


def _swa_prompt_kernel(sinks_ref, q_ref, kc_ref, kp_ref, vc_ref, vp_ref, o_ref, kfull, vfull, bias):
    tq = q_ref.shape[0]

    @pl.when(jnp.logical_and(pl.program_id(0) == 0, pl.program_id(1) == 0))
    def _():
        r = lax.broadcasted_iota(jnp.int32, (WINDOW, 2 * WINDOW), 0)
        c = lax.broadcasted_iota(jnp.int32, (WINDOW, 2 * WINDOW), 1)
        band = (c > r) & (c - WINDOW <= r)
        for kind, visible in enumerate((band, band & (c >= WINDOW))):
            base = jnp.where(visible, 0.0, NEG)
            for h in range(N_Q_HEADS):
                bias[kind, h] = jnp.where(c == 0, sinks_ref[h] * LOG2E, base)

    first_kind = jnp.where(pl.program_id(1) == 0, 1, 0)
    kfull[:, 0:WINDOW] = kp_ref[...]
    kfull[:, WINDOW:] = kc_ref[...]
    vfull[:, 0:WINDOW] = vp_ref[...]
    vfull[:, WINDOW:] = vc_ref[...]
    col0 = lax.broadcasted_iota(jnp.int32, (PAIR, 2 * WINDOW), 1) == 0
    zero_slab = jnp.zeros((PAIR, 2 * WINDOW), BF16)
    low = lax.broadcasted_iota(jnp.int32, (WINDOW, PAIR), 1) < HEAD_DIM
    for qb in range(tq // WINDOW):
        r0 = qb * WINDOW
        kind = first_kind if qb == 0 else 0
        slabs = [None] * (GQA_GROUP * KV_DIM // PAIR)
        for pr in range(KV_DIM // PAIR):
            kslab = jnp.where(col0, zero_slab, kfull[pr * PAIR:(pr + 1) * PAIR, r0:r0 + 2 * WINDOW])
            vslab = jnp.where(col0, zero_slab, vfull[pr * PAIR:(pr + 1) * PAIR, r0:r0 + 2 * WINDOW])
            lhs = _pair_lhs([q_ref[r0:r0 + WINDOW, g * KV_DIM + pr * PAIR:g * KV_DIM + (pr + 1) * PAIR]
                             for g in range(GQA_GROUP)])
            s_all = jnp.dot(lhs, kslab, preferred_element_type=F32)
            probs, dens = [], []
            for g in range(GQA_GROUP):
                for e in range(2):
                    idx = g * 2 + e
                    head = (2 * pr + e) * GQA_GROUP + g
                    p, den = _softmax_rows(s_all[idx * WINDOW:(idx + 1) * WINDOW] + bias[kind, head])
                    probs.append(p)
                    dens.append(den)
            res = lax.dot_general(jnp.concatenate(probs, axis=0), vslab, _NT,
                                  preferred_element_type=F32)
            for g in range(GQA_GROUP):
                slabs[g * (KV_DIM // PAIR) + pr] = _pair_normalise(
                    low, res[(2 * g) * WINDOW:(2 * g + 1) * WINDOW], res[(2 * g + 1) * WINDOW:(2 * g + 2) * WINDOW],
                    dens[2 * g], dens[2 * g + 1])
        o_ref[r0:r0 + WINDOW, :] = jnp.concatenate(slabs, axis=1).astype(o_ref.dtype)


def _swa_prompt(sinks, q, kt, vt, *, batch, seq):
    tq = MIX_TILE
    nt = seq // tq
    per = tq // WINDOW
    rows = lambda b, j: (b * nt + j, 0)
    cur = lambda b, j: (0, b * nt + j)
    prev = lambda b, j: (0, b * nt * per + jnp.maximum(j * per - 1, 0))
    return pl.pallas_call(
        _swa_prompt_kernel,
        grid=(batch, nt),
        in_specs=[
            pl.BlockSpec(memory_space=pltpu.SMEM),
            pl.BlockSpec((tq, CONV_CH), rows),
            pl.BlockSpec((KV_DIM, tq), cur),
            pl.BlockSpec((KV_DIM, WINDOW), prev),
            pl.BlockSpec((KV_DIM, tq), cur),
            pl.BlockSpec((KV_DIM, WINDOW), prev),
        ],
        out_specs=pl.BlockSpec((tq, CONV_CH), rows),
        out_shape=jax.ShapeDtypeStruct(q.shape, BF16),
        scratch_shapes=[pltpu.VMEM((KV_DIM, tq + WINDOW), BF16)] * 2
        + [pltpu.VMEM((2, N_Q_HEADS, WINDOW, 2 * WINDOW), F32)],
        name="swa_prompt",
    )(sinks, q, kt, kt, vt, vt)


def _swa_sample_kernel(*refs, dec, carried, slot):
    if carried:
        sinks_ref, q_ref, kn_ref, vn_ref, kb_ref, vb_ref, _, _, o_ref, ko_ref, vo_ref = refs
    else:
        sinks_ref, q_ref, kn_ref, vn_ref, kb_ref, vb_ref, o_ref, ko_ref, vo_ref = refs
    nseq = kb_ref.shape[1]
    for other in range(ko_ref.shape[0]):
        if other != slot:
            ko_ref[other] = jnp.zeros(ko_ref.shape[1:], F32)
            vo_ref[other] = jnp.zeros(vo_ref.shape[1:], F32)
    steps_per_tile = LANES // (nseq * dec)
    base = (pl.program_id(0) % steps_per_tile) * (nseq * dec)
    grp = N_KV_HEADS * dec
    rows = GQA_GROUP * grp
    ri = lax.broadcasted_iota(jnp.int32, (rows, KV_DIM), 0)
    li = lax.broadcasted_iota(jnp.int32, (rows, KV_DIM), 1)
    head_mask = ((ri % grp) // dec) == (li // HEAD_DIM)
    tok = lax.broadcasted_iota(jnp.int32, (rows, WINDOW + LANES), 0) % dec
    col = lax.broadcasted_iota(jnp.int32, (rows, WINDOW + LANES), 1)
    cache_mask = (col < WINDOW) & (col > tok)
    lane = lax.broadcasted_iota(jnp.int32, (KV_DIM, WINDOW), 1)
    sink = jnp.concatenate(
        [jnp.full((dec, 1), sinks_ref[kv * GQA_GROUP + g] * LOG2E, F32)
         for g in range(GQA_GROUP) for kv in range(N_KV_HEADS)], axis=0)
    kn = kn_ref[...]
    vn = vn_ref[...]
    kn16 = kn.astype(BF16)
    vn16 = vn.astype(BF16)
    for s0 in range(0, nseq, SEQ_GROUP):
        seqs = range(s0, s0 + SEQ_GROUP)
        scores, values = [], []
        for s in seqs:
            off = base + s * dec
            new = col - (WINDOW + off)
            mask = cache_mask | ((new >= 0) & (new <= tok))
            qs = q_ref[pl.ds(s * dec, dec), :]
            qbd = jnp.concatenate(
                [jnp.concatenate([qs[:, g * KV_DIM:(g + 1) * KV_DIM]] * N_KV_HEADS, axis=0)
                 for g in range(GQA_GROUP)], axis=0)
            qbd = jnp.where(head_mask, qbd, 0.0).astype(BF16)
            keys = jnp.concatenate([kb_ref[0, s].astype(BF16), kn16], axis=1)
            values.append(jnp.concatenate([vb_ref[0, s].astype(BF16), vn16], axis=1))
            scores.append(jnp.where(mask, jnp.dot(qbd, keys, preferred_element_type=F32), NEG))
        probs = [_softmax_rows(sc, sink) for sc in scores]
        for s, (p, den), vals in zip(seqs, probs, values):
            o = lax.dot_general(p, vals, _NT, preferred_element_type=F32) * (1.0 / den)
            o = jnp.where(head_mask, o, 0.0)
            folded = []
            for g in range(GQA_GROUP):
                og = o[g * grp:g * grp + dec, :]
                for kv in range(1, N_KV_HEADS):
                    og = og + o[g * grp + kv * dec:g * grp + (kv + 1) * dec, :]
                folded.append(og)
            o_ref[pl.ds(s * dec, dec), :] = jnp.concatenate(folded, axis=1)
        for s in seqs:
            shift_new = (WINDOW - dec + LANES - (base + s * dec)) % LANES
            ko_ref[slot, s] = jnp.where(lane < WINDOW - dec, pltpu.roll(kb_ref[0, s], WINDOW - dec, axis=1),
                                     pltpu.roll(kn, shift_new, axis=1))
            vo_ref[slot, s] = jnp.where(lane < WINDOW - dec, pltpu.roll(vb_ref[0, s], WINDOW - dec, axis=1),
                                     pltpu.roll(vn, shift_new, axis=1))


def _att_sample_kernel(*refs, dec, carried, slot):
    n_in = 8 if carried else 6
    qx_ref, mk_ref, mv_ref = refs[n_in:n_in + 3]
    o_ref, ko_ref, vo_ref, ox_ref = refs[n_in + 3:]
    _xattn_sample_kernel(qx_ref, mk_ref, mv_ref, ox_ref, dec=dec)
    _swa_sample_kernel(*refs[:n_in], o_ref, ko_ref, vo_ref, dec=dec, carried=carried, slot=slot)


def _swa_sample(sinks, q, kt_new, vt_new, kbuf, vbuf, j, carry, *, dec, xattn=None):
    nseq = kbuf.shape[1]
    sb = SEQ_BLOCK
    steps_per_tile = LANES // (sb * dec)
    row = lambda i: (i, 0)
    tile = lambda i: (0, i // steps_per_tile)
    cache = lambda i: (j, i, 0, 0)
    if carry is None:
        new_cache = pl.BlockSpec((kbuf.shape[0], sb, KV_DIM, WINDOW), lambda i: (0, i, 0, 0))
    else:
        new_cache = pl.BlockSpec((1, sb, KV_DIM, WINDOW), cache)
    in_specs = [
        pl.BlockSpec(memory_space=pltpu.SMEM),
        pl.BlockSpec((sb * dec, CONV_CH), row),
        pl.BlockSpec((KV_DIM, LANES), tile),
        pl.BlockSpec((KV_DIM, LANES), tile),
        pl.BlockSpec((1, sb, KV_DIM, WINDOW), cache),
        pl.BlockSpec((1, sb, KV_DIM, WINDOW), cache),
    ]
    args = [sinks, q, kt_new, vt_new, kbuf, vbuf]
    aliases = {}
    if carry is not None:
        aliases = {len(args): 1, len(args) + 1: 2}
        in_specs += [pl.BlockSpec(memory_space=pl.ANY)] * 2
        args += list(carry)
    out_specs = [pl.BlockSpec((sb * dec, CONV_CH), row), new_cache, new_cache]
    out_shape = [jax.ShapeDtypeStruct(q.shape, F32), jax.ShapeDtypeStruct(kbuf.shape, F32),
                 jax.ShapeDtypeStruct(vbuf.shape, F32)]
    body = _swa_sample_kernel
    if xattn is not None:
        qx, mk, mv, layer = xattn
        mem = lambda i: (layer, i, 0, 0)
        in_specs += [pl.BlockSpec((sb * dec, X_DIM), row)] + [pl.BlockSpec((1, sb, X_DIM, mk.shape[3]), mem)] * 2
        args += [qx, mk, mv]
        out_specs.append(pl.BlockSpec((sb * dec, X_DIM), row))
        out_shape.append(jax.ShapeDtypeStruct(qx.shape, F32))
        body = _att_sample_kernel
    return pl.pallas_call(
        functools.partial(body, dec=dec, carried=carry is not None, slot=j if carry is None else 0),
        grid=(nseq // sb,),
        in_specs=in_specs,
        out_specs=out_specs,
        out_shape=out_shape,
        input_output_aliases=aliases,
        name="swa_sample",
    )(*args)


def _xattn_tile(q_ref, mk_ref, mv_ref):
    tq = q_ref.shape[0]
    mk = mk_ref[0, 0].astype(BF16)
    mv = mv_ref[0, 0].astype(BF16)
    low = lax.broadcasted_iota(jnp.int32, (WINDOW, PAIR), 1) < HEAD_DIM
    blocks = []
    for rb in range(tq // WINDOW):
        r0 = rb * WINDOW
        slabs = []
        for pr in range(X_DIM // PAIR):
            lhs = _pair_lhs([q_ref[r0:r0 + WINDOW, pr * PAIR:(pr + 1) * PAIR]])
            s_all = jnp.dot(lhs, mk[pr * PAIR:(pr + 1) * PAIR], preferred_element_type=F32)
            p0, den0 = _softmax_rows(s_all[:WINDOW])
            p1, den1 = _softmax_rows(s_all[WINDOW:])
            res = lax.dot_general(jnp.concatenate([p0, p1], axis=0), mv[pr * PAIR:(pr + 1) * PAIR], _NT,
                                  preferred_element_type=F32)
            slabs.append(_pair_normalise(low, res[:WINDOW], res[WINDOW:], den0, den1))
        blocks.append(jnp.concatenate(slabs, axis=1).astype(BF16))
    return jnp.concatenate(blocks, axis=0)


def _xattn_sample_kernel(q_ref, mk_ref, mv_ref, o_ref, *, dec):
    nseq = mk_ref.shape[1]
    rows = N_X_HEADS * dec
    head_mask = (lax.broadcasted_iota(jnp.int32, (rows, X_DIM), 0) // dec) == (
        lax.broadcasted_iota(jnp.int32, (rows, X_DIM), 1) // HEAD_DIM)
    for s0 in range(0, nseq, SEQ_GROUP):
        seqs = range(s0, s0 + SEQ_GROUP)
        scores = []
        for s in seqs:
            qs = q_ref[pl.ds(s * dec, dec), :]
            qbd = jnp.where(head_mask, jnp.concatenate([qs] * N_X_HEADS, axis=0), 0.0).astype(BF16)
            scores.append(jnp.dot(qbd, mk_ref[0, s].astype(BF16), preferred_element_type=F32))
        probs = [_softmax_rows(sc) for sc in scores]
        for s, (p, den) in zip(seqs, probs):
            o = lax.dot_general(p, mv_ref[0, s].astype(BF16), _NT, preferred_element_type=F32) * (1.0 / den)
            o = jnp.where(head_mask, o, 0.0)
            acc = o[0:dec, :]
            for h in range(1, N_X_HEADS):
                acc = acc + o[h * dec:(h + 1) * dec, :]
            o_ref[pl.ds(s * dec, dec), :] = acc


def _mlp_tile(x, o, wo, gm, wu_ref, wd_ref):
    x1 = x + jnp.dot(o, wo, preferred_element_type=F32)
    hm = _rms_rows(x1, gm).astype(BF16)
    acc = x1
    for c in range(D_FF // FF_CHUNK):
        a = jnp.dot(hm, wu_ref[:, c * FF_CHUNK:(c + 1) * FF_CHUNK], preferred_element_type=F32)
        a = jnp.square(jnp.maximum(a, 0.0)).astype(BF16)
        acc = acc + jnp.dot(a, wd_ref[c * FF_CHUNK:(c + 1) * FF_CHUNK, :], preferred_element_type=F32)
    return acc


def _out_mlp_kernel(*refs, n_prompt_tiles, n_convert):
    (xp_ref, osp_ref, qxp_ref, mk_ref, mv_ref, xs_ref, oss_ref, oxs_ref,
     wo_ref, gm_ref, wu_ref, wd_ref) = refs[:12]
    yp_ref, ys_ref = refs[12 + n_convert:14 + n_convert]
    for src, dst in zip(refs[12:12 + n_convert], refs[14 + n_convert:]):
        dst[...] = src[0].astype(BF16)
    is_sample = pl.program_id(0) == n_prompt_tiles

    @pl.when(jnp.logical_not(is_sample))
    def _():
        o = jnp.concatenate([osp_ref[...], _xattn_tile(qxp_ref, mk_ref, mv_ref)], axis=1)
        yp_ref[...] = _mlp_tile(xp_ref[...], o, wo_ref[...], gm_ref[0], wu_ref, wd_ref)

    @pl.when(is_sample)
    def _():
        o = jnp.concatenate([oss_ref[...].astype(BF16), oxs_ref[...].astype(BF16)], axis=1)
        ys_ref[...] = _mlp_tile(xs_ref[...], o, wo_ref[...], gm_ref[0], wu_ref, wd_ref)


def _out_mlp(xp, os_p, qx_p, memory, xs, os_s, ox_s, layer, wo, gm, wu, wd, *, tiles_per_batch, convert=()):
    n = xp.shape[0]
    tm = TOKEN_TILE
    assert xs.shape[0] == tm
    npt = n // tm
    n_mem = memory[0].shape[3]
    last = npt - 1
    prow = lambda i: (jnp.minimum(i, last), 0)
    mem = lambda i: (layer, jnp.minimum(i, last) // tiles_per_batch, 0, 0)
    cv_in, cv_args, cv_out, cv_shape = _convert_specs(convert, npt, last)
    return pl.pallas_call(
        functools.partial(_out_mlp_kernel, n_prompt_tiles=npt, n_convert=len(convert)),
        grid=(npt + 1,),
        in_specs=[
            pl.BlockSpec((tm, D_MODEL), prow),
            pl.BlockSpec((tm, CONV_CH), prow),
            pl.BlockSpec((tm, X_DIM), prow),
            pl.BlockSpec((1, 1, X_DIM, n_mem), mem),
            pl.BlockSpec((1, 1, X_DIM, n_mem), mem),
            _const_spec((tm, D_MODEL)),
            _const_spec((tm, CONV_CH)),
            _const_spec((tm, X_DIM)),
            _const_spec((CONV_CH + X_DIM, D_MODEL)),
            _layer_spec((1, D_MODEL), layer),
            _const_spec((D_MODEL, D_FF)),
            _const_spec((D_FF, D_MODEL)),
        ] + cv_in,
        out_specs=[pl.BlockSpec((tm, D_MODEL), prow), pl.BlockSpec((tm, D_MODEL), lambda i: (0, 0))] + cv_out,
        out_shape=[jax.ShapeDtypeStruct((n, D_MODEL), F32), jax.ShapeDtypeStruct((tm, D_MODEL), F32)] + cv_shape,
        name="out_mlp",
    )(xp, os_p, qx_p, *memory, xs, os_s, ox_s, wo, gm, wu, wd, *cv_args)


def _rope_tables(pos):
    half = HEAD_DIM // 2
    inv = ROPE_THETA ** (-jnp.arange(half, dtype=F32) * 2.0 / HEAD_DIM)
    ang = pos[:, None] * inv[None, :]
    return jnp.cos(ang), jnp.sin(ang)


def _rope_lane_tables(cos, sin):
    reps = LANES // (HEAD_DIM // 2)
    return jnp.tile(cos, (1, reps)), jnp.concatenate([-sin, sin] * (reps // 2), axis=1)


def _group_major(w, axis):
    shape = w.shape
    w = w.reshape(shape[:axis] + (N_KV_HEADS, GQA_GROUP, HEAD_DIM) + shape[axis + 1:])
    return jnp.swapaxes(w, axis, axis + 1).reshape(shape)


def _feature_major(cache):
    lead = cache.shape[:-3]
    pos, heads, hd = cache.shape[-3:]
    nd = len(lead)
    perm = tuple(range(nd)) + (nd + 1, nd + 2, nd)
    return jnp.transpose(cache, perm).reshape(lead + (heads * hd, pos))


def _position_major(cache_t, heads):
    lead = cache_t.shape[:-2]
    pos = cache_t.shape[-1]
    nd = len(lead)
    perm = tuple(range(nd)) + (nd + 2, nd, nd + 1)
    return jnp.transpose(cache_t.reshape(lead + (heads, HEAD_DIM, pos)), perm)


def _gain_cols(g):
    return jnp.broadcast_to(g[:, :, None], g.shape + (LANES,))


def kernel(x_prompt, x_sample, mem_prompt, cache_swa_k, cache_swa_v, state_conv, cache_mem_k, cache_mem_v,
           norm_mix, w_in_att, q_norm_att, k_norm_att, sinks, w_in_conv, conv_w, norm_mem, w_mem_kv,
           q_norm_x, k_norm_x, w_out, norm_mlp, w_up, w_down):
    batch, seq, _ = x_prompt.shape
    nseq, dec, _ = x_sample.shape
    n_mem = mem_prompt.shape[1]
    assert seq % MIX_TILE == 0 and seq % ATT_TILE == 0
    assert (nseq * dec) == TOKEN_TILE and nseq % SEQ_BLOCK == 0
    assert LANES % (SEQ_BLOCK * dec) == 0

    kv0, kv1 = CONV_CH, CONV_CH + 2 * KV_DIM
    w_q = jnp.concatenate([_group_major(w_in_att[:, :, :kv0], 2), w_in_att[:, :, kv1:]], axis=2).astype(BF16)
    w_kv_t = jnp.swapaxes(w_in_att[:, :, kv0:kv1], 1, 2).astype(BF16)
    w_o = [jnp.concatenate([_group_major(w_out[i, :CONV_CH], 0), w_out[i, CONV_CH:]], axis=0).astype(BF16)
           if i % 2 == 0 else w_out[i].astype(BF16) for i in range(DEPTH)]
    w_u = w_d = w_conv = None
    w_mkv_t = jnp.swapaxes(w_mem_kv, 1, 2).astype(BF16)

    g_mix = norm_mix[:, None, :]
    g_mlp = norm_mlp[:, None, :]
    gq = jnp.tile(q_norm_att, (1, N_Q_HEADS))[:, None, :]
    gx = jnp.tile(q_norm_x, (1, N_X_HEADS))[:, None, :]
    gk_cols = _gain_cols(k_norm_att)
    gkx_cols = _gain_cols(k_norm_x)

    cos_p, sin_p = _rope_tables(jnp.arange(seq, dtype=F32))
    cos_s, sin_s = _rope_tables(PAST_LEN + (jnp.arange(nseq * dec) % dec).astype(F32))
    rope_p = _rope_lane_tables(cos_p, sin_p) + (cos_p.T, sin_p.T)
    rope_s = _rope_lane_tables(cos_s, sin_s) + (cos_s.T, sin_s.T)

    mk_p, mv_p = _mem_kv(mem_prompt.reshape(batch * n_mem, D_MODEL), norm_mem[:, None, :], w_mkv_t, gkx_cols,
                         batch=batch, n_mem=n_mem)
    mk_s = _feature_major(cache_mem_k)
    mv_s = _feature_major(cache_mem_v)
    kbuf = _feature_major(cache_swa_k)
    vbuf = _feature_major(cache_swa_v)

    xp = x_prompt.reshape(batch * seq, D_MODEL)
    xs = x_sample.reshape(nseq * dec, D_MODEL)
    k_p, v_p, c_p, c_s = [], [], [], []
    cache_s = None
    for i in range(DEPTH):
        j = i // 2
        if i % 2 == 0:
            first = [(w_up, 0), (w_down, 0)] if i == 0 else []
            q, qx_p, kt, vt, kt32, vt32, *w_first = _in_att(
                xp, i, j, g_mix, w_q, w_kv_t, gq, gx, gk_cols, *rope_p,
                tiles_per_seq=seq // ATT_TILE, keep_cols=WINDOW, act_dtype=BF16, convert=first)
            if w_first:
                w_u, w_d = w_first
            k_p.append(kt32)
            v_p.append(vt32)
            os_p = _swa_prompt(sinks[j], q, kt, vt, batch=batch, seq=seq)
            q, qx_s, kt, vt = _in_att(xs, i, j, g_mix, w_q, w_kv_t, gq, gx, gk_cols, *rope_s,
                                      tiles_per_seq=1, keep_cols=0, act_dtype=F32)
            os_s, *cache_s, ox_s = _swa_sample(sinks[j], q, kt, vt, kbuf, vbuf, j, cache_s, dec=dec,
                                               xattn=(qx_s, mk_s, mv_s, i))
        else:
            os_p, qx_p, tail = _in_conv(xp, i, j, g_mix, w_conv, gx, conv_w, None,
                                        tiles_per_seq=seq // MIX_TILE, period=0,
                                        keep_rows=SUBLANES, act_dtype=BF16)
            c_p.append(tail.reshape(batch, SUBLANES, CONV_CH)[:, SUBLANES - (CONV_W - 1):])
            prefix = jnp.pad(state_conv[j], ((0, 0), (0, dec - (CONV_W - 1)), (0, 0)))
            os_s, _, tail, ox_s = _in_conv(xs, i, j, g_mix, w_conv, gx, conv_w,
                                           prefix.reshape(nseq * dec, CONV_CH),
                                           tiles_per_seq=1, period=dec,
                                           keep_rows=SEQ_BLOCK * dec, act_dtype=F32, xattn=(mk_s, mv_s))
            c_s.append(tail.reshape(nseq, dec, CONV_CH)[:, dec - (CONV_W - 1):])
        convert = []
        if i + 1 < DEPTH:
            convert = [(w_up, i + 1), (w_down, i + 1)] + ([(w_in_conv, (i + 1) // 2)] if i % 2 == 0 else [])
        xp, xs, *w_next = _out_mlp(xp, os_p, qx_p, (mk_p, mv_p), xs, os_s, ox_s, i, w_o[i], g_mlp, w_u, w_d,
                                   tiles_per_batch=seq // TOKEN_TILE, convert=convert)
        if w_next:
            w_u, w_d = w_next[:2]
            w_conv = w_next[2] if len(w_next) > 2 else None

    return (xp.reshape(batch, seq, D_MODEL),
            xs.reshape(nseq, dec, D_MODEL),
            _position_major(jnp.stack(k_p), N_KV_HEADS),
            _position_major(jnp.stack(v_p), N_KV_HEADS),
            jnp.stack(c_p),
            _position_major(mk_p, N_X_HEADS),
            _position_major(mv_p, N_X_HEADS),
            _position_major(cache_s[0], N_KV_HEADS),
            _position_major(cache_s[1], N_KV_HEADS),
            jnp.stack(c_s))
```

```python
import functools

import jax
import jax.numpy as jnp
from jax import lax
from jax.experimental import pallas as pl
from jax.experimental.pallas import tpu as pltpu

F32 = jnp.float32
BF16 = jnp.bfloat16

D_MODEL = 1024
DEPTH = 4
HEAD_DIM = 64
N_Q_HEADS = 12
N_KV_HEADS = 4
GQA_GROUP = N_Q_HEADS // N_KV_HEADS
WINDOW = 128
PAST_LEN = 8192
ROPE_THETA = 10000.0
CONV_CH = N_Q_HEADS * HEAD_DIM
CONV_W = 3
N_X_HEADS = 4
X_DIM = N_X_HEADS * HEAD_DIM
KV_DIM = N_KV_HEADS * HEAD_DIM
D_FF = 4 * D_MODEL
CONV_IN = 3 * CONV_CH + X_DIM
EPS = 1e-6
NEG = -1e30
LOG2E = 1.4426950408889634
Q_SCALE = HEAD_DIM ** -0.5 * LOG2E

LANES = 128
SUBLANES = 8
MXU_DIM = 256
TOKEN_TILE = 512
ATT_TILE = 1024
MIX_TILE = 2048
FF_CHUNK = 2048
SEQ_BLOCK = 16
CONV_SEQ_BLOCK = 32
SEQ_GROUP = 16
PAIR = 2 * HEAD_DIM

_NT = (((1,), (1,)), ((), ()))


def _const_spec(shape):
    nd = len(shape)
    return pl.BlockSpec(shape, lambda *_: (0,) * nd, pipeline_mode=pl.Buffered(1))


def _layer_spec(shape, layer):
    nd = len(shape)
    return pl.BlockSpec((1,) + shape, lambda *_: (layer,) + (0,) * nd, pipeline_mode=pl.Buffered(1))


def _rms_rows(x, g):
    return x * lax.rsqrt(jnp.mean(x * x, axis=-1, keepdims=True) + EPS) * g


def _head_blockdiag():
    r = lax.broadcasted_iota(jnp.int32, (MXU_DIM, MXU_DIM), 0) // HEAD_DIM
    c = lax.broadcasted_iota(jnp.int32, (MXU_DIM, MXU_DIM), 1) // HEAD_DIM
    return jnp.where(r == c, 1.0, 0.0).astype(BF16)


def _head_mean_sq(z, bd):
    sq = z * z
    hi = sq.astype(BF16)
    lo = (sq - hi.astype(F32)).astype(BF16)
    parts = []
    for c in range(z.shape[1] // MXU_DIM):
        sl = slice(c * MXU_DIM, (c + 1) * MXU_DIM)
        parts.append(jnp.dot(hi[:, sl], bd, preferred_element_type=F32)
                     + jnp.dot(lo[:, sl], bd, preferred_element_type=F32))
    ms = parts[0] if len(parts) == 1 else jnp.concatenate(parts, axis=1)
    return ms * (1.0 / HEAD_DIM)


def _rope_rows(x, cos, sin_signed):
    t = x.shape[0]
    first_half = (lax.broadcasted_iota(jnp.int32, (t, LANES), 1) & (HEAD_DIM - 1)) < HEAD_DIM // 2
    parts = []
    for c in range(x.shape[1] // LANES):
        xc = x[:, c * LANES:(c + 1) * LANES]
        partner = jnp.where(first_half,
                            pltpu.roll(xc, LANES - HEAD_DIM // 2, axis=1),
                            pltpu.roll(xc, HEAD_DIM // 2, axis=1))
        parts.append(xc * cos + partner * sin_signed)
    return jnp.concatenate(parts, axis=1)


def _lane_tile(col, n):
    return col if n == LANES else jnp.concatenate([col] * (n // LANES), axis=1)


def _head_norm_cols(xt, g_col):
    t = xt.shape[1]
    g = _lane_tile(g_col, t)
    parts = []
    for h in range(xt.shape[0] // HEAD_DIM):
        blk = xt[h * HEAD_DIM:(h + 1) * HEAD_DIM, :]
        ms = jnp.mean(blk * blk, axis=0, keepdims=True)
        parts.append(blk * lax.rsqrt(ms + EPS) * g)
    return parts


def _softmax_rows(s, sink=None):
    m = jnp.max(s, axis=-1, keepdims=True)
    if sink is not None:
        m = jnp.maximum(m, sink)
    p = jnp.exp2(s - m)
    den = jnp.sum(p, axis=-1, keepdims=True)
    if sink is not None:
        den = den + jnp.exp2(sink - m)
    return p.astype(BF16), den


def _pair_normalise(low, res_lo, res_hi, den_lo, den_hi):
    return jnp.where(low, res_lo, res_hi) / jnp.where(low, den_lo, den_hi)


def _convert_specs(convert, steps, last=None):
    last = steps - 1 if last is None else last
    in_specs, args, out_specs, out_shape = [], [], [], []
    for stack, l in convert:
        _, r, c = stack.shape
        in_specs.append(pl.BlockSpec((1, r // steps, c), lambda i, l=l: (l, jnp.minimum(i, last), 0)))
        args.append(stack)
        out_specs.append(pl.BlockSpec((r // steps, c), lambda i: (jnp.minimum(i, last), 0)))
        out_shape.append(jax.ShapeDtypeStruct((r, c), BF16))
    return in_specs, args, out_specs, out_shape


def _mem_kv_kernel(mem_ref, g_ref, wt_ref, gk_ref, mk_ref, mv_ref):
    x = mem_ref[...]
    xn = x * lax.rsqrt(jnp.mean(x * x, axis=-1, keepdims=True) + EPS)
    for i in range(DEPTH):
        h = (xn * g_ref[i]).astype(BF16)
        kvt = lax.dot_general(wt_ref[i], h, _NT, preferred_element_type=F32)
        mk_ref[i, 0] = jnp.concatenate(_head_norm_cols(kvt[:X_DIM], gk_ref[i]), axis=0)
        mv_ref[i, 0] = kvt[X_DIM:]


def _mem_kv(mem2d, norm_mem, w_mem_kv_t, gk_cols, *, batch, n_mem):
    out = jax.ShapeDtypeStruct((DEPTH, batch, X_DIM, n_mem), F32)
    return pl.pallas_call(
        _mem_kv_kernel,
        grid=(batch,),
        in_specs=[
            pl.BlockSpec((n_mem, D_MODEL), lambda b: (b, 0)),
            _const_spec((DEPTH, 1, D_MODEL)),
            _const_spec((DEPTH, 2 * X_DIM, D_MODEL)),
            _const_spec((DEPTH, HEAD_DIM, LANES)),
        ],
        out_specs=[pl.BlockSpec((DEPTH, 1, X_DIM, n_mem), lambda b: (0, b, 0, 0))] * 2,
        out_shape=[out, out],
        name="mem_kv",
    )(mem2d, norm_mem, w_mem_kv_t, gk_cols)


def _in_att_kernel(*refs, keep_cols, n_convert):
    (x_ref, g_ref, wq_ref, wkv_ref, gq_ref, gx_ref, gk_ref, cos_ref, sin_ref, cost_ref, sint_ref) = refs[:11]
    n_out = 6 if keep_cols else 4
    outs = refs[11 + n_convert:11 + n_convert + n_out]
    q_ref, qx_ref, kt_ref, vt_ref = outs[:4]
    for src, dst in zip(refs[11:11 + n_convert], refs[11 + n_convert + n_out:]):
        dst[...] = src[0].astype(BF16)
    t = x_ref.shape[0]
    h = _rms_rows(x_ref[...], g_ref[0]).astype(BF16)
    bd = _head_blockdiag()
    z = jnp.dot(h, wq_ref[0], preferred_element_type=F32)
    q = z[:, :CONV_CH]
    q = q * lax.rsqrt(_head_mean_sq(q, bd) + EPS) * gq_ref[0]
    q_ref[...] = (_rope_rows(q, cos_ref[...], sin_ref[...]) * Q_SCALE).astype(q_ref.dtype)
    qx = z[:, CONV_CH:]
    qx = qx * lax.rsqrt(_head_mean_sq(qx, bd) + EPS) * gx_ref[0]
    qx_ref[...] = (qx * Q_SCALE).astype(qx_ref.dtype)

    kvt = lax.dot_general(wkv_ref[0], h, _NT, preferred_element_type=F32)
    cos_t, sin_t = cost_ref[...], sint_ref[...]
    half = HEAD_DIM // 2
    k_parts = []
    for blk in _head_norm_cols(kvt[:KV_DIM], gk_ref[0]):
        x1, x2 = blk[:half], blk[half:]
        k_parts += [x1 * cos_t - x2 * sin_t, x2 * cos_t + x1 * sin_t]
    kt = jnp.concatenate(k_parts, axis=0)
    vt = kvt[KV_DIM:]
    kt_ref[...] = kt.astype(kt_ref.dtype)
    vt_ref[...] = vt.astype(vt_ref.dtype)
    if keep_cols:
        kt32_ref, vt32_ref = outs[4:]
        kt32_ref[0] = kt[:, t - keep_cols:]
        vt32_ref[0] = vt[:, t - keep_cols:]


def _in_att(x2d, layer, j, g_mix, wq, wkv_t, gq, gx, gk_cols, cos, sin, cos_t, sin_t,
            *, tiles_per_seq, keep_cols, act_dtype, convert=()):
    n = x2d.shape[0]
    tm = min(ATT_TILE, n)
    nt = n // tm
    nseq = nt // tiles_per_seq
    row = lambda i: (i, 0)
    col = lambda i: (0, i)
    pos = lambda i: (i % tiles_per_seq, 0)
    pos_t = lambda i: (0, i % tiles_per_seq)
    out_specs = [
        pl.BlockSpec((tm, CONV_CH), row),
        pl.BlockSpec((tm, X_DIM), row),
        pl.BlockSpec((KV_DIM, tm), col),
        pl.BlockSpec((KV_DIM, tm), col),
    ]
    out_shape = [
        jax.ShapeDtypeStruct((n, CONV_CH), act_dtype),
        jax.ShapeDtypeStruct((n, X_DIM), act_dtype),
        jax.ShapeDtypeStruct((KV_DIM, n), act_dtype),
        jax.ShapeDtypeStruct((KV_DIM, n), act_dtype),
    ]
    if keep_cols:
        out_specs += [pl.BlockSpec((1, KV_DIM, keep_cols), lambda i: (i // tiles_per_seq, 0, 0))] * 2
        out_shape += [jax.ShapeDtypeStruct((nseq, KV_DIM, keep_cols), F32)] * 2
    cv_in, cv_args, cv_out, cv_shape = _convert_specs(convert, nt)
    return pl.pallas_call(
        functools.partial(_in_att_kernel, keep_cols=keep_cols, n_convert=len(convert)),
        grid=(nt,),
        in_specs=[
            pl.BlockSpec((tm, D_MODEL), row),
            _layer_spec((1, D_MODEL), layer),
            _layer_spec((D_MODEL, CONV_CH + X_DIM), j),
            _layer_spec((2 * KV_DIM, D_MODEL), j),
            _layer_spec((1, CONV_CH), j),
            _layer_spec((1, X_DIM), layer),
            _layer_spec((HEAD_DIM, LANES), j),
            pl.BlockSpec((tm, LANES), pos),
            pl.BlockSpec((tm, LANES), pos),
            pl.BlockSpec((HEAD_DIM // 2, tm), pos_t),
            pl.BlockSpec((HEAD_DIM // 2, tm), pos_t),
        ] + cv_in,
        out_specs=out_specs + cv_out,
        out_shape=out_shape + cv_shape,
        name="in_att",
    )(x2d, g_mix, wq, wkv_t, gq, gx, gk_cols, cos, sin, cos_t, sin_t, *cv_args)


def _in_conv_kernel(*refs, tiles_per_seq, period, xattn):
    if xattn:
        (x_ref, g_ref, w_ref, gx_ref, cw_ref, prefix_ref, mk_ref, mv_ref,
         o_ref, qx_ref, gt_ref, ox_ref, gbuf, pbuf) = refs
    elif period:
        (x_ref, g_ref, w_ref, gx_ref, cw_ref, prefix_ref,
         o_ref, qx_ref, gt_ref, gbuf, pbuf) = refs
    else:
        (x_ref, g_ref, w_ref, gx_ref, cw_ref,
         o_ref, qx_ref, gt_ref, gbuf) = refs
    t = x_ref.shape[0]
    keep = gt_ref.shape[0]
    h = _rms_rows(x_ref[...], g_ref[0]).astype(BF16)

    def proj(lo, width):
        return jnp.dot(h, w_ref[:, lo:lo + width], preferred_element_type=F32)

    @pl.when(pl.program_id(0) % tiles_per_seq == 0)
    def _():
        gbuf[0:SUBLANES, :] = jnp.zeros((SUBLANES, CONV_CH), F32)

    if period:
        pbuf[0:t, :] = prefix_ref[...]
        pbuf[t:t + SUBLANES, :] = jnp.zeros((SUBLANES, CONV_CH), F32)
        tok = lax.broadcasted_iota(jnp.int32, (t, MXU_DIM), 0) % period
    cw = cw_ref[0]
    for c in range(CONV_CH // MXU_DIM):
        sl = slice(c * MXU_DIM, (c + 1) * MXU_DIM)
        gate_b = proj(sl.start, MXU_DIM)
        u = proj(CONV_CH + sl.start, MXU_DIM) * proj(2 * CONV_CH + sl.start, MXU_DIM)
        gbuf[SUBLANES:SUBLANES + t, sl] = u
        back1 = gbuf[SUBLANES - 1:SUBLANES - 1 + t, sl]
        back2 = gbuf[SUBLANES - 2:SUBLANES - 2 + t, sl]
        if period:
            back1 = jnp.where(tok >= 1, back1, pbuf[1:1 + t, sl])
            back2 = jnp.where(tok >= 2, back2, pbuf[0:t, sl])
        y = back2 * cw[0:1, sl] + back1 * cw[1:2, sl] + u * cw[2:3, sl]
        o_ref[:, sl] = (gate_b * y).astype(o_ref.dtype)
        gt_ref[:, sl] = u[t - keep:, :]
    gbuf[0:SUBLANES, :] = gbuf[t:t + SUBLANES, :]
    qx = proj(3 * CONV_CH, X_DIM)
    qx = qx * lax.rsqrt(_head_mean_sq(qx, _head_blockdiag()) + EPS) * gx_ref[0]
    qx_ref[...] = (qx * Q_SCALE).astype(qx_ref.dtype)
    if xattn:
        _xattn_sample_kernel(qx_ref, mk_ref, mv_ref, ox_ref, dec=period)


def _in_conv(x2d, layer, j, g_mix, w, gx, cw, prefix, *, tiles_per_seq, period, keep_rows, act_dtype, xattn=None):
    n = x2d.shape[0]
    tm = CONV_SEQ_BLOCK * period if xattn else min(MIX_TILE, n)
    nt = n // tm
    nseq = nt // tiles_per_seq
    row = lambda i: (i, 0)
    in_specs = [
        pl.BlockSpec((tm, D_MODEL), row),
        _layer_spec((1, D_MODEL), layer),
        _const_spec((D_MODEL, CONV_IN)),
        _layer_spec((1, X_DIM), layer),
        _layer_spec((CONV_W, CONV_CH), j),
    ]
    args = [x2d, g_mix, w, gx, cw]
    scratch = [pltpu.VMEM((tm + 2 * SUBLANES, CONV_CH), F32)]
    if period:
        in_specs.append(pl.BlockSpec((tm, CONV_CH), row))
        args.append(prefix)
        scratch.append(pltpu.VMEM((tm + SUBLANES, CONV_CH), F32))
    out_specs = [
        pl.BlockSpec((tm, CONV_CH), row),
        pl.BlockSpec((tm, X_DIM), row),
        pl.BlockSpec((keep_rows, CONV_CH), lambda i: (i // tiles_per_seq, 0)),
    ]
    out_shape = [
        jax.ShapeDtypeStruct((n, CONV_CH), act_dtype),
        jax.ShapeDtypeStruct((n, X_DIM), act_dtype),
        jax.ShapeDtypeStruct((nseq * keep_rows, CONV_CH), F32),
    ]
    if xattn:
        mk, mv = xattn
        mem = lambda i: (layer, i, 0, 0)
        in_specs += [pl.BlockSpec((1, CONV_SEQ_BLOCK, X_DIM, mk.shape[3]), mem)] * 2
        args += [mk, mv]
        out_specs.append(pl.BlockSpec((tm, X_DIM), row))
        out_shape.append(jax.ShapeDtypeStruct((n, X_DIM), F32))
    return pl.pallas_call(
        functools.partial(_in_conv_kernel, tiles_per_seq=tiles_per_seq, period=period, xattn=xattn is not None),
        grid=(nt,),
        in_specs=in_specs,
        out_specs=out_specs,
        out_shape=out_shape,
        scratch_shapes=scratch,
        name="in_conv",
    )(*args)


def _pair_lhs(slabs):
    low = lax.broadcasted_iota(jnp.int32, slabs[0].shape, 1) < HEAD_DIM
    zero = jnp.zeros_like(slabs[0])
    parts = []
    for s in slabs:
        parts += [jnp.where(low, s, zero), jnp.where(low, zero, s)]
    return jnp.concatenate(parts, axis=0)


def _swa_prompt_kernel(sinks_ref, q_ref, kc_ref, kp_ref, vc_ref, vp_ref, o_ref, kfull, vfull, bias):
    tq = q_ref.shape[0]

    @pl.when(jnp.logical_and(pl.program_id(0) == 0, pl.program_id(1) == 0))
    def _():
        r = lax.broadcasted_iota(jnp.int32, (WINDOW, 2 * WINDOW), 0)
        c = lax.broadcasted_iota(jnp.int32, (WINDOW, 2 * WINDOW), 1)
        band = (c > r) & (c - WINDOW <= r)
        for kind, visible in enumerate((band, band & (c >= WINDOW))):
            base = jnp.where(visible, 0.0, NEG)
            for h in range(N_Q_HEADS):
                bias[kind, h] = jnp.where(c == 0, sinks_ref[h] * LOG2E, base)

    first_kind = jnp.where(pl.program_id(1) == 0, 1, 0)
    kfull[:, 0:WINDOW] = kp_ref[...]
    kfull[:, WINDOW:] = kc_ref[...]
    vfull[:, 0:WINDOW] = vp_ref[...]
    vfull[:, WINDOW:] = vc_ref[...]
    col0 = lax.broadcasted_iota(jnp.int32, (PAIR, 2 * WINDOW), 1) == 0
    zero_slab = jnp.zeros((PAIR, 2 * WINDOW), BF16)
    low = lax.broadcasted_iota(jnp.int32, (WINDOW, PAIR), 1) < HEAD_DIM
    for qb in range(tq // WINDOW):
        r0 = qb * WINDOW
        kind = first_kind if qb == 0 else 0
        slabs = [None] * (GQA_GROUP * KV_DIM // PAIR)
        for pr in range(KV_DIM // PAIR):
            kslab = jnp.where(col0, zero_slab, kfull[pr * PAIR:(pr + 1) * PAIR, r0:r0 + 2 * WINDOW])
            vslab = jnp.where(col0, zero_slab, vfull[pr * PAIR:(pr + 1) * PAIR, r0:r0 + 2 * WINDOW])
            lhs = _pair_lhs([q_ref[r0:r0 + WINDOW, g * KV_DIM + pr * PAIR:g * KV_DIM + (pr + 1) * PAIR]
                             for g in range(GQA_GROUP)])
            s_all = jnp.dot(lhs, kslab, preferred_element_type=F32)
            probs, dens = [], []
            for g in range(GQA_GROUP):
                for e in range(2):
                    idx = g * 2 + e
                    head = (2 * pr + e) * GQA_GROUP + g
                    p, den = _softmax_rows(s_all[idx * WINDOW:(idx + 1) * WINDOW] + bias[kind, head])
                    probs.append(p)
                    dens.append(den)
            res = lax.dot_general(jnp.concatenate(probs, axis=0), vslab, _NT,
                                  preferred_element_type=F32)
            for g in range(GQA_GROUP):
                slabs[g * (KV_DIM // PAIR) + pr] = _pair_normalise(
                    low, res[(2 * g) * WINDOW:(2 * g + 1) * WINDOW], res[(2 * g + 1) * WINDOW:(2 * g + 2) * WINDOW],
                    dens[2 * g], dens[2 * g + 1])
        o_ref[r0:r0 + WINDOW, :] = jnp.concatenate(slabs, axis=1).astype(o_ref.dtype)


def _swa_prompt(sinks, q, kt, vt, *, batch, seq):
    tq = MIX_TILE
    nt = seq // tq
    per = tq // WINDOW
    rows = lambda b, j: (b * nt + j, 0)
    cur = lambda b, j: (0, b * nt + j)
    prev = lambda b, j: (0, b * nt * per + jnp.maximum(j * per - 1, 0))
    return pl.pallas_call(
        _swa_prompt_kernel,
        grid=(batch, nt),
        in_specs=[
            pl.BlockSpec(memory_space=pltpu.SMEM),
            pl.BlockSpec((tq, CONV_CH), rows),
            pl.BlockSpec((KV_DIM, tq), cur),
            pl.BlockSpec((KV_DIM, WINDOW), prev),
            pl.BlockSpec((KV_DIM, tq), cur),
            pl.BlockSpec((KV_DIM, WINDOW), prev),
        ],
        out_specs=pl.BlockSpec((tq, CONV_CH), rows),
        out_shape=jax.ShapeDtypeStruct(q.shape, BF16),
        scratch_shapes=[pltpu.VMEM((KV_DIM, tq + WINDOW), BF16)] * 2
        + [pltpu.VMEM((2, N_Q_HEADS, WINDOW, 2 * WINDOW), F32)],
        name="swa_prompt",
    )(sinks, q, kt, kt, vt, vt)


def _swa_sample_kernel(*refs, dec, carried, slot):
    if carried:
        sinks_ref, q_ref, kn_ref, vn_ref, kb_ref, vb_ref, _, _, o_ref, ko_ref, vo_ref = refs
    else:
        sinks_ref, q_ref, kn_ref, vn_ref, kb_ref, vb_ref, o_ref, ko_ref, vo_ref = refs
    nseq = kb_ref.shape[1]
    for other in range(ko_ref.shape[0]):
        if other != slot:
            ko_ref[other] = jnp.zeros(ko_ref.shape[1:], F32)
            vo_ref[other] = jnp.zeros(vo_ref.shape[1:], F32)
    steps_per_tile = LANES // (nseq * dec)
    base = (pl.program_id(0) % steps_per_tile) * (nseq * dec)
    grp = N_KV_HEADS * dec
    rows = GQA_GROUP * grp
    ri = lax.broadcasted_iota(jnp.int32, (rows, KV_DIM), 0)
    li = lax.broadcasted_iota(jnp.int32, (rows, KV_DIM), 1)
    head_mask = ((ri % grp) // dec) == (li // HEAD_DIM)
    tok = lax.broadcasted_iota(jnp.int32, (rows, WINDOW + LANES), 0) % dec
    col = lax.broadcasted_iota(jnp.int32, (rows, WINDOW + LANES), 1)
    cache_mask = (col < WINDOW) & (col > tok)
    lane = lax.broadcasted_iota(jnp.int32, (KV_DIM, WINDOW), 1)
    sink = jnp.concatenate(
        [jnp.full((dec, 1), sinks_ref[kv * GQA_GROUP + g] * LOG2E, F32)
         for g in range(GQA_GROUP) for kv in range(N_KV_HEADS)], axis=0)
    kn = kn_ref[...]
    vn = vn_ref[...]
    kn16 = kn.astype(BF16)
    vn16 = vn.astype(BF16)
    for s0 in range(0, nseq, SEQ_GROUP):
        seqs = range(s0, s0 + SEQ_GROUP)
        scores, values = [], []
        for s in seqs:
            off = base + s * dec
            new = col - (WINDOW + off)
            mask = cache_mask | ((new >= 0) & (new <= tok))
            qs = q_ref[pl.ds(s * dec, dec), :]
            qbd = jnp.concatenate(
                [jnp.concatenate([qs[:, g * KV_DIM:(g + 1) * KV_DIM]] * N_KV_HEADS, axis=0)
                 for g in range(GQA_GROUP)], axis=0)
            qbd = jnp.where(head_mask, qbd, 0.0).astype(BF16)
            keys = jnp.concatenate([kb_ref[0, s].astype(BF16), kn16], axis=1)
            values.append(jnp.concatenate([vb_ref[0, s].astype(BF16), vn16], axis=1))
            scores.append(jnp.where(mask, jnp.dot(qbd, keys, preferred_element_type=F32), NEG))
        probs = [_softmax_rows(sc, sink) for sc in scores]
        for s, (p, den), vals in zip(seqs, probs, values):
            o = lax.dot_general(p, vals, _NT, preferred_element_type=F32) * (1.0 / den)
            o = jnp.where(head_mask, o, 0.0)
            folded = []
            for g in range(GQA_GROUP):
                og = o[g * grp:g * grp + dec, :]
                for kv in range(1, N_KV_HEADS):
                    og = og + o[g * grp + kv * dec:g * grp + (kv + 1) * dec, :]
                folded.append(og)
            o_ref[pl.ds(s * dec, dec), :] = jnp.concatenate(folded, axis=1)
        for s in seqs:
            shift_new = (WINDOW - dec + LANES - (base + s * dec)) % LANES
            ko_ref[slot, s] = jnp.where(lane < WINDOW - dec, pltpu.roll(kb_ref[0, s], WINDOW - dec, axis=1),
                                     pltpu.roll(kn, shift_new, axis=1))
            vo_ref[slot, s] = jnp.where(lane < WINDOW - dec, pltpu.roll(vb_ref[0, s], WINDOW - dec, axis=1),
                                     pltpu.roll(vn, shift_new, axis=1))


def _att_sample_kernel(*refs, dec, carried, slot):
    n_in = 8 if carried else 6
    qx_ref, mk_ref, mv_ref = refs[n_in:n_in + 3]
    o_ref, ko_ref, vo_ref, ox_ref = refs[n_in + 3:]
    _xattn_sample_kernel(qx_ref, mk_ref, mv_ref, ox_ref, dec=dec)
    _swa_sample_kernel(*refs[:n_in], o_ref, ko_ref, vo_ref, dec=dec, carried=carried, slot=slot)


def _swa_sample(sinks, q, kt_new, vt_new, kbuf, vbuf, j, carry, *, dec, xattn=None):
    nseq = kbuf.shape[1]
    sb = SEQ_BLOCK
    steps_per_tile = LANES // (sb * dec)
    row = lambda i: (i, 0)
    tile = lambda i: (0, i // steps_per_tile)
    cache = lambda i: (j, i, 0, 0)
    if carry is None:
        new_cache = pl.BlockSpec((kbuf.shape[0], sb, KV_DIM, WINDOW), lambda i: (0, i, 0, 0))
    else:
        new_cache = pl.BlockSpec((1, sb, KV_DIM, WINDOW), cache)
    in_specs = [
        pl.BlockSpec(memory_space=pltpu.SMEM),
        pl.BlockSpec((sb * dec, CONV_CH), row),
        pl.BlockSpec((KV_DIM, LANES), tile),
        pl.BlockSpec((KV_DIM, LANES), tile),
        pl.BlockSpec((1, sb, KV_DIM, WINDOW), cache),
        pl.BlockSpec((1, sb, KV_DIM, WINDOW), cache),
    ]
    args = [sinks, q, kt_new, vt_new, kbuf, vbuf]
    aliases = {}
    if carry is not None:
        aliases = {len(args): 1, len(args) + 1: 2}
        in_specs += [pl.BlockSpec(memory_space=pl.ANY)] * 2
        args += list(carry)
    out_specs = [pl.BlockSpec((sb * dec, CONV_CH), row), new_cache, new_cache]
    out_shape = [jax.ShapeDtypeStruct(q.shape, F32), jax.ShapeDtypeStruct(kbuf.shape, F32),
                 jax.ShapeDtypeStruct(vbuf.shape, F32)]
    body = _swa_sample_kernel
    if xattn is not None:
        qx, mk, mv, layer = xattn
        mem = lambda i: (layer, i, 0, 0)
        in_specs += [pl.BlockSpec((sb * dec, X_DIM), row)] + [pl.BlockSpec((1, sb, X_DIM, mk.shape[3]), mem)] * 2
        args += [qx, mk, mv]
        out_specs.append(pl.BlockSpec((sb * dec, X_DIM), row))
        out_shape.append(jax.ShapeDtypeStruct(qx.shape, F32))
        body = _att_sample_kernel
    return pl.pallas_call(
        functools.partial(body, dec=dec, carried=carry is not None, slot=j if carry is None else 0),
        grid=(nseq // sb,),
        in_specs=in_specs,
        out_specs=out_specs,
        out_shape=out_shape,
        input_output_aliases=aliases,
        name="swa_sample",
    )(*args)


def _xattn_tile(q_ref, mk_ref, mv_ref):
    tq = q_ref.shape[0]
    mk = mk_ref[0, 0].astype(BF16)
    mv = mv_ref[0, 0].astype(BF16)
    low = lax.broadcasted_iota(jnp.int32, (WINDOW, PAIR), 1) < HEAD_DIM
    blocks = []
    for rb in range(tq // WINDOW):
        r0 = rb * WINDOW
        slabs = []
        for pr in range(X_DIM // PAIR):
            lhs = _pair_lhs([q_ref[r0:r0 + WINDOW, pr * PAIR:(pr + 1) * PAIR]])
            s_all = jnp.dot(lhs, mk[pr * PAIR:(pr + 1) * PAIR], preferred_element_type=F32)
            p0, den0 = _softmax_rows(s_all[:WINDOW])
            p1, den1 = _softmax_rows(s_all[WINDOW:])
            res = lax.dot_general(jnp.concatenate([p0, p1], axis=0), mv[pr * PAIR:(pr + 1) * PAIR], _NT,
                                  preferred_element_type=F32)
            slabs.append(_pair_normalise(low, res[:WINDOW], res[WINDOW:], den0, den1))
        blocks.append(jnp.concatenate(slabs, axis=1).astype(BF16))
    return jnp.concatenate(blocks, axis=0)


def _xattn_sample_kernel(q_ref, mk_ref, mv_ref, o_ref, *, dec):
    nseq = mk_ref.shape[1]
    rows = N_X_HEADS * dec
    head_mask = (lax.broadcasted_iota(jnp.int32, (rows, X_DIM), 0) // dec) == (
        lax.broadcasted_iota(jnp.int32, (rows, X_DIM), 1) // HEAD_DIM)
    for s0 in range(0, nseq, SEQ_GROUP):
        seqs = range(s0, s0 + SEQ_GROUP)
        scores = []
        for s in seqs:
            qs = q_ref[pl.ds(s * dec, dec), :]
            qbd = jnp.where(head_mask, jnp.concatenate([qs] * N_X_HEADS, axis=0), 0.0).astype(BF16)
            scores.append(jnp.dot(qbd, mk_ref[0, s].astype(BF16), preferred_element_type=F32))
        probs = [_softmax_rows(sc) for sc in scores]
        for s, (p, den) in zip(seqs, probs):
            o = lax.dot_general(p, mv_ref[0, s].astype(BF16), _NT, preferred_element_type=F32) * (1.0 / den)
            o = jnp.where(head_mask, o, 0.0)
            acc = o[0:dec, :]
            for h in range(1, N_X_HEADS):
                acc = acc + o[h * dec:(h + 1) * dec, :]
            o_ref[pl.ds(s * dec, dec), :] = acc


def _mlp_tile(x, o, wo, gm, wu_ref, wd_ref):
    x1 = x + jnp.dot(o, wo, preferred_element_type=F32)
    hm = _rms_rows(x1, gm).astype(BF16)
    acc = x1
    for c in range(D_FF // FF_CHUNK):
        a = jnp.dot(hm, wu_ref[:, c * FF_CHUNK:(c + 1) * FF_CHUNK], preferred_element_type=F32)
        a = jnp.square(jnp.maximum(a, 0.0)).astype(BF16)
        acc = acc + jnp.dot(a, wd_ref[c * FF_CHUNK:(c + 1) * FF_CHUNK, :], preferred_element_type=F32)
    return acc


def _out_mlp_kernel(*refs, n_prompt_tiles, n_convert):
    (xp_ref, osp_ref, qxp_ref, mk_ref, mv_ref, xs_ref, oss_ref, oxs_ref,
     wo_ref, gm_ref, wu_ref, wd_ref) = refs[:12]
    yp_ref, ys_ref = refs[12 + n_convert:14 + n_convert]
    for src, dst in zip(refs[12:12 + n_convert], refs[14 + n_convert:]):
        dst[...] = src[0].astype(BF16)
    is_sample = pl.program_id(0) == n_prompt_tiles

    @pl.when(jnp.logical_not(is_sample))
    def _():
        o = jnp.concatenate([osp_ref[...], _xattn_tile(qxp_ref, mk_ref, mv_ref)], axis=1)
        yp_ref[...] = _mlp_tile(xp_ref[...], o, wo_ref[...], gm_ref[0], wu_ref, wd_ref)

    @pl.when(is_sample)
    def _():
        o = jnp.concatenate([oss_ref[...].astype(BF16), oxs_ref[...].astype(BF16)], axis=1)
        ys_ref[...] = _mlp_tile(xs_ref[...], o, wo_ref[...], gm_ref[0], wu_ref, wd_ref)


def _out_mlp(xp, os_p, qx_p, memory, xs, os_s, ox_s, layer, wo, gm, wu, wd, *, tiles_per_batch, convert=()):
    n = xp.shape[0]
    tm = TOKEN_TILE
    assert xs.shape[0] == tm
    npt = n // tm
    n_mem = memory[0].shape[3]
    last = npt - 1
    prow = lambda i: (jnp.minimum(i, last), 0)
    mem = lambda i: (layer, jnp.minimum(i, last) // tiles_per_batch, 0, 0)
    cv_in, cv_args, cv_out, cv_shape = _convert_specs(convert, npt, last)
    return pl.pallas_call(
        functools.partial(_out_mlp_kernel, n_prompt_tiles=npt, n_convert=len(convert)),
        grid=(npt + 1,),
        in_specs=[
            pl.BlockSpec((tm, D_MODEL), prow),
            pl.BlockSpec((tm, CONV_CH), prow),
            pl.BlockSpec((tm, X_DIM), prow),
            pl.BlockSpec((1, 1, X_DIM, n_mem), mem),
            pl.BlockSpec((1, 1, X_DIM, n_mem), mem),
            _const_spec((tm, D_MODEL)),
            _const_spec((tm, CONV_CH)),
            _const_spec((tm, X_DIM)),
            _const_spec((CONV_CH + X_DIM, D_MODEL)),
            _layer_spec((1, D_MODEL), layer),
            _const_spec((D_MODEL, D_FF)),
            _const_spec((D_FF, D_MODEL)),
        ] + cv_in,
        out_specs=[pl.BlockSpec((tm, D_MODEL), prow), pl.BlockSpec((tm, D_MODEL), lambda i: (0, 0))] + cv_out,
        out_shape=[jax.ShapeDtypeStruct((n, D_MODEL), F32), jax.ShapeDtypeStruct((tm, D_MODEL), F32)] + cv_shape,
        name="out_mlp",
    )(xp, os_p, qx_p, *memory, xs, os_s, ox_s, wo, gm, wu, wd, *cv_args)


def _rope_tables(pos):
    half = HEAD_DIM // 2
    inv = ROPE_THETA ** (-jnp.arange(half, dtype=F32) * 2.0 / HEAD_DIM)
    ang = pos[:, None] * inv[None, :]
    return jnp.cos(ang), jnp.sin(ang)


def _rope_lane_tables(cos, sin):
    reps = LANES // (HEAD_DIM // 2)
    return jnp.tile(cos, (1, reps)), jnp.concatenate([-sin, sin] * (reps // 2), axis=1)


def _group_major(w, axis):
    shape = w.shape
    w = w.reshape(shape[:axis] + (N_KV_HEADS, GQA_GROUP, HEAD_DIM) + shape[axis + 1:])
    return jnp.swapaxes(w, axis, axis + 1).reshape(shape)


def _feature_major(cache):
    lead = cache.shape[:-3]
    pos, heads, hd = cache.shape[-3:]
    nd = len(lead)
    perm = tuple(range(nd)) + (nd + 1, nd + 2, nd)
    return jnp.transpose(cache, perm).reshape(lead + (heads * hd, pos))


def _position_major(cache_t, heads):
    lead = cache_t.shape[:-2]
    pos = cache_t.shape[-1]
    nd = len(lead)
    perm = tuple(range(nd)) + (nd + 2, nd, nd + 1)
    return jnp.transpose(cache_t.reshape(lead + (heads, HEAD_DIM, pos)), perm)


def _gain_cols(g):
    return jnp.broadcast_to(g[:, :, None], g.shape + (LANES,))


def kernel(x_prompt, x_sample, mem_prompt, cache_swa_k, cache_swa_v, state_conv, cache_mem_k, cache_mem_v,
           norm_mix, w_in_att, q_norm_att, k_norm_att, sinks, w_in_conv, conv_w, norm_mem, w_mem_kv,
           q_norm_x, k_norm_x, w_out, norm_mlp, w_up, w_down):
    batch, seq, _ = x_prompt.shape
    nseq, dec, _ = x_sample.shape
    n_mem = mem_prompt.shape[1]
    assert seq % MIX_TILE == 0 and seq % ATT_TILE == 0
    assert (nseq * dec) == TOKEN_TILE and nseq % SEQ_BLOCK == 0
    assert LANES % (SEQ_BLOCK * dec) == 0

    kv0, kv1 = CONV_CH, CONV_CH + 2 * KV_DIM
    w_q = jnp.concatenate([_group_major(w_in_att[:, :, :kv0], 2), w_in_att[:, :, kv1:]], axis=2).astype(BF16)
    w_kv_t = jnp.swapaxes(w_in_att[:, :, kv0:kv1], 1, 2).astype(BF16)
    w_o = [jnp.concatenate([_group_major(w_out[i, :CONV_CH], 0), w_out[i, CONV_CH:]], axis=0).astype(BF16)
           if i % 2 == 0 else w_out[i].astype(BF16) for i in range(DEPTH)]
    w_u = w_d = w_conv = None
    w_mkv_t = jnp.swapaxes(w_mem_kv, 1, 2).astype(BF16)

    g_mix = norm_mix[:, None, :]
    g_mlp = norm_mlp[:, None, :]
    gq = jnp.tile(q_norm_att, (1, N_Q_HEADS))[:, None, :]
    gx = jnp.tile(q_norm_x, (1, N_X_HEADS))[:, None, :]
    gk_cols = _gain_cols(k_norm_att)
    gkx_cols = _gain_cols(k_norm_x)

    cos_p, sin_p = _rope_tables(jnp.arange(seq, dtype=F32))
    cos_s, sin_s = _rope_tables(PAST_LEN + (jnp.arange(nseq * dec) % dec).astype(F32))
    rope_p = _rope_lane_tables(cos_p, sin_p) + (cos_p.T, sin_p.T)
    rope_s = _rope_lane_tables(cos_s, sin_s) + (cos_s.T, sin_s.T)

    mk_p, mv_p = _mem_kv(mem_prompt.reshape(batch * n_mem, D_MODEL), norm_mem[:, None, :], w_mkv_t, gkx_cols,
                         batch=batch, n_mem=n_mem)
    mk_s = _feature_major(cache_mem_k)
    mv_s = _feature_major(cache_mem_v)
    kbuf = _feature_major(cache_swa_k)
    vbuf = _feature_major(cache_swa_v)

    xp = x_prompt.reshape(batch * seq, D_MODEL)
    xs = x_sample.reshape(nseq * dec, D_MODEL)
    k_p, v_p, c_p, c_s = [], [], [], []
    cache_s = None
    for i in range(DEPTH):
        j = i // 2
        if i % 2 == 0:
            first = [(w_up, 0), (w_down, 0)] if i == 0 else []
            q, qx_p, kt, vt, kt32, vt32, *w_first = _in_att(
                xp, i, j, g_mix, w_q, w_kv_t, gq, gx, gk_cols, *rope_p,
                tiles_per_seq=seq // ATT_TILE, keep_cols=WINDOW, act_dtype=BF16, convert=first)
            if w_first:
                w_u, w_d = w_first
            k_p.append(kt32)
            v_p.append(vt32)
            os_p = _swa_prompt(sinks[j], q, kt, vt, batch=batch, seq=seq)
            q, qx_s, kt, vt = _in_att(xs, i, j, g_mix, w_q, w_kv_t, gq, gx, gk_cols, *rope_s,
                                      tiles_per_seq=1, keep_cols=0, act_dtype=F32)
            os_s, *cache_s, ox_s = _swa_sample(sinks[j], q, kt, vt, kbuf, vbuf, j, cache_s, dec=dec,
                                               xattn=(qx_s, mk_s, mv_s, i))
        else:
            os_p, qx_p, tail = _in_conv(xp, i, j, g_mix, w_conv, gx, conv_w, None,
                                        tiles_per_seq=seq // MIX_TILE, period=0,
                                        keep_rows=SUBLANES, act_dtype=BF16)
            c_p.append(tail.reshape(batch, SUBLANES, CONV_CH)[:, SUBLANES - (CONV_W - 1):])
            prefix = jnp.pad(state_conv[j], ((0, 0), (0, dec - (CONV_W - 1)), (0, 0)))
            os_s, _, tail, ox_s = _in_conv(xs, i, j, g_mix, w_conv, gx, conv_w,
                                           prefix.reshape(nseq * dec, CONV_CH),
                                           tiles_per_seq=1, period=dec,
                                           keep_rows=CONV_SEQ_BLOCK * dec, act_dtype=F32, xattn=(mk_s, mv_s))
            c_s.append(tail.reshape(nseq, dec, CONV_CH)[:, dec - (CONV_W - 1):])
        convert = []
        if i + 1 < DEPTH:
            convert = [(w_up, i + 1), (w_down, i + 1)] + ([(w_in_conv, (i + 1) // 2)] if i % 2 == 0 else [])
        xp, xs, *w_next = _out_mlp(xp, os_p, qx_p, (mk_p, mv_p), xs, os_s, ox_s, i, w_o[i], g_mlp, w_u, w_d,
                                   tiles_per_batch=seq // TOKEN_TILE, convert=convert)
        if w_next:
            w_u, w_d = w_next[:2]
            w_conv = w_next[2] if len(w_next) > 2 else None

    return (xp.reshape(batch, seq, D_MODEL),
            xs.reshape(nseq, dec, D_MODEL),
            _position_major(jnp.stack(k_p), N_KV_HEADS),
            _position_major(jnp.stack(v_p), N_KV_HEADS),
            jnp.stack(c_p),
            _position_major(mk_p, N_X_HEADS),
            _position_major(mv_p, N_X_HEADS),
            _position_major(cache_s[0], N_KV_HEADS),
            _position_major(cache_s[1], N_KV_HEADS),
            jnp.stack(c_s))
```
